```python
import math
import jax
import jax.numpy as jnp
from jax import lax
import numpy as np


D_MODEL = 1024
BATCH = 16
SEQ = 256
DEPTH = 2
DEC_BATCH = 2
DEC_SEQ = 1024
PAST_LEN = 512

GRID_W = 64
D_MIX = D_MODEL
CHUNK = 128
N_DIR = 2
EPS = 1e-6
SSD_INNER = D_MIX // 2
SSD_HEAD_DIM = 64
SSD_HEADS = SSD_INNER // SSD_HEAD_DIM
SSD_GROUPS = 2
SSD_STATE = 128
SSD_CONV = 5
SSD_CONV_DIM = SSD_INNER + 2 * SSD_GROUPS * SSD_STATE
SGU_DIM = D_MIX // 4
SGU_HEADS = 4
SGU_HEAD_DIM = SGU_DIM // SGU_HEADS
S5_DIM = D_MIX // 4
S5_GROUP_CH = 16
S5_GROUPS = S5_DIM // S5_GROUP_CH
S5_STATE = 64
D_FF = 4 * D_MODEL
OFF_Z = 0
OFF_XBC = OFF_Z + SSD_INNER
OFF_DT = OFF_XBC + SSD_CONV_DIM
OFF_SGU = OFF_DT + N_DIR * SSD_HEADS
OFF_S5 = OFF_SGU + 2 * SGU_DIM
IN_DIM = OFF_S5 + S5_DIM

kernel_name = 'hybrid_ssd_sgu_s5_diffusion_step'


def rmsnorm(x, g):
    xf = x.astype(jnp.float32)
    y = xf * lax.rsqrt(jnp.mean(xf * xf, axis=-1, keepdims=True) + EPS)
    return (y * g.astype(jnp.float32)).astype(x.dtype)


def centred_depthwise_conv(x, w, b):
    pad = (SSD_CONV - 1) // 2
    y = lax.conv_general_dilated(x, w[:, None, :], window_strides=(1,), padding=[(pad, pad)],
                                 dimension_numbers=('NWC', 'WIO', 'NWC'),
                                 feature_group_count=x.shape[-1])
    return y + b


def ssd_scan(x, dt, a, bm, cm, h0):
    b_, L, H, P = x.shape
    nc = L // CHUNK
    R = H // SSD_GROUPS
    x = x.reshape(b_, nc, CHUNK, SSD_GROUPS, R, P)
    dt = dt.reshape(b_, nc, CHUNK, SSD_GROUPS, R)
    bm = bm.reshape(b_, nc, CHUNK, SSD_GROUPS, SSD_STATE)
    cm = cm.reshape(b_, nc, CHUNK, SSD_GROUPS, SSD_STATE)
    a_cum = jnp.cumsum(dt * a.reshape(SSD_GROUPS, R), axis=2)
    xdt = x * dt[..., None]
    seg = a_cum[:, :, :, None] - a_cum[:, :, None, :]
    lower = jnp.tril(jnp.ones((CHUNK, CHUNK), dtype=bool))[:, :, None, None]
    lmat = jnp.exp(jnp.where(lower, seg, -jnp.inf))
    cb = jnp.einsum('bcqgn,bckgn->bcqkg', cm, bm)
    y_diag = jnp.einsum('bcqkg,bcqkgr,bckgrp->bcqgrp', cb, lmat, xdt)
    decay_to_end = jnp.exp(a_cum[:, :, -1:] - a_cum)
    chunk_states = jnp.einsum('bckgn,bckgr,bckgrp->bcgrpn', bm, decay_to_end, xdt)
    chunk_decay = jnp.exp(a_cum[:, :, -1])

    def step(h, inp):
        dec, st = inp
        return dec[..., None, None] * h + st, h

    h_final, h_prev = lax.scan(step, h0.reshape(b_, SSD_GROUPS, R, P, SSD_STATE),
                               (jnp.moveaxis(chunk_decay, 1, 0), jnp.moveaxis(chunk_states, 1, 0)))
    h_prev = jnp.moveaxis(h_prev, 0, 1)
    y_off = jnp.einsum('bcqgn,bcgrpn,bcqgr->bcqgrp', cm, h_prev, jnp.exp(a_cum))
    y = (y_diag + y_off).reshape(b_, L, H, P)
    return y, h_final.reshape(b_, H, P, SSD_STATE)


def ssd_mixer(s, p, h0):
    dtype = s.dtype
    b_, L, _ = s.shape
    f32 = jnp.float32
    z = s[..., OFF_Z:OFF_XBC]
    xbc = jax.nn.silu(centred_depthwise_conv(s[..., OFF_XBC:OFF_DT], p['ssd_conv_w'], p['ssd_conv_b'])).astype(f32)
    gn = SSD_GROUPS * SSD_STATE
    xs = xbc[..., :SSD_INNER].reshape(b_, L, SSD_HEADS, SSD_HEAD_DIM)
    bm = xbc[..., SSD_INNER:SSD_INNER + gn].reshape(b_, L, SSD_GROUPS, SSD_STATE)
    cm = xbc[..., SSD_INNER + gn:].reshape(b_, L, SSD_GROUPS, SSD_STATE)
    dt = jax.nn.softplus(s[..., OFF_DT:OFF_SGU].astype(f32).reshape(b_, L, N_DIR, SSD_HEADS)
                         + p['ssd_dt_bias'].astype(f32))
    a = -jnp.exp(p['ssd_a_log'].astype(f32))
    flip = lambda t: jnp.flip(t, axis=1)
    y_f, h_f = ssd_scan(xs, dt[:, :, 0], a[0], bm, cm, h0[:, 0])
    y_b, h_b = ssd_scan(flip(xs), flip(dt[:, :, 1]), a[1], flip(bm), flip(cm), h0[:, 1])
    y = y_f + flip(y_b) + p['ssd_d'].astype(f32)[:, None] * xs
    y = y.reshape(b_, L, SSD_INNER) * jax.nn.silu(z.astype(f32))
    y = rmsnorm(y, p['ssd_norm_g']).astype(dtype)
    return y, jnp.stack([h_f, h_b], axis=1)


def sgu_mixer(uv, p):
    uv = jax.nn.gelu(uv)
    u, v = uv[..., :SGU_DIM], uv[..., SGU_DIM:]
    v = rmsnorm(v, p['sgu_norm_g'])
    b_, L, _ = v.shape
    v = v.reshape(b_, L // CHUNK, CHUNK, SGU_HEADS, SGU_HEAD_DIM)
    mix = (jnp.einsum('hqk,bckhd->bcqhd', p['sgu_w'], v)
           + jnp.swapaxes(p['sgu_b'], 0, 1)[None, None, :, :, None])
    return u * mix.reshape(b_, L, SGU_DIM)


def to_col_major(u):
    b_, L, C = u.shape
    rows = L // GRID_W
    return u.reshape(b_, rows, GRID_W, C).transpose(0, 2, 1, 3).reshape(b_, L, C)


def from_col_major(u):
    b_, L, C = u.shape
    rows = L // GRID_W
    return u.reshape(b_, GRID_W, rows, C).transpose(0, 2, 1, 3).reshape(b_, L, C)


def s5_scan(bu, lam_bar, h0):
    a = jnp.broadcast_to(lam_bar, bu.shape)

    def combine(left, right):
        a_l, b_l = left
        a_r, b_r = right
        return a_r * a_l, a_r * b_l + b_r

    a_cum, b_cum = lax.associative_scan(combine, (a, bu), axis=1)
    h = a_cum * h0[:, None] + b_cum
    return h, h[:, -1]


def s5_mixer(u, p, h0, column_major):
    dtype = u.dtype
    f32 = jnp.float32
    if column_major:
        u = to_col_major(u)
    b_, L, _ = u.shape
    uf = u.astype(f32).reshape(b_, L, S5_GROUPS, S5_GROUP_CH)
    b_mat = lax.complex(p['s5_b_re'].astype(f32), p['s5_b_im'].astype(f32))
    c_mat = lax.complex(p['s5_c_re'].astype(f32), p['s5_c_im'].astype(f32))
    h_sum = 0.0
    finals = []
    for d in range(N_DIR):
        lam = lax.complex(p['s5_lambda_re'][d].astype(f32), p['s5_lambda_im'][d].astype(f32))
        step = jnp.exp(p['s5_log_dt'][d].astype(f32))[:, None]
        lam_bar = jnp.exp(lam * step)
        b_bar = ((lam_bar - 1.0) / lam)[..., None] * b_mat
        u_d = uf if d == 0 else jnp.flip(uf, axis=1)
        bu = jnp.einsum('gsc,blgc->blgs', b_bar, u_d.astype(jnp.complex64))
        h, h_last = s5_scan(bu, lam_bar, h0[:, d])
        h_sum = h_sum + (h if d == 0 else jnp.flip(h, axis=1))
        finals.append(h_last)
    y = jnp.real(jnp.einsum('gcs,blgs->blgc', c_mat, h_sum))
    y = y + p['s5_d'].astype(f32).reshape(S5_GROUPS, S5_GROUP_CH) * uf
    y = jax.nn.gelu(y.reshape(b_, L, S5_DIM))
    y = y * jax.nn.sigmoid(y @ p['s5_glu_w'].astype(f32) + p['s5_glu_b'].astype(f32))
    y = y.astype(dtype)
    if column_major:
        y = from_col_major(y)
    return y, jnp.stack(finals, axis=1)


def trunk_layer(x, mod, p, h0_ssd, h0_s5, latent):
    shift1, scale1, gate1, shift2, scale2, gate2 = jnp.split(mod, 6, axis=-1)
    h = rmsnorm(x, p['norm1_g']) * (1 + scale1) + shift1
    proj = h @ p['w_in']
    y_ssd, hs_ssd = ssd_mixer(proj[..., :OFF_SGU], p, h0_ssd)
    y_sgu = sgu_mixer(proj[..., OFF_SGU:OFF_S5], p)
    y_s5, hs_s5 = s5_mixer(proj[..., OFF_S5:], p, h0_s5, latent)
    mixed = jnp.concatenate([y_ssd, y_sgu, y_s5], axis=-1)
    x = x + gate1 * (mixed @ p['w_out'])
    h = rmsnorm(x, p['norm2_g']) * (1 + scale2) + shift2
    x = x + gate2 * (jnp.square(jax.nn.relu(h @ p['ffn_w1'])) @ p['ffn_w2'])
    return x, hs_ssd, hs_s5


def setup_inputs(seed: int = 0) -> dict:
    key = jax.random.key(seed)
    ks = iter(jax.random.split(key, 48))
    f32 = jnp.float32

    def nrm(shape, scale):
        return scale * jax.random.normal(next(ks), shape, f32)

    lo, hi = math.log(1e-3), math.log(1e-1)
    x_prompt = nrm((BATCH, SEQ, D_MODEL), 1.0)
    x_sample = nrm((DEC_BATCH, DEC_SEQ, D_MODEL), 1.0)
    state_ssd = nrm((DEC_BATCH, DEPTH, N_DIR, SSD_HEADS, SSD_HEAD_DIM, SSD_STATE), 0.5)
    state_s5_re = nrm((DEC_BATCH, DEPTH, N_DIR, S5_GROUPS, S5_STATE), 0.5)
    state_s5_im = nrm((DEC_BATCH, DEPTH, N_DIR, S5_GROUPS, S5_STATE), 0.5)
    c = nrm((DEC_BATCH, D_MODEL), 1.0)
    c_ctx = nrm((D_MODEL,), 1.0)
    ada_w = nrm((DEPTH, D_MODEL, 6 * D_MODEL), 0.3 * D_MODEL ** -0.5)
    ada_b = nrm((DEPTH, 6 * D_MODEL), 0.01)
    norm1_g = 1.0 + nrm((DEPTH, D_MODEL), 0.01)
    norm2_g = 1.0 + nrm((DEPTH, D_MODEL), 0.01)
    w_in = nrm((DEPTH, D_MODEL, IN_DIM), D_MODEL ** -0.5)
    ssd_conv_w = nrm((DEPTH, SSD_CONV, SSD_CONV_DIM), SSD_CONV ** -0.5)
    ssd_conv_b = nrm((DEPTH, SSD_CONV_DIM), 0.01)
    dt0 = jnp.exp(jax.random.uniform(next(ks), (DEPTH, N_DIR, SSD_HEADS), f32, lo, hi))
    ssd_dt_bias = dt0 + jnp.log(-jnp.expm1(-dt0))
    ssd_a_log = jnp.log(jax.random.uniform(next(ks), (DEPTH, N_DIR, SSD_HEADS), f32, 1.0, 16.0))
    ssd_d = 1.0 + nrm((DEPTH, SSD_HEADS), 0.01)
    ssd_norm_g = 1.0 + nrm((DEPTH, SSD_INNER), 0.01)
    sgu_norm_g = 1.0 + nrm((DEPTH, SGU_DIM), 0.01)
    sgu_w = nrm((DEPTH, SGU_HEADS, CHUNK, CHUNK), CHUNK ** -0.5)
    sgu_b = 1.0 + nrm((DEPTH, SGU_HEADS, CHUNK), 0.01)
    s5_lambda_re = -0.5 + nrm((DEPTH, N_DIR, S5_GROUPS, S5_STATE), 0.01)
    s5_lambda_im = math.pi * jnp.arange(S5_STATE, dtype=f32) + nrm((DEPTH, N_DIR, S5_GROUPS, S5_STATE), 0.01)
    s5_log_dt = jax.random.uniform(next(ks), (DEPTH, N_DIR, S5_GROUPS), f32, lo, hi)
    s5_b_re = nrm((DEPTH, S5_GROUPS, S5_STATE, S5_GROUP_CH), (2 * S5_GROUP_CH) ** -0.5)
    s5_b_im = nrm((DEPTH, S5_GROUPS, S5_STATE, S5_GROUP_CH), (2 * S5_GROUP_CH) ** -0.5)
    s5_c_re = nrm((DEPTH, S5_GROUPS, S5_GROUP_CH, S5_STATE), (2 * S5_STATE) ** -0.5)
    s5_c_im = nrm((DEPTH, S5_GROUPS, S5_GROUP_CH, S5_STATE), (2 * S5_STATE) ** -0.5)
    s5_d = nrm((DEPTH, S5_DIM), 1.0)
    s5_glu_w = nrm((DEPTH, S5_DIM, S5_DIM), S5_DIM ** -0.5)
    s5_glu_b = nrm((DEPTH, S5_DIM), 0.01)
    w_out = nrm((DEPTH, D_MIX, D_MODEL), D_MIX ** -0.5)
    ffn_w1 = nrm((DEPTH, D_MODEL, D_FF), D_MODEL ** -0.5)
    ffn_w2 = nrm((DEPTH, D_FF, D_MODEL), D_FF ** -0.5)
    final_norm_g = 1.0 + nrm((D_MODEL,), 0.01)
    return {'x_prompt': x_prompt, 'x_sample': x_sample, 'state_ssd': state_ssd,
            'state_s5_re': state_s5_re, 'state_s5_im': state_s5_im, 'c': c, 'c_ctx': c_ctx,
            'ada_w': ada_w, 'ada_b': ada_b, 'norm1_g': norm1_g, 'norm2_g': norm2_g, 'w_in': w_in,
            'ssd_conv_w': ssd_conv_w, 'ssd_conv_b': ssd_conv_b, 'ssd_dt_bias': ssd_dt_bias,
            'ssd_a_log': ssd_a_log, 'ssd_d': ssd_d, 'ssd_norm_g': ssd_norm_g,
            'sgu_norm_g': sgu_norm_g, 'sgu_w': sgu_w, 'sgu_b': sgu_b,
            's5_lambda_re': s5_lambda_re, 's5_lambda_im': s5_lambda_im, 's5_log_dt': s5_log_dt,
            's5_b_re': s5_b_re, 's5_b_im': s5_b_im, 's5_c_re': s5_c_re, 's5_c_im': s5_c_im,
            's5_d': s5_d, 's5_glu_w': s5_glu_w, 's5_glu_b': s5_glu_b, 'w_out': w_out,
            'ffn_w1': ffn_w1, 'ffn_w2': ffn_w2, 'final_norm_g': final_norm_g}


def reference(x_prompt, x_sample, state_ssd, state_s5_re, state_s5_im, c, c_ctx,
              ada_w, ada_b, norm1_g, norm2_g, w_in, ssd_conv_w, ssd_conv_b, ssd_dt_bias,
              ssd_a_log, ssd_d, ssd_norm_g, sgu_norm_g, sgu_w, sgu_b,
              s5_lambda_re, s5_lambda_im, s5_log_dt, s5_b_re, s5_b_im, s5_c_re, s5_c_im,
              s5_d, s5_glu_w, s5_glu_b, w_out, ffn_w1, ffn_w2, final_norm_g):
    f32 = jnp.float32

    def params_of(l):
        return dict(norm1_g=norm1_g[l], norm2_g=norm2_g[l], w_in=w_in[l],
                    ssd_conv_w=ssd_conv_w[l], ssd_conv_b=ssd_conv_b[l], ssd_dt_bias=ssd_dt_bias[l],
                    ssd_a_log=ssd_a_log[l], ssd_d=ssd_d[l], ssd_norm_g=ssd_norm_g[l],
                    sgu_norm_g=sgu_norm_g[l], sgu_w=sgu_w[l], sgu_b=sgu_b[l],
                    s5_lambda_re=s5_lambda_re[l], s5_lambda_im=s5_lambda_im[l], s5_log_dt=s5_log_dt[l],
                    s5_b_re=s5_b_re[l], s5_b_im=s5_b_im[l], s5_c_re=s5_c_re[l], s5_c_im=s5_c_im[l],
                    s5_d=s5_d[l], s5_glu_w=s5_glu_w[l], s5_glu_b=s5_glu_b[l], w_out=w_out[l],
                    ffn_w1=ffn_w1[l], ffn_w2=ffn_w2[l])

    bp = x_prompt.shape[0]
    x = x_prompt
    ctx_ssd, ctx_s5 = [], []
    zero_ssd = jnp.zeros((bp, N_DIR, SSD_HEADS, SSD_HEAD_DIM, SSD_STATE), f32)
    zero_s5 = jnp.zeros((bp, N_DIR, S5_GROUPS, S5_STATE), jnp.complex64)
    for l in range(DEPTH):
        mod = (jax.nn.silu(c_ctx)[None, :] @ ada_w[l] + ada_b[l])[:, None, :]
        x, hs_ssd, hs_s5 = trunk_layer(x, mod, params_of(l), zero_ssd, zero_s5, False)
        ctx_ssd.append(hs_ssd)
        ctx_s5.append(hs_s5)
    y_prompt = rmsnorm(x, final_norm_g)
    new_state_ssd = jnp.stack(ctx_ssd, axis=1).astype(x_prompt.dtype)
    s5_all = jnp.stack(ctx_s5, axis=1)
    new_state_s5_re = jnp.real(s5_all).astype(x_prompt.dtype)
    new_state_s5_im = jnp.imag(s5_all).astype(x_prompt.dtype)

    x = x_sample
    for l in range(DEPTH):
        mod = (jax.nn.silu(c) @ ada_w[l] + ada_b[l])[:, None, :]
        h0_ssd = state_ssd[:, l].astype(f32)
        h0_s5 = lax.complex(state_s5_re[:, l].astype(f32), state_s5_im[:, l].astype(f32))
        x, _, _ = trunk_layer(x, mod, params_of(l), h0_ssd, h0_s5, True)
    y_sample = rmsnorm(x, final_norm_g)

    return (y_prompt, y_sample, new_state_ssd, new_state_s5_re, new_state_s5_im)
```

```python
import functools
import math

import jax
import jax.numpy as jnp
from jax import lax
from jax.experimental import pallas as pl
from jax.experimental.pallas import tpu as pltpu

F32 = jnp.float32
BF16 = jnp.bfloat16

D_MODEL = 1024
BATCH = 16
SEQ = 256
DEPTH = 2
DEC_BATCH = 2
DEC_SEQ = 1024
GRID_W = 64
CHUNK = 128
N_DIR = 2
EPS = 1e-6
SSD_INNER = 512
SSD_HEAD_DIM = 64
SSD_HEADS = 8
SSD_GROUPS = 2
SSD_STATE = 128
SSD_CONV = 5
SSD_CONV_DIM = SSD_INNER + 2 * SSD_GROUPS * SSD_STATE
SGU_DIM = 256
SGU_HEADS = 4
S5_DIM = 256
S5_GROUP_CH = 16
S5_GROUPS = 16
S5_STATE = 64
D_FF = 4 * D_MODEL
OFF_XBC = SSD_INNER
OFF_DT = OFF_XBC + SSD_CONV_DIM
OFF_SGU = OFF_DT + N_DIR * SSD_HEADS
OFF_S5 = OFF_SGU + 2 * SGU_DIM
IN_DIM = OFF_S5 + S5_DIM

N_CTX = BATCH * SEQ
N_LAT = DEC_BATCH * DEC_SEQ
N_TOK = N_CTX + N_LAT
LANES = 128
SUBLANES = 8
MOD_ROWS = SUBLANES
S5_LANES = S5_GROUPS * S5_STATE
S5_SEG = SUBLANES
S5_PW_ROWS = DEC_SEQ // S5_SEG
PAIR = 2 * SSD_HEAD_DIM
N_PAIR = SSD_HEADS // 2
TOK_TILE = 256
ROW_SLAB = 256
VMEM_LIMIT = 56 * 1024 * 1024

_NT = (((1,), (1,)), ((), ()))
_TN = (((0,), (0,)), ((), ()))


def _params(n_axes=1):
    return pltpu.CompilerParams(dimension_semantics=("arbitrary",) * n_axes,
                                vmem_limit_bytes=VMEM_LIMIT)


def _mod_row(i, tm):
    n_ctx_tiles = N_CTX // tm
    per_seq = DEC_SEQ // tm
    return jnp.where(i < n_ctx_tiles, 0, 1 + (i - n_ctx_tiles) // per_seq)


def _silu(x):
    return x * jax.nn.sigmoid(x)


def _gelu_tanh(x):
    c = math.sqrt(2.0 / math.pi)
    return 0.5 * x * (1.0 + jnp.tanh(c * (x + 0.044715 * (x * x * x))))


def _rms(x):
    return x * lax.rsqrt(jnp.mean(x * x, axis=-1, keepdims=True) + EPS)


def _bdot(a, b):
    return jnp.dot(a.astype(BF16), b.astype(BF16), preferred_element_type=F32)


def _mod_kernel(c_ref, w_ref, b_ref, o_ref):
    o_ref[...] = _bdot(_silu(c_ref[...]), w_ref[...]) + b_ref[...]


def _adaln_mods(cvec, ada_w, ada_b):
    n_blk = 6
    return pl.pallas_call(
        _mod_kernel,
        grid=(DEPTH, n_blk),
        in_specs=[pl.BlockSpec((MOD_ROWS, D_MODEL), lambda l, j: (0, 0)),
                  pl.BlockSpec((None, D_MODEL, D_MODEL), lambda l, j: (l, 0, j)),
                  pl.BlockSpec((None, 1, D_MODEL), lambda l, j: (l, 0, j))],
        out_specs=pl.BlockSpec((None, MOD_ROWS, D_MODEL), lambda l, j: (l, 0, j)),
        out_shape=jax.ShapeDtypeStruct((DEPTH, MOD_ROWS, 6 * D_MODEL), F32),
        compiler_params=_params(2),
        name="adaln_mod",
    )(cvec, ada_w, ada_b.reshape(DEPTH, 1, 6 * D_MODEL))


_C_Z = 0
_C_XBC = _C_Z + SSD_INNER
_C_SGU = _C_XBC + SSD_CONV_DIM
_C_S5 = _C_SGU + 2 * SGU_DIM
_C_DT = _C_S5 + S5_DIM
_C_END = _C_DT + LANES


def _inproj_kernel(x_ref, sh_ref, sc_ref, g_ref, w_ref, z_ref, xbc_ref, sgu_ref, s5_ref, dt_ref, *, tm):
    r = _mod_row(pl.program_id(0), tm)
    shift = sh_ref[pl.ds(r, 1), :]
    scale = sc_ref[pl.ds(r, 1), :]
    h = (_rms(x_ref[...]) * g_ref[...]) * (1.0 + scale) + shift
    hb = h.astype(BF16)
    z_ref[...] = jnp.dot(hb, w_ref[:, _C_Z:_C_XBC], preferred_element_type=F32)
    xbc_ref[...] = jnp.dot(hb, w_ref[:, _C_XBC:_C_SGU], preferred_element_type=F32)
    sgu_ref[...] = jnp.dot(hb, w_ref[:, _C_SGU:_C_S5], preferred_element_type=F32)
    s5_ref[...] = jnp.dot(hb, w_ref[:, _C_S5:_C_DT], preferred_element_type=F32)
    dt_ref[...] = jnp.dot(hb, w_ref[:, _C_DT:_C_END], preferred_element_type=F32)


def _in_proj(x, mods_l, norm_g, w_perm):
    tm = TOK_TILE
    tok = lambda w: pl.BlockSpec((tm, w), lambda i: (i, 0))
    mod = lambda k: pl.BlockSpec((MOD_ROWS, D_MODEL), lambda i, k=k: (0, k))
    widths = (SSD_INNER, SSD_CONV_DIM, 2 * SGU_DIM, S5_DIM, LANES)
    return pl.pallas_call(
        functools.partial(_inproj_kernel, tm=tm),
        grid=(N_TOK // tm,),
        in_specs=[tok(D_MODEL), mod(0), mod(1),
                  pl.BlockSpec((1, D_MODEL), lambda i: (0, 0)),
                  pl.BlockSpec((D_MODEL, _C_END), lambda i: (0, 0), pipeline_mode=pl.Buffered(1))],
        out_specs=[tok(w) for w in widths],
        out_shape=[jax.ShapeDtypeStruct((N_TOK, w), F32) for w in widths],
        compiler_params=_params(),
        name="in_proj",
    )(x, mods_l, mods_l, norm_g.reshape(1, D_MODEL), w_perm)


def _ssd_kernel(z_ref, xbc_ref, dt_ref, cw_ref, cb_ref, dtb_ref, alog_ref, dvec_ref, ng_ref, h0_ref,
                y_ref, hout_ref, xpad, xc, acol, dtsp, yacc, hst, *, L):
    Q = CHUNK
    nc = L // Q
    halo = SUBLANES
    pad = (SSD_CONV - 1) // 2

    xpad[0:halo, :] = jnp.zeros((halo, SSD_CONV_DIM), F32)
    xpad[halo + L:2 * halo + L, :] = jnp.zeros((halo, SSD_CONV_DIM), F32)
    for c in range(nc):
        xpad[halo + c * Q:halo + (c + 1) * Q, :] = xbc_ref[c * Q:(c + 1) * Q, :]
    for c in range(nc):
        acc = cb_ref[...] + cw_ref[0:1, :] * xpad[halo - pad + c * Q:halo - pad + (c + 1) * Q, :]
        for k in range(1, SSD_CONV):
            o = halo - pad + k + c * Q
            acc = acc + cw_ref[k:k + 1, :] * xpad[o:o + Q, :]
        xc[c * Q:(c + 1) * Q, :] = _silu(acc)

    raw = dt_ref[...] + dtb_ref[...]
    dt = jnp.maximum(raw, 0.0) + jnp.log(1.0 + jnp.exp(-jnp.abs(raw)))
    dtsp[...] = dt
    a_neg = -jnp.exp(alog_ref[...])
    row = lax.broadcasted_iota(jnp.int32, (Q, Q), 0)
    col = lax.broadcasted_iota(jnp.int32, (Q, Q), 1)
    lower = row >= col
    upper = col >= row
    tri_l = lower.astype(F32)
    tri_u = upper.astype(F32)
    fwd_lane = lax.broadcasted_iota(jnp.int32, (Q, LANES), 1) < SSD_HEADS
    for c in range(nc):
        dta = dtsp[c * Q:(c + 1) * Q, :] * a_neg
        pre = jnp.dot(tri_l, dta, precision=lax.Precision.HIGHEST, preferred_element_type=F32)
        suf = jnp.dot(tri_u, dta, precision=lax.Precision.HIGHEST, preferred_element_type=F32)
        acol[c * Q:(c + 1) * Q, :] = jnp.where(fwd_lane, pre, suf)

    for d in range(N_DIR):
        for p in range(N_PAIR):
            hst[d, p] = h0_ref[d, p * PAIR:(p + 1) * PAIR, :]

    lo_lane = lax.broadcasted_iota(jnp.int32, (Q, PAIR), 1) < SSD_HEAD_DIM
    lo_row = lax.broadcasted_iota(jnp.int32, (PAIR, SSD_STATE), 0) < SSD_HEAD_DIM

    def chunk(c, d):
        r0 = pl.multiple_of(c * Q, Q)
        rows = pl.ds(r0, Q)
        a = acol[rows, :]
        a_t = a.T
        dtc = dtsp[rows, :]
        a_end = a[Q - 1:Q, :] if d == 0 else a[0:1, :]
        dec = jnp.exp(a_end)
        mask = lower if d == 0 else upper
        outs = []
        for g in range(SSD_GROUPS):
            bmb = xc[rows, SSD_INNER + g * SSD_STATE:SSD_INNER + (g + 1) * SSD_STATE].astype(BF16)
            c0 = SSD_INNER + (SSD_GROUPS + g) * SSD_STATE
            cmb = xc[rows, c0:c0 + SSD_STATE].astype(BF16)
            cb = lax.dot_general(cmb, bmb, _NT, preferred_element_type=F32)
            for pr in range(2):
                p = g * 2 + pr
                j0 = d * SSD_HEADS + 2 * p
                j1 = j0 + 1
                xs = xc[rows, p * PAIR:(p + 1) * PAIR]
                ac0 = a[:, j0:j0 + 1]
                ac1 = a[:, j1:j1 + 1]
                l0 = jnp.exp(jnp.where(mask, ac0 - a_t[j0:j0 + 1, :], -jnp.inf))
                l1 = jnp.exp(jnp.where(mask, ac1 - a_t[j1:j1 + 1, :], -jnp.inf))
                m = jnp.concatenate([(cb * l0).astype(BF16), (cb * l1).astype(BF16)], axis=1)
                xdt = xs * jnp.where(lo_lane, dtc[:, j0:j0 + 1], dtc[:, j1:j1 + 1])
                rhs = jnp.concatenate([jnp.where(lo_lane, xdt, 0.0).astype(BF16),
                                       jnp.where(lo_lane, 0.0, xdt).astype(BF16)], axis=0)
                y_diag = jnp.dot(m, rhs, preferred_element_type=F32)
                hp = hst[d, p]
                y_off = lax.dot_general(cmb, hp.astype(BF16), _NT, preferred_element_type=F32)
                e_a = jnp.where(lo_lane, jnp.exp(ac0), jnp.exp(ac1))
                y = y_diag + e_a * y_off
                w_end = jnp.where(lo_lane, jnp.exp(a_end[:, j0:j0 + 1] - ac0),
                                  jnp.exp(a_end[:, j1:j1 + 1] - ac1))
                xw = (xdt * w_end).astype(BF16)
                s_new = lax.dot_general(xw, bmb, _TN, preferred_element_type=F32)
                decp = jnp.where(lo_row, dec[:, j0:j0 + 1], dec[:, j1:j1 + 1])
                hst[d, p] = decp * hp + s_new
                if d == 0:
                    yacc[rows, p * PAIR:(p + 1) * PAIR] = y + dvec_ref[:, p * PAIR:(p + 1) * PAIR] * xs
                else:
                    outs.append(y)
        if d == 1:
            y = yacc[rows, :] + jnp.concatenate(outs, axis=1)
            y = y * _silu(z_ref[rows, :])
            y_ref[rows, :] = _rms(y) * ng_ref[...]

    def fwd_body(t, carry):
        chunk(t, 0)
        return carry

    def bwd_body(t, carry):
        chunk(nc - 1 - t, 1)
        return carry

    lax.fori_loop(0, nc, fwd_body, 0)
    lax.fori_loop(0, nc, bwd_body, 0)
    for d in range(N_DIR):
        for p in range(N_PAIR):
            hout_ref[d, p * PAIR:(p + 1) * PAIR, :] = hst[d, p]


def _ssd_mixer(z, xbc, dtr, h0, prm, *, L, blk0):
    nb = h0.shape[0]
    hp = SSD_HEADS * SSD_HEAD_DIM
    seq = lambda w: pl.BlockSpec((L, w), lambda b: (blk0 + b, 0))
    full = lambda s: pl.BlockSpec(s, lambda b: (0,) * len(s))
    st = pl.BlockSpec((None, N_DIR, hp, SSD_STATE), lambda b: (b, 0, 0, 0))
    return pl.pallas_call(
        functools.partial(_ssd_kernel, L=L),
        grid=(nb,),
        in_specs=[seq(SSD_INNER), seq(SSD_CONV_DIM), seq(LANES),
                  full((SSD_CONV, SSD_CONV_DIM)), full((1, SSD_CONV_DIM)), full((1, LANES)),
                  full((1, LANES)), full((1, SSD_INNER)), full((1, SSD_INNER)), st],
        out_specs=[pl.BlockSpec((L, SSD_INNER), lambda b: (b, 0)), st],
        out_shape=[jax.ShapeDtypeStruct((nb * L, SSD_INNER), F32),
                   jax.ShapeDtypeStruct((nb, N_DIR, hp, SSD_STATE), F32)],
        scratch_shapes=[pltpu.VMEM((L + 2 * SUBLANES, SSD_CONV_DIM), F32),
                        pltpu.VMEM((L, SSD_CONV_DIM), F32),
                        pltpu.VMEM((L, LANES), F32),
                        pltpu.VMEM((L, LANES), F32),
                        pltpu.VMEM((L, SSD_INNER), F32),
                        pltpu.VMEM((N_DIR, N_PAIR, PAIR, SSD_STATE), F32)],
        compiler_params=_params(),
        name=f"ssd_mixer_L{L}",
    )(z, xbc, dtr, prm["conv_w"], prm["conv_b"], prm["dt_bias"], prm["a_log"], prm["d_vec"],
      prm["norm_g"], h0)


def _sgu_kernel(s_ref, g_ref, w_ref, b_ref, o_ref, *, tm):
    uv = _gelu_tanh(s_ref[...])
    u = uv[:, :SGU_DIM]
    v = _rms(uv[:, SGU_DIM:]) * g_ref[...]
    lo_lane = lax.broadcasted_iota(jnp.int32, (CHUNK, LANES), 1) < (LANES // 2)
    for c in range(tm // CHUNK):
        rows = slice(c * CHUNK, (c + 1) * CHUNK)
        mix = []
        for pr in range(SGU_HEADS // 2):
            vp = v[rows, pr * LANES:(pr + 1) * LANES]
            rhs = jnp.concatenate([jnp.where(lo_lane, vp, 0.0).astype(BF16),
                                   jnp.where(lo_lane, 0.0, vp).astype(BF16)], axis=0)
            mix.append(jnp.dot(w_ref[pr], rhs, preferred_element_type=F32))
        o_ref[rows, :] = u[rows, :] * (jnp.concatenate(mix, axis=1) + b_ref[...])


def _sgu_mixer(sgu, norm_g, w_pair, b_full):
    tm = TOK_TILE
    return pl.pallas_call(
        functools.partial(_sgu_kernel, tm=tm),
        grid=(N_TOK // tm,),
        in_specs=[pl.BlockSpec((tm, 2 * SGU_DIM), lambda i: (i, 0)),
                  pl.BlockSpec((1, SGU_DIM), lambda i: (0, 0)),
                  pl.BlockSpec((SGU_HEADS // 2, CHUNK, 2 * CHUNK), lambda i: (0, 0, 0)),
                  pl.BlockSpec((CHUNK, SGU_DIM), lambda i: (0, 0))],
        out_specs=pl.BlockSpec((tm, SGU_DIM), lambda i: (i, 0)),
        out_shape=jax.ShapeDtypeStruct((N_TOK, SGU_DIM), F32),
        compiler_params=_params(),
        name="sgu_mixer",
    )(sgu, norm_g.reshape(1, SGU_DIM), w_pair, b_full)


_LOG2_CH = S5_GROUP_CH.bit_length() - 1
_LOG2_ST = S5_STATE.bit_length() - 1


def _cmul(ar, ai, br, bi):
    return ar * br - ai * bi, ar * bi + ai * br


def _s5prep_kernel(lre_ref, lim_ref, ldt_ref, btr_ref, bti_ref, ctr_ref, cti_ref,
                   bblk_ref, cblk_ref, pw_ref):
    brow = lax.broadcasted_iota(jnp.int32, (S5_DIM, S5_LANES), 0) >> _LOG2_CH
    bcol = lax.broadcasted_iota(jnp.int32, (S5_DIM, S5_LANES), 1) >> _LOG2_ST
    bmask = brow == bcol
    p1 = lax.broadcasted_iota(jnp.int32, (S5_PW_ROWS, S5_LANES), 0) + 1
    for d in range(N_DIR):
        lre = lre_ref[d]
        lim = lim_ref[d]
        step = jnp.exp(ldt_ref[d])
        mag = jnp.exp(lre * step)
        lbr = mag * jnp.cos(lim * step)
        lbi = mag * jnp.sin(lim * step)
        den = lre * lre + lim * lim
        nr = lbr - 1.0
        cr = (nr * lre + lbi * lim) / den
        ci = (lbi * lre - nr * lim) / den
        br, bi = _cmul(cr, ci, btr_ref[...], bti_ref[...])
        bblk_ref[d, :, 0:S5_LANES] = jnp.where(bmask, br, 0.0).astype(BF16)
        bblk_ref[d, :, S5_LANES:2 * S5_LANES] = jnp.where(bmask, bi, 0.0).astype(BF16)
        rr = jnp.ones((S5_PW_ROWS, S5_LANES), F32)
        ri = jnp.zeros((S5_PW_ROWS, S5_LANES), F32)
        sr, si = lbr, lbi
        for k in range(S5_PW_ROWS.bit_length()):
            bit = ((p1 >> k) & 1) == 1
            tr, ti = _cmul(rr, ri, sr, si)
            rr = jnp.where(bit, tr, rr)
            ri = jnp.where(bit, ti, ri)
            sr, si = _cmul(sr, si, sr, si)
        pw_ref[d, 0] = rr
        pw_ref[d, 1] = ri
    crow = lax.broadcasted_iota(jnp.int32, (S5_LANES, S5_DIM), 0) >> _LOG2_ST
    ccol = lax.broadcasted_iota(jnp.int32, (S5_LANES, S5_DIM), 1) >> _LOG2_CH
    cmask = crow == ccol
    cblk_ref[0:S5_LANES, :] = jnp.where(cmask, ctr_ref[...], 0.0).astype(BF16)
    cblk_ref[S5_LANES:2 * S5_LANES, :] = jnp.where(cmask, -cti_ref[...], 0.0).astype(BF16)


def _s5_prep(lam_re, lam_im, log_dt, b_re, b_im, c_re, c_im):
    row = lambda t: t.reshape(DEPTH, N_DIR, 1, S5_LANES)
    ldt = jnp.repeat(log_dt, S5_STATE, axis=-1)
    bt = lambda t: jnp.tile(t.transpose(0, 3, 1, 2).reshape(DEPTH, S5_GROUP_CH, S5_LANES),
                            (1, S5_GROUPS, 1))
    ct = lambda t: jnp.tile(t.transpose(0, 1, 3, 2).reshape(DEPTH, S5_LANES, S5_GROUP_CH),
                            (1, 1, S5_GROUPS))
    vec = pl.BlockSpec((None, N_DIR, 1, S5_LANES), lambda l: (l, 0, 0, 0))
    bsp = pl.BlockSpec((None, S5_DIM, S5_LANES), lambda l: (l, 0, 0))
    csp = pl.BlockSpec((None, S5_LANES, S5_DIM), lambda l: (l, 0, 0))
    return pl.pallas_call(
        _s5prep_kernel,
        grid=(DEPTH,),
        in_specs=[vec, vec, vec, bsp, bsp, csp, csp],
        out_specs=[pl.BlockSpec((None, N_DIR, S5_DIM, 2 * S5_LANES), lambda l: (l, 0, 0, 0)),
                   pl.BlockSpec((None, 2 * S5_LANES, S5_DIM), lambda l: (l, 0, 0)),
                   pl.BlockSpec((None, N_DIR, 2, S5_PW_ROWS, S5_LANES), lambda l: (l, 0, 0, 0, 0))],
        out_shape=[jax.ShapeDtypeStruct((DEPTH, N_DIR, S5_DIM, 2 * S5_LANES), BF16),
                   jax.ShapeDtypeStruct((DEPTH, 2 * S5_LANES, S5_DIM), BF16),
                   jax.ShapeDtypeStruct((DEPTH, N_DIR, 2, S5_PW_ROWS, S5_LANES), F32)],
        compiler_params=_params(),
        name="s5_prep",
    )(row(lam_re), row(lam_im), row(ldt), bt(b_re), bt(b_im), ct(c_re), ct(c_im))


def _s5_kernel(u_ref, bblk_ref, cblk_ref, pw_ref, h0r_ref, h0i_ref, dvec_ref, gw_ref, gb_ref,
               y_ref, fr_ref, fi_ref, hf, hb, cin, *, L):
    P = L // S5_SEG
    RE = slice(0, S5_LANES)
    IM = slice(S5_LANES, 2 * S5_LANES)
    zero = jnp.zeros((S5_SEG, S5_LANES), F32)

    def local_scan(d, buf):
        for r0 in range(0, L, ROW_SLAB):
            buf[r0:r0 + ROW_SLAB, :] = jnp.dot(u_ref[r0:r0 + ROW_SLAB, :].astype(BF16), bblk_ref[d],
                                               preferred_element_type=F32)
        lr = pw_ref[d, 0, 0:1, :]
        li = pw_ref[d, 1, 0:1, :]

        def body(t, carry):
            hr, hi = carry
            p = t if d == 0 else P - 1 - t
            rows = pl.ds(pl.multiple_of(p * S5_SEG, S5_SEG), S5_SEG)
            nr = lr * hr - li * hi + buf[rows, RE]
            ni = lr * hi + li * hr + buf[rows, IM]
            buf[rows, RE] = nr
            buf[rows, IM] = ni
            return nr, ni

        return lax.fori_loop(0, P, body, (zero, zero))

    def carries(d, er, ei):
        lpr = pw_ref[d, 0, P - 1:P, :]
        lpi = pw_ref[d, 1, P - 1:P, :]
        pr = h0r_ref[d:d + 1, :]
        pi = h0i_ref[d:d + 1, :]
        order = range(S5_SEG) if d == 0 else range(S5_SEG - 1, -1, -1)
        for s in order:
            cin[d, 0, s:s + 1, :] = pr
            cin[d, 1, s:s + 1, :] = pi
            tr, ti = _cmul(lpr, lpi, pr, pi)
            pr = er[s:s + 1, :] + tr
            pi = ei[s:s + 1, :] + ti
        fr_ref[d:d + 1, :] = pr
        fi_ref[d:d + 1, :] = pi

    efr, efi = local_scan(0, hf)
    ebr, ebi = local_scan(1, hb)
    carries(0, efr, efi)
    carries(1, ebr, ebi)
    cfr = cin[0, 0]
    cfi = cin[0, 1]
    cbr = cin[1, 0]
    cbi = cin[1, 1]

    def fix(p, carry):
        rows = pl.ds(pl.multiple_of(p * S5_SEG, S5_SEG), S5_SEG)
        tfr, tfi = _cmul(pw_ref[0, 0, pl.ds(p, 1), :], pw_ref[0, 1, pl.ds(p, 1), :], cfr, cfi)
        q = P - 1 - p
        tbr, tbi = _cmul(pw_ref[1, 0, pl.ds(q, 1), :], pw_ref[1, 1, pl.ds(q, 1), :], cbr, cbi)
        hf[rows, RE] = hf[rows, RE] + hb[rows, RE] + tfr + tbr
        hf[rows, IM] = hf[rows, IM] + hb[rows, IM] + tfi + tbi
        return carry

    lax.fori_loop(0, P, fix, 0)
    gwb = gw_ref[...].astype(BF16)
    for r0 in range(0, L, ROW_SLAB):
        rows = slice(r0, r0 + ROW_SLAB)
        y = jnp.dot(hf[rows, :].astype(BF16), cblk_ref[...], preferred_element_type=F32)
        y = _gelu_tanh(y + dvec_ref[...] * u_ref[rows, :])
        gate = jnp.dot(y.astype(BF16), gwb, preferred_element_type=F32) + gb_ref[...]
        y_ref[rows, :] = y * jax.nn.sigmoid(gate)


def _s5_mixer(u_perm, h0r, h0i, bblk, cblk, pw, prm, *, L):
    nb = u_perm.shape[0] // L
    full = lambda s: pl.BlockSpec(s, lambda b: (0,) * len(s))
    st = pl.BlockSpec((None, N_DIR, S5_LANES), lambda b: (b, 0, 0))
    seq = pl.BlockSpec((L, S5_DIM), lambda b: (b, 0))
    return pl.pallas_call(
        functools.partial(_s5_kernel, L=L),
        grid=(nb,),
        in_specs=[seq, full((N_DIR, S5_DIM, 2 * S5_LANES)), full((2 * S5_LANES, S5_DIM)),
                  full((N_DIR, 2, S5_PW_ROWS, S5_LANES)), st, st,
                  full((1, S5_DIM)), full((S5_DIM, S5_DIM)), full((1, S5_DIM))],
        out_specs=[seq, st, st],
        out_shape=[jax.ShapeDtypeStruct((nb * L, S5_DIM), F32),
                   jax.ShapeDtypeStruct((nb, N_DIR, S5_LANES), F32),
                   jax.ShapeDtypeStruct((nb, N_DIR, S5_LANES), F32)],
        scratch_shapes=[pltpu.VMEM((L, 2 * S5_LANES), F32),
                        pltpu.VMEM((L, 2 * S5_LANES), F32),
                        pltpu.VMEM((N_DIR, 2, S5_SEG, S5_LANES), F32)],
        compiler_params=_params(),
        name=f"s5_mixer_L{L}",
    )(u_perm, bblk, cblk, pw, h0r, h0i, prm["d_vec"], prm["glu_w"], prm["glu_b"])


FF_SLAB = 1024


def _outffn_kernel(x_ref, ys_ref, yg_ref, y5_ref, g1_ref, sh_ref, sc_ref, g2_ref, ng_ref, fg_ref,
                   wo_ref, w1_ref, w2_ref, o_ref, *, tm, final):
    r = _mod_row(pl.program_id(0), tm)
    row = lambda ref: ref[pl.ds(r, 1), :]
    mixed = jnp.concatenate([ys_ref[...].astype(BF16), yg_ref[...].astype(BF16),
                             y5_ref[...].astype(BF16)], axis=1)
    x1 = x_ref[...] + row(g1_ref) * jnp.dot(mixed, wo_ref[...], preferred_element_type=F32)
    h = ((_rms(x1) * ng_ref[...]) * (1.0 + row(sc_ref)) + row(sh_ref)).astype(BF16)
    acc = jnp.zeros((tm, D_MODEL), F32)
    for j in range(D_FF // FF_SLAB):
        a = jnp.dot(h, w1_ref[:, j * FF_SLAB:(j + 1) * FF_SLAB], preferred_element_type=F32)
        a = jnp.maximum(a, 0.0)
        acc = acc + jnp.dot((a * a).astype(BF16), w2_ref[j * FF_SLAB:(j + 1) * FF_SLAB, :],
                            preferred_element_type=F32)
    x2 = x1 + row(g2_ref) * acc
    if final:
        x2 = _rms(x2) * fg_ref[...]
    o_ref[...] = x2


def _out_ffn(x, y_ssd, y_sgu, y_s5, mods_l, norm_g, final_g, w_out, w1, w2, *, final):
    tm = TOK_TILE
    tok = lambda w: pl.BlockSpec((tm, w), lambda i: (i, 0))
    mod = lambda k: pl.BlockSpec((MOD_ROWS, D_MODEL), lambda i, k=k: (0, k))
    vec = pl.BlockSpec((1, D_MODEL), lambda i: (0, 0))
    res = lambda s: pl.BlockSpec(s, lambda i: (0, 0), pipeline_mode=pl.Buffered(1))
    return pl.pallas_call(
        functools.partial(_outffn_kernel, tm=tm, final=final),
        grid=(N_TOK // tm,),
        in_specs=[tok(D_MODEL), tok(SSD_INNER), tok(SGU_DIM), tok(S5_DIM),
                  mod(2), mod(3), mod(4), mod(5), vec, vec,
                  res((D_MODEL, D_MODEL)), res((D_MODEL, D_FF)), res((D_FF, D_MODEL))],
        out_specs=tok(D_MODEL),
        out_shape=jax.ShapeDtypeStruct((N_TOK, D_MODEL), F32),
        compiler_params=_params(),
        name="out_ffn_final" if final else "out_ffn",
    )(x, y_ssd, y_sgu, y_s5, mods_l, mods_l, mods_l, mods_l, norm_g.reshape(1, D_MODEL),
      final_g.reshape(1, D_MODEL), w_out, w1, w2)


def _s5_perm_ctx(u):
    p = SEQ // S5_SEG
    return u.reshape(BATCH, S5_SEG, p, S5_DIM).transpose(0, 2, 1, 3).reshape(N_CTX, S5_DIM)


def _s5_unperm_ctx(y):
    p = SEQ // S5_SEG
    return y.reshape(BATCH, p, S5_SEG, S5_DIM).transpose(0, 2, 1, 3).reshape(N_CTX, S5_DIM)


def _s5_perm_lat(u):
    rows = DEC_SEQ // GRID_W
    wseg = GRID_W // S5_SEG
    t = u.reshape(DEC_BATCH, rows, S5_SEG, wseg, S5_DIM).transpose(0, 3, 1, 2, 4)
    return t.reshape(N_LAT, S5_DIM)


def _s5_unperm_lat(y):
    rows = DEC_SEQ // GRID_W
    wseg = GRID_W // S5_SEG
    t = y.reshape(DEC_BATCH, wseg, rows, S5_SEG, S5_DIM).transpose(0, 2, 3, 1, 4)
    return t.reshape(N_LAT, S5_DIM)


def kernel(x_prompt, x_sample, state_ssd, state_s5_re, state_s5_im, c, c_ctx, ada_w, ada_b, norm1_g,
           norm2_g, w_in, ssd_conv_w, ssd_conv_b, ssd_dt_bias, ssd_a_log, ssd_d, ssd_norm_g,
           sgu_norm_g, sgu_w, sgu_b, s5_lambda_re, s5_lambda_im, s5_log_dt, s5_b_re, s5_b_im,
           s5_c_re, s5_c_im, s5_d, s5_glu_w, s5_glu_b, w_out, ffn_w1, ffn_w2, final_norm_g):
    cvec = jnp.concatenate([c_ctx[None, :], c,
                            jnp.zeros((MOD_ROWS - 1 - DEC_BATCH, D_MODEL), F32)], axis=0)
    mods = _adaln_mods(cvec, ada_w, ada_b)
    bblk, cblk, pw = _s5_prep(s5_lambda_re, s5_lambda_im, s5_log_dt, s5_b_re, s5_b_im,
                              s5_c_re, s5_c_im)

    w_perm = jnp.concatenate(
        [w_in[:, :, 0:OFF_DT], w_in[:, :, OFF_SGU:IN_DIM], w_in[:, :, OFF_DT:OFF_SGU],
         jnp.zeros((DEPTH, D_MODEL, LANES - N_DIR * SSD_HEADS), F32)], axis=2).astype(BF16)
    w_out_b = w_out.astype(BF16)
    w1_b = ffn_w1.astype(BF16)
    w2_b = ffn_w2.astype(BF16)
    lane_pad = lambda t: jnp.pad(t.reshape(DEPTH, 1, -1), ((0, 0), (0, 0), (0, LANES - t[0].size)))
    dt_bias = lane_pad(ssd_dt_bias)
    a_log = lane_pad(ssd_a_log)
    ssd_d_vec = jnp.repeat(ssd_d, SSD_HEAD_DIM, axis=-1).reshape(DEPTH, 1, SSD_INNER)
    sgu_w_pair = sgu_w.reshape(DEPTH, SGU_HEADS // 2, 2, CHUNK, CHUNK).transpose(0, 1, 3, 2, 4)
    sgu_w_pair = sgu_w_pair.reshape(DEPTH, SGU_HEADS // 2, CHUNK, 2 * CHUNK).astype(BF16)
    sgu_b_full = jnp.repeat(sgu_b.transpose(0, 2, 1), SGU_DIM // SGU_HEADS, axis=2)

    hp = SSD_HEADS * SSD_HEAD_DIM
    zero_ssd = jnp.zeros((BATCH, N_DIR, hp, SSD_STATE), F32)
    zero_s5 = jnp.zeros((BATCH, N_DIR, S5_LANES), F32)
    lat_ssd = state_ssd.reshape(DEC_BATCH, DEPTH, N_DIR, hp, SSD_STATE)
    lat_s5r = state_s5_re.reshape(DEC_BATCH, DEPTH, N_DIR, S5_LANES)
    lat_s5i = state_s5_im.reshape(DEC_BATCH, DEPTH, N_DIR, S5_LANES)

    x = jnp.concatenate([x_prompt.reshape(N_CTX, D_MODEL), x_sample.reshape(N_LAT, D_MODEL)], axis=0)
    new_ssd, new_s5r, new_s5i = [], [], []
    for l in range(DEPTH):
        z, xbc, sgu, s5u, dtr = _in_proj(x, mods[l], norm1_g[l], w_perm[l])
        ssd_prm = dict(conv_w=ssd_conv_w[l], conv_b=ssd_conv_b[l].reshape(1, -1), dt_bias=dt_bias[l],
                       a_log=a_log[l], d_vec=ssd_d_vec[l], norm_g=ssd_norm_g[l].reshape(1, -1))
        y_ssd_c, st_c = _ssd_mixer(z, xbc, dtr, zero_ssd, ssd_prm, L=SEQ, blk0=0)
        y_ssd_l, _ = _ssd_mixer(z, xbc, dtr, lat_ssd[:, l], ssd_prm, L=DEC_SEQ, blk0=N_CTX // DEC_SEQ)
        y_ssd = jnp.concatenate([y_ssd_c, y_ssd_l], axis=0)
        new_ssd.append(st_c)

        y_sgu = _sgu_mixer(sgu, sgu_norm_g[l], sgu_w_pair[l], sgu_b_full[l])

        s5_prm = dict(d_vec=s5_d[l].reshape(1, -1), glu_w=s5_glu_w[l], glu_b=s5_glu_b[l].reshape(1, -1))
        y5_c, fr, fi = _s5_mixer(_s5_perm_ctx(s5u[:N_CTX]), zero_s5, zero_s5, bblk[l], cblk[l], pw[l],
                                 s5_prm, L=SEQ)
        y5_l, _, _ = _s5_mixer(_s5_perm_lat(s5u[N_CTX:]), lat_s5r[:, l], lat_s5i[:, l], bblk[l],
                               cblk[l], pw[l], s5_prm, L=DEC_SEQ)
        y_s5 = jnp.concatenate([_s5_unperm_ctx(y5_c), _s5_unperm_lat(y5_l)], axis=0)
        new_s5r.append(fr)
        new_s5i.append(fi)

        x = _out_ffn(x, y_ssd, y_sgu, y_s5, mods[l], norm2_g[l], final_norm_g, w_out_b[l], w1_b[l],
                     w2_b[l], final=(l == DEPTH - 1))

    y_prompt = x[:N_CTX].reshape(BATCH, SEQ, D_MODEL)
    y_sample = x[N_CTX:].reshape(DEC_BATCH, DEC_SEQ, D_MODEL)
    new_state_ssd = jnp.stack(new_ssd, axis=1).reshape(BATCH, DEPTH, N_DIR, SSD_HEADS, SSD_HEAD_DIM,
                                                       SSD_STATE)
    s5_shape = (BATCH, DEPTH, N_DIR, S5_GROUPS, S5_STATE)
    new_state_s5_re = jnp.stack(new_s5r, axis=1).reshape(s5_shape)
    new_state_s5_im = jnp.stack(new_s5i, axis=1).reshape(s5_shape)
    return (y_prompt, y_sample, new_state_ssd, new_state_s5_re, new_state_s5_im)
```

```python
import functools
import math

import jax
import jax.numpy as jnp
from jax import lax
from jax.experimental import pallas as pl
from jax.experimental.pallas import tpu as pltpu

F32 = jnp.float32
BF16 = jnp.bfloat16

D_MODEL = 1024
BATCH = 16
SEQ = 256
DEPTH = 2
DEC_BATCH = 2
DEC_SEQ = 1024
GRID_W = 64
CHUNK = 128
N_DIR = 2
EPS = 1e-6
SSD_INNER = 512
SSD_HEAD_DIM = 64
SSD_HEADS = 8
SSD_GROUPS = 2
SSD_STATE = 128
SSD_CONV = 5
SSD_CONV_DIM = SSD_INNER + 2 * SSD_GROUPS * SSD_STATE
SGU_DIM = 256
SGU_HEADS = 4
S5_DIM = 256
S5_GROUP_CH = 16
S5_GROUPS = 16
S5_STATE = 64
D_FF = 4 * D_MODEL
OFF_XBC = SSD_INNER
OFF_DT = OFF_XBC + SSD_CONV_DIM
OFF_SGU = OFF_DT + N_DIR * SSD_HEADS
OFF_S5 = OFF_SGU + 2 * SGU_DIM
IN_DIM = OFF_S5 + S5_DIM

N_CTX = BATCH * SEQ
N_LAT = DEC_BATCH * DEC_SEQ
N_TOK = N_CTX + N_LAT
LANES = 128
SUBLANES = 8
MOD_ROWS = SUBLANES
SSD_HP = SSD_HEADS * SSD_HEAD_DIM
S5_LANES = S5_GROUPS * S5_STATE
S5_SEG = SUBLANES
S5_PW_ROWS = DEC_SEQ // S5_SEG
PAIR = 2 * SSD_HEAD_DIM
N_PAIR = SSD_HEADS // 2
TOK_TILE = 256
N_CTX_TILES = N_CTX // TOK_TILE
ROW_SLAB = 256
CONV_LANES = 256
FF_SLAB = 1024
VMEM_LIMIT = 56 * 1024 * 1024

_NT = (((1,), (1,)), ((), ()))
_TN = (((0,), (0,)), ((), ()))


def _params(n_axes=1):
    return pltpu.CompilerParams(dimension_semantics=("arbitrary",) * n_axes,
                                vmem_limit_bytes=VMEM_LIMIT)


def _layer_spec(l, shape, **kw):
    return pl.BlockSpec((None,) + tuple(shape), lambda *_: (l,) + (0,) * len(shape), **kw)


def _mod_spec(l, k):
    return pl.BlockSpec((None, MOD_ROWS, D_MODEL), lambda *_: (l, 0, k))


def _any_spec():
    return pl.BlockSpec(memory_space=pl.ANY)


def _mod_row(i):
    per_seq = DEC_SEQ // TOK_TILE
    return jnp.where(i < N_CTX_TILES, 0, 1 + (i - N_CTX_TILES) // per_seq)


def _split_tok_specs(width, b_rows_offset):
    a = pl.BlockSpec((TOK_TILE, width), lambda i: (jnp.minimum(i, N_CTX_TILES - 1), 0))
    b = pl.BlockSpec((TOK_TILE, width),
                     lambda i: (jnp.maximum(i, N_CTX_TILES) - N_CTX_TILES + b_rows_offset, 0))
    return a, b


def _silu(x):
    return x * jax.nn.sigmoid(x)


def _gelu_tanh(x):
    c = math.sqrt(2.0 / math.pi)
    return 0.5 * x * (1.0 + jnp.tanh(c * (x + 0.044715 * (x * x * x))))


def _rms(x):
    return x * lax.rsqrt(jnp.mean(x * x, axis=-1, keepdims=True) + EPS)


def _bdot(a, b):
    return jnp.dot(a.astype(BF16), b.astype(BF16), preferred_element_type=F32)


def _mod_kernel(c_ref, w_ref, b_ref, o_ref):
    o_ref[...] = _bdot(_silu(c_ref[...]), w_ref[...]) + b_ref[...]


def _adaln_mods(cvec, ada_w, ada_b):
    n_blk = 6
    return pl.pallas_call(
        _mod_kernel,
        grid=(DEPTH, n_blk),
        in_specs=[pl.BlockSpec((MOD_ROWS, D_MODEL), lambda l, j: (0, 0)),
                  pl.BlockSpec((None, D_MODEL, D_MODEL), lambda l, j: (l, 0, j)),
                  pl.BlockSpec((None, 1, D_MODEL), lambda l, j: (l, 0, j))],
        out_specs=pl.BlockSpec((None, MOD_ROWS, D_MODEL), lambda l, j: (l, 0, j)),
        out_shape=jax.ShapeDtypeStruct((DEPTH, MOD_ROWS, 6 * D_MODEL), F32),
        compiler_params=_params(2),
        name="adaln_mod",
    )(cvec, ada_w, ada_b.reshape(DEPTH, 1, 6 * D_MODEL))


_C_Z = 0
_C_XBC = _C_Z + SSD_INNER
_C_SGU = _C_XBC + SSD_CONV_DIM
_C_S5 = _C_SGU + 2 * SGU_DIM
_C_DT = _C_S5 + S5_DIM
_C_END = _C_DT + LANES


def _inproj_kernel(xa_ref, xb_ref, sh_ref, sc_ref, g_ref, w_ref, z_ref, xbc_ref, sgu_ref, s5_ref, dt_ref):
    i = pl.program_id(0)
    r = _mod_row(i)
    x = jnp.where(i < N_CTX_TILES, xa_ref[...], xb_ref[...])
    shift = sh_ref[pl.ds(r, 1), :]
    scale = sc_ref[pl.ds(r, 1), :]
    h = (_rms(x) * g_ref[...]) * (1.0 + scale) + shift
    hb = h.astype(BF16)
    z_ref[...] = jnp.dot(hb, w_ref[:, _C_Z:_C_XBC], preferred_element_type=F32)
    xbc_ref[...] = jnp.dot(hb, w_ref[:, _C_XBC:_C_SGU], preferred_element_type=F32)
    sgu_ref[...] = jnp.dot(hb, w_ref[:, _C_SGU:_C_S5], preferred_element_type=F32)
    s5_ref[...] = jnp.dot(hb, w_ref[:, _C_S5:_C_DT], preferred_element_type=F32)
    dt_ref[...] = jnp.dot(hb, w_ref[:, _C_DT:_C_END], preferred_element_type=F32)


def _in_proj(xa, xb, b_off, l, mods, norm_g, w_perm):
    tok = lambda w: pl.BlockSpec((TOK_TILE, w), lambda i: (i, 0))
    widths = (SSD_INNER, SSD_CONV_DIM, 2 * SGU_DIM, S5_DIM, LANES)
    sa, sb = _split_tok_specs(D_MODEL, b_off)
    return pl.pallas_call(
        _inproj_kernel,
        grid=(N_TOK // TOK_TILE,),
        in_specs=[sa, sb, _mod_spec(l, 0), _mod_spec(l, 1), _layer_spec(l, (1, D_MODEL)),
                  _layer_spec(l, (D_MODEL, _C_END), pipeline_mode=pl.Buffered(1))],
        out_specs=[tok(w) for w in widths],
        out_shape=[jax.ShapeDtypeStruct((N_TOK, w), F32) for w in widths],
        compiler_params=_params(),
        name="in_proj",
    )(xa, xb, mods, mods, norm_g, w_perm)


def _ssd_kernel(*refs, L, has_h0, n_alias, emit_state):
    z_ref, xbc_ref, dt_ref, cw_ref, cb_ref, dtb_ref, alog_ref, dvec_ref, ng_ref = refs[:9]
    k = 9
    h0_ref = refs[k] if has_h0 else None
    k += int(has_h0) + n_alias
    y_ref = refs[k]
    hout_ref = refs[k + 1] if emit_state else None
    xpad, xc, acol, dtsp, yacc, hst = refs[k + 1 + int(emit_state):]
    Q = CHUNK
    nc = L // Q
    halo = SUBLANES
    pad = (SSD_CONV - 1) // 2

    xpad[0:halo, :] = jnp.zeros((halo, SSD_CONV_DIM), F32)
    xpad[halo + L:2 * halo + L, :] = jnp.zeros((halo, SSD_CONV_DIM), F32)
    for c in range(nc):
        xpad[halo + c * Q:halo + (c + 1) * Q, :] = xbc_ref[c * Q:(c + 1) * Q, :]
    for c in range(nc):
        for lb in range(0, SSD_CONV_DIM, CONV_LANES):
            ln = slice(lb, lb + CONV_LANES)
            o = halo - pad + c * Q
            acc = cb_ref[:, ln] + cw_ref[0:1, ln] * xpad[o:o + Q, ln]
            for t in range(1, SSD_CONV):
                acc = acc + cw_ref[t:t + 1, ln] * xpad[o + t:o + t + Q, ln]
            xc[c * Q:(c + 1) * Q, ln] = _silu(acc)

    raw = dt_ref[...] + dtb_ref[...]
    dt = jnp.maximum(raw, 0.0) + jnp.log(1.0 + jnp.exp(-jnp.abs(raw)))
    dtsp[...] = dt
    a_neg = -jnp.exp(alog_ref[...])
    row = lax.broadcasted_iota(jnp.int32, (Q, Q), 0)
    col = lax.broadcasted_iota(jnp.int32, (Q, Q), 1)
    lower = row >= col
    upper = col >= row
    tri_l = lower.astype(F32)
    tri_u = upper.astype(F32)
    fwd_lane = lax.broadcasted_iota(jnp.int32, (Q, LANES), 1) < SSD_HEADS
    for c in range(nc):
        dta = dtsp[c * Q:(c + 1) * Q, :] * a_neg
        pre = jnp.dot(tri_l, dta, precision=lax.Precision.HIGHEST, preferred_element_type=F32)
        suf = jnp.dot(tri_u, dta, precision=lax.Precision.HIGHEST, preferred_element_type=F32)
        acol[c * Q:(c + 1) * Q, :] = jnp.where(fwd_lane, pre, suf)

    for d in range(N_DIR):
        for p in range(N_PAIR):
            if has_h0:
                hst[d, p] = h0_ref[d, p * PAIR:(p + 1) * PAIR, :]
            else:
                hst[d, p] = jnp.zeros((PAIR, SSD_STATE), F32)

    lo_lane = lax.broadcasted_iota(jnp.int32, (Q, PAIR), 1) < SSD_HEAD_DIM
    lo_row = lax.broadcasted_iota(jnp.int32, (PAIR, SSD_STATE), 0) < SSD_HEAD_DIM

    def chunk(c, d):
        r0 = pl.multiple_of(c * Q, Q)
        rows = pl.ds(r0, Q)
        a = acol[rows, :]
        a_t = a.T
        dtc = dtsp[rows, :]
        a_end = a[Q - 1:Q, :] if d == 0 else a[0:1, :]
        dec = jnp.exp(a_end)
        mask = lower if d == 0 else upper
        outs = []
        for g in range(SSD_GROUPS):
            bmb = xc[rows, SSD_INNER + g * SSD_STATE:SSD_INNER + (g + 1) * SSD_STATE].astype(BF16)
            c0 = SSD_INNER + (SSD_GROUPS + g) * SSD_STATE
            cmb = xc[rows, c0:c0 + SSD_STATE].astype(BF16)
            cb = lax.dot_general(cmb, bmb, _NT, preferred_element_type=F32)
            for pr in range(2):
                p = g * 2 + pr
                j0 = d * SSD_HEADS + 2 * p
                j1 = j0 + 1
                xs = xc[rows, p * PAIR:(p + 1) * PAIR]
                ac0 = a[:, j0:j0 + 1]
                ac1 = a[:, j1:j1 + 1]
                l0 = jnp.exp(jnp.where(mask, ac0 - a_t[j0:j0 + 1, :], -jnp.inf))
                l1 = jnp.exp(jnp.where(mask, ac1 - a_t[j1:j1 + 1, :], -jnp.inf))
                m = jnp.concatenate([(cb * l0).astype(BF16), (cb * l1).astype(BF16)], axis=1)
                xdt = xs * jnp.where(lo_lane, dtc[:, j0:j0 + 1], dtc[:, j1:j1 + 1])
                rhs = jnp.concatenate([jnp.where(lo_lane, xdt, 0.0).astype(BF16),
                                       jnp.where(lo_lane, 0.0, xdt).astype(BF16)], axis=0)
                y_diag = jnp.dot(m, rhs, preferred_element_type=F32)
                hp = hst[d, p]
                y_off = lax.dot_general(cmb, hp.astype(BF16), _NT, preferred_element_type=F32)
                e_a = jnp.where(lo_lane, jnp.exp(ac0), jnp.exp(ac1))
                y = y_diag + e_a * y_off
                w_end = jnp.where(lo_lane, jnp.exp(a_end[:, j0:j0 + 1] - ac0),
                                  jnp.exp(a_end[:, j1:j1 + 1] - ac1))
                xw = (xdt * w_end).astype(BF16)
                s_new = lax.dot_general(xw, bmb, _TN, preferred_element_type=F32)
                decp = jnp.where(lo_row, dec[:, j0:j0 + 1], dec[:, j1:j1 + 1])
                hst[d, p] = decp * hp + s_new
                if d == 0:
                    yacc[rows, p * PAIR:(p + 1) * PAIR] = y + dvec_ref[:, p * PAIR:(p + 1) * PAIR] * xs
                else:
                    outs.append(y)
        if d == 1:
            y = yacc[rows, :] + jnp.concatenate(outs, axis=1)
            y = y * _silu(z_ref[rows, :])
            y_ref[rows, :] = _rms(y) * ng_ref[...]

    def fwd_body(t, carry):
        chunk(t, 0)
        return carry

    def bwd_body(t, carry):
        chunk(nc - 1 - t, 1)
        return carry

    lax.fori_loop(0, nc, fwd_body, 0)
    lax.fori_loop(0, nc, bwd_body, 0)
    if emit_state:
        for d in range(N_DIR):
            for p in range(N_PAIR):
                hout_ref[d, p * PAIR:(p + 1) * PAIR, :] = hst[d, p]


def _ssd_mixer(z, xbc, dtr, prm, l, *, L, nb, blk0, h0=None, y_buf=None, st_buf=None, emit_state):
    seq = lambda w: pl.BlockSpec((L, w), lambda b: (blk0 + b, 0))
    st = pl.BlockSpec((None, None, N_DIR, SSD_HP, SSD_STATE), lambda b: (b, l, 0, 0, 0))
    args = [z, xbc, dtr, prm["conv_w"], prm["conv_b"], prm["dt_bias"], prm["a_log"], prm["d_vec"],
            prm["norm_g"]]
    in_specs = [seq(SSD_INNER), seq(SSD_CONV_DIM), seq(LANES),
                _layer_spec(l, (SSD_CONV, SSD_CONV_DIM)), _layer_spec(l, (1, SSD_CONV_DIM)),
                _layer_spec(l, (1, LANES)), _layer_spec(l, (1, LANES)),
                _layer_spec(l, (1, SSD_INNER)), _layer_spec(l, (1, SSD_INNER))]
    if h0 is not None:
        args.append(h0)
        in_specs.append(st)
    out_shape = [jax.ShapeDtypeStruct((N_TOK, SSD_INNER), F32)]
    out_specs = [seq(SSD_INNER)]
    aliases = {}
    if y_buf is not None:
        aliases[len(args)] = 0
        args.append(y_buf)
        in_specs.append(_any_spec())
    if emit_state:
        out_shape.append(jax.ShapeDtypeStruct((BATCH, DEPTH, N_DIR, SSD_HP, SSD_STATE), F32))
        out_specs.append(st)
        if st_buf is not None:
            aliases[len(args)] = 1
            args.append(st_buf)
            in_specs.append(_any_spec())
    res = pl.pallas_call(
        functools.partial(_ssd_kernel, L=L, has_h0=h0 is not None, n_alias=len(aliases),
                          emit_state=emit_state),
        grid=(nb,),
        in_specs=in_specs,
        out_specs=out_specs,
        out_shape=out_shape,
        input_output_aliases=aliases,
        scratch_shapes=[pltpu.VMEM((L + 2 * SUBLANES, SSD_CONV_DIM), F32),
                        pltpu.VMEM((L, SSD_CONV_DIM), F32),
                        pltpu.VMEM((L, LANES), F32),
                        pltpu.VMEM((L, LANES), F32),
                        pltpu.VMEM((L, SSD_INNER), F32),
                        pltpu.VMEM((N_DIR, N_PAIR, PAIR, SSD_STATE), F32)],
        compiler_params=_params(),
        name=f"ssd_mixer_L{L}",
    )(*args)
    return res if emit_state else (res[0], None)


def _sgu_kernel(s_ref, g_ref, w_ref, b_ref, o_ref):
    uv = _gelu_tanh(s_ref[...])
    u = uv[:, :SGU_DIM]
    v = _rms(uv[:, SGU_DIM:]) * g_ref[...]
    lo_lane = lax.broadcasted_iota(jnp.int32, (CHUNK, LANES), 1) < (LANES // 2)
    for c in range(TOK_TILE // CHUNK):
        rows = slice(c * CHUNK, (c + 1) * CHUNK)
        mix = []
        for pr in range(SGU_HEADS // 2):
            vp = v[rows, pr * LANES:(pr + 1) * LANES]
            rhs = jnp.concatenate([jnp.where(lo_lane, vp, 0.0).astype(BF16),
                                   jnp.where(lo_lane, 0.0, vp).astype(BF16)], axis=0)
            mix.append(jnp.dot(w_ref[pr], rhs, preferred_element_type=F32))
        o_ref[rows, :] = u[rows, :] * (jnp.concatenate(mix, axis=1) + b_ref[...])


def _sgu_mixer(sgu, l, norm_g, w_pair, b_full):
    return pl.pallas_call(
        _sgu_kernel,
        grid=(N_TOK // TOK_TILE,),
        in_specs=[pl.BlockSpec((TOK_TILE, 2 * SGU_DIM), lambda i: (i, 0)),
                  _layer_spec(l, (1, SGU_DIM)),
                  _layer_spec(l, (SGU_HEADS // 2, CHUNK, 2 * CHUNK)),
                  _layer_spec(l, (CHUNK, SGU_DIM))],
        out_specs=pl.BlockSpec((TOK_TILE, SGU_DIM), lambda i: (i, 0)),
        out_shape=jax.ShapeDtypeStruct((N_TOK, SGU_DIM), F32),
        compiler_params=_params(),
        name="sgu_mixer",
    )(sgu, norm_g, w_pair, b_full)


_LOG2_CH = S5_GROUP_CH.bit_length() - 1
_LOG2_ST = S5_STATE.bit_length() - 1


def _cmul(ar, ai, br, bi):
    return ar * br - ai * bi, ar * bi + ai * br


def _s5prep_kernel(lre_ref, lim_ref, ldt_ref, btr_ref, bti_ref, ctr_ref, cti_ref,
                   bblk_ref, cblk_ref, pw_ref):
    brow = lax.broadcasted_iota(jnp.int32, (S5_DIM, S5_LANES), 0) >> _LOG2_CH
    bcol = lax.broadcasted_iota(jnp.int32, (S5_DIM, S5_LANES), 1) >> _LOG2_ST
    bmask = brow == bcol
    p1 = lax.broadcasted_iota(jnp.int32, (S5_PW_ROWS, S5_LANES), 0) + 1
    for d in range(N_DIR):
        lre = lre_ref[d]
        lim = lim_ref[d]
        step = jnp.exp(ldt_ref[d])
        mag = jnp.exp(lre * step)
        lbr = mag * jnp.cos(lim * step)
        lbi = mag * jnp.sin(lim * step)
        den = lre * lre + lim * lim
        nr = lbr - 1.0
        cr = (nr * lre + lbi * lim) / den
        ci = (lbi * lre - nr * lim) / den
        br, bi = _cmul(cr, ci, btr_ref[...], bti_ref[...])
        bblk_ref[d, :, 0:S5_LANES] = jnp.where(bmask, br, 0.0).astype(BF16)
        bblk_ref[d, :, S5_LANES:2 * S5_LANES] = jnp.where(bmask, bi, 0.0).astype(BF16)
        rr = jnp.ones((S5_PW_ROWS, S5_LANES), F32)
        ri = jnp.zeros((S5_PW_ROWS, S5_LANES), F32)
        sr, si = lbr, lbi
        for k in range(S5_PW_ROWS.bit_length()):
            bit = ((p1 >> k) & 1) == 1
            tr, ti = _cmul(rr, ri, sr, si)
            rr = jnp.where(bit, tr, rr)
            ri = jnp.where(bit, ti, ri)
            sr, si = _cmul(sr, si, sr, si)
        pw_ref[d, 0] = rr
        pw_ref[d, 1] = ri
    crow = lax.broadcasted_iota(jnp.int32, (S5_LANES, S5_DIM), 0) >> _LOG2_ST
    ccol = lax.broadcasted_iota(jnp.int32, (S5_LANES, S5_DIM), 1) >> _LOG2_CH
    cmask = crow == ccol
    cblk_ref[0:S5_LANES, :] = jnp.where(cmask, ctr_ref[...], 0.0).astype(BF16)
    cblk_ref[S5_LANES:2 * S5_LANES, :] = jnp.where(cmask, -cti_ref[...], 0.0).astype(BF16)


def _s5_prep(lam_re, lam_im, log_dt, b_re, b_im, c_re, c_im):
    row = lambda t: t.reshape(DEPTH, N_DIR, 1, S5_LANES)
    ldt = jnp.repeat(log_dt, S5_STATE, axis=-1)
    bt = lambda t: jnp.tile(t.transpose(0, 3, 1, 2).reshape(DEPTH, S5_GROUP_CH, S5_LANES),
                            (1, S5_GROUPS, 1))
    ct = lambda t: jnp.tile(t.transpose(0, 1, 3, 2).reshape(DEPTH, S5_LANES, S5_GROUP_CH),
                            (1, 1, S5_GROUPS))
    vec = pl.BlockSpec((None, N_DIR, 1, S5_LANES), lambda l: (l, 0, 0, 0))
    bsp = pl.BlockSpec((None, S5_DIM, S5_LANES), lambda l: (l, 0, 0))
    csp = pl.BlockSpec((None, S5_LANES, S5_DIM), lambda l: (l, 0, 0))
    return pl.pallas_call(
        _s5prep_kernel,
        grid=(DEPTH,),
        in_specs=[vec, vec, vec, bsp, bsp, csp, csp],
        out_specs=[pl.BlockSpec((None, N_DIR, S5_DIM, 2 * S5_LANES), lambda l: (l, 0, 0, 0)),
                   pl.BlockSpec((None, 2 * S5_LANES, S5_DIM), lambda l: (l, 0, 0)),
                   pl.BlockSpec((None, N_DIR, 2, S5_PW_ROWS, S5_LANES), lambda l: (l, 0, 0, 0, 0))],
        out_shape=[jax.ShapeDtypeStruct((DEPTH, N_DIR, S5_DIM, 2 * S5_LANES), BF16),
                   jax.ShapeDtypeStruct((DEPTH, 2 * S5_LANES, S5_DIM), BF16),
                   jax.ShapeDtypeStruct((DEPTH, N_DIR, 2, S5_PW_ROWS, S5_LANES), F32)],
        compiler_params=_params(),
        name="s5_prep",
    )(row(lam_re), row(lam_im), row(ldt), bt(b_re), bt(b_im), ct(c_re), ct(c_im))


def _s5_kernel(*refs, L, col_major, has_h0, n_alias, emit_state):
    u_ref, bblk_ref, cblk_ref, pw_ref, dvec_ref, gw_ref, gb_ref = refs[:7]
    k = 7
    h0r_ref, h0i_ref = (refs[k], refs[k + 1]) if has_h0 else (None, None)
    k += 2 * int(has_h0) + n_alias
    y_ref = refs[k]
    fr_ref, fi_ref = (refs[k + 1], refs[k + 2]) if emit_state else (None, None)
    hf, hb, cin = refs[k + 1 + 2 * int(emit_state):]
    P = L // S5_SEG
    NT = S5_LANES // LANES
    zero = jnp.zeros((S5_SEG, S5_LANES), F32)
    zrow = jnp.zeros((1, S5_LANES), F32)

    def get(buf, rows, part):
        return jnp.concatenate([buf[part * NT + t, rows, :] for t in range(NT)], axis=1)

    def put(buf, rows, part, val):
        for t in range(NT):
            buf[part * NT + t, rows, :] = val[:, t * LANES:(t + 1) * LANES]

    def seg_rows(p):
        if col_major:
            grid_rows = L // GRID_W
            start = (p & (grid_rows - 1)) * GRID_W + (p >> (grid_rows.bit_length() - 1))
            return pl.ds(start, S5_SEG, stride=GRID_W // S5_SEG)
        return pl.ds(p, S5_SEG, stride=P)

    def local_scan(d, buf):
        for r0 in range(0, L, ROW_SLAB):
            rs = slice(r0, r0 + ROW_SLAB)
            bu = jnp.dot(u_ref[rs, :].astype(BF16), bblk_ref[d], preferred_element_type=F32)
            put(buf, rs, 0, bu[:, 0:S5_LANES])
            put(buf, rs, 1, bu[:, S5_LANES:2 * S5_LANES])
        lr = jnp.broadcast_to(pw_ref[d, 0, 0:1, :], (S5_SEG, S5_LANES))
        li = jnp.broadcast_to(pw_ref[d, 1, 0:1, :], (S5_SEG, S5_LANES))

        def body(t, carry):
            hr, hi = carry
            rows = seg_rows(t if d == 0 else P - 1 - t)
            nr = lr * hr - li * hi + get(buf, rows, 0)
            ni = lr * hi + li * hr + get(buf, rows, 1)
            put(buf, rows, 0, nr)
            put(buf, rows, 1, ni)
            return nr, ni

        return lax.fori_loop(0, P, body, (zero, zero))

    def carries(d, er, ei):
        lpr = pw_ref[d, 0, P - 1:P, :]
        lpi = pw_ref[d, 1, P - 1:P, :]
        pr = h0r_ref[d:d + 1, :] if has_h0 else zrow
        pi = h0i_ref[d:d + 1, :] if has_h0 else zrow
        order = range(S5_SEG) if d == 0 else range(S5_SEG - 1, -1, -1)
        for s in order:
            cin[d, 0, s:s + 1, :] = pr
            cin[d, 1, s:s + 1, :] = pi
            tr, ti = _cmul(lpr, lpi, pr, pi)
            pr = er[s:s + 1, :] + tr
            pi = ei[s:s + 1, :] + ti
        if emit_state:
            fr_ref[d:d + 1, :] = pr
            fi_ref[d:d + 1, :] = pi

    efr, efi = local_scan(0, hf)
    ebr, ebi = local_scan(1, hb)
    carries(0, efr, efi)
    carries(1, ebr, ebi)
    cfr = cin[0, 0]
    cfi = cin[0, 1]
    cbr = cin[1, 0]
    cbi = cin[1, 1]

    def fix(p, carry):
        rows = seg_rows(p)
        tfr, tfi = _cmul(pw_ref[0, 0, pl.ds(p, 1), :], pw_ref[0, 1, pl.ds(p, 1), :], cfr, cfi)
        q = P - 1 - p
        tbr, tbi = _cmul(pw_ref[1, 0, pl.ds(q, 1), :], pw_ref[1, 1, pl.ds(q, 1), :], cbr, cbi)
        put(hf, rows, 0, get(hf, rows, 0) + get(hb, rows, 0) + tfr + tbr)
        put(hf, rows, 1, get(hf, rows, 1) + get(hb, rows, 1) + tfi + tbi)
        return carry

    lax.fori_loop(0, P, fix, 0)
    gwb = gw_ref[...].astype(BF16)
    for r0 in range(0, L, ROW_SLAB):
        rows = slice(r0, r0 + ROW_SLAB)
        hsum = jnp.concatenate([get(hf, rows, 0), get(hf, rows, 1)], axis=1).astype(BF16)
        y = jnp.dot(hsum, cblk_ref[...], preferred_element_type=F32)
        y = _gelu_tanh(y + dvec_ref[...] * u_ref[rows, :])
        gate = jnp.dot(y.astype(BF16), gwb, preferred_element_type=F32) + gb_ref[...]
        y_ref[rows, :] = y * jax.nn.sigmoid(gate)


def _s5_mixer(u, bblk, cblk, pw, prm, l, *, L, nb, blk0, col_major, h0=None, y_buf=None, st_buf=None,
              emit_state):
    seq = pl.BlockSpec((L, S5_DIM), lambda b: (blk0 + b, 0))
    st = pl.BlockSpec((None, None, N_DIR, S5_LANES), lambda b: (b, l, 0, 0))
    args = [u, bblk, cblk, pw, prm["d_vec"], prm["glu_w"], prm["glu_b"]]
    in_specs = [seq, _layer_spec(l, (N_DIR, S5_DIM, 2 * S5_LANES)), _layer_spec(l, (2 * S5_LANES, S5_DIM)),
                _layer_spec(l, (N_DIR, 2, S5_PW_ROWS, S5_LANES)), _layer_spec(l, (1, S5_DIM)),
                _layer_spec(l, (S5_DIM, S5_DIM)), _layer_spec(l, (1, S5_DIM))]
    if h0 is not None:
        args += list(h0)
        in_specs += [st, st]
    out_shape = [jax.ShapeDtypeStruct((N_TOK, S5_DIM), F32)]
    out_specs = [seq]
    aliases = {}
    if y_buf is not None:
        aliases[len(args)] = 0
        args.append(y_buf)
        in_specs.append(_any_spec())
    if emit_state:
        out_shape += [jax.ShapeDtypeStruct((BATCH, DEPTH, N_DIR, S5_LANES), F32)] * 2
        out_specs += [st, st]
        if st_buf is not None:
            for j, buf in enumerate(st_buf):
                aliases[len(args)] = 1 + j
                args.append(buf)
                in_specs.append(_any_spec())
    res = pl.pallas_call(
        functools.partial(_s5_kernel, L=L, col_major=col_major, has_h0=h0 is not None,
                          n_alias=len(aliases), emit_state=emit_state),
        grid=(nb,),
        in_specs=in_specs,
        out_specs=out_specs,
        out_shape=out_shape,
        input_output_aliases=aliases,
        scratch_shapes=[pltpu.VMEM((2 * S5_LANES // LANES, L, LANES), F32),
                        pltpu.VMEM((2 * S5_LANES // LANES, L, LANES), F32),
                        pltpu.VMEM((N_DIR, 2, S5_SEG, S5_LANES), F32)],
        compiler_params=_params(),
        name=f"s5_mixer_L{L}",
    )(*args)
    return (res[0], res[1], res[2]) if emit_state else (res[0], None, None)


def _outffn_kernel(xa_ref, xb_ref, ys_ref, yg_ref, y5_ref, g1_ref, sh_ref, sc_ref, g2_ref, ng_ref,
                   fg_ref, wo_ref, w1_ref, w2_ref, *o_refs, final):
    i = pl.program_id(0)
    r = _mod_row(i)
    row = lambda ref: ref[pl.ds(r, 1), :]
    x = jnp.where(i < N_CTX_TILES, xa_ref[...], xb_ref[...])
    mixed = jnp.concatenate([ys_ref[...].astype(BF16), yg_ref[...].astype(BF16),
                             y5_ref[...].astype(BF16)], axis=1)
    x1 = x + row(g1_ref) * jnp.dot(mixed, wo_ref[...], preferred_element_type=F32)
    h = ((_rms(x1) * ng_ref[...]) * (1.0 + row(sc_ref)) + row(sh_ref)).astype(BF16)
    acc = jnp.zeros((TOK_TILE, D_MODEL), F32)
    for j in range(D_FF // FF_SLAB):
        a = jnp.dot(h, w1_ref[:, j * FF_SLAB:(j + 1) * FF_SLAB], preferred_element_type=F32)
        a = jnp.maximum(a, 0.0)
        acc = acc + jnp.dot((a * a).astype(BF16), w2_ref[j * FF_SLAB:(j + 1) * FF_SLAB, :],
                            preferred_element_type=F32)
    x2 = x1 + row(g2_ref) * acc
    if not final:
        o_refs[0][...] = x2
    else:
        y = _rms(x2) * fg_ref[...]

        @pl.when(i < N_CTX_TILES)
        def _():
            o_refs[0][...] = y

        @pl.when(i >= N_CTX_TILES)
        def _():
            o_refs[1][...] = y


def _out_ffn(xa, xb, b_off, y_ssd, y_sgu, y_s5, l, mods, norm_g, final_g, w_out, w1, w2, *, final):
    tok = lambda w: pl.BlockSpec((TOK_TILE, w), lambda i: (i, 0))
    res = lambda s: _layer_spec(l, s, pipeline_mode=pl.Buffered(1))
    sa, sb = _split_tok_specs(D_MODEL, b_off)
    if final:
        out_specs = list(_split_tok_specs(D_MODEL, 0))
        out_shape = [jax.ShapeDtypeStruct((N_CTX, D_MODEL), F32), jax.ShapeDtypeStruct((N_LAT, D_MODEL), F32)]
    else:
        out_specs = [tok(D_MODEL)]
        out_shape = [jax.ShapeDtypeStruct((N_TOK, D_MODEL), F32)]
    return pl.pallas_call(
        functools.partial(_outffn_kernel, final=final),
        grid=(N_TOK // TOK_TILE,),
        in_specs=[sa, sb, tok(SSD_INNER), tok(SGU_DIM), tok(S5_DIM),
                  _mod_spec(l, 2), _mod_spec(l, 3), _mod_spec(l, 4), _mod_spec(l, 5),
                  _layer_spec(l, (1, D_MODEL)), pl.BlockSpec((1, D_MODEL), lambda i: (0, 0)),
                  res((D_MODEL, D_MODEL)), res((D_MODEL, D_FF)), res((D_FF, D_MODEL))],
        out_specs=out_specs,
        out_shape=out_shape,
        compiler_params=_params(),
        name="out_ffn_final" if final else "out_ffn",
    )(xa, xb, y_ssd, y_sgu, y_s5, mods, mods, mods, mods, norm_g, final_g, w_out, w1, w2)


def kernel(x_prompt, x_sample, state_ssd, state_s5_re, state_s5_im, c, c_ctx, ada_w, ada_b, norm1_g,
           norm2_g, w_in, ssd_conv_w, ssd_conv_b, ssd_dt_bias, ssd_a_log, ssd_d, ssd_norm_g,
           sgu_norm_g, sgu_w, sgu_b, s5_lambda_re, s5_lambda_im, s5_log_dt, s5_b_re, s5_b_im,
           s5_c_re, s5_c_im, s5_d, s5_glu_w, s5_glu_b, w_out, ffn_w1, ffn_w2, final_norm_g):
    cvec = jnp.concatenate([c_ctx[None, :], c,
                            jnp.zeros((MOD_ROWS - 1 - DEC_BATCH, D_MODEL), F32)], axis=0)
    mods = _adaln_mods(cvec, ada_w, ada_b)
    bblk, cblk, pw = _s5_prep(s5_lambda_re, s5_lambda_im, s5_log_dt, s5_b_re, s5_b_im,
                              s5_c_re, s5_c_im)

    w_perm = jnp.concatenate(
        [w_in[:, :, 0:OFF_DT], w_in[:, :, OFF_SGU:IN_DIM], w_in[:, :, OFF_DT:OFF_SGU],
         jnp.zeros((DEPTH, D_MODEL, LANES - N_DIR * SSD_HEADS), F32)], axis=2).astype(BF16)
    w_out_b = w_out.astype(BF16)
    w1_b = ffn_w1.astype(BF16)
    w2_b = ffn_w2.astype(BF16)
    vec = lambda t: t.reshape(DEPTH, 1, -1)
    lane_pad = lambda t: jnp.pad(vec(t), ((0, 0), (0, 0), (0, LANES - t[0].size)))
    ssd_prm = dict(conv_w=ssd_conv_w, conv_b=vec(ssd_conv_b), dt_bias=lane_pad(ssd_dt_bias),
                   a_log=lane_pad(ssd_a_log), d_vec=vec(jnp.repeat(ssd_d, SSD_HEAD_DIM, axis=-1)),
                   norm_g=vec(ssd_norm_g))
    s5_prm = dict(d_vec=vec(s5_d), glu_w=s5_glu_w, glu_b=vec(s5_glu_b))
    sgu_w_pair = sgu_w.reshape(DEPTH, SGU_HEADS // 2, 2, CHUNK, CHUNK).transpose(0, 1, 3, 2, 4)
    sgu_w_pair = sgu_w_pair.reshape(DEPTH, SGU_HEADS // 2, CHUNK, 2 * CHUNK).astype(BF16)
    sgu_b_full = jnp.repeat(sgu_b.transpose(0, 2, 1), SGU_DIM // SGU_HEADS, axis=2)
    norm1 = vec(norm1_g)
    norm2 = vec(norm2_g)
    sgu_g = vec(sgu_norm_g)
    final_g = final_norm_g.reshape(1, D_MODEL)

    lat_ssd = state_ssd.reshape(DEC_BATCH, DEPTH, N_DIR, SSD_HP, SSD_STATE)
    lat_s5 = (state_s5_re.reshape(DEC_BATCH, DEPTH, N_DIR, S5_LANES),
              state_s5_im.reshape(DEC_BATCH, DEPTH, N_DIR, S5_LANES))
    lat_blk = N_CTX // DEC_SEQ

    xa = x_prompt.reshape(N_CTX, D_MODEL)
    xb = x_sample.reshape(N_LAT, D_MODEL)
    b_off = 0
    st_ssd = None
    st_s5 = None
    for l in range(DEPTH):
        z, xbc, sgu, s5u, dtr = _in_proj(xa, xb, b_off, l, mods, norm1, w_perm)
        y_ssd, st_ssd = _ssd_mixer(z, xbc, dtr, ssd_prm, l, L=SEQ, nb=BATCH, blk0=0, st_buf=st_ssd,
                                   emit_state=True)
        y_ssd, _ = _ssd_mixer(z, xbc, dtr, ssd_prm, l, L=DEC_SEQ, nb=DEC_BATCH, blk0=lat_blk,
                              h0=lat_ssd, y_buf=y_ssd, emit_state=False)
        y_sgu = _sgu_mixer(sgu, l, sgu_g, sgu_w_pair, sgu_b_full)
        y_s5, fr, fi = _s5_mixer(s5u, bblk, cblk, pw, s5_prm, l, L=SEQ, nb=BATCH, blk0=0,
                                 col_major=False, st_buf=st_s5, emit_state=True)
        st_s5 = (fr, fi)
        y_s5, _, _ = _s5_mixer(s5u, bblk, cblk, pw, s5_prm, l, L=DEC_SEQ, nb=DEC_BATCH, blk0=lat_blk,
                               col_major=True, h0=lat_s5, y_buf=y_s5, emit_state=False)
        final = l == DEPTH - 1
        out = _out_ffn(xa, xb, b_off, y_ssd, y_sgu, y_s5, l, mods, norm2, final_g, w_out_b, w1_b,
                       w2_b, final=final)
        if not final:
            xa = xb = out[0]
            b_off = N_CTX_TILES

    y_prompt = out[0].reshape(BATCH, SEQ, D_MODEL)
    y_sample = out[1].reshape(DEC_BATCH, DEC_SEQ, D_MODEL)
    new_state_ssd = st_ssd.reshape(BATCH, DEPTH, N_DIR, SSD_HEADS, SSD_HEAD_DIM, SSD_STATE)
    s5_shape = (BATCH, DEPTH, N_DIR, S5_GROUPS, S5_STATE)
    return (y_prompt, y_sample, new_state_ssd, st_s5[0].reshape(s5_shape), st_s5[1].reshape(s5_shape))
```

```python
import functools
import math

import jax
import jax.numpy as jnp
from jax import lax
from jax.experimental import pallas as pl
from jax.experimental.pallas import tpu as pltpu

F32 = jnp.float32
BF16 = jnp.bfloat16

D_MODEL = 1024
BATCH = 16
SEQ = 256
DEPTH = 2
DEC_BATCH = 2
DEC_SEQ = 1024
GRID_W = 64
CHUNK = 128
N_DIR = 2
EPS = 1e-6
SSD_INNER = 512
SSD_HEAD_DIM = 64
SSD_HEADS = 8
SSD_GROUPS = 2
SSD_STATE = 128
SSD_CONV = 5
SSD_CONV_DIM = SSD_INNER + 2 * SSD_GROUPS * SSD_STATE
SGU_DIM = 256
SGU_HEADS = 4
S5_DIM = 256
S5_GROUP_CH = 16
S5_GROUPS = 16
S5_STATE = 64
D_FF = 4 * D_MODEL
OFF_XBC = SSD_INNER
OFF_DT = OFF_XBC + SSD_CONV_DIM
OFF_SGU = OFF_DT + N_DIR * SSD_HEADS
OFF_S5 = OFF_SGU + 2 * SGU_DIM
IN_DIM = OFF_S5 + S5_DIM

N_CTX = BATCH * SEQ
N_LAT = DEC_BATCH * DEC_SEQ
N_TOK = N_CTX + N_LAT
LANES = 128
SUBLANES = 8
MOD_ROWS = SUBLANES
SSD_HP = SSD_HEADS * SSD_HEAD_DIM
S5_LANES = S5_GROUPS * S5_STATE
S5_SEG = SUBLANES
S5_PW_ROWS = DEC_SEQ // S5_SEG
S5_SLAB = 256
S5_NSLAB = S5_LANES // S5_SLAB
PAIR = 2 * SSD_HEAD_DIM
N_PAIR = SSD_HEADS // 2
TOK_TILE = 256
N_CTX_TILES = N_CTX // TOK_TILE
ROW_SLAB = 256
CONV_LANES = 256
FF_SLAB = 1024
VMEM_LIMIT = 56 * 1024 * 1024

_NT = (((1,), (1,)), ((), ()))
_TN = (((0,), (0,)), ((), ()))


def _params(n_axes=1):
    return pltpu.CompilerParams(dimension_semantics=("arbitrary",) * n_axes,
                                vmem_limit_bytes=VMEM_LIMIT)


def _layer_spec(l, shape, **kw):
    return pl.BlockSpec((None,) + tuple(shape), lambda *_: (l,) + (0,) * len(shape), **kw)


def _mod_spec(l, k):
    return pl.BlockSpec((None, MOD_ROWS, D_MODEL), lambda *_: (l, 0, k))


def _any_spec():
    return pl.BlockSpec(memory_space=pl.ANY)


def _mod_row(i):
    per_seq = DEC_SEQ // TOK_TILE
    return jnp.where(i < N_CTX_TILES, 0, 1 + (i - N_CTX_TILES) // per_seq)


def _split_tok_specs(width, b_rows_offset):
    a = pl.BlockSpec((TOK_TILE, width), lambda i: (jnp.minimum(i, N_CTX_TILES - 1), 0))
    b = pl.BlockSpec((TOK_TILE, width),
                     lambda i: (jnp.maximum(i, N_CTX_TILES) - N_CTX_TILES + b_rows_offset, 0))
    return a, b


def _silu(x):
    return x * jax.nn.sigmoid(x)


def _gelu_tanh(x):
    c = math.sqrt(2.0 / math.pi)
    return 0.5 * x * (1.0 + jnp.tanh(c * (x + 0.044715 * (x * x * x))))


def _rms(x):
    return x * lax.rsqrt(jnp.mean(x * x, axis=-1, keepdims=True) + EPS)


def _bdot(a, b):
    return jnp.dot(a.astype(BF16), b.astype(BF16), preferred_element_type=F32)


def _split3(x):
    hi = x.astype(BF16)
    r = x - hi.astype(F32)
    mid = r.astype(BF16)
    lo = (r - mid.astype(F32)).astype(BF16)
    return jnp.concatenate([hi, mid, lo], axis=1)


def _mod_kernel(c_ref, w_ref, b_ref, o_ref):
    o_ref[...] = _bdot(_silu(c_ref[...]), w_ref[...]) + b_ref[...]


def _adaln_mods(cvec, ada_w, ada_b):
    n_blk = 6
    return pl.pallas_call(
        _mod_kernel,
        grid=(DEPTH, n_blk),
        in_specs=[pl.BlockSpec((MOD_ROWS, D_MODEL), lambda l, j: (0, 0)),
                  pl.BlockSpec((None, D_MODEL, D_MODEL), lambda l, j: (l, 0, j)),
                  pl.BlockSpec((None, 1, D_MODEL), lambda l, j: (l, 0, j))],
        out_specs=pl.BlockSpec((None, MOD_ROWS, D_MODEL), lambda l, j: (l, 0, j)),
        out_shape=jax.ShapeDtypeStruct((DEPTH, MOD_ROWS, 6 * D_MODEL), F32),
        compiler_params=_params(2),
        name="adaln_mod",
    )(cvec, ada_w, ada_b.reshape(DEPTH, 1, 6 * D_MODEL))


_C_Z = 0
_C_XBC = _C_Z + SSD_INNER
_C_SGU = _C_XBC + SSD_CONV_DIM
_C_S5 = _C_SGU + 2 * SGU_DIM
_C_DT = _C_S5 + S5_DIM
_C_END = _C_DT + LANES


def _inproj_kernel(xa_ref, xb_ref, sh_ref, sc_ref, g_ref, w_ref, z_ref, xbc_ref, sgu_ref, s5_ref, dt_ref):
    i = pl.program_id(0)
    r = _mod_row(i)
    x = jnp.where(i < N_CTX_TILES, xa_ref[...], xb_ref[...])
    shift = sh_ref[pl.ds(r, 1), :]
    scale = sc_ref[pl.ds(r, 1), :]
    h = (_rms(x) * g_ref[...]) * (1.0 + scale) + shift
    hb = h.astype(BF16)
    z_ref[...] = jnp.dot(hb, w_ref[:, _C_Z:_C_XBC], preferred_element_type=F32)
    xbc_ref[...] = jnp.dot(hb, w_ref[:, _C_XBC:_C_SGU], preferred_element_type=F32)
    sgu_ref[...] = jnp.dot(hb, w_ref[:, _C_SGU:_C_S5], preferred_element_type=F32)
    s5_ref[...] = jnp.dot(hb, w_ref[:, _C_S5:_C_DT], preferred_element_type=F32)
    dt_ref[...] = jnp.dot(hb, w_ref[:, _C_DT:_C_END], preferred_element_type=F32)


def _in_proj(xa, xb, b_off, l, mods, norm_g, w_perm):
    tok = lambda w: pl.BlockSpec((TOK_TILE, w), lambda i: (i, 0))
    widths = (SSD_INNER, SSD_CONV_DIM, 2 * SGU_DIM, S5_DIM, LANES)
    sa, sb = _split_tok_specs(D_MODEL, b_off)
    return pl.pallas_call(
        _inproj_kernel,
        grid=(N_TOK // TOK_TILE,),
        in_specs=[sa, sb, _mod_spec(l, 0), _mod_spec(l, 1), _layer_spec(l, (1, D_MODEL)),
                  _layer_spec(l, (D_MODEL, _C_END), pipeline_mode=pl.Buffered(1))],
        out_specs=[tok(w) for w in widths],
        out_shape=[jax.ShapeDtypeStruct((N_TOK, w), F32) for w in widths],
        compiler_params=_params(),
        name="in_proj",
    )(xa, xb, mods, mods, norm_g, w_perm)


def _ssd_kernel(*refs, L, has_h0, n_alias, emit_state):
    z_ref, xbc_ref, dt_ref, cw_ref, cb_ref, dtb_ref, alog_ref, dvec_ref, ng_ref = refs[:9]
    k = 9
    h0_ref = refs[k] if has_h0 else None
    k += int(has_h0) + n_alias
    y_ref = refs[k]
    hout_ref = refs[k + 1] if emit_state else None
    xpad, xc, acol, atr, cbs, dtsp, yacc, hst = refs[k + 1 + int(emit_state):]
    Q = CHUNK
    nc = L // Q
    halo = SUBLANES
    pad = (SSD_CONV - 1) // 2

    xpad[0:halo, :] = jnp.zeros((halo, SSD_CONV_DIM), F32)
    xpad[halo + L:2 * halo + L, :] = jnp.zeros((halo, SSD_CONV_DIM), F32)
    for c in range(nc):
        xpad[halo + c * Q:halo + (c + 1) * Q, :] = xbc_ref[c * Q:(c + 1) * Q, :]
    for c in range(nc):
        for lb in range(0, SSD_CONV_DIM, CONV_LANES):
            ln = slice(lb, lb + CONV_LANES)
            o = halo - pad + c * Q
            acc = cb_ref[:, ln] + cw_ref[0:1, ln] * xpad[o:o + Q, ln]
            for t in range(1, SSD_CONV):
                acc = acc + cw_ref[t:t + 1, ln] * xpad[o + t:o + t + Q, ln]
            xc[c * Q:(c + 1) * Q, ln] = _silu(acc)

    raw = dt_ref[...] + dtb_ref[...]
    dt = jnp.maximum(raw, 0.0) + jnp.log(1.0 + jnp.exp(-jnp.abs(raw)))
    dtsp[...] = dt
    a_neg = -jnp.exp(alog_ref[...])
    row = lax.broadcasted_iota(jnp.int32, (Q, Q), 0)
    col = lax.broadcasted_iota(jnp.int32, (Q, Q), 1)
    lower = row >= col
    upper = col >= row
    tri_l = lower.astype(F32)
    tri_u = upper.astype(F32)
    fwd_lane = lax.broadcasted_iota(jnp.int32, (Q, LANES), 1) < SSD_HEADS
    for c in range(nc):
        dta = dtsp[c * Q:(c + 1) * Q, :] * a_neg
        pre = jnp.dot(tri_l, dta, precision=lax.Precision.HIGHEST, preferred_element_type=F32)
        suf = jnp.dot(tri_u, dta, precision=lax.Precision.HIGHEST, preferred_element_type=F32)
        a = jnp.where(fwd_lane, pre, suf)
        acol[c * Q:(c + 1) * Q, :] = a
        atr[c] = a.T[0:N_DIR * SSD_HEADS, :]
        for g in range(SSD_GROUPS):
            b0 = SSD_INNER + g * SSD_STATE
            c0 = SSD_INNER + (SSD_GROUPS + g) * SSD_STATE
            cbs[c, g] = lax.dot_general(xc[c * Q:(c + 1) * Q, c0:c0 + SSD_STATE].astype(BF16),
                                        xc[c * Q:(c + 1) * Q, b0:b0 + SSD_STATE].astype(BF16), _NT,
                                        preferred_element_type=F32)

    sel_row = lax.broadcasted_iota(jnp.int32, (3 * LANES, SSD_INNER), 0) & (LANES - 1)
    sel_head = lax.broadcasted_iota(jnp.int32, (3 * LANES, SSD_INNER), 1) >> (SSD_HEAD_DIM.bit_length() - 1)
    sel = [(sel_row == sel_head + d * SSD_HEADS).astype(BF16) for d in range(N_DIR)]
    mask2 = [jnp.concatenate([m, m], axis=1) for m in (lower, upper)]

    for d in range(N_DIR):
        for p in range(N_PAIR):
            if has_h0:
                hst[d, p] = h0_ref[d, p * PAIR:(p + 1) * PAIR, :]
            else:
                hst[d, p] = jnp.zeros((PAIR, SSD_STATE), F32)

    lo_lane = lax.broadcasted_iota(jnp.int32, (Q, PAIR), 1) < SSD_HEAD_DIM
    lo_row = lax.broadcasted_iota(jnp.int32, (PAIR, SSD_STATE), 0) < SSD_HEAD_DIM

    def chunk(c, d):
        r0 = pl.multiple_of(c * Q, Q)
        rows = pl.ds(r0, Q)
        a = acol[rows, :]
        a_t = atr[c]
        dtp = jnp.dot(_split3(dtsp[rows, :]), sel[d], preferred_element_type=F32)
        a_end = a[Q - 1:Q, :] if d == 0 else a[0:1, :]
        dec = jnp.exp(a_end)
        outs = []
        for g in range(SSD_GROUPS):
            bmb = xc[rows, SSD_INNER + g * SSD_STATE:SSD_INNER + (g + 1) * SSD_STATE].astype(BF16)
            c0 = SSD_INNER + (SSD_GROUPS + g) * SSD_STATE
            cmb = xc[rows, c0:c0 + SSD_STATE].astype(BF16)
            cb = cbs[c, g]
            cb2 = jnp.concatenate([cb, cb], axis=1)
            for pr in range(2):
                p = g * 2 + pr
                j0 = d * SSD_HEADS + 2 * p
                j1 = j0 + 1
                xs = xc[rows, p * PAIR:(p + 1) * PAIR]
                ab0 = jnp.broadcast_to(a[:, j0:j0 + 1], (Q, Q))
                ab1 = jnp.broadcast_to(a[:, j1:j1 + 1], (Q, Q))
                seg = (jnp.concatenate([ab0, ab1], axis=1)
                       - jnp.concatenate([a_t[j0:j0 + 1, :], a_t[j1:j1 + 1, :]], axis=1))
                m = (cb2 * jnp.exp(jnp.where(mask2[d], seg, -jnp.inf))).astype(BF16)
                xdt = xs * dtp[:, p * PAIR:(p + 1) * PAIR]
                rhs = jnp.concatenate([jnp.where(lo_lane, xdt, 0.0).astype(BF16),
                                       jnp.where(lo_lane, 0.0, xdt).astype(BF16)], axis=0)
                y_diag = jnp.dot(m, rhs, preferred_element_type=F32)
                hp = hst[d, p]
                y_off = lax.dot_general(cmb, hp.astype(BF16), _NT, preferred_element_type=F32)
                a_pair = jnp.where(lo_lane, ab0, ab1)
                y = y_diag + jnp.exp(a_pair) * y_off
                a_end_pair = jnp.where(lo_lane[0:1, :], a_end[:, j0:j0 + 1], a_end[:, j1:j1 + 1])
                xw = (xdt * jnp.exp(a_end_pair - a_pair)).astype(BF16)
                s_new = lax.dot_general(xw, bmb, _TN, preferred_element_type=F32)
                decp = jnp.where(lo_row, dec[:, j0:j0 + 1], dec[:, j1:j1 + 1])
                hst[d, p] = decp * hp + s_new
                if d == 0:
                    yacc[rows, p * PAIR:(p + 1) * PAIR] = y + dvec_ref[:, p * PAIR:(p + 1) * PAIR] * xs
                else:
                    outs.append(y)
        if d == 1:
            y = yacc[rows, :] + jnp.concatenate(outs, axis=1)
            y = y * _silu(z_ref[rows, :])
            y_ref[rows, :] = _rms(y) * ng_ref[...]

    def fwd_body(t, carry):
        chunk(t, 0)
        return carry

    def bwd_body(t, carry):
        chunk(nc - 1 - t, 1)
        return carry

    lax.fori_loop(0, nc, fwd_body, 0)
    lax.fori_loop(0, nc, bwd_body, 0)
    if emit_state:
        for d in range(N_DIR):
            for p in range(N_PAIR):
                hout_ref[d, p * PAIR:(p + 1) * PAIR, :] = hst[d, p]


def _ssd_mixer(z, xbc, dtr, prm, l, *, L, nb, blk0, h0=None, y_buf=None, st_buf=None, emit_state):
    seq = lambda w: pl.BlockSpec((L, w), lambda b: (blk0 + b, 0))
    st = pl.BlockSpec((None, None, N_DIR, SSD_HP, SSD_STATE), lambda b: (b, l, 0, 0, 0))
    args = [z, xbc, dtr, prm["conv_w"], prm["conv_b"], prm["dt_bias"], prm["a_log"], prm["d_vec"],
            prm["norm_g"]]
    in_specs = [seq(SSD_INNER), seq(SSD_CONV_DIM), seq(LANES),
                _layer_spec(l, (SSD_CONV, SSD_CONV_DIM)), _layer_spec(l, (1, SSD_CONV_DIM)),
                _layer_spec(l, (1, LANES)), _layer_spec(l, (1, LANES)),
                _layer_spec(l, (1, SSD_INNER)), _layer_spec(l, (1, SSD_INNER))]
    if h0 is not None:
        args.append(h0)
        in_specs.append(st)
    out_shape = [jax.ShapeDtypeStruct((N_TOK, SSD_INNER), F32)]
    out_specs = [seq(SSD_INNER)]
    aliases = {}
    if y_buf is not None:
        aliases[len(args)] = 0
        args.append(y_buf)
        in_specs.append(_any_spec())
    if emit_state:
        out_shape.append(jax.ShapeDtypeStruct((BATCH, DEPTH, N_DIR, SSD_HP, SSD_STATE), F32))
        out_specs.append(st)
        if st_buf is not None:
            aliases[len(args)] = 1
            args.append(st_buf)
            in_specs.append(_any_spec())
    res = pl.pallas_call(
        functools.partial(_ssd_kernel, L=L, has_h0=h0 is not None, n_alias=len(aliases),
                          emit_state=emit_state),
        grid=(nb,),
        in_specs=in_specs,
        out_specs=out_specs,
        out_shape=out_shape,
        input_output_aliases=aliases,
        scratch_shapes=[pltpu.VMEM((L + 2 * SUBLANES, SSD_CONV_DIM), F32),
                        pltpu.VMEM((L, SSD_CONV_DIM), F32),
                        pltpu.VMEM((L, LANES), F32),
                        pltpu.VMEM((L // CHUNK, N_DIR * SSD_HEADS, CHUNK), F32),
                        pltpu.VMEM((L // CHUNK, SSD_GROUPS, CHUNK, CHUNK), F32),
                        pltpu.VMEM((L, LANES), F32),
                        pltpu.VMEM((L, SSD_INNER), F32),
                        pltpu.VMEM((N_DIR, N_PAIR, PAIR, SSD_STATE), F32)],
        compiler_params=_params(),
        name=f"ssd_mixer_L{L}",
    )(*args)
    return res if emit_state else (res[0], None)


def _sgu_kernel(s_ref, g_ref, w_ref, b_ref, o_ref):
    uv = _gelu_tanh(s_ref[...])
    u = uv[:, :SGU_DIM]
    v = _rms(uv[:, SGU_DIM:]) * g_ref[...]
    lo_lane = lax.broadcasted_iota(jnp.int32, (CHUNK, LANES), 1) < (LANES // 2)
    for c in range(TOK_TILE // CHUNK):
        rows = slice(c * CHUNK, (c + 1) * CHUNK)
        mix = []
        for pr in range(SGU_HEADS // 2):
            vp = v[rows, pr * LANES:(pr + 1) * LANES]
            rhs = jnp.concatenate([jnp.where(lo_lane, vp, 0.0).astype(BF16),
                                   jnp.where(lo_lane, 0.0, vp).astype(BF16)], axis=0)
            mix.append(jnp.dot(w_ref[pr], rhs, preferred_element_type=F32))
        o_ref[rows, :] = u[rows, :] * (jnp.concatenate(mix, axis=1) + b_ref[...])


def _sgu_mixer(sgu, l, norm_g, w_pair, b_full):
    return pl.pallas_call(
        _sgu_kernel,
        grid=(N_TOK // TOK_TILE,),
        in_specs=[pl.BlockSpec((TOK_TILE, 2 * SGU_DIM), lambda i: (i, 0)),
                  _layer_spec(l, (1, SGU_DIM)),
                  _layer_spec(l, (SGU_HEADS // 2, CHUNK, 2 * CHUNK)),
                  _layer_spec(l, (CHUNK, SGU_DIM))],
        out_specs=pl.BlockSpec((TOK_TILE, SGU_DIM), lambda i: (i, 0)),
        out_shape=jax.ShapeDtypeStruct((N_TOK, SGU_DIM), F32),
        compiler_params=_params(),
        name="sgu_mixer",
    )(sgu, norm_g, w_pair, b_full)


_LOG2_CH = S5_GROUP_CH.bit_length() - 1
_LOG2_ST = S5_STATE.bit_length() - 1


def _cmul(ar, ai, br, bi):
    return ar * br - ai * bi, ar * bi + ai * br


def _s5prep_kernel(lre_ref, lim_ref, ldt_ref, btr_ref, bti_ref, ctr_ref, cti_ref,
                   bblk_ref, cblk_ref, pw_ref):
    brow = lax.broadcasted_iota(jnp.int32, (S5_DIM, S5_LANES), 0) >> _LOG2_CH
    bcol = lax.broadcasted_iota(jnp.int32, (S5_DIM, S5_LANES), 1) >> _LOG2_ST
    bmask = brow == bcol
    p1 = lax.broadcasted_iota(jnp.int32, (S5_PW_ROWS, S5_LANES), 0) + 1
    for d in range(N_DIR):
        lre = lre_ref[d]
        lim = lim_ref[d]
        step = jnp.exp(ldt_ref[d])
        mag = jnp.exp(lre * step)
        lbr = mag * jnp.cos(lim * step)
        lbi = mag * jnp.sin(lim * step)
        den = lre * lre + lim * lim
        nr = lbr - 1.0
        cr = (nr * lre + lbi * lim) / den
        ci = (lbi * lre - nr * lim) / den
        br, bi = _cmul(cr, ci, btr_ref[...], bti_ref[...])
        br = jnp.where(bmask, br, 0.0).astype(BF16)
        bi = jnp.where(bmask, bi, 0.0).astype(BF16)
        for sl in range(S5_NSLAB):
            bblk_ref[d, sl, :, 0:S5_SLAB] = br[:, sl * S5_SLAB:(sl + 1) * S5_SLAB]
            bblk_ref[d, sl, :, S5_SLAB:2 * S5_SLAB] = bi[:, sl * S5_SLAB:(sl + 1) * S5_SLAB]
        rr = jnp.ones((S5_PW_ROWS, S5_LANES), F32)
        ri = jnp.zeros((S5_PW_ROWS, S5_LANES), F32)
        sr, si = lbr, lbi
        for k in range(S5_PW_ROWS.bit_length()):
            bit = ((p1 >> k) & 1) == 1
            tr, ti = _cmul(rr, ri, sr, si)
            rr = jnp.where(bit, tr, rr)
            ri = jnp.where(bit, ti, ri)
            sr, si = _cmul(sr, si, sr, si)
        pw_ref[d, 0] = rr
        pw_ref[d, 1] = ri
    crow = lax.broadcasted_iota(jnp.int32, (S5_LANES, S5_DIM), 0) >> _LOG2_ST
    ccol = lax.broadcasted_iota(jnp.int32, (S5_LANES, S5_DIM), 1) >> _LOG2_CH
    cmask = crow == ccol
    cr = jnp.where(cmask, ctr_ref[...], 0.0).astype(BF16)
    ci = jnp.where(cmask, -cti_ref[...], 0.0).astype(BF16)
    for sl in range(S5_NSLAB):
        cblk_ref[sl, 0:S5_SLAB, :] = cr[sl * S5_SLAB:(sl + 1) * S5_SLAB, :]
        cblk_ref[sl, S5_SLAB:2 * S5_SLAB, :] = ci[sl * S5_SLAB:(sl + 1) * S5_SLAB, :]


def _s5_prep(lam_re, lam_im, log_dt, b_re, b_im, c_re, c_im):
    row = lambda t: t.reshape(DEPTH, N_DIR, 1, S5_LANES)
    ldt = jnp.repeat(log_dt, S5_STATE, axis=-1)
    bt = lambda t: jnp.tile(t.transpose(0, 3, 1, 2).reshape(DEPTH, S5_GROUP_CH, S5_LANES),
                            (1, S5_GROUPS, 1))
    ct = lambda t: jnp.tile(t.transpose(0, 1, 3, 2).reshape(DEPTH, S5_LANES, S5_GROUP_CH),
                            (1, 1, S5_GROUPS))
    vec = pl.BlockSpec((None, N_DIR, 1, S5_LANES), lambda l: (l, 0, 0, 0))
    bsp = pl.BlockSpec((None, S5_DIM, S5_LANES), lambda l: (l, 0, 0))
    csp = pl.BlockSpec((None, S5_LANES, S5_DIM), lambda l: (l, 0, 0))
    return pl.pallas_call(
        _s5prep_kernel,
        grid=(DEPTH,),
        in_specs=[vec, vec, vec, bsp, bsp, csp, csp],
        out_specs=[pl.BlockSpec((None, N_DIR, S5_NSLAB, S5_DIM, 2 * S5_SLAB), lambda l: (l, 0, 0, 0, 0)),
                   pl.BlockSpec((None, S5_NSLAB, 2 * S5_SLAB, S5_DIM), lambda l: (l, 0, 0, 0)),
                   pl.BlockSpec((None, N_DIR, 2, S5_PW_ROWS, S5_LANES), lambda l: (l, 0, 0, 0, 0))],
        out_shape=[jax.ShapeDtypeStruct((DEPTH, N_DIR, S5_NSLAB, S5_DIM, 2 * S5_SLAB), BF16),
                   jax.ShapeDtypeStruct((DEPTH, S5_NSLAB, 2 * S5_SLAB, S5_DIM), BF16),
                   jax.ShapeDtypeStruct((DEPTH, N_DIR, 2, S5_PW_ROWS, S5_LANES), F32)],
        compiler_params=_params(),
        name="s5_prep",
    )(row(lam_re), row(lam_im), row(ldt), bt(b_re), bt(b_im), ct(c_re), ct(c_im))


def _s5_moves(n_seq, n_seg, seq_len, col_major):
    steps = seq_len // n_seg
    if not col_major:
        return [(q * seq_len + s * steps, steps, q * n_seg + s, S5_SEG)
                for q in range(n_seq) for s in range(n_seg)]
    assert n_seq == 1 and n_seg == S5_SEG
    grid_rows = seq_len // GRID_W
    wseg = GRID_W // n_seg
    return [(r * GRID_W + s * wseg, wseg, r * S5_SEG + s, grid_rows * S5_SEG)
            for r in range(grid_rows) for s in range(n_seg)]


def _s5_kernel(*refs, n_seq, n_seg, seq_len, col_major, has_h0, n_alias, emit_state):
    u_ref, bblk_ref, cblk_ref, pw_ref, dvec_ref, gw_ref, gb_ref = refs[:7]
    k = 7
    h0r_ref, h0i_ref = (refs[k], refs[k + 1]) if has_h0 else (None, None)
    k += 2 * int(has_h0) + n_alias
    y_ref = refs[k]
    fr_ref, fi_ref = (refs[k + 1], refs[k + 2]) if emit_state else (None, None)
    up, buf_f, buf_b, yacc, yp, cin, fin = refs[k + 1 + 2 * int(emit_state):]
    assert n_seq * n_seg == S5_SEG and not (has_h0 and n_seg == 1) and not (emit_state and n_seg > 1)
    n_rows = n_seq * seq_len
    steps = seq_len // n_seg
    n_planes = S5_DIM // LANES
    W = S5_SLAB
    moves = _s5_moves(n_seq, n_seg, seq_len, col_major)

    for src, n, dst, stride in moves:
        for t in range(n_planes):
            up[t, pl.ds(dst, n, stride=stride), :] = u_ref[src:src + n, t * LANES:(t + 1) * LANES]

    def u_rows(rs):
        return jnp.concatenate([up[t, rs, :] for t in range(n_planes)], axis=1)

    zero = jnp.zeros((S5_SEG, W), F32)
    zrow = jnp.zeros((1, W), F32)
    for sl in range(S5_NSLAB):
        ln = slice(sl * W, (sl + 1) * W)
        for r0 in range(0, n_rows, ROW_SLAB):
            rs = slice(r0, r0 + ROW_SLAB)
            ub = u_rows(rs).astype(BF16)
            buf_f[rs, :] = jnp.dot(ub, bblk_ref[0, sl], preferred_element_type=F32)
            buf_b[rs, :] = jnp.dot(ub, bblk_ref[1, sl], preferred_element_type=F32)
        lam = [[jnp.broadcast_to(pw_ref[d, c, 0:1, ln], (S5_SEG, W)) for c in range(2)]
               for d in range(N_DIR)]

        def body(t, carry):
            fr, fi, br, bi = carry
            rf = pl.ds(pl.multiple_of(t * S5_SEG, S5_SEG), S5_SEG)
            rb = pl.ds(pl.multiple_of((steps - 1 - t) * S5_SEG, S5_SEG), S5_SEG)
            nfr = lam[0][0] * fr - lam[0][1] * fi + buf_f[rf, 0:W]
            nfi = lam[0][0] * fi + lam[0][1] * fr + buf_f[rf, W:2 * W]
            nbr = lam[1][0] * br - lam[1][1] * bi + buf_b[rb, 0:W]
            nbi = lam[1][0] * bi + lam[1][1] * br + buf_b[rb, W:2 * W]
            buf_f[rf, 0:W] = nfr
            buf_f[rf, W:2 * W] = nfi
            buf_b[rb, 0:W] = nbr
            buf_b[rb, W:2 * W] = nbi
            return nfr, nfi, nbr, nbi

        ends = lax.fori_loop(0, steps, body, (zero, zero, zero, zero), unroll=2)

        if n_seg == 1:
            if emit_state:
                for d in range(N_DIR):
                    fin[d, 0, :, ln] = ends[2 * d]
                    fin[d, 1, :, ln] = ends[2 * d + 1]
        else:
            for d in range(N_DIR):
                er, ei = ends[2 * d], ends[2 * d + 1]
                lpr = pw_ref[d, 0, steps - 1:steps, ln]
                lpi = pw_ref[d, 1, steps - 1:steps, ln]
                pr = h0r_ref[d:d + 1, ln] if has_h0 else zrow
                pi = h0i_ref[d:d + 1, ln] if has_h0 else zrow
                order = range(n_seg) if d == 0 else range(n_seg - 1, -1, -1)
                for s in order:
                    cin[d, 0, s:s + 1, :] = pr
                    cin[d, 1, s:s + 1, :] = pi
                    tr, ti = _cmul(lpr, lpi, pr, pi)
                    pr = er[s:s + 1, :] + tr
                    pi = ei[s:s + 1, :] + ti
            cfr = cin[0, 0]
            cfi = cin[0, 1]
            cbr = cin[1, 0]
            cbi = cin[1, 1]

            def fix(p, carry):
                rows = pl.ds(pl.multiple_of(p * S5_SEG, S5_SEG), S5_SEG)
                q = steps - 1 - p
                tfr, tfi = _cmul(pw_ref[0, 0, pl.ds(p, 1), ln], pw_ref[0, 1, pl.ds(p, 1), ln], cfr, cfi)
                tbr, tbi = _cmul(pw_ref[1, 0, pl.ds(q, 1), ln], pw_ref[1, 1, pl.ds(q, 1), ln], cbr, cbi)
                buf_f[rows, 0:W] = buf_f[rows, 0:W] + buf_b[rows, 0:W] + tfr + tbr
                buf_f[rows, W:2 * W] = buf_f[rows, W:2 * W] + buf_b[rows, W:2 * W] + tfi + tbi
                return carry

            lax.fori_loop(0, steps, fix, 0, unroll=2)

        for r0 in range(0, n_rows, ROW_SLAB):
            rs = slice(r0, r0 + ROW_SLAB)
            hs = buf_f[rs, :] if n_seg > 1 else buf_f[rs, :] + buf_b[rs, :]
            part = jnp.dot(hs.astype(BF16), cblk_ref[sl], preferred_element_type=F32)
            if sl == 0:
                yacc[rs, :] = part
            else:
                yacc[rs, :] = yacc[rs, :] + part

    gwb = gw_ref[...].astype(BF16)
    for r0 in range(0, n_rows, ROW_SLAB):
        rs = slice(r0, r0 + ROW_SLAB)
        y = _gelu_tanh(yacc[rs, :] + dvec_ref[...] * u_rows(rs))
        gate = jnp.dot(y.astype(BF16), gwb, preferred_element_type=F32) + gb_ref[...]
        y = y * jax.nn.sigmoid(gate)
        for t in range(n_planes):
            yp[t, rs, :] = y[:, t * LANES:(t + 1) * LANES]
    for src, n, dst, stride in moves:
        for t in range(n_planes):
            y_ref[src:src + n, t * LANES:(t + 1) * LANES] = yp[t, pl.ds(dst, n, stride=stride), :]
    if emit_state:
        for q in range(n_seq):
            for d in range(N_DIR):
                fr_ref[q, d:d + 1, :] = fin[d, 0, q:q + 1, :]
                fi_ref[q, d:d + 1, :] = fin[d, 1, q:q + 1, :]


def _s5_mixer(u, bblk, cblk, pw, prm, l, *, seq_len, n_seq, nb, blk0, col_major, h0=None, y_buf=None,
              st_buf=None, emit_state):
    n_rows = n_seq * seq_len
    n_seg = S5_SEG // n_seq
    seq = pl.BlockSpec((n_rows, S5_DIM), lambda b: (blk0 + b, 0))
    st = pl.BlockSpec((n_seq, None, N_DIR, S5_LANES), lambda b: (b, l, 0, 0))
    once = dict(pipeline_mode=pl.Buffered(1))
    args = [u, bblk, cblk, pw, prm["d_vec"], prm["glu_w"], prm["glu_b"]]
    in_specs = [seq, _layer_spec(l, (N_DIR, S5_NSLAB, S5_DIM, 2 * S5_SLAB), **once),
                _layer_spec(l, (S5_NSLAB, 2 * S5_SLAB, S5_DIM), **once),
                _layer_spec(l, (N_DIR, 2, S5_PW_ROWS, S5_LANES), **once), _layer_spec(l, (1, S5_DIM)),
                _layer_spec(l, (S5_DIM, S5_DIM)), _layer_spec(l, (1, S5_DIM))]
    if h0 is not None:
        h0_spec = pl.BlockSpec((None, None, N_DIR, S5_LANES), lambda b: (b, l, 0, 0))
        args += list(h0)
        in_specs += [h0_spec, h0_spec]
    out_shape = [jax.ShapeDtypeStruct((N_TOK, S5_DIM), F32)]
    out_specs = [seq]
    aliases = {}
    if y_buf is not None:
        aliases[len(args)] = 0
        args.append(y_buf)
        in_specs.append(_any_spec())
    if emit_state:
        out_shape += [jax.ShapeDtypeStruct((BATCH, DEPTH, N_DIR, S5_LANES), F32)] * 2
        out_specs += [st, st]
        if st_buf is not None:
            for j, buf in enumerate(st_buf):
                aliases[len(args)] = 1 + j
                args.append(buf)
                in_specs.append(_any_spec())
    res = pl.pallas_call(
        functools.partial(_s5_kernel, n_seq=n_seq, n_seg=n_seg, seq_len=seq_len, col_major=col_major,
                          has_h0=h0 is not None, n_alias=len(aliases), emit_state=emit_state),
        grid=(nb,),
        in_specs=in_specs,
        out_specs=out_specs,
        out_shape=out_shape,
        input_output_aliases=aliases,
        scratch_shapes=[pltpu.VMEM((S5_DIM // LANES, n_rows, LANES), F32),
                        pltpu.VMEM((n_rows, 2 * S5_SLAB), F32),
                        pltpu.VMEM((n_rows, 2 * S5_SLAB), F32),
                        pltpu.VMEM((n_rows, S5_DIM), F32),
                        pltpu.VMEM((S5_DIM // LANES, n_rows, LANES), F32),
                        pltpu.VMEM((N_DIR, 2, S5_SEG, S5_SLAB), F32),
                        pltpu.VMEM((N_DIR, 2, S5_SEG, S5_LANES), F32)],
        compiler_params=_params(),
        name=f"s5_mixer_L{seq_len}",
    )(*args)
    return (res[0], res[1], res[2]) if emit_state else (res[0], None, None)


def _outffn_kernel(xa_ref, xb_ref, ys_ref, yg_ref, y5_ref, g1_ref, sh_ref, sc_ref, g2_ref, ng_ref,
                   fg_ref, wo_ref, w1_ref, w2_ref, *o_refs, final):
    i = pl.program_id(0)
    r = _mod_row(i)
    row = lambda ref: ref[pl.ds(r, 1), :]
    x = jnp.where(i < N_CTX_TILES, xa_ref[...], xb_ref[...])
    mixed = jnp.concatenate([ys_ref[...].astype(BF16), yg_ref[...].astype(BF16),
                             y5_ref[...].astype(BF16)], axis=1)
    x1 = x + row(g1_ref) * jnp.dot(mixed, wo_ref[...], preferred_element_type=F32)
    h = ((_rms(x1) * ng_ref[...]) * (1.0 + row(sc_ref)) + row(sh_ref)).astype(BF16)
    acc = jnp.zeros((TOK_TILE, D_MODEL), F32)
    for j in range(D_FF // FF_SLAB):
        a = jnp.dot(h, w1_ref[:, j * FF_SLAB:(j + 1) * FF_SLAB], preferred_element_type=F32)
        a = jnp.maximum(a, 0.0)
        acc = acc + jnp.dot((a * a).astype(BF16), w2_ref[j * FF_SLAB:(j + 1) * FF_SLAB, :],
                            preferred_element_type=F32)
    x2 = x1 + row(g2_ref) * acc
    if not final:
        o_refs[0][...] = x2
    else:
        y = _rms(x2) * fg_ref[...]

        @pl.when(i < N_CTX_TILES)
        def _():
            o_refs[0][...] = y

        @pl.when(i >= N_CTX_TILES)
        def _():
            o_refs[1][...] = y


def _out_ffn(xa, xb, b_off, y_ssd, y_sgu, y_s5, l, mods, norm_g, final_g, w_out, w1, w2, *, final):
    tok = lambda w: pl.BlockSpec((TOK_TILE, w), lambda i: (i, 0))
    res = lambda s: _layer_spec(l, s, pipeline_mode=pl.Buffered(1))
    sa, sb = _split_tok_specs(D_MODEL, b_off)
    if final:
        out_specs = list(_split_tok_specs(D_MODEL, 0))
        out_shape = [jax.ShapeDtypeStruct((N_CTX, D_MODEL), F32), jax.ShapeDtypeStruct((N_LAT, D_MODEL), F32)]
    else:
        out_specs = [tok(D_MODEL)]
        out_shape = [jax.ShapeDtypeStruct((N_TOK, D_MODEL), F32)]
    return pl.pallas_call(
        functools.partial(_outffn_kernel, final=final),
        grid=(N_TOK // TOK_TILE,),
        in_specs=[sa, sb, tok(SSD_INNER), tok(SGU_DIM), tok(S5_DIM),
                  _mod_spec(l, 2), _mod_spec(l, 3), _mod_spec(l, 4), _mod_spec(l, 5),
                  _layer_spec(l, (1, D_MODEL)), pl.BlockSpec((1, D_MODEL), lambda i: (0, 0)),
                  res((D_MODEL, D_MODEL)), res((D_MODEL, D_FF)), res((D_FF, D_MODEL))],
        out_specs=out_specs,
        out_shape=out_shape,
        compiler_params=_params(),
        name="out_ffn_final" if final else "out_ffn",
    )(xa, xb, y_ssd, y_sgu, y_s5, mods, mods, mods, mods, norm_g, final_g, w_out, w1, w2)


def kernel(x_prompt, x_sample, state_ssd, state_s5_re, state_s5_im, c, c_ctx, ada_w, ada_b, norm1_g,
           norm2_g, w_in, ssd_conv_w, ssd_conv_b, ssd_dt_bias, ssd_a_log, ssd_d, ssd_norm_g,
           sgu_norm_g, sgu_w, sgu_b, s5_lambda_re, s5_lambda_im, s5_log_dt, s5_b_re, s5_b_im,
           s5_c_re, s5_c_im, s5_d, s5_glu_w, s5_glu_b, w_out, ffn_w1, ffn_w2, final_norm_g):
    cvec = jnp.concatenate([c_ctx[None, :], c,
                            jnp.zeros((MOD_ROWS - 1 - DEC_BATCH, D_MODEL), F32)], axis=0)
    mods = _adaln_mods(cvec, ada_w, ada_b)
    bblk, cblk, pw = _s5_prep(s5_lambda_re, s5_lambda_im, s5_log_dt, s5_b_re, s5_b_im,
                              s5_c_re, s5_c_im)

    w_in_b = w_in.astype(BF16)
    w_perm = jnp.concatenate(
        [w_in_b[:, :, 0:OFF_DT], w_in_b[:, :, OFF_SGU:IN_DIM], w_in_b[:, :, OFF_DT:OFF_SGU],
         jnp.zeros((DEPTH, D_MODEL, LANES - N_DIR * SSD_HEADS), BF16)], axis=2)
    w_out_b = w_out.astype(BF16)
    w1_b = ffn_w1.astype(BF16)
    w2_b = ffn_w2.astype(BF16)
    vec = lambda t: t.reshape(DEPTH, 1, -1)
    lane_pad = lambda t: jnp.pad(vec(t), ((0, 0), (0, 0), (0, LANES - t[0].size)))
    ssd_prm = dict(conv_w=ssd_conv_w, conv_b=vec(ssd_conv_b), dt_bias=lane_pad(ssd_dt_bias),
                   a_log=lane_pad(ssd_a_log), d_vec=vec(jnp.repeat(ssd_d, SSD_HEAD_DIM, axis=-1)),
                   norm_g=vec(ssd_norm_g))
    s5_prm = dict(d_vec=vec(s5_d), glu_w=s5_glu_w, glu_b=vec(s5_glu_b))
    sgu_w_pair = sgu_w.reshape(DEPTH, SGU_HEADS // 2, 2, CHUNK, CHUNK).transpose(0, 1, 3, 2, 4)
    sgu_w_pair = sgu_w_pair.reshape(DEPTH, SGU_HEADS // 2, CHUNK, 2 * CHUNK).astype(BF16)
    sgu_b_full = jnp.repeat(sgu_b.transpose(0, 2, 1), SGU_DIM // SGU_HEADS, axis=2)
    norm1 = vec(norm1_g)
    norm2 = vec(norm2_g)
    sgu_g = vec(sgu_norm_g)
    final_g = final_norm_g.reshape(1, D_MODEL)

    lat_ssd = state_ssd.reshape(DEC_BATCH, DEPTH, N_DIR, SSD_HP, SSD_STATE)
    lat_s5 = (state_s5_re.reshape(DEC_BATCH, DEPTH, N_DIR, S5_LANES),
              state_s5_im.reshape(DEC_BATCH, DEPTH, N_DIR, S5_LANES))
    lat_blk = N_CTX // DEC_SEQ

    xa = x_prompt.reshape(N_CTX, D_MODEL)
    xb = x_sample.reshape(N_LAT, D_MODEL)
    b_off = 0
    st_ssd = None
    st_s5 = None
    for l in range(DEPTH):
        z, xbc, sgu, s5u, dtr = _in_proj(xa, xb, b_off, l, mods, norm1, w_perm)
        y_ssd, st_ssd = _ssd_mixer(z, xbc, dtr, ssd_prm, l, L=SEQ, nb=BATCH, blk0=0, st_buf=st_ssd,
                                   emit_state=True)
        y_ssd, _ = _ssd_mixer(z, xbc, dtr, ssd_prm, l, L=DEC_SEQ, nb=DEC_BATCH, blk0=lat_blk,
                              h0=lat_ssd, y_buf=y_ssd, emit_state=False)
        y_sgu = _sgu_mixer(sgu, l, sgu_g, sgu_w_pair, sgu_b_full)
        y_s5, fr, fi = _s5_mixer(s5u, bblk, cblk, pw, s5_prm, l, seq_len=SEQ, n_seq=S5_SEG,
                                 nb=BATCH // S5_SEG, blk0=0, col_major=False, st_buf=st_s5,
                                 emit_state=True)
        st_s5 = (fr, fi)
        y_s5, _, _ = _s5_mixer(s5u, bblk, cblk, pw, s5_prm, l, seq_len=DEC_SEQ, n_seq=1, nb=DEC_BATCH,
                               blk0=lat_blk, col_major=True, h0=lat_s5, y_buf=y_s5, emit_state=False)
        final = l == DEPTH - 1
        out = _out_ffn(xa, xb, b_off, y_ssd, y_sgu, y_s5, l, mods, norm2, final_g, w_out_b, w1_b,
                       w2_b, final=final)
        if not final:
            xa = xb = out[0]
            b_off = N_CTX_TILES

    y_prompt = out[0].reshape(BATCH, SEQ, D_MODEL)
    y_sample = out[1].reshape(DEC_BATCH, DEC_SEQ, D_MODEL)
    new_state_ssd = st_ssd.reshape(BATCH, DEPTH, N_DIR, SSD_HEADS, SSD_HEAD_DIM, SSD_STATE)
    s5_shape = (BATCH, DEPTH, N_DIR, S5_GROUPS, S5_STATE)
    return (y_prompt, y_sample, new_state_ssd, st_s5[0].reshape(s5_shape), st_s5[1].reshape(s5_shape))
```

```python
import functools
import math

import jax
import jax.numpy as jnp
from jax import lax
from jax.experimental import pallas as pl
from jax.experimental.pallas import tpu as pltpu

F32 = jnp.float32
BF16 = jnp.bfloat16

D_MODEL = 1024
BATCH = 16
SEQ = 256
DEPTH = 2
DEC_BATCH = 2
DEC_SEQ = 1024
GRID_W = 64
CHUNK = 128
N_DIR = 2
EPS = 1e-6
SSD_INNER = 512
SSD_HEAD_DIM = 64
SSD_HEADS = 8
SSD_GROUPS = 2
SSD_STATE = 128
SSD_CONV = 5
SSD_CONV_DIM = SSD_INNER + 2 * SSD_GROUPS * SSD_STATE
SGU_DIM = 256
SGU_HEADS = 4
S5_DIM = 256
S5_GROUP_CH = 16
S5_GROUPS = 16
S5_STATE = 64
D_FF = 4 * D_MODEL
OFF_XBC = SSD_INNER
OFF_DT = OFF_XBC + SSD_CONV_DIM
OFF_SGU = OFF_DT + N_DIR * SSD_HEADS
OFF_S5 = OFF_SGU + 2 * SGU_DIM
IN_DIM = OFF_S5 + S5_DIM

N_CTX = BATCH * SEQ
N_LAT = DEC_BATCH * DEC_SEQ
N_TOK = N_CTX + N_LAT
LANES = 128
SUBLANES = 8
MOD_ROWS = SUBLANES
SSD_HP = SSD_HEADS * SSD_HEAD_DIM
S5_LANES = S5_GROUPS * S5_STATE
S5_SEG = SUBLANES
S5_PW_ROWS = DEC_SEQ // S5_SEG
S5_SLAB = 256
S5_NSLAB = S5_LANES // S5_SLAB
PAIR = 2 * SSD_HEAD_DIM
N_PAIR = SSD_HEADS // 2
TOK_TILE = 256
N_CTX_TILES = N_CTX // TOK_TILE
FFN_TILE = 512
ROW_SLAB = 256
CONV_LANES = 256
FF_SLAB = 1024
VMEM_LIMIT = 56 * 1024 * 1024

_NT = (((1,), (1,)), ((), ()))
_TN = (((0,), (0,)), ((), ()))


def _params(n_axes=1):
    return pltpu.CompilerParams(dimension_semantics=("arbitrary",) * n_axes,
                                vmem_limit_bytes=VMEM_LIMIT)


def _layer_spec(l, shape, **kw):
    return pl.BlockSpec((None,) + tuple(shape), lambda *_: (l,) + (0,) * len(shape), **kw)


def _mod_spec(l, k):
    return pl.BlockSpec((None, MOD_ROWS, D_MODEL), lambda *_: (l, 0, k))


def _any_spec():
    return pl.BlockSpec(memory_space=pl.ANY)


def _mod_row(i, tm):
    n_ctx = N_CTX // tm
    return jnp.where(i < n_ctx, 0, 1 + (i - n_ctx) // (DEC_SEQ // tm))


def _split_tok_specs(tm, width, b_has_ctx_rows):
    n_ctx = N_CTX // tm
    b_off = n_ctx if b_has_ctx_rows else 0
    a = pl.BlockSpec((tm, width), lambda i: (jnp.minimum(i, n_ctx - 1), 0))
    b = pl.BlockSpec((tm, width), lambda i: (jnp.maximum(i, n_ctx) - n_ctx + b_off, 0))
    return a, b


def _silu(x):
    return x * jax.nn.sigmoid(x)


def _gelu_tanh(x):
    c = math.sqrt(2.0 / math.pi)
    return 0.5 * x * (1.0 + jnp.tanh(c * (x + 0.044715 * (x * x * x))))


def _rms(x):
    return x * lax.rsqrt(jnp.mean(x * x, axis=-1, keepdims=True) + EPS)


def _bdot(a, b):
    return jnp.dot(a.astype(BF16), b.astype(BF16), preferred_element_type=F32)


def _split3(x):
    hi = x.astype(BF16)
    r = x - hi.astype(F32)
    mid = r.astype(BF16)
    lo = (r - mid.astype(F32)).astype(BF16)
    return jnp.concatenate([hi, mid, lo], axis=1)


def _mod_kernel(c_ref, w_ref, b_ref, o_ref):
    o_ref[...] = _bdot(_silu(c_ref[...]), w_ref[...]) + b_ref[...]


def _adaln_mods(cvec, ada_w, ada_b):
    n_blk = 6
    return pl.pallas_call(
        _mod_kernel,
        grid=(DEPTH, n_blk),
        in_specs=[pl.BlockSpec((MOD_ROWS, D_MODEL), lambda l, j: (0, 0)),
                  pl.BlockSpec((None, D_MODEL, D_MODEL), lambda l, j: (l, 0, j)),
                  pl.BlockSpec((None, 1, D_MODEL), lambda l, j: (l, 0, j))],
        out_specs=pl.BlockSpec((None, MOD_ROWS, D_MODEL), lambda l, j: (l, 0, j)),
        out_shape=jax.ShapeDtypeStruct((DEPTH, MOD_ROWS, 6 * D_MODEL), F32),
        compiler_params=_params(2),
        name="adaln_mod",
    )(cvec, ada_w, ada_b.reshape(DEPTH, 1, 6 * D_MODEL))


_C_Z = 0
_C_XBC = _C_Z + SSD_INNER
_C_SGU = _C_XBC + SSD_CONV_DIM
_C_S5 = _C_SGU + 2 * SGU_DIM
_C_DT = _C_S5 + S5_DIM
_C_END = _C_DT + LANES


def _inproj_kernel(xa_ref, xb_ref, sh_ref, sc_ref, g_ref, w_ref, sg_ref, sw_ref, sb_ref,
                   z_ref, xbc_ref, ysgu_ref, s5_ref, dt_ref):
    i = pl.program_id(0)
    r = _mod_row(i, TOK_TILE)
    x = jnp.where(i < N_CTX_TILES, xa_ref[...], xb_ref[...])
    shift = sh_ref[pl.ds(r, 1), :]
    scale = sc_ref[pl.ds(r, 1), :]
    h = (_rms(x) * g_ref[...]) * (1.0 + scale) + shift
    hb = h.astype(BF16)
    z_ref[...] = jnp.dot(hb, w_ref[:, _C_Z:_C_XBC], preferred_element_type=F32)
    xbc_ref[...] = jnp.dot(hb, w_ref[:, _C_XBC:_C_SGU], preferred_element_type=F32)
    s5_ref[...] = jnp.dot(hb, w_ref[:, _C_S5:_C_DT], preferred_element_type=F32)
    dt_ref[...] = jnp.dot(hb, w_ref[:, _C_DT:_C_END], preferred_element_type=F32)

    uv = _gelu_tanh(jnp.dot(hb, w_ref[:, _C_SGU:_C_S5], preferred_element_type=F32))
    u = uv[:, :SGU_DIM]
    v = _rms(uv[:, SGU_DIM:]) * sg_ref[...]
    lo_lane = lax.broadcasted_iota(jnp.int32, (CHUNK, LANES), 1) < (LANES // 2)
    for c in range(TOK_TILE // CHUNK):
        rows = slice(c * CHUNK, (c + 1) * CHUNK)
        mix = []
        for pr in range(SGU_HEADS // 2):
            vp = v[rows, pr * LANES:(pr + 1) * LANES]
            rhs = jnp.concatenate([jnp.where(lo_lane, vp, 0.0).astype(BF16),
                                   jnp.where(lo_lane, 0.0, vp).astype(BF16)], axis=0)
            mix.append(jnp.dot(sw_ref[pr], rhs, preferred_element_type=F32))
        ysgu_ref[rows, :] = u[rows, :] * (jnp.concatenate(mix, axis=1) + sb_ref[...])


def _in_proj(xa, xb, b_is_stream, l, mods, norm_g, w_perm, sgu_g, sgu_w_pair, sgu_b_full):
    tok = lambda w: pl.BlockSpec((TOK_TILE, w), lambda i: (i, 0))
    widths = (SSD_INNER, SSD_CONV_DIM, SGU_DIM, S5_DIM, LANES)
    sa, sb = _split_tok_specs(TOK_TILE, D_MODEL, b_is_stream)
    return pl.pallas_call(
        _inproj_kernel,
        grid=(N_TOK // TOK_TILE,),
        in_specs=[sa, sb, _mod_spec(l, 0), _mod_spec(l, 1), _layer_spec(l, (1, D_MODEL)),
                  _layer_spec(l, (D_MODEL, _C_END), pipeline_mode=pl.Buffered(1)),
                  _layer_spec(l, (1, SGU_DIM)), _layer_spec(l, (SGU_HEADS // 2, CHUNK, 2 * CHUNK)),
                  _layer_spec(l, (CHUNK, SGU_DIM))],
        out_specs=[tok(w) for w in widths],
        out_shape=[jax.ShapeDtypeStruct((N_TOK, w), F32) for w in widths],
        compiler_params=_params(),
        name="in_proj",
    )(xa, xb, mods, mods, norm_g, w_perm, sgu_g, sgu_w_pair, sgu_b_full)


def _ssd_kernel(*refs, L, has_h0, n_alias, emit_state):
    z_ref, xbc_ref, dt_ref, cw_ref, cb_ref, dtb_ref, alog_ref, dvec_ref, ng_ref = refs[:9]
    k = 9
    h0_ref = refs[k] if has_h0 else None
    k += int(has_h0) + n_alias
    y_ref = refs[k]
    hout_ref = refs[k + 1] if emit_state else None
    xpad, xc, acol, atr, cbs, bmt, dtsp, yacc, hst = refs[k + 1 + int(emit_state):]
    Q = CHUNK
    nc = L // Q
    halo = SUBLANES
    pad = (SSD_CONV - 1) // 2

    xpad[0:halo, :] = jnp.zeros((halo, SSD_CONV_DIM), F32)
    xpad[halo + L:2 * halo + L, :] = jnp.zeros((halo, SSD_CONV_DIM), F32)
    for c in range(nc):
        xpad[halo + c * Q:halo + (c + 1) * Q, :] = xbc_ref[c * Q:(c + 1) * Q, :]
    win = Q + 2 * halo
    for c in range(nc):
        for lb in range(0, SSD_CONV_DIM, CONV_LANES):
            ln = slice(lb, lb + CONV_LANES)
            xa = xpad[c * Q:c * Q + win, ln]
            acc = cb_ref[:, ln] + cw_ref[pad:pad + 1, ln] * xa[halo:halo + Q, :]
            for t in range(SSD_CONV):
                if t != pad:
                    rolled = pltpu.roll(xa, (pad - t) % win, 0)
                    acc = acc + cw_ref[t:t + 1, ln] * rolled[halo:halo + Q, :]
            xc[c * Q:(c + 1) * Q, ln] = _silu(acc)

    raw = dt_ref[...] + dtb_ref[...]
    dt = jnp.maximum(raw, 0.0) + jnp.log(1.0 + jnp.exp(-jnp.abs(raw)))
    dtsp[...] = dt
    a_neg = -jnp.exp(alog_ref[...])
    row = lax.broadcasted_iota(jnp.int32, (Q, Q), 0)
    col = lax.broadcasted_iota(jnp.int32, (Q, Q), 1)
    lower = row >= col
    upper = col >= row
    tri_l = lower.astype(F32)
    tri_u = upper.astype(F32)
    fwd_lane = lax.broadcasted_iota(jnp.int32, (Q, LANES), 1) < SSD_HEADS
    for c in range(nc):
        dta = dtsp[c * Q:(c + 1) * Q, :] * a_neg
        pre = jnp.dot(tri_l, dta, precision=lax.Precision.HIGHEST, preferred_element_type=F32)
        suf = jnp.dot(tri_u, dta, precision=lax.Precision.HIGHEST, preferred_element_type=F32)
        a = jnp.where(fwd_lane, pre, suf)
        acol[c * Q:(c + 1) * Q, :] = a
        atr[c] = a.T[0:N_DIR * SSD_HEADS, :]
        for g in range(SSD_GROUPS):
            b0 = SSD_INNER + g * SSD_STATE
            c0 = SSD_INNER + (SSD_GROUPS + g) * SSD_STATE
            bm = xc[c * Q:(c + 1) * Q, b0:b0 + SSD_STATE]
            cbs[c, g] = lax.dot_general(xc[c * Q:(c + 1) * Q, c0:c0 + SSD_STATE].astype(BF16),
                                        bm.astype(BF16), _NT, preferred_element_type=F32)
            bmt[c, g] = bm.T.astype(BF16)

    sel_row = lax.broadcasted_iota(jnp.int32, (3 * LANES, SSD_INNER), 0) & (LANES - 1)
    sel_head = lax.broadcasted_iota(jnp.int32, (3 * LANES, SSD_INNER), 1) >> (SSD_HEAD_DIM.bit_length() - 1)
    sel = [(sel_row == sel_head + d * SSD_HEADS).astype(BF16) for d in range(N_DIR)]
    mask2 = [jnp.concatenate([m, m], axis=1) for m in (lower, upper)]

    for d in range(N_DIR):
        for p in range(N_PAIR):
            if has_h0:
                hst[d, p] = h0_ref[d, p * PAIR:(p + 1) * PAIR, :].T
            else:
                hst[d, p] = jnp.zeros((SSD_STATE, PAIR), F32)

    lo_lane = lax.broadcasted_iota(jnp.int32, (Q, PAIR), 1) < SSD_HEAD_DIM

    def chunk(c, d):
        r0 = pl.multiple_of(c * Q, Q)
        rows = pl.ds(r0, Q)
        a = acol[rows, :]
        a_t = atr[c]
        dtp = jnp.dot(_split3(dtsp[rows, :]), sel[d], preferred_element_type=F32)
        a_end = a[Q - 1:Q, :] if d == 0 else a[0:1, :]
        dec = jnp.exp(a_end)
        for g in range(SSD_GROUPS):
            c0 = SSD_INNER + (SSD_GROUPS + g) * SSD_STATE
            cmb = xc[rows, c0:c0 + SSD_STATE].astype(BF16)
            cb = cbs[c, g]
            cb2 = jnp.concatenate([cb, cb], axis=1)
            for pr in range(2):
                p = g * 2 + pr
                j0 = d * SSD_HEADS + 2 * p
                j1 = j0 + 1
                xs = xc[rows, p * PAIR:(p + 1) * PAIR]
                ab0 = jnp.broadcast_to(a[:, j0:j0 + 1], (Q, Q))
                ab1 = jnp.broadcast_to(a[:, j1:j1 + 1], (Q, Q))
                seg = (jnp.concatenate([ab0, ab1], axis=1)
                       - jnp.concatenate([a_t[j0:j0 + 1, :], a_t[j1:j1 + 1, :]], axis=1))
                m = (cb2 * jnp.exp(jnp.where(mask2[d], seg, -jnp.inf))).astype(BF16)
                xdt = xs * dtp[:, p * PAIR:(p + 1) * PAIR]
                rhs = jnp.concatenate([jnp.where(lo_lane, xdt, 0.0).astype(BF16),
                                       jnp.where(lo_lane, 0.0, xdt).astype(BF16)], axis=0)
                y_diag = jnp.dot(m, rhs, preferred_element_type=F32)
                hp = hst[d, p]
                y_off = jnp.dot(cmb, hp.astype(BF16), preferred_element_type=F32)
                a_pair = jnp.where(lo_lane, ab0, ab1)
                y = y_diag + jnp.exp(a_pair) * y_off
                a_end_pair = jnp.where(lo_lane[0:1, :], a_end[:, j0:j0 + 1], a_end[:, j1:j1 + 1])
                xw = (xdt * jnp.exp(a_end_pair - a_pair)).astype(BF16)
                s_new = jnp.dot(bmt[c, g], xw, preferred_element_type=F32)
                decp = jnp.where(lo_lane[0:1, :], dec[:, j0:j0 + 1], dec[:, j1:j1 + 1])
                hst[d, p] = decp * hp + s_new
                if d == 0:
                    y = y + dvec_ref[:, p * PAIR:(p + 1) * PAIR] * xs
                yacc[d, rows, p * PAIR:(p + 1) * PAIR] = y

    def both(t, carry):
        chunk(t, 0)
        chunk(nc - 1 - t, 1)
        return carry

    lax.fori_loop(0, nc, both, 0)
    for c in range(nc):
        rows = slice(c * Q, (c + 1) * Q)
        y = (yacc[0, rows, :] + yacc[1, rows, :]) * _silu(z_ref[rows, :])
        y_ref[rows, :] = _rms(y) * ng_ref[...]
    if emit_state:
        for d in range(N_DIR):
            for p in range(N_PAIR):
                hout_ref[d, p * PAIR:(p + 1) * PAIR, :] = hst[d, p].T


def _ssd_mixer(z, xbc, dtr, prm, l, *, L, nb, blk0, h0=None, y_buf=None, st_buf=None, emit_state):
    seq = lambda w: pl.BlockSpec((L, w), lambda b: (blk0 + b, 0))
    st = pl.BlockSpec((None, None, N_DIR, SSD_HP, SSD_STATE), lambda b: (b, l, 0, 0, 0))
    args = [z, xbc, dtr, prm["conv_w"], prm["conv_b"], prm["dt_bias"], prm["a_log"], prm["d_vec"],
            prm["norm_g"]]
    in_specs = [seq(SSD_INNER), seq(SSD_CONV_DIM), seq(LANES),
                _layer_spec(l, (SSD_CONV, SSD_CONV_DIM)), _layer_spec(l, (1, SSD_CONV_DIM)),
                _layer_spec(l, (1, LANES)), _layer_spec(l, (1, LANES)),
                _layer_spec(l, (1, SSD_INNER)), _layer_spec(l, (1, SSD_INNER))]
    if h0 is not None:
        args.append(h0)
        in_specs.append(st)
    out_shape = [jax.ShapeDtypeStruct((N_TOK, SSD_INNER), F32)]
    out_specs = [seq(SSD_INNER)]
    aliases = {}
    if y_buf is not None:
        aliases[len(args)] = 0
        args.append(y_buf)
        in_specs.append(_any_spec())
    if emit_state:
        out_shape.append(jax.ShapeDtypeStruct((BATCH, DEPTH, N_DIR, SSD_HP, SSD_STATE), F32))
        out_specs.append(st)
        if st_buf is not None:
            aliases[len(args)] = 1
            args.append(st_buf)
            in_specs.append(_any_spec())
    res = pl.pallas_call(
        functools.partial(_ssd_kernel, L=L, has_h0=h0 is not None, n_alias=len(aliases),
                          emit_state=emit_state),
        grid=(nb,),
        in_specs=in_specs,
        out_specs=out_specs,
        out_shape=out_shape,
        input_output_aliases=aliases,
        scratch_shapes=[pltpu.VMEM((L + 2 * SUBLANES, SSD_CONV_DIM), F32),
                        pltpu.VMEM((L, SSD_CONV_DIM), F32),
                        pltpu.VMEM((L, LANES), F32),
                        pltpu.VMEM((L // CHUNK, N_DIR * SSD_HEADS, CHUNK), F32),
                        pltpu.VMEM((L // CHUNK, SSD_GROUPS, CHUNK, CHUNK), F32),
                        pltpu.VMEM((L // CHUNK, SSD_GROUPS, SSD_STATE, CHUNK), BF16),
                        pltpu.VMEM((L, LANES), F32),
                        pltpu.VMEM((N_DIR, L, SSD_INNER), F32),
                        pltpu.VMEM((N_DIR, N_PAIR, PAIR, SSD_STATE), F32)],
        compiler_params=_params(),
        name=f"ssd_mixer_L{L}",
    )(*args)
    return res if emit_state else (res[0], None)


_LOG2_CH = S5_GROUP_CH.bit_length() - 1
_LOG2_ST = S5_STATE.bit_length() - 1


def _cmul(ar, ai, br, bi):
    return ar * br - ai * bi, ar * bi + ai * br


def _s5prep_kernel(lre_ref, lim_ref, ldt_ref, btr_ref, bti_ref, ctr_ref, cti_ref,
                   bblk_ref, cblk_ref, pw_ref):
    brow = lax.broadcasted_iota(jnp.int32, (S5_DIM, S5_LANES), 0) >> _LOG2_CH
    bcol = lax.broadcasted_iota(jnp.int32, (S5_DIM, S5_LANES), 1) >> _LOG2_ST
    bmask = brow == bcol
    p1 = lax.broadcasted_iota(jnp.int32, (S5_PW_ROWS, S5_LANES), 0) + 1
    for d in range(N_DIR):
        lre = lre_ref[d]
        lim = lim_ref[d]
        step = jnp.exp(ldt_ref[d])
        mag = jnp.exp(lre * step)
        lbr = mag * jnp.cos(lim * step)
        lbi = mag * jnp.sin(lim * step)
        den = lre * lre + lim * lim
        nr = lbr - 1.0
        cr = (nr * lre + lbi * lim) / den
        ci = (lbi * lre - nr * lim) / den
        br, bi = _cmul(cr, ci, btr_ref[...], bti_ref[...])
        br = jnp.where(bmask, br, 0.0).astype(BF16)
        bi = jnp.where(bmask, bi, 0.0).astype(BF16)
        for sl in range(S5_NSLAB):
            bblk_ref[d, sl, :, 0:S5_SLAB] = br[:, sl * S5_SLAB:(sl + 1) * S5_SLAB]
            bblk_ref[d, sl, :, S5_SLAB:2 * S5_SLAB] = bi[:, sl * S5_SLAB:(sl + 1) * S5_SLAB]
        rr = jnp.ones((S5_PW_ROWS, S5_LANES), F32)
        ri = jnp.zeros((S5_PW_ROWS, S5_LANES), F32)
        sr, si = lbr, lbi
        for k in range(S5_PW_ROWS.bit_length()):
            bit = ((p1 >> k) & 1) == 1
            tr, ti = _cmul(rr, ri, sr, si)
            rr = jnp.where(bit, tr, rr)
            ri = jnp.where(bit, ti, ri)
            sr, si = _cmul(sr, si, sr, si)
        pw_ref[d, 0] = rr
        pw_ref[d, 1] = ri
    crow = lax.broadcasted_iota(jnp.int32, (S5_LANES, S5_DIM), 0) >> _LOG2_ST
    ccol = lax.broadcasted_iota(jnp.int32, (S5_LANES, S5_DIM), 1) >> _LOG2_CH
    cmask = crow == ccol
    cr = jnp.where(cmask, ctr_ref[...], 0.0).astype(BF16)
    ci = jnp.where(cmask, -cti_ref[...], 0.0).astype(BF16)
    for sl in range(S5_NSLAB):
        cblk_ref[sl, 0:S5_SLAB, :] = cr[sl * S5_SLAB:(sl + 1) * S5_SLAB, :]
        cblk_ref[sl, S5_SLAB:2 * S5_SLAB, :] = ci[sl * S5_SLAB:(sl + 1) * S5_SLAB, :]


def _s5_prep(lam_re, lam_im, log_dt, b_re, b_im, c_re, c_im):
    row = lambda t: t.reshape(DEPTH, N_DIR, 1, S5_LANES)
    ldt = jnp.repeat(log_dt, S5_STATE, axis=-1)
    bt = lambda t: jnp.tile(t.transpose(0, 3, 1, 2).reshape(DEPTH, S5_GROUP_CH, S5_LANES),
                            (1, S5_GROUPS, 1))
    ct = lambda t: jnp.tile(t.transpose(0, 1, 3, 2).reshape(DEPTH, S5_LANES, S5_GROUP_CH),
                            (1, 1, S5_GROUPS))
    vec = pl.BlockSpec((None, N_DIR, 1, S5_LANES), lambda l: (l, 0, 0, 0))
    bsp = pl.BlockSpec((None, S5_DIM, S5_LANES), lambda l: (l, 0, 0))
    csp = pl.BlockSpec((None, S5_LANES, S5_DIM), lambda l: (l, 0, 0))
    return pl.pallas_call(
        _s5prep_kernel,
        grid=(DEPTH,),
        in_specs=[vec, vec, vec, bsp, bsp, csp, csp],
        out_specs=[pl.BlockSpec((None, N_DIR, S5_NSLAB, S5_DIM, 2 * S5_SLAB), lambda l: (l, 0, 0, 0, 0)),
                   pl.BlockSpec((None, S5_NSLAB, 2 * S5_SLAB, S5_DIM), lambda l: (l, 0, 0, 0)),
                   pl.BlockSpec((None, N_DIR, 2, S5_PW_ROWS, S5_LANES), lambda l: (l, 0, 0, 0, 0))],
        out_shape=[jax.ShapeDtypeStruct((DEPTH, N_DIR, S5_NSLAB, S5_DIM, 2 * S5_SLAB), BF16),
                   jax.ShapeDtypeStruct((DEPTH, S5_NSLAB, 2 * S5_SLAB, S5_DIM), BF16),
                   jax.ShapeDtypeStruct((DEPTH, N_DIR, 2, S5_PW_ROWS, S5_LANES), F32)],
        compiler_params=_params(),
        name="s5_prep",
    )(row(lam_re), row(lam_im), row(ldt), bt(b_re), bt(b_im), ct(c_re), ct(c_im))


def _s5_moves(n_seq, n_seg, seq_len, col_major):
    steps = seq_len // n_seg
    if not col_major:
        return [(q * seq_len + s * steps, steps, q * n_seg + s, S5_SEG)
                for q in range(n_seq) for s in range(n_seg)]
    assert n_seq == 1 and n_seg == S5_SEG
    grid_rows = seq_len // GRID_W
    wseg = GRID_W // n_seg
    return [(r * GRID_W + s * wseg, wseg, r * S5_SEG + s, grid_rows * S5_SEG)
            for r in range(grid_rows) for s in range(n_seg)]


def _s5_kernel(*refs, n_seq, n_seg, seq_len, col_major, has_h0, n_alias, emit_state):
    u_ref, bblk_ref, cblk_ref, pw_ref, dvec_ref, gw_ref, gb_ref = refs[:7]
    k = 7
    h0r_ref, h0i_ref = (refs[k], refs[k + 1]) if has_h0 else (None, None)
    k += 2 * int(has_h0) + n_alias
    y_ref = refs[k]
    fr_ref, fi_ref = (refs[k + 1], refs[k + 2]) if emit_state else (None, None)
    up, buf_f, buf_b, yacc, yp, cin, fin = refs[k + 1 + 2 * int(emit_state):]
    assert n_seq * n_seg == S5_SEG and not (has_h0 and n_seg == 1) and not (emit_state and n_seg > 1)
    n_rows = n_seq * seq_len
    steps = seq_len // n_seg
    n_planes = S5_DIM // LANES
    W = S5_SLAB
    moves = _s5_moves(n_seq, n_seg, seq_len, col_major)

    for src, n, dst, stride in moves:
        for t in range(n_planes):
            up[t, pl.ds(dst, n, stride=stride), :] = u_ref[src:src + n, t * LANES:(t + 1) * LANES]

    def u_rows(rs):
        return jnp.concatenate([up[t, rs, :] for t in range(n_planes)], axis=1)

    zero = jnp.zeros((S5_SEG, W), F32)
    zrow = jnp.zeros((1, W), F32)
    for sl in range(S5_NSLAB):
        ln = slice(sl * W, (sl + 1) * W)
        for r0 in range(0, n_rows, ROW_SLAB):
            rs = slice(r0, r0 + ROW_SLAB)
            ub = u_rows(rs).astype(BF16)
            buf_f[rs, :] = jnp.dot(ub, bblk_ref[0, sl], preferred_element_type=F32)
            buf_b[rs, :] = jnp.dot(ub, bblk_ref[1, sl], preferred_element_type=F32)
        lam = [[jnp.broadcast_to(pw_ref[d, c, 0:1, ln], (S5_SEG, W)) for c in range(2)]
               for d in range(N_DIR)]

        def body(t, carry):
            fr, fi, br, bi = carry
            rf = pl.ds(pl.multiple_of(t * S5_SEG, S5_SEG), S5_SEG)
            rb = pl.ds(pl.multiple_of((steps - 1 - t) * S5_SEG, S5_SEG), S5_SEG)
            nfr = lam[0][0] * fr - lam[0][1] * fi + buf_f[rf, 0:W]
            nfi = lam[0][0] * fi + lam[0][1] * fr + buf_f[rf, W:2 * W]
            nbr = lam[1][0] * br - lam[1][1] * bi + buf_b[rb, 0:W]
            nbi = lam[1][0] * bi + lam[1][1] * br + buf_b[rb, W:2 * W]
            buf_f[rf, 0:W] = nfr
            buf_f[rf, W:2 * W] = nfi
            buf_b[rb, 0:W] = nbr
            buf_b[rb, W:2 * W] = nbi
            return nfr, nfi, nbr, nbi

        ends = lax.fori_loop(0, steps, body, (zero, zero, zero, zero), unroll=2)

        if n_seg == 1:
            if emit_state:
                for d in range(N_DIR):
                    fin[d, 0, :, ln] = ends[2 * d]
                    fin[d, 1, :, ln] = ends[2 * d + 1]
        else:
            for d in range(N_DIR):
                er, ei = ends[2 * d], ends[2 * d + 1]
                lpr = pw_ref[d, 0, steps - 1:steps, ln]
                lpi = pw_ref[d, 1, steps - 1:steps, ln]
                pr = h0r_ref[d:d + 1, ln] if has_h0 else zrow
                pi = h0i_ref[d:d + 1, ln] if has_h0 else zrow
                order = range(n_seg) if d == 0 else range(n_seg - 1, -1, -1)
                for s in order:
                    cin[d, 0, s:s + 1, :] = pr
                    cin[d, 1, s:s + 1, :] = pi
                    tr, ti = _cmul(lpr, lpi, pr, pi)
                    pr = er[s:s + 1, :] + tr
                    pi = ei[s:s + 1, :] + ti
            cfr = cin[0, 0]
            cfi = cin[0, 1]
            cbr = cin[1, 0]
            cbi = cin[1, 1]

            def fix(p, carry):
                rows = pl.ds(pl.multiple_of(p * S5_SEG, S5_SEG), S5_SEG)
                q = steps - 1 - p
                tfr, tfi = _cmul(pw_ref[0, 0, pl.ds(p, 1), ln], pw_ref[0, 1, pl.ds(p, 1), ln], cfr, cfi)
                tbr, tbi = _cmul(pw_ref[1, 0, pl.ds(q, 1), ln], pw_ref[1, 1, pl.ds(q, 1), ln], cbr, cbi)
                buf_f[rows, 0:W] = buf_f[rows, 0:W] + buf_b[rows, 0:W] + tfr + tbr
                buf_f[rows, W:2 * W] = buf_f[rows, W:2 * W] + buf_b[rows, W:2 * W] + tfi + tbi
                return carry

            lax.fori_loop(0, steps, fix, 0, unroll=2)

        for r0 in range(0, n_rows, ROW_SLAB):
            rs = slice(r0, r0 + ROW_SLAB)
            hs = buf_f[rs, :] if n_seg > 1 else buf_f[rs, :] + buf_b[rs, :]
            part = jnp.dot(hs.astype(BF16), cblk_ref[sl], preferred_element_type=F32)
            if sl == 0:
                yacc[rs, :] = part
            else:
                yacc[rs, :] = yacc[rs, :] + part

    gwb = gw_ref[...].astype(BF16)
    for r0 in range(0, n_rows, ROW_SLAB):
        rs = slice(r0, r0 + ROW_SLAB)
        y = _gelu_tanh(yacc[rs, :] + dvec_ref[...] * u_rows(rs))
        gate = jnp.dot(y.astype(BF16), gwb, preferred_element_type=F32) + gb_ref[...]
        y = y * jax.nn.sigmoid(gate)
        for t in range(n_planes):
            yp[t, rs, :] = y[:, t * LANES:(t + 1) * LANES]
    for src, n, dst, stride in moves:
        for t in range(n_planes):
            y_ref[src:src + n, t * LANES:(t + 1) * LANES] = yp[t, pl.ds(dst, n, stride=stride), :]
    if emit_state:
        for q in range(n_seq):
            for d in range(N_DIR):
                fr_ref[q, d:d + 1, :] = fin[d, 0, q:q + 1, :]
                fi_ref[q, d:d + 1, :] = fin[d, 1, q:q + 1, :]


def _s5_mixer(u, bblk, cblk, pw, prm, l, *, seq_len, n_seq, nb, blk0, col_major, h0=None, y_buf=None,
              st_buf=None, emit_state):
    n_rows = n_seq * seq_len
    n_seg = S5_SEG // n_seq
    seq = pl.BlockSpec((n_rows, S5_DIM), lambda b: (blk0 + b, 0))
    st = pl.BlockSpec((n_seq, None, N_DIR, S5_LANES), lambda b: (b, l, 0, 0))
    once = dict(pipeline_mode=pl.Buffered(1))
    args = [u, bblk, cblk, pw, prm["d_vec"], prm["glu_w"], prm["glu_b"]]
    in_specs = [seq, _layer_spec(l, (N_DIR, S5_NSLAB, S5_DIM, 2 * S5_SLAB), **once),
                _layer_spec(l, (S5_NSLAB, 2 * S5_SLAB, S5_DIM), **once),
                _layer_spec(l, (N_DIR, 2, S5_PW_ROWS, S5_LANES), **once), _layer_spec(l, (1, S5_DIM)),
                _layer_spec(l, (S5_DIM, S5_DIM)), _layer_spec(l, (1, S5_DIM))]
    if h0 is not None:
        h0_spec = pl.BlockSpec((None, None, N_DIR, S5_LANES), lambda b: (b, l, 0, 0))
        args += list(h0)
        in_specs += [h0_spec, h0_spec]
    out_shape = [jax.ShapeDtypeStruct((N_TOK, S5_DIM), F32)]
    out_specs = [seq]
    aliases = {}
    if y_buf is not None:
        aliases[len(args)] = 0
        args.append(y_buf)
        in_specs.append(_any_spec())
    if emit_state:
        out_shape += [jax.ShapeDtypeStruct((BATCH, DEPTH, N_DIR, S5_LANES), F32)] * 2
        out_specs += [st, st]
        if st_buf is not None:
            for j, buf in enumerate(st_buf):
                aliases[len(args)] = 1 + j
                args.append(buf)
                in_specs.append(_any_spec())
    res = pl.pallas_call(
        functools.partial(_s5_kernel, n_seq=n_seq, n_seg=n_seg, seq_len=seq_len, col_major=col_major,
                          has_h0=h0 is not None, n_alias=len(aliases), emit_state=emit_state),
        grid=(nb,),
        in_specs=in_specs,
        out_specs=out_specs,
        out_shape=out_shape,
        input_output_aliases=aliases,
        scratch_shapes=[pltpu.VMEM((S5_DIM // LANES, n_rows, LANES), F32),
                        pltpu.VMEM((n_rows, 2 * S5_SLAB), F32),
                        pltpu.VMEM((n_rows, 2 * S5_SLAB), F32),
                        pltpu.VMEM((n_rows, S5_DIM), F32),
                        pltpu.VMEM((S5_DIM // LANES, n_rows, LANES), F32),
                        pltpu.VMEM((N_DIR, 2, S5_SEG, S5_SLAB), F32),
                        pltpu.VMEM((N_DIR, 2, S5_SEG, S5_LANES), F32)],
        compiler_params=_params(),
        name=f"s5_mixer_L{seq_len}",
    )(*args)
    return (res[0], res[1], res[2]) if emit_state else (res[0], None, None)


def _outffn_kernel(xa_ref, xb_ref, ys_ref, yg_ref, y5_ref, g1_ref, sh_ref, sc_ref, g2_ref, ng_ref,
                   fg_ref, wo_ref, w1_ref, w2_ref, *o_refs, final):
    i = pl.program_id(0)
    n_ctx = N_CTX // FFN_TILE
    r = _mod_row(i, FFN_TILE)
    row = lambda ref: ref[pl.ds(r, 1), :]
    x = jnp.where(i < n_ctx, xa_ref[...], xb_ref[...])
    mixed = jnp.concatenate([ys_ref[...].astype(BF16), yg_ref[...].astype(BF16),
                             y5_ref[...].astype(BF16)], axis=1)
    x1 = x + row(g1_ref) * jnp.dot(mixed, wo_ref[...], preferred_element_type=F32)
    h = ((_rms(x1) * ng_ref[...]) * (1.0 + row(sc_ref)) + row(sh_ref)).astype(BF16)
    acc = jnp.zeros((FFN_TILE, D_MODEL), F32)
    for j in range(D_FF // FF_SLAB):
        a = jnp.dot(h, w1_ref[:, j * FF_SLAB:(j + 1) * FF_SLAB], preferred_element_type=F32)
        a = jnp.maximum(a, 0.0)
        acc = acc + jnp.dot((a * a).astype(BF16), w2_ref[j * FF_SLAB:(j + 1) * FF_SLAB, :],
                            preferred_element_type=F32)
    x2 = x1 + row(g2_ref) * acc
    if not final:
        o_refs[0][...] = x2
    else:
        y = _rms(x2) * fg_ref[...]

        @pl.when(i < n_ctx)
        def _():
            o_refs[0][...] = y

        @pl.when(i >= n_ctx)
        def _():
            o_refs[1][...] = y


def _out_ffn(xa, xb, b_is_stream, y_ssd, y_sgu, y_s5, l, mods, norm_g, final_g, w_out, w1, w2, *, final):
    tok = lambda w: pl.BlockSpec((FFN_TILE, w), lambda i: (i, 0))
    res = lambda s: _layer_spec(l, s, pipeline_mode=pl.Buffered(1))
    sa, sb = _split_tok_specs(FFN_TILE, D_MODEL, b_is_stream)
    if final:
        out_specs = list(_split_tok_specs(FFN_TILE, D_MODEL, False))
        out_shape = [jax.ShapeDtypeStruct((N_CTX, D_MODEL), F32), jax.ShapeDtypeStruct((N_LAT, D_MODEL), F32)]
    else:
        out_specs = [tok(D_MODEL)]
        out_shape = [jax.ShapeDtypeStruct((N_TOK, D_MODEL), F32)]
    return pl.pallas_call(
        functools.partial(_outffn_kernel, final=final),
        grid=(N_TOK // FFN_TILE,),
        in_specs=[sa, sb, tok(SSD_INNER), tok(SGU_DIM), tok(S5_DIM),
                  _mod_spec(l, 2), _mod_spec(l, 3), _mod_spec(l, 4), _mod_spec(l, 5),
                  _layer_spec(l, (1, D_MODEL)), pl.BlockSpec((1, D_MODEL), lambda i: (0, 0)),
                  res((D_MODEL, D_MODEL)), res((D_MODEL, D_FF)), res((D_FF, D_MODEL))],
        out_specs=out_specs,
        out_shape=out_shape,
        compiler_params=_params(),
        name="out_ffn_final" if final else "out_ffn",
    )(xa, xb, y_ssd, y_sgu, y_s5, mods, mods, mods, mods, norm_g, final_g, w_out, w1, w2)


def kernel(x_prompt, x_sample, state_ssd, state_s5_re, state_s5_im, c, c_ctx, ada_w, ada_b, norm1_g,
           norm2_g, w_in, ssd_conv_w, ssd_conv_b, ssd_dt_bias, ssd_a_log, ssd_d, ssd_norm_g,
           sgu_norm_g, sgu_w, sgu_b, s5_lambda_re, s5_lambda_im, s5_log_dt, s5_b_re, s5_b_im,
           s5_c_re, s5_c_im, s5_d, s5_glu_w, s5_glu_b, w_out, ffn_w1, ffn_w2, final_norm_g):
    cvec = jnp.concatenate([c_ctx[None, :], c,
                            jnp.zeros((MOD_ROWS - 1 - DEC_BATCH, D_MODEL), F32)], axis=0)
    mods = _adaln_mods(cvec, ada_w, ada_b)
    bblk, cblk, pw = _s5_prep(s5_lambda_re, s5_lambda_im, s5_log_dt, s5_b_re, s5_b_im,
                              s5_c_re, s5_c_im)

    w_in_b = w_in.astype(BF16)
    w_perm = jnp.concatenate(
        [w_in_b[:, :, 0:OFF_DT], w_in_b[:, :, OFF_SGU:IN_DIM], w_in_b[:, :, OFF_DT:OFF_SGU],
         jnp.zeros((DEPTH, D_MODEL, LANES - N_DIR * SSD_HEADS), BF16)], axis=2)
    w_out_b = w_out.astype(BF16)
    w1_b = ffn_w1.astype(BF16)
    w2_b = ffn_w2.astype(BF16)
    vec = lambda t: t.reshape(DEPTH, 1, -1)
    lane_pad = lambda t: jnp.pad(vec(t), ((0, 0), (0, 0), (0, LANES - t[0].size)))
    ssd_prm = dict(conv_w=ssd_conv_w, conv_b=vec(ssd_conv_b), dt_bias=lane_pad(ssd_dt_bias),
                   a_log=lane_pad(ssd_a_log), d_vec=vec(jnp.repeat(ssd_d, SSD_HEAD_DIM, axis=-1)),
                   norm_g=vec(ssd_norm_g))
    s5_prm = dict(d_vec=vec(s5_d), glu_w=s5_glu_w, glu_b=vec(s5_glu_b))
    sgu_w_pair = sgu_w.reshape(DEPTH, SGU_HEADS // 2, 2, CHUNK, CHUNK).transpose(0, 1, 3, 2, 4)
    sgu_w_pair = sgu_w_pair.reshape(DEPTH, SGU_HEADS // 2, CHUNK, 2 * CHUNK).astype(BF16)
    sgu_b_full = jnp.repeat(sgu_b.transpose(0, 2, 1), SGU_DIM // SGU_HEADS, axis=2)
    norm1 = vec(norm1_g)
    norm2 = vec(norm2_g)
    sgu_g = vec(sgu_norm_g)
    final_g = final_norm_g.reshape(1, D_MODEL)

    lat_ssd = state_ssd.reshape(DEC_BATCH, DEPTH, N_DIR, SSD_HP, SSD_STATE)
    lat_s5 = (state_s5_re.reshape(DEC_BATCH, DEPTH, N_DIR, S5_LANES),
              state_s5_im.reshape(DEC_BATCH, DEPTH, N_DIR, S5_LANES))
    lat_blk = N_CTX // DEC_SEQ

    xa = x_prompt.reshape(N_CTX, D_MODEL)
    xb = x_sample.reshape(N_LAT, D_MODEL)
    b_is_stream = False
    st_ssd = None
    st_s5 = None
    for l in range(DEPTH):
        z, xbc, y_sgu, s5u, dtr = _in_proj(xa, xb, b_is_stream, l, mods, norm1, w_perm, sgu_g,
                                           sgu_w_pair, sgu_b_full)
        y_ssd, st_ssd = _ssd_mixer(z, xbc, dtr, ssd_prm, l, L=SEQ, nb=BATCH, blk0=0, st_buf=st_ssd,
                                   emit_state=True)
        y_ssd, _ = _ssd_mixer(z, xbc, dtr, ssd_prm, l, L=DEC_SEQ, nb=DEC_BATCH, blk0=lat_blk,
                              h0=lat_ssd, y_buf=y_ssd, emit_state=False)
        y_s5, fr, fi = _s5_mixer(s5u, bblk, cblk, pw, s5_prm, l, seq_len=SEQ, n_seq=S5_SEG,
                                 nb=BATCH // S5_SEG, blk0=0, col_major=False, st_buf=st_s5,
                                 emit_state=True)
        st_s5 = (fr, fi)
        y_s5, _, _ = _s5_mixer(s5u, bblk, cblk, pw, s5_prm, l, seq_len=DEC_SEQ, n_seq=1, nb=DEC_BATCH,
                               blk0=lat_blk, col_major=True, h0=lat_s5, y_buf=y_s5, emit_state=False)
        final = l == DEPTH - 1
        out = _out_ffn(xa, xb, b_is_stream, y_ssd, y_sgu, y_s5, l, mods, norm2, final_g, w_out_b, w1_b,
                       w2_b, final=final)
        if not final:
            xa = xb = out[0]
            b_is_stream = True

    y_prompt = out[0].reshape(BATCH, SEQ, D_MODEL)
    y_sample = out[1].reshape(DEC_BATCH, DEC_SEQ, D_MODEL)
    new_state_ssd = st_ssd.reshape(BATCH, DEPTH, N_DIR, SSD_HEADS, SSD_HEAD_DIM, SSD_STATE)
    s5_shape = (BATCH, DEPTH, N_DIR, S5_GROUPS, S5_STATE)
    return (y_prompt, y_sample, new_state_ssd, st_s5[0].reshape(s5_shape), st_s5[1].reshape(s5_shape))
```

```python
import functools
import math

import jax
import jax.numpy as jnp
from jax import lax
from jax.experimental import pallas as pl
from jax.experimental.pallas import tpu as pltpu

F32 = jnp.float32
BF16 = jnp.bfloat16

D_MODEL = 1024
BATCH = 16
SEQ = 256
DEPTH = 2
DEC_BATCH = 2
DEC_SEQ = 1024
GRID_W = 64
CHUNK = 128
N_DIR = 2
EPS = 1e-6
SSD_INNER = 512
SSD_HEAD_DIM = 64
SSD_HEADS = 8
SSD_GROUPS = 2
SSD_STATE = 128
SSD_CONV = 5
SSD_CONV_DIM = SSD_INNER + 2 * SSD_GROUPS * SSD_STATE
SGU_DIM = 256
SGU_HEADS = 4
S5_DIM = 256
S5_GROUP_CH = 16
S5_GROUPS = 16
S5_STATE = 64
D_FF = 4 * D_MODEL
OFF_XBC = SSD_INNER
OFF_DT = OFF_XBC + SSD_CONV_DIM
OFF_SGU = OFF_DT + N_DIR * SSD_HEADS
OFF_S5 = OFF_SGU + 2 * SGU_DIM
IN_DIM = OFF_S5 + S5_DIM

N_CTX = BATCH * SEQ
N_LAT = DEC_BATCH * DEC_SEQ
N_TOK = N_CTX + N_LAT
LANES = 128
SUBLANES = 8
MOD_ROWS = SUBLANES
SSD_HP = SSD_HEADS * SSD_HEAD_DIM
S5_LANES = S5_GROUPS * S5_STATE
S5_SEG = SUBLANES
S5_PW_ROWS = DEC_SEQ // S5_SEG
S5_SLAB = 256
S5_NSLAB = S5_LANES // S5_SLAB
PAIR = 2 * SSD_HEAD_DIM
N_PAIR = SSD_HEADS // 2
TOK_TILE = 256
N_CTX_TILES = N_CTX // TOK_TILE
FFN_TILE = 512
FFN_PREP = 8
ROW_SLAB = 256
CONV_LANES = 256
VMEM_LIMIT = 56 * 1024 * 1024

_NT = (((1,), (1,)), ((), ()))


def _params(n_axes=1):
    return pltpu.CompilerParams(dimension_semantics=("arbitrary",) * n_axes,
                                vmem_limit_bytes=VMEM_LIMIT)


def _layer_spec(l, shape, **kw):
    return pl.BlockSpec((None,) + tuple(shape), lambda *_: (l,) + (0,) * len(shape), **kw)


def _mod_spec(l, k):
    return pl.BlockSpec((None, MOD_ROWS, D_MODEL), lambda *_: (l, 0, k))


def _any_spec():
    return pl.BlockSpec(memory_space=pl.ANY)


def _mod_row(i, tm):
    n_ctx = N_CTX // tm
    return jnp.where(i < n_ctx, 0, 1 + (i - n_ctx) // (DEC_SEQ // tm))


def _split_tok_specs(tm, width, b_has_ctx_rows, tile_of=lambda i: i):
    n_ctx = N_CTX // tm
    b_off = n_ctx if b_has_ctx_rows else 0
    a = pl.BlockSpec((tm, width), lambda i: (jnp.minimum(tile_of(i), n_ctx - 1), 0))
    b = pl.BlockSpec((tm, width), lambda i: (jnp.maximum(tile_of(i), n_ctx) - n_ctx + b_off, 0))
    return a, b


def _silu(x):
    return x * jax.nn.sigmoid(x)


def _gelu_tanh(x):
    c = math.sqrt(2.0 / math.pi)
    return 0.5 * x * (1.0 + jnp.tanh(c * (x + 0.044715 * (x * x * x))))


def _rms(x):
    return x * lax.rsqrt(jnp.mean(x * x, axis=-1, keepdims=True) + EPS)


def _bdot(a, b):
    return jnp.dot(a.astype(BF16), b.astype(BF16), preferred_element_type=F32)


def _split3(x):
    hi = x.astype(BF16)
    r = x - hi.astype(F32)
    mid = r.astype(BF16)
    lo = (r - mid.astype(F32)).astype(BF16)
    return jnp.concatenate([hi, mid, lo], axis=1)


def _mod_kernel(c_ref, w_ref, b_ref, o_ref):
    o_ref[...] = _bdot(_silu(c_ref[...]), w_ref[...]) + b_ref[...]


def _adaln_mods(cvec, ada_w, ada_b):
    n_blk = 6
    return pl.pallas_call(
        _mod_kernel,
        grid=(DEPTH, n_blk),
        in_specs=[pl.BlockSpec((MOD_ROWS, D_MODEL), lambda l, j: (0, 0)),
                  pl.BlockSpec((None, D_MODEL, D_MODEL), lambda l, j: (l, 0, j)),
                  pl.BlockSpec((None, 1, D_MODEL), lambda l, j: (l, 0, j))],
        out_specs=pl.BlockSpec((None, MOD_ROWS, D_MODEL), lambda l, j: (l, 0, j)),
        out_shape=jax.ShapeDtypeStruct((DEPTH, MOD_ROWS, 6 * D_MODEL), F32),
        compiler_params=_params(2),
        name="adaln_mod",
    )(cvec, ada_w, ada_b.reshape(DEPTH, 1, 6 * D_MODEL))


_C_Z = 0
_C_XBC = _C_Z + SSD_INNER
_C_SGU = _C_XBC + SSD_CONV_DIM
_C_S5 = _C_SGU + 2 * SGU_DIM
_C_DT = _C_S5 + S5_DIM
_C_END = _C_DT + LANES


def _inproj_kernel(xa_ref, xb_ref, sh_ref, sc_ref, g_ref, win_ref, sg_ref, sw_ref, sb_ref,
                   z_ref, xbc_ref, ysgu_ref, s5_ref, dt_ref, w_ref):
    i = pl.program_id(0)

    @pl.when(i == 0)
    def _():
        for r0 in range(0, D_MODEL, ROW_SLAB):
            rs = slice(r0, r0 + ROW_SLAB)
            w_ref[rs, _C_Z:_C_SGU] = win_ref[rs, 0:OFF_DT].astype(BF16)
            w_ref[rs, _C_SGU:_C_DT] = win_ref[rs, OFF_SGU:IN_DIM].astype(BF16)
            dt_cols = win_ref[rs, OFF_DT:OFF_SGU].astype(BF16)
            w_ref[rs, _C_DT:_C_END] = jnp.concatenate(
                [dt_cols, jnp.zeros((ROW_SLAB, LANES - N_DIR * SSD_HEADS), BF16)], axis=1)

    r = _mod_row(i, TOK_TILE)
    x = jnp.where(i < N_CTX_TILES, xa_ref[...], xb_ref[...])
    shift = sh_ref[pl.ds(r, 1), :]
    scale = sc_ref[pl.ds(r, 1), :]
    h = (_rms(x) * g_ref[...]) * (1.0 + scale) + shift
    hb = h.astype(BF16)
    z_ref[...] = jnp.dot(hb, w_ref[:, _C_Z:_C_XBC], preferred_element_type=F32)
    xbc_ref[...] = jnp.dot(hb, w_ref[:, _C_XBC:_C_SGU], preferred_element_type=F32)
    s5_ref[...] = jnp.dot(hb, w_ref[:, _C_S5:_C_DT], preferred_element_type=F32)
    dt_ref[...] = jnp.dot(hb, w_ref[:, _C_DT:_C_END], preferred_element_type=F32)

    uv = _gelu_tanh(jnp.dot(hb, w_ref[:, _C_SGU:_C_S5], preferred_element_type=F32))
    u = uv[:, :SGU_DIM]
    v = _rms(uv[:, SGU_DIM:]) * sg_ref[...]
    lo_lane = lax.broadcasted_iota(jnp.int32, (CHUNK, LANES), 1) < (LANES // 2)
    for c in range(TOK_TILE // CHUNK):
        rows = slice(c * CHUNK, (c + 1) * CHUNK)
        mix = []
        for pr in range(SGU_HEADS // 2):
            vp = v[rows, pr * LANES:(pr + 1) * LANES]
            rhs = jnp.concatenate([jnp.where(lo_lane, vp, 0.0).astype(BF16),
                                   jnp.where(lo_lane, 0.0, vp).astype(BF16)], axis=0)
            mix.append(jnp.dot(sw_ref[pr], rhs, preferred_element_type=F32))
        ysgu_ref[rows, :] = u[rows, :] * (jnp.concatenate(mix, axis=1) + sb_ref[...])


def _in_proj(xa, xb, b_is_stream, l, mods, norm_g, w_in, sgu_g, sgu_w_pair, sgu_b_full):
    tok = lambda w: pl.BlockSpec((TOK_TILE, w), lambda i: (i, 0))
    widths = (SSD_INNER, SSD_CONV_DIM, SGU_DIM, S5_DIM, LANES)
    sa, sb = _split_tok_specs(TOK_TILE, D_MODEL, b_is_stream)
    return pl.pallas_call(
        _inproj_kernel,
        grid=(N_TOK // TOK_TILE,),
        in_specs=[sa, sb, _mod_spec(l, 0), _mod_spec(l, 1), _layer_spec(l, (1, D_MODEL)),
                  _layer_spec(l, (D_MODEL, IN_DIM), pipeline_mode=pl.Buffered(1)),
                  _layer_spec(l, (1, SGU_DIM)), _layer_spec(l, (SGU_HEADS // 2, CHUNK, 2 * CHUNK)),
                  _layer_spec(l, (CHUNK, SGU_DIM))],
        out_specs=[tok(w) for w in widths],
        out_shape=[jax.ShapeDtypeStruct((N_TOK, w), F32) for w in widths],
        scratch_shapes=[pltpu.VMEM((D_MODEL, _C_END), BF16)],
        compiler_params=_params(),
        name="in_proj",
    )(xa, xb, mods, mods, norm_g, w_in, sgu_g, sgu_w_pair, sgu_b_full)


def _ssd_kernel(*refs, L, has_h0, n_alias, emit_state):
    z_ref, xbc_ref, dt_ref, cw_ref, cb_ref, dtb_ref, alog_ref, dvec_ref, ng_ref = refs[:9]
    k = 9
    h0_ref = refs[k] if has_h0 else None
    k += int(has_h0) + n_alias
    y_ref = refs[k]
    hout_ref = refs[k + 1] if emit_state else None
    xpad, xc, acol, atr, cbs, bmt, dtsp, yacc, hst = refs[k + 1 + int(emit_state):]
    Q = CHUNK
    nc = L // Q
    halo = SUBLANES
    pad = (SSD_CONV - 1) // 2

    xpad[0:halo, :] = jnp.zeros((halo, SSD_CONV_DIM), F32)
    xpad[halo + L:2 * halo + L, :] = jnp.zeros((halo, SSD_CONV_DIM), F32)
    for c in range(nc):
        xpad[halo + c * Q:halo + (c + 1) * Q, :] = xbc_ref[c * Q:(c + 1) * Q, :]
    win = Q + 2 * halo
    for c in range(nc):
        for lb in range(0, SSD_CONV_DIM, CONV_LANES):
            ln = slice(lb, lb + CONV_LANES)
            xa = xpad[c * Q:c * Q + win, ln]
            acc = cb_ref[:, ln] + cw_ref[pad:pad + 1, ln] * xa[halo:halo + Q, :]
            for t in range(SSD_CONV):
                if t != pad:
                    rolled = pltpu.roll(xa, (pad - t) % win, 0)
                    acc = acc + cw_ref[t:t + 1, ln] * rolled[halo:halo + Q, :]
            xc[c * Q:(c + 1) * Q, ln] = _silu(acc)

    raw = dt_ref[...] + dtb_ref[...]
    dt = jnp.maximum(raw, 0.0) + jnp.log(1.0 + jnp.exp(-jnp.abs(raw)))
    dtsp[...] = dt
    a_neg = -jnp.exp(alog_ref[...])
    row = lax.broadcasted_iota(jnp.int32, (Q, Q), 0)
    col = lax.broadcasted_iota(jnp.int32, (Q, Q), 1)
    lower = row >= col
    upper = col >= row
    tri_l = lower.astype(F32)
    tri_u = upper.astype(F32)
    fwd_lane = lax.broadcasted_iota(jnp.int32, (Q, LANES), 1) < SSD_HEADS
    for c in range(nc):
        dta = dtsp[c * Q:(c + 1) * Q, :] * a_neg
        pre = jnp.dot(tri_l, dta, precision=lax.Precision.HIGHEST, preferred_element_type=F32)
        suf = jnp.dot(tri_u, dta, precision=lax.Precision.HIGHEST, preferred_element_type=F32)
        a = jnp.where(fwd_lane, pre, suf)
        acol[c * Q:(c + 1) * Q, :] = a
        atr[c] = a.T[0:N_DIR * SSD_HEADS, :]
        for g in range(SSD_GROUPS):
            b0 = SSD_INNER + g * SSD_STATE
            c0 = SSD_INNER + (SSD_GROUPS + g) * SSD_STATE
            bm = xc[c * Q:(c + 1) * Q, b0:b0 + SSD_STATE]
            cbs[c, g] = lax.dot_general(xc[c * Q:(c + 1) * Q, c0:c0 + SSD_STATE].astype(BF16),
                                        bm.astype(BF16), _NT, preferred_element_type=F32)
            bmt[c, g] = bm.T.astype(BF16)

    sel_row = lax.broadcasted_iota(jnp.int32, (3 * LANES, SSD_INNER), 0) & (LANES - 1)
    sel_head = lax.broadcasted_iota(jnp.int32, (3 * LANES, SSD_INNER), 1) >> (SSD_HEAD_DIM.bit_length() - 1)
    sel = [(sel_row == sel_head + d * SSD_HEADS).astype(BF16) for d in range(N_DIR)]
    mask2 = [jnp.concatenate([m, m], axis=1) for m in (lower, upper)]

    for d in range(N_DIR):
        for p in range(N_PAIR):
            if has_h0:
                hst[d, p] = h0_ref[d, p * PAIR:(p + 1) * PAIR, :].T
            else:
                hst[d, p] = jnp.zeros((SSD_STATE, PAIR), F32)

    lo_lane = lax.broadcasted_iota(jnp.int32, (Q, PAIR), 1) < SSD_HEAD_DIM

    def chunk(c, d):
        r0 = pl.multiple_of(c * Q, Q)
        rows = pl.ds(r0, Q)
        a = acol[rows, :]
        a_t = atr[c]
        dtp = jnp.dot(_split3(dtsp[rows, :]), sel[d], preferred_element_type=F32)
        a_end = a[Q - 1:Q, :] if d == 0 else a[0:1, :]
        dec = jnp.exp(a_end)
        for g in range(SSD_GROUPS):
            c0 = SSD_INNER + (SSD_GROUPS + g) * SSD_STATE
            cmb = xc[rows, c0:c0 + SSD_STATE].astype(BF16)
            cb = cbs[c, g]
            cb2 = jnp.concatenate([cb, cb], axis=1)
            for pr in range(2):
                p = g * 2 + pr
                j0 = d * SSD_HEADS + 2 * p
                j1 = j0 + 1
                xs = xc[rows, p * PAIR:(p + 1) * PAIR]
                ab0 = jnp.broadcast_to(a[:, j0:j0 + 1], (Q, Q))
                ab1 = jnp.broadcast_to(a[:, j1:j1 + 1], (Q, Q))
                seg = (jnp.concatenate([ab0, ab1], axis=1)
                       - jnp.concatenate([a_t[j0:j0 + 1, :], a_t[j1:j1 + 1, :]], axis=1))
                m = (cb2 * jnp.exp(jnp.where(mask2[d], seg, -jnp.inf))).astype(BF16)
                xdt = xs * dtp[:, p * PAIR:(p + 1) * PAIR]
                rhs = jnp.concatenate([jnp.where(lo_lane, xdt, 0.0).astype(BF16),
                                       jnp.where(lo_lane, 0.0, xdt).astype(BF16)], axis=0)
                y_diag = jnp.dot(m, rhs, preferred_element_type=F32)
                hp = hst[d, p]
                y_off = jnp.dot(cmb, hp.astype(BF16), preferred_element_type=F32)
                a_pair = jnp.where(lo_lane, ab0, ab1)
                y = y_diag + jnp.exp(a_pair) * y_off
                a_end_pair = jnp.where(lo_lane[0:1, :], a_end[:, j0:j0 + 1], a_end[:, j1:j1 + 1])
                xw = (xdt * jnp.exp(a_end_pair - a_pair)).astype(BF16)
                s_new = jnp.dot(bmt[c, g], xw, preferred_element_type=F32)
                decp = jnp.where(lo_lane[0:1, :], dec[:, j0:j0 + 1], dec[:, j1:j1 + 1])
                hst[d, p] = decp * hp + s_new
                if d == 0:
                    y = y + dvec_ref[:, p * PAIR:(p + 1) * PAIR] * xs
                yacc[d, rows, p * PAIR:(p + 1) * PAIR] = y

    def both(t, carry):
        chunk(t, 0)
        chunk(nc - 1 - t, 1)
        return carry

    lax.fori_loop(0, nc, both, 0)
    for c in range(nc):
        rows = slice(c * Q, (c + 1) * Q)
        y = (yacc[0, rows, :] + yacc[1, rows, :]) * _silu(z_ref[rows, :])
        y_ref[rows, :] = _rms(y) * ng_ref[...]
    if emit_state:
        for d in range(N_DIR):
            for p in range(N_PAIR):
                hout_ref[d, p * PAIR:(p + 1) * PAIR, :] = hst[d, p].T


def _ssd_mixer(z, xbc, dtr, prm, l, *, L, nb, blk0, h0=None, y_buf=None, st_buf=None, emit_state):
    seq = lambda w: pl.BlockSpec((L, w), lambda b: (blk0 + b, 0))
    st = pl.BlockSpec((None, None, N_DIR, SSD_HP, SSD_STATE), lambda b: (b, l, 0, 0, 0))
    args = [z, xbc, dtr, prm["conv_w"], prm["conv_b"], prm["dt_bias"], prm["a_log"], prm["d_vec"],
            prm["norm_g"]]
    in_specs = [seq(SSD_INNER), seq(SSD_CONV_DIM), seq(LANES),
                _layer_spec(l, (SSD_CONV, SSD_CONV_DIM)), _layer_spec(l, (1, SSD_CONV_DIM)),
                _layer_spec(l, (1, LANES)), _layer_spec(l, (1, LANES)),
                _layer_spec(l, (1, SSD_INNER)), _layer_spec(l, (1, SSD_INNER))]
    if h0 is not None:
        args.append(h0)
        in_specs.append(st)
    out_shape = [jax.ShapeDtypeStruct((N_TOK, SSD_INNER), F32)]
    out_specs = [seq(SSD_INNER)]
    aliases = {}
    if y_buf is not None:
        aliases[len(args)] = 0
        args.append(y_buf)
        in_specs.append(_any_spec())
    if emit_state:
        out_shape.append(jax.ShapeDtypeStruct((BATCH, DEPTH, N_DIR, SSD_HP, SSD_STATE), F32))
        out_specs.append(st)
        if st_buf is not None:
            aliases[len(args)] = 1
            args.append(st_buf)
            in_specs.append(_any_spec())
    res = pl.pallas_call(
        functools.partial(_ssd_kernel, L=L, has_h0=h0 is not None, n_alias=len(aliases),
                          emit_state=emit_state),
        grid=(nb,),
        in_specs=in_specs,
        out_specs=out_specs,
        out_shape=out_shape,
        input_output_aliases=aliases,
        scratch_shapes=[pltpu.VMEM((L + 2 * SUBLANES, SSD_CONV_DIM), F32),
                        pltpu.VMEM((L, SSD_CONV_DIM), F32),
                        pltpu.VMEM((L, LANES), F32),
                        pltpu.VMEM((L // CHUNK, N_DIR * SSD_HEADS, CHUNK), F32),
                        pltpu.VMEM((L // CHUNK, SSD_GROUPS, CHUNK, CHUNK), F32),
                        pltpu.VMEM((L // CHUNK, SSD_GROUPS, SSD_STATE, CHUNK), BF16),
                        pltpu.VMEM((L, LANES), F32),
                        pltpu.VMEM((N_DIR, L, SSD_INNER), F32),
                        pltpu.VMEM((N_DIR, N_PAIR, PAIR, SSD_STATE), F32)],
        compiler_params=_params(),
        name=f"ssd_mixer_L{L}",
    )(*args)
    return res if emit_state else (res[0], None)


_LOG2_CH = S5_GROUP_CH.bit_length() - 1
_LOG2_ST = S5_STATE.bit_length() - 1


def _cmul(ar, ai, br, bi):
    return ar * br - ai * bi, ar * bi + ai * br


def _s5prep_kernel(lre_ref, lim_ref, ldt_ref, btr_ref, bti_ref, ctr_ref, cti_ref,
                   bblk_ref, cblk_ref, pw_ref):
    brow = lax.broadcasted_iota(jnp.int32, (S5_DIM, S5_LANES), 0) >> _LOG2_CH
    bcol = lax.broadcasted_iota(jnp.int32, (S5_DIM, S5_LANES), 1) >> _LOG2_ST
    bmask = brow == bcol
    p1 = lax.broadcasted_iota(jnp.int32, (S5_PW_ROWS, S5_LANES), 0) + 1
    for d in range(N_DIR):
        lre = lre_ref[d]
        lim = lim_ref[d]
        step = jnp.exp(ldt_ref[d])
        mag = jnp.exp(lre * step)
        lbr = mag * jnp.cos(lim * step)
        lbi = mag * jnp.sin(lim * step)
        den = lre * lre + lim * lim
        nr = lbr - 1.0
        cr = (nr * lre + lbi * lim) / den
        ci = (lbi * lre - nr * lim) / den
        br, bi = _cmul(cr, ci, btr_ref[...], bti_ref[...])
        br = jnp.where(bmask, br, 0.0).astype(BF16)
        bi = jnp.where(bmask, bi, 0.0).astype(BF16)
        for sl in range(S5_NSLAB):
            bblk_ref[d, sl, :, 0:S5_SLAB] = br[:, sl * S5_SLAB:(sl + 1) * S5_SLAB]
            bblk_ref[d, sl, :, S5_SLAB:2 * S5_SLAB] = bi[:, sl * S5_SLAB:(sl + 1) * S5_SLAB]
        rr = jnp.ones((S5_PW_ROWS, S5_LANES), F32)
        ri = jnp.zeros((S5_PW_ROWS, S5_LANES), F32)
        sr, si = lbr, lbi
        for k in range(S5_PW_ROWS.bit_length()):
            bit = ((p1 >> k) & 1) == 1
            tr, ti = _cmul(rr, ri, sr, si)
            rr = jnp.where(bit, tr, rr)
            ri = jnp.where(bit, ti, ri)
            sr, si = _cmul(sr, si, sr, si)
        pw_ref[d, 0] = rr
        pw_ref[d, 1] = ri
    crow = lax.broadcasted_iota(jnp.int32, (S5_LANES, S5_DIM), 0) >> _LOG2_ST
    ccol = lax.broadcasted_iota(jnp.int32, (S5_LANES, S5_DIM), 1) >> _LOG2_CH
    cmask = crow == ccol
    cr = jnp.where(cmask, ctr_ref[...], 0.0).astype(BF16)
    ci = jnp.where(cmask, -cti_ref[...], 0.0).astype(BF16)
    for sl in range(S5_NSLAB):
        cblk_ref[sl, 0:S5_SLAB, :] = cr[sl * S5_SLAB:(sl + 1) * S5_SLAB, :]
        cblk_ref[sl, S5_SLAB:2 * S5_SLAB, :] = ci[sl * S5_SLAB:(sl + 1) * S5_SLAB, :]


def _s5_prep(lam_re, lam_im, log_dt, b_re, b_im, c_re, c_im):
    row = lambda t: t.reshape(DEPTH, N_DIR, 1, S5_LANES)
    ldt = jnp.repeat(log_dt, S5_STATE, axis=-1)
    bt = lambda t: jnp.tile(t.transpose(0, 3, 1, 2).reshape(DEPTH, S5_GROUP_CH, S5_LANES),
                            (1, S5_GROUPS, 1))
    ct = lambda t: jnp.tile(t.transpose(0, 1, 3, 2).reshape(DEPTH, S5_LANES, S5_GROUP_CH),
                            (1, 1, S5_GROUPS))
    vec = pl.BlockSpec((None, N_DIR, 1, S5_LANES), lambda l: (l, 0, 0, 0))
    bsp = pl.BlockSpec((None, S5_DIM, S5_LANES), lambda l: (l, 0, 0))
    csp = pl.BlockSpec((None, S5_LANES, S5_DIM), lambda l: (l, 0, 0))
    return pl.pallas_call(
        _s5prep_kernel,
        grid=(DEPTH,),
        in_specs=[vec, vec, vec, bsp, bsp, csp, csp],
        out_specs=[pl.BlockSpec((None, N_DIR, S5_NSLAB, S5_DIM, 2 * S5_SLAB), lambda l: (l, 0, 0, 0, 0)),
                   pl.BlockSpec((None, S5_NSLAB, 2 * S5_SLAB, S5_DIM), lambda l: (l, 0, 0, 0)),
                   pl.BlockSpec((None, N_DIR, 2, S5_PW_ROWS, S5_LANES), lambda l: (l, 0, 0, 0, 0))],
        out_shape=[jax.ShapeDtypeStruct((DEPTH, N_DIR, S5_NSLAB, S5_DIM, 2 * S5_SLAB), BF16),
                   jax.ShapeDtypeStruct((DEPTH, S5_NSLAB, 2 * S5_SLAB, S5_DIM), BF16),
                   jax.ShapeDtypeStruct((DEPTH, N_DIR, 2, S5_PW_ROWS, S5_LANES), F32)],
        compiler_params=_params(),
        name="s5_prep",
    )(row(lam_re), row(lam_im), row(ldt), bt(b_re), bt(b_im), ct(c_re), ct(c_im))


def _s5_moves(n_seq, n_seg, seq_len, col_major):
    steps = seq_len // n_seg
    if not col_major:
        return [(q * seq_len + s * steps, steps, q * n_seg + s, S5_SEG)
                for q in range(n_seq) for s in range(n_seg)]
    assert n_seq == 1 and n_seg == S5_SEG
    grid_rows = seq_len // GRID_W
    wseg = GRID_W // n_seg
    return [(r * GRID_W + s * wseg, wseg, r * S5_SEG + s, grid_rows * S5_SEG)
            for r in range(grid_rows) for s in range(n_seg)]


def _s5_kernel(*refs, n_seq, n_seg, seq_len, col_major, has_h0, n_alias, emit_state):
    u_ref, bblk_ref, cblk_ref, pw_ref, dvec_ref, gw_ref, gb_ref = refs[:7]
    k = 7
    h0r_ref, h0i_ref = (refs[k], refs[k + 1]) if has_h0 else (None, None)
    k += 2 * int(has_h0) + n_alias
    y_ref = refs[k]
    fr_ref, fi_ref = (refs[k + 1], refs[k + 2]) if emit_state else (None, None)
    up, buf_f, buf_b, yacc, yp, cin, fin = refs[k + 1 + 2 * int(emit_state):]
    assert n_seq * n_seg == S5_SEG and not (has_h0 and n_seg == 1) and not (emit_state and n_seg > 1)
    n_rows = n_seq * seq_len
    steps = seq_len // n_seg
    n_planes = S5_DIM // LANES
    W = S5_SLAB
    moves = _s5_moves(n_seq, n_seg, seq_len, col_major)

    for src, n, dst, stride in moves:
        for t in range(n_planes):
            up[t, pl.ds(dst, n, stride=stride), :] = u_ref[src:src + n, t * LANES:(t + 1) * LANES]

    def u_rows(rs):
        return jnp.concatenate([up[t, rs, :] for t in range(n_planes)], axis=1)

    zero = jnp.zeros((S5_SEG, W), F32)
    zrow = jnp.zeros((1, W), F32)
    for sl in range(S5_NSLAB):
        ln = slice(sl * W, (sl + 1) * W)
        for r0 in range(0, n_rows, ROW_SLAB):
            rs = slice(r0, r0 + ROW_SLAB)
            ub = u_rows(rs).astype(BF16)
            buf_f[rs, :] = jnp.dot(ub, bblk_ref[0, sl], preferred_element_type=F32)
            buf_b[rs, :] = jnp.dot(ub, bblk_ref[1, sl], preferred_element_type=F32)
        lam = [[jnp.broadcast_to(pw_ref[d, c, 0:1, ln], (S5_SEG, W)) for c in range(2)]
               for d in range(N_DIR)]

        def body(t, carry):
            fr, fi, br, bi = carry
            rf = pl.ds(pl.multiple_of(t * S5_SEG, S5_SEG), S5_SEG)
            rb = pl.ds(pl.multiple_of((steps - 1 - t) * S5_SEG, S5_SEG), S5_SEG)
            nfr = lam[0][0] * fr - lam[0][1] * fi + buf_f[rf, 0:W]
            nfi = lam[0][0] * fi + lam[0][1] * fr + buf_f[rf, W:2 * W]
            nbr = lam[1][0] * br - lam[1][1] * bi + buf_b[rb, 0:W]
            nbi = lam[1][0] * bi + lam[1][1] * br + buf_b[rb, W:2 * W]
            buf_f[rf, 0:W] = nfr
            buf_f[rf, W:2 * W] = nfi
            buf_b[rb, 0:W] = nbr
            buf_b[rb, W:2 * W] = nbi
            return nfr, nfi, nbr, nbi

        ends = lax.fori_loop(0, steps, body, (zero, zero, zero, zero), unroll=2)

        if n_seg == 1:
            if emit_state:
                for d in range(N_DIR):
                    fin[d, 0, :, ln] = ends[2 * d]
                    fin[d, 1, :, ln] = ends[2 * d + 1]
        else:
            for d in range(N_DIR):
                er, ei = ends[2 * d], ends[2 * d + 1]
                lpr = pw_ref[d, 0, steps - 1:steps, ln]
                lpi = pw_ref[d, 1, steps - 1:steps, ln]
                pr = h0r_ref[d:d + 1, ln] if has_h0 else zrow
                pi = h0i_ref[d:d + 1, ln] if has_h0 else zrow
                order = range(n_seg) if d == 0 else range(n_seg - 1, -1, -1)
                for s in order:
                    cin[d, 0, s:s + 1, :] = pr
                    cin[d, 1, s:s + 1, :] = pi
                    tr, ti = _cmul(lpr, lpi, pr, pi)
                    pr = er[s:s + 1, :] + tr
                    pi = ei[s:s + 1, :] + ti
            cfr = cin[0, 0]
            cfi = cin[0, 1]
            cbr = cin[1, 0]
            cbi = cin[1, 1]

            def fix(p, carry):
                rows = pl.ds(pl.multiple_of(p * S5_SEG, S5_SEG), S5_SEG)
                q = steps - 1 - p
                tfr, tfi = _cmul(pw_ref[0, 0, pl.ds(p, 1), ln], pw_ref[0, 1, pl.ds(p, 1), ln], cfr, cfi)
                tbr, tbi = _cmul(pw_ref[1, 0, pl.ds(q, 1), ln], pw_ref[1, 1, pl.ds(q, 1), ln], cbr, cbi)
                buf_f[rows, 0:W] = buf_f[rows, 0:W] + buf_b[rows, 0:W] + tfr + tbr
                buf_f[rows, W:2 * W] = buf_f[rows, W:2 * W] + buf_b[rows, W:2 * W] + tfi + tbi
                return carry

            lax.fori_loop(0, steps, fix, 0, unroll=2)

        for r0 in range(0, n_rows, ROW_SLAB):
            rs = slice(r0, r0 + ROW_SLAB)
            hs = buf_f[rs, :] if n_seg > 1 else buf_f[rs, :] + buf_b[rs, :]
            part = jnp.dot(hs.astype(BF16), cblk_ref[sl], preferred_element_type=F32)
            if sl == 0:
                yacc[rs, :] = part
            else:
                yacc[rs, :] = yacc[rs, :] + part

    gwb = gw_ref[...].astype(BF16)
    for r0 in range(0, n_rows, ROW_SLAB):
        rs = slice(r0, r0 + ROW_SLAB)
        y = _gelu_tanh(yacc[rs, :] + dvec_ref[...] * u_rows(rs))
        gate = jnp.dot(y.astype(BF16), gwb, preferred_element_type=F32) + gb_ref[...]
        y = y * jax.nn.sigmoid(gate)
        for t in range(n_planes):
            yp[t, rs, :] = y[:, t * LANES:(t + 1) * LANES]
    for src, n, dst, stride in moves:
        for t in range(n_planes):
            y_ref[src:src + n, t * LANES:(t + 1) * LANES] = yp[t, pl.ds(dst, n, stride=stride), :]
    if emit_state:
        for q in range(n_seq):
            for d in range(N_DIR):
                fr_ref[q, d:d + 1, :] = fin[d, 0, q:q + 1, :]
                fi_ref[q, d:d + 1, :] = fin[d, 1, q:q + 1, :]


def _s5_mixer(u, bblk, cblk, pw, prm, l, *, seq_len, n_seq, nb, blk0, col_major, h0=None, y_buf=None,
              st_buf=None, emit_state):
    n_rows = n_seq * seq_len
    n_seg = S5_SEG // n_seq
    seq = pl.BlockSpec((n_rows, S5_DIM), lambda b: (blk0 + b, 0))
    st = pl.BlockSpec((n_seq, None, N_DIR, S5_LANES), lambda b: (b, l, 0, 0))
    once = dict(pipeline_mode=pl.Buffered(1))
    args = [u, bblk, cblk, pw, prm["d_vec"], prm["glu_w"], prm["glu_b"]]
    in_specs = [seq, _layer_spec(l, (N_DIR, S5_NSLAB, S5_DIM, 2 * S5_SLAB), **once),
                _layer_spec(l, (S5_NSLAB, 2 * S5_SLAB, S5_DIM), **once),
                _layer_spec(l, (N_DIR, 2, S5_PW_ROWS, S5_LANES), **once), _layer_spec(l, (1, S5_DIM)),
                _layer_spec(l, (S5_DIM, S5_DIM)), _layer_spec(l, (1, S5_DIM))]
    if h0 is not None:
        h0_spec = pl.BlockSpec((None, None, N_DIR, S5_LANES), lambda b: (b, l, 0, 0))
        args += list(h0)
        in_specs += [h0_spec, h0_spec]
    out_shape = [jax.ShapeDtypeStruct((N_TOK, S5_DIM), F32)]
    out_specs = [seq]
    aliases = {}
    if y_buf is not None:
        aliases[len(args)] = 0
        args.append(y_buf)
        in_specs.append(_any_spec())
    if emit_state:
        out_shape += [jax.ShapeDtypeStruct((BATCH, DEPTH, N_DIR, S5_LANES), F32)] * 2
        out_specs += [st, st]
        if st_buf is not None:
            for j, buf in enumerate(st_buf):
                aliases[len(args)] = 1 + j
                args.append(buf)
                in_specs.append(_any_spec())
    res = pl.pallas_call(
        functools.partial(_s5_kernel, n_seq=n_seq, n_seg=n_seg, seq_len=seq_len, col_major=col_major,
                          has_h0=h0 is not None, n_alias=len(aliases), emit_state=emit_state),
        grid=(nb,),
        in_specs=in_specs,
        out_specs=out_specs,
        out_shape=out_shape,
        input_output_aliases=aliases,
        scratch_shapes=[pltpu.VMEM((S5_DIM // LANES, n_rows, LANES), F32),
                        pltpu.VMEM((n_rows, 2 * S5_SLAB), F32),
                        pltpu.VMEM((n_rows, 2 * S5_SLAB), F32),
                        pltpu.VMEM((n_rows, S5_DIM), F32),
                        pltpu.VMEM((S5_DIM // LANES, n_rows, LANES), F32),
                        pltpu.VMEM((N_DIR, 2, S5_SEG, S5_SLAB), F32),
                        pltpu.VMEM((N_DIR, 2, S5_SEG, S5_LANES), F32)],
        compiler_params=_params(),
        name=f"s5_mixer_L{seq_len}",
    )(*args)
    return (res[0], res[1], res[2]) if emit_state else (res[0], None, None)


def _outffn_kernel(xa_ref, xb_ref, ys_ref, yg_ref, y5_ref, g1_ref, sh_ref, sc_ref, g2_ref, ng_ref,
                   fg_ref, wo_ref, w1_ref, w2_ref, *rest, final):
    n_out = 2 if final else 1
    o_refs = rest[:n_out]
    wob, w1b, w2b = rest[n_out:]
    i = pl.program_id(0)

    @pl.when(i < FFN_PREP)
    def _():
        wob[i] = wo_ref[...].astype(BF16)
        w1b[i] = w1_ref[...].astype(BF16)
        w2b[i] = w2_ref[...].astype(BF16)

    @pl.when(i >= FFN_PREP)
    def _():
        t = i - FFN_PREP
        n_ctx = N_CTX // FFN_TILE
        r = _mod_row(t, FFN_TILE)
        row = lambda ref: ref[pl.ds(r, 1), :]
        x = jnp.where(t < n_ctx, xa_ref[...], xb_ref[...])
        mixed = jnp.concatenate([ys_ref[...].astype(BF16), yg_ref[...].astype(BF16),
                                 y5_ref[...].astype(BF16)], axis=1)
        w_out = wob[...].reshape(D_MODEL, D_MODEL)
        x1 = x + row(g1_ref) * jnp.dot(mixed, w_out, preferred_element_type=F32)
        h = ((_rms(x1) * ng_ref[...]) * (1.0 + row(sc_ref)) + row(sh_ref)).astype(BF16)
        acc = jnp.zeros((FFN_TILE, D_MODEL), F32)
        for j in range(FFN_PREP):
            a = jnp.maximum(jnp.dot(h, w1b[j], preferred_element_type=F32), 0.0)
            acc = acc + jnp.dot((a * a).astype(BF16), w2b[j], preferred_element_type=F32)
        x2 = x1 + row(g2_ref) * acc
        if not final:
            o_refs[0][...] = x2
        else:
            y = _rms(x2) * fg_ref[...]

            @pl.when(t < n_ctx)
            def _():
                o_refs[0][...] = y

            @pl.when(t >= n_ctx)
            def _():
                o_refs[1][...] = y


def _out_ffn(xa, xb, b_is_stream, y_ssd, y_sgu, y_s5, l, mods, norm_g, final_g, w_out, w1, w2, *, final):
    tile_of = lambda i: jnp.maximum(i - FFN_PREP, 0)
    tok = lambda w: pl.BlockSpec((FFN_TILE, w), lambda i: (tile_of(i), 0))
    chunk_of = lambda i: jnp.minimum(i, FFN_PREP - 1)
    sa, sb = _split_tok_specs(FFN_TILE, D_MODEL, b_is_stream, tile_of)
    if final:
        out_specs = list(_split_tok_specs(FFN_TILE, D_MODEL, False, tile_of))
        out_shape = [jax.ShapeDtypeStruct((N_CTX, D_MODEL), F32), jax.ShapeDtypeStruct((N_LAT, D_MODEL), F32)]
    else:
        out_specs = [tok(D_MODEL)]
        out_shape = [jax.ShapeDtypeStruct((N_TOK, D_MODEL), F32)]
    ko, kf = D_MODEL // FFN_PREP, D_FF // FFN_PREP
    return pl.pallas_call(
        functools.partial(_outffn_kernel, final=final),
        grid=(FFN_PREP + N_TOK // FFN_TILE,),
        in_specs=[sa, sb, tok(SSD_INNER), tok(SGU_DIM), tok(S5_DIM),
                  _mod_spec(l, 2), _mod_spec(l, 3), _mod_spec(l, 4), _mod_spec(l, 5),
                  _layer_spec(l, (1, D_MODEL)), pl.BlockSpec((1, D_MODEL), lambda i: (0, 0)),
                  pl.BlockSpec((None, ko, D_MODEL), lambda i: (l, chunk_of(i), 0)),
                  pl.BlockSpec((None, D_MODEL, kf), lambda i: (l, 0, chunk_of(i))),
                  pl.BlockSpec((None, kf, D_MODEL), lambda i: (l, chunk_of(i), 0))],
        out_specs=out_specs,
        out_shape=out_shape,
        scratch_shapes=[pltpu.VMEM((FFN_PREP, ko, D_MODEL), BF16),
                        pltpu.VMEM((FFN_PREP, D_MODEL, kf), BF16),
                        pltpu.VMEM((FFN_PREP, kf, D_MODEL), BF16)],
        compiler_params=_params(),
        name="out_ffn_final" if final else "out_ffn",
    )(xa, xb, y_ssd, y_sgu, y_s5, mods, mods, mods, mods, norm_g, final_g, w_out, w1, w2)


def kernel(x_prompt, x_sample, state_ssd, state_s5_re, state_s5_im, c, c_ctx, ada_w, ada_b, norm1_g,
           norm2_g, w_in, ssd_conv_w, ssd_conv_b, ssd_dt_bias, ssd_a_log, ssd_d, ssd_norm_g,
           sgu_norm_g, sgu_w, sgu_b, s5_lambda_re, s5_lambda_im, s5_log_dt, s5_b_re, s5_b_im,
           s5_c_re, s5_c_im, s5_d, s5_glu_w, s5_glu_b, w_out, ffn_w1, ffn_w2, final_norm_g):
    cvec = jnp.concatenate([c_ctx[None, :], c,
                            jnp.zeros((MOD_ROWS - 1 - DEC_BATCH, D_MODEL), F32)], axis=0)
    mods = _adaln_mods(cvec, ada_w, ada_b)
    bblk, cblk, pw = _s5_prep(s5_lambda_re, s5_lambda_im, s5_log_dt, s5_b_re, s5_b_im,
                              s5_c_re, s5_c_im)

    vec =lambda t: t.reshape(DEPTH, 1, -1)
    lane_pad = lambda t: jnp.pad(vec(t), ((0, 0), (0, 0), (0, LANES - t[0].size)))
    ssd_prm = dict(conv_w=ssd_conv_w, conv_b=vec(ssd_conv_b), dt_bias=lane_pad(ssd_dt_bias),
                   a_log=lane_pad(ssd_a_log), d_vec=vec(jnp.repeat(ssd_d, SSD_HEAD_DIM, axis=-1)),
                   norm_g=vec(ssd_norm_g))
    s5_prm = dict(d_vec=vec(s5_d), glu_w=s5_glu_w, glu_b=vec(s5_glu_b))
    sgu_w_pair = sgu_w.reshape(DEPTH, SGU_HEADS // 2, 2, CHUNK, CHUNK).transpose(0, 1, 3, 2, 4)
    sgu_w_pair = sgu_w_pair.reshape(DEPTH, SGU_HEADS // 2, CHUNK, 2 * CHUNK).astype(BF16)
    sgu_b_full = jnp.repeat(sgu_b.transpose(0, 2, 1), SGU_DIM // SGU_HEADS, axis=2)
    norm1 = vec(norm1_g)
    norm2 = vec(norm2_g)
    sgu_g = vec(sgu_norm_g)
    final_g = final_norm_g.reshape(1, D_MODEL)

    lat_ssd = state_ssd.reshape(DEC_BATCH, DEPTH, N_DIR, SSD_HP, SSD_STATE)
    lat_s5 = (state_s5_re.reshape(DEC_BATCH, DEPTH, N_DIR, S5_LANES),
              state_s5_im.reshape(DEC_BATCH, DEPTH, N_DIR, S5_LANES))
    lat_blk = N_CTX // DEC_SEQ

    xa = x_prompt.reshape(N_CTX, D_MODEL)
    xb = x_sample.reshape(N_LAT, D_MODEL)
    b_is_stream = False
    st_ssd = None
    st_s5 = None
    for l in range(DEPTH):
        z, xbc, y_sgu, s5u, dtr = _in_proj(xa, xb, b_is_stream, l, mods, norm1, w_in, sgu_g,
                                           sgu_w_pair, sgu_b_full)
        y_ssd, st_ssd = _ssd_mixer(z, xbc, dtr, ssd_prm, l, L=SEQ, nb=BATCH, blk0=0, st_buf=st_ssd,
                                   emit_state=True)
        y_ssd, _ = _ssd_mixer(z, xbc, dtr, ssd_prm, l, L=DEC_SEQ, nb=DEC_BATCH, blk0=lat_blk,
                              h0=lat_ssd, y_buf=y_ssd, emit_state=False)
        y_s5, fr, fi = _s5_mixer(s5u, bblk, cblk, pw, s5_prm, l, seq_len=SEQ, n_seq=S5_SEG,
                                 nb=BATCH // S5_SEG, blk0=0, col_major=False, st_buf=st_s5,
                                 emit_state=True)
        st_s5 = (fr, fi)
        y_s5, _, _ = _s5_mixer(s5u, bblk, cblk, pw, s5_prm, l, seq_len=DEC_SEQ, n_seq=1, nb=DEC_BATCH,
                               blk0=lat_blk, col_major=True, h0=lat_s5, y_buf=y_s5, emit_state=False)
        final = l == DEPTH - 1
        out = _out_ffn(xa, xb, b_is_stream, y_ssd, y_sgu, y_s5, l, mods, norm2, final_g, w_out, ffn_w1,
                       ffn_w2, final=final)
        if not final:
            xa = xb = out[0]
            b_is_stream = True

    y_prompt = out[0].reshape(BATCH, SEQ, D_MODEL)
    y_sample = out[1].reshape(DEC_BATCH, DEC_SEQ, D_MODEL)
    new_state_ssd = st_ssd.reshape(BATCH, DEPTH, N_DIR, SSD_HEADS, SSD_HEAD_DIM, SSD_STATE)
    s5_shape = (BATCH, DEPTH, N_DIR, S5_GROUPS, S5_STATE)
    return (y_prompt, y_sample, new_state_ssd, st_s5[0].reshape(s5_shape), st_s5[1].reshape(s5_shape))
```

```python
import functools
import math

import jax
import jax.numpy as jnp
from jax import lax
from jax.experimental import pallas as pl
from jax.experimental.pallas import tpu as pltpu

F32 = jnp.float32
BF16 = jnp.bfloat16

D_MODEL = 1024
BATCH = 16
SEQ = 256
DEPTH = 2
DEC_BATCH = 2
DEC_SEQ = 1024
GRID_W = 64
CHUNK = 128
N_DIR = 2
EPS = 1e-6
SSD_INNER = 512
SSD_HEAD_DIM = 64
SSD_HEADS = 8
SSD_GROUPS = 2
SSD_STATE = 128
SSD_CONV = 5
SSD_CONV_DIM = SSD_INNER + 2 * SSD_GROUPS * SSD_STATE
SGU_DIM = 256
SGU_HEADS = 4
S5_DIM = 256
S5_GROUP_CH = 16
S5_GROUPS = 16
S5_STATE = 64
D_FF = 4 * D_MODEL
OFF_XBC = SSD_INNER
OFF_DT = OFF_XBC + SSD_CONV_DIM
OFF_SGU = OFF_DT + N_DIR * SSD_HEADS
OFF_S5 = OFF_SGU + 2 * SGU_DIM
IN_DIM = OFF_S5 + S5_DIM

N_CTX = BATCH * SEQ
N_LAT = DEC_BATCH * DEC_SEQ
N_TOK = N_CTX + N_LAT
LANES = 128
SUBLANES = 8
MOD_ROWS = SUBLANES
SSD_HP = SSD_HEADS * SSD_HEAD_DIM
S5_LANES = S5_GROUPS * S5_STATE
S5_SEG = SUBLANES
S5_PW_ROWS = DEC_SEQ // S5_SEG
S5_SLAB = 256
S5_NSLAB = S5_LANES // S5_SLAB
PAIR = 2 * SSD_HEAD_DIM
N_PAIR = SSD_HEADS // 2
TOK_TILE = 512
N_CTX_TILES = N_CTX // TOK_TILE
FFN_TILE = 512
FFN_PREP = 8
ROW_SLAB = 256
CONV_LANES = 256
VMEM_LIMIT = 56 * 1024 * 1024

_NT = (((1,), (1,)), ((), ()))


def _params(n_axes=1):
    return pltpu.CompilerParams(dimension_semantics=("arbitrary",) * n_axes,
                                vmem_limit_bytes=VMEM_LIMIT)


def _layer_spec(l, shape, **kw):
    return pl.BlockSpec((None,) + tuple(shape), lambda *_: (l,) + (0,) * len(shape), **kw)


def _mod_spec(l, k):
    return pl.BlockSpec((None, MOD_ROWS, D_MODEL), lambda *_: (l, 0, k))


def _any_spec():
    return pl.BlockSpec(memory_space=pl.ANY)


def _mod_row(i, tm):
    n_ctx = N_CTX // tm
    return jnp.where(i < n_ctx, 0, 1 + (i - n_ctx) // (DEC_SEQ // tm))


def _split_tok_specs(tm, width, b_has_ctx_rows, tile_of=lambda i: i):
    n_ctx = N_CTX // tm
    b_off = n_ctx if b_has_ctx_rows else 0
    a = pl.BlockSpec((tm, width), lambda i: (jnp.minimum(tile_of(i), n_ctx - 1), 0))
    b = pl.BlockSpec((tm, width), lambda i: (jnp.maximum(tile_of(i), n_ctx) - n_ctx + b_off, 0))
    return a, b


def _silu(x):
    return x * jax.nn.sigmoid(x)


def _gelu_tanh(x):
    c = math.sqrt(2.0 / math.pi)
    return 0.5 * x * (1.0 + jnp.tanh(c * (x + 0.044715 * (x * x * x))))


def _rms(x):
    return x * lax.rsqrt(jnp.mean(x * x, axis=-1, keepdims=True) + EPS)


def _bdot(a, b):
    return jnp.dot(a.astype(BF16), b.astype(BF16), preferred_element_type=F32)


def _split3(x):
    hi = x.astype(BF16)
    r = x - hi.astype(F32)
    mid = r.astype(BF16)
    lo = (r - mid.astype(F32)).astype(BF16)
    return jnp.concatenate([hi, mid, lo], axis=1)


def _mod_kernel(c_ref, w_ref, b_ref, o_ref):
    o_ref[...] = _bdot(_silu(c_ref[...]), w_ref[...]) + b_ref[...]


def _adaln_mods(cvec, ada_w, ada_b):
    n_blk = 6
    return pl.pallas_call(
        _mod_kernel,
        grid=(DEPTH, n_blk),
        in_specs=[pl.BlockSpec((MOD_ROWS, D_MODEL), lambda l, j: (0, 0)),
                  pl.BlockSpec((None, D_MODEL, D_MODEL), lambda l, j: (l, 0, j)),
                  pl.BlockSpec((None, 1, D_MODEL), lambda l, j: (l, 0, j))],
        out_specs=pl.BlockSpec((None, MOD_ROWS, D_MODEL), lambda l, j: (l, 0, j)),
        out_shape=jax.ShapeDtypeStruct((DEPTH, MOD_ROWS, 6 * D_MODEL), F32),
        compiler_params=_params(2),
        name="adaln_mod",
    )(cvec, ada_w, ada_b.reshape(DEPTH, 1, 6 * D_MODEL))


_C_Z = 0
_C_XBC = _C_Z + SSD_INNER
_C_SGU = _C_XBC + SSD_CONV_DIM
_C_S5 = _C_SGU + 2 * SGU_DIM
_C_DT = _C_S5 + S5_DIM
_C_END = _C_DT + LANES


def _inproj_kernel(xa_ref, xb_ref, sh_ref, sc_ref, g_ref, win_ref, sg_ref, sw_ref, sb_ref,
                   z_ref, xbc_ref, ysgu_ref, s5_ref, dt_ref, w_ref):
    i = pl.program_id(0)

    @pl.when(i == 0)
    def _():
        moves = ((0, OFF_DT, _C_Z), (OFF_SGU, IN_DIM, _C_SGU), (OFF_DT, OFF_SGU, _C_DT))
        for src0, src1, dst0 in moves:
            for r0 in range(src0, src1, ROW_SLAB):
                n = min(ROW_SLAB, src1 - r0)
                w_ref[dst0 + r0 - src0:dst0 + r0 - src0 + n, :] = win_ref[r0:r0 + n, :].astype(BF16)
        n_pad = LANES - N_DIR * SSD_HEADS
        w_ref[_C_END - n_pad:_C_END, :] = jnp.zeros((n_pad, D_MODEL), BF16)

    r = _mod_row(i, TOK_TILE)
    x = jnp.where(i < N_CTX_TILES, xa_ref[...], xb_ref[...])
    shift = sh_ref[pl.ds(r, 1), :]
    scale = sc_ref[pl.ds(r, 1), :]
    h = (_rms(x) * g_ref[...]) * (1.0 + scale) + shift
    hb = h.astype(BF16)
    proj = lambda c0, c1: lax.dot_general(hb, w_ref[c0:c1, :], _NT, preferred_element_type=F32)
    z_ref[...] = proj(_C_Z, _C_XBC)
    xbc_ref[...] = proj(_C_XBC, _C_SGU)
    s5_ref[...] = proj(_C_S5, _C_DT)
    dt_ref[...] = proj(_C_DT, _C_END)

    uv = _gelu_tanh(proj(_C_SGU, _C_S5))
    u = uv[:, :SGU_DIM]
    v = _rms(uv[:, SGU_DIM:]) * sg_ref[...]
    lo_lane = lax.broadcasted_iota(jnp.int32, (CHUNK, LANES), 1) < (LANES // 2)
    for c in range(TOK_TILE // CHUNK):
        rows = slice(c * CHUNK, (c + 1) * CHUNK)
        mix = []
        for pr in range(SGU_HEADS // 2):
            vp = v[rows, pr * LANES:(pr + 1) * LANES]
            rhs = jnp.concatenate([jnp.where(lo_lane, vp, 0.0).astype(BF16),
                                   jnp.where(lo_lane, 0.0, vp).astype(BF16)], axis=0)
            mix.append(jnp.dot(sw_ref[pr], rhs, preferred_element_type=F32))
        ysgu_ref[rows, :] = u[rows, :] * (jnp.concatenate(mix, axis=1) + sb_ref[...])


def _in_proj(xa, xb, b_is_stream, l, mods, norm_g, w_in, sgu_g, sgu_w_pair, sgu_b_full):
    tok = lambda w: pl.BlockSpec((TOK_TILE, w), lambda i: (i, 0))
    widths = (SSD_INNER, SSD_CONV_DIM, SGU_DIM, S5_DIM, LANES)
    sa, sb = _split_tok_specs(TOK_TILE, D_MODEL, b_is_stream)
    return pl.pallas_call(
        _inproj_kernel,
        grid=(N_TOK // TOK_TILE,),
        in_specs=[sa, sb, _mod_spec(l, 0), _mod_spec(l, 1), _layer_spec(l, (1, D_MODEL)),
                  _layer_spec(l, (IN_DIM, D_MODEL), pipeline_mode=pl.Buffered(1)),
                  _layer_spec(l, (1, SGU_DIM)), _layer_spec(l, (SGU_HEADS // 2, CHUNK, 2 * CHUNK)),
                  _layer_spec(l, (CHUNK, SGU_DIM))],
        out_specs=[tok(w) for w in widths],
        out_shape=[jax.ShapeDtypeStruct((N_TOK, w), F32) for w in widths],
        scratch_shapes=[pltpu.VMEM((_C_END, D_MODEL), BF16)],
        compiler_params=_params(),
        name="in_proj",
    )(xa, xb, mods, mods, norm_g, w_in, sgu_g, sgu_w_pair, sgu_b_full)


def _ssd_kernel(*refs, L, has_h0, n_alias, emit_state):
    z_ref, xbc_ref, dt_ref, cw_ref, cb_ref, dtb_ref, alog_ref, dvec_ref, ng_ref = refs[:9]
    k = 9
    h0_ref = refs[k] if has_h0 else None
    k += int(has_h0) + n_alias
    y_ref = refs[k]
    hout_ref = refs[k + 1] if emit_state else None
    xpad, xc, acol, atr, cbs, bmt, dtsp, yacc, hst = refs[k + 1 + int(emit_state):]
    Q = CHUNK
    nc = L // Q
    halo = SUBLANES
    pad = (SSD_CONV - 1) // 2

    xpad[0:halo, :] = jnp.zeros((halo, SSD_CONV_DIM), F32)
    xpad[halo + L:2 * halo + L, :] = jnp.zeros((halo, SSD_CONV_DIM), F32)
    for c in range(nc):
        xpad[halo + c * Q:halo + (c + 1) * Q, :] = xbc_ref[c * Q:(c + 1) * Q, :]
    win = Q + 2 * halo
    for c in range(nc):
        for lb in range(0, SSD_CONV_DIM, CONV_LANES):
            ln = slice(lb, lb + CONV_LANES)
            xa = xpad[c * Q:c * Q + win, ln]
            acc = cb_ref[:, ln] + cw_ref[pad:pad + 1, ln] * xa[halo:halo + Q, :]
            for t in range(SSD_CONV):
                if t != pad:
                    rolled = pltpu.roll(xa, (pad - t) % win, 0)
                    acc = acc + cw_ref[t:t + 1, ln] * rolled[halo:halo + Q, :]
            xc[c * Q:(c + 1) * Q, ln] = _silu(acc)

    raw = dt_ref[...] + dtb_ref[...]
    dt = jnp.maximum(raw, 0.0) + jnp.log(1.0 + jnp.exp(-jnp.abs(raw)))
    dtsp[...] = dt
    a_neg = -jnp.exp(alog_ref[...])
    row = lax.broadcasted_iota(jnp.int32, (Q, Q), 0)
    col = lax.broadcasted_iota(jnp.int32, (Q, Q), 1)
    lower = row >= col
    upper = col >= row
    tri_l = lower.astype(F32)
    tri_u = upper.astype(F32)
    fwd_lane = lax.broadcasted_iota(jnp.int32, (Q, LANES), 1) < SSD_HEADS
    for c in range(nc):
        dta = dtsp[c * Q:(c + 1) * Q, :] * a_neg
        pre = jnp.dot(tri_l, dta, precision=lax.Precision.HIGHEST, preferred_element_type=F32)
        suf = jnp.dot(tri_u, dta, precision=lax.Precision.HIGHEST, preferred_element_type=F32)
        a = jnp.where(fwd_lane, pre, suf)
        acol[c * Q:(c + 1) * Q, :] = a
        atr[c] = a.T[0:N_DIR * SSD_HEADS, :]
        for g in range(SSD_GROUPS):
            b0 = SSD_INNER + g * SSD_STATE
            c0 = SSD_INNER + (SSD_GROUPS + g) * SSD_STATE
            bm = xc[c * Q:(c + 1) * Q, b0:b0 + SSD_STATE]
            cbs[c, g] = lax.dot_general(xc[c * Q:(c + 1) * Q, c0:c0 + SSD_STATE].astype(BF16),
                                        bm.astype(BF16), _NT, preferred_element_type=F32)
            bmt[c, g] = bm.T.astype(BF16)

    sel_row = lax.broadcasted_iota(jnp.int32, (3 * LANES, SSD_INNER), 0) & (LANES - 1)
    sel_head = lax.broadcasted_iota(jnp.int32, (3 * LANES, SSD_INNER), 1) >> (SSD_HEAD_DIM.bit_length() - 1)
    sel = [(sel_row == sel_head + d * SSD_HEADS).astype(BF16) for d in range(N_DIR)]
    mask2 = [jnp.concatenate([m, m], axis=1) for m in (lower, upper)]

    for d in range(N_DIR):
        for p in range(N_PAIR):
            if has_h0:
                hst[d, p] = h0_ref[d, p * PAIR:(p + 1) * PAIR, :].T
            else:
                hst[d, p] = jnp.zeros((SSD_STATE, PAIR), F32)

    lo_lane = lax.broadcasted_iota(jnp.int32, (Q, PAIR), 1) < SSD_HEAD_DIM

    def chunk(c, d):
        r0 = pl.multiple_of(c * Q, Q)
        rows = pl.ds(r0, Q)
        a = acol[rows, :]
        a_t = atr[c]
        dtp = jnp.dot(_split3(dtsp[rows, :]), sel[d], preferred_element_type=F32)
        a_end = a[Q - 1:Q, :] if d == 0 else a[0:1, :]
        dec = jnp.exp(a_end)
        for g in range(SSD_GROUPS):
            c0 = SSD_INNER + (SSD_GROUPS + g) * SSD_STATE
            cmb = xc[rows, c0:c0 + SSD_STATE].astype(BF16)
            cb = cbs[c, g]
            cb2 = jnp.concatenate([cb, cb], axis=1)
            for pr in range(2):
                p = g * 2 + pr
                j0 = d * SSD_HEADS + 2 * p
                j1 = j0 + 1
                xs = xc[rows, p * PAIR:(p + 1) * PAIR]
                ab0 = jnp.broadcast_to(a[:, j0:j0 + 1], (Q, Q))
                ab1 = jnp.broadcast_to(a[:, j1:j1 + 1], (Q, Q))
                seg = (jnp.concatenate([ab0, ab1], axis=1)
                       - jnp.concatenate([a_t[j0:j0 + 1, :], a_t[j1:j1 + 1, :]], axis=1))
                m = (cb2 * jnp.exp(jnp.where(mask2[d], seg, -jnp.inf))).astype(BF16)
                xdt = xs * dtp[:, p * PAIR:(p + 1) * PAIR]
                rhs = jnp.concatenate([jnp.where(lo_lane, xdt, 0.0).astype(BF16),
                                       jnp.where(lo_lane, 0.0, xdt).astype(BF16)], axis=0)
                y_diag = jnp.dot(m, rhs, preferred_element_type=F32)
                hp = hst[d, p]
                y_off = jnp.dot(cmb, hp.astype(BF16), preferred_element_type=F32)
                a_pair = jnp.where(lo_lane, ab0, ab1)
                y = y_diag + jnp.exp(a_pair) * y_off
                a_end_pair = jnp.where(lo_lane[0:1, :], a_end[:, j0:j0 + 1], a_end[:, j1:j1 + 1])
                xw = (xdt * jnp.exp(a_end_pair - a_pair)).astype(BF16)
                s_new = jnp.dot(bmt[c, g], xw, preferred_element_type=F32)
                decp = jnp.where(lo_lane[0:1, :], dec[:, j0:j0 + 1], dec[:, j1:j1 + 1])
                hst[d, p] = decp * hp + s_new
                if d == 0:
                    y = y + dvec_ref[:, p * PAIR:(p + 1) * PAIR] * xs
                yacc[d, rows, p * PAIR:(p + 1) * PAIR] = y

    def both(t, carry):
        chunk(t, 0)
        chunk(nc - 1 - t, 1)
        return carry

    lax.fori_loop(0, nc, both, 0)
    for c in range(nc):
        rows = slice(c * Q, (c + 1) * Q)
        y = (yacc[0, rows, :] + yacc[1, rows, :]) * _silu(z_ref[rows, :])
        y_ref[rows, :] = _rms(y) * ng_ref[...]
    if emit_state:
        for d in range(N_DIR):
            for p in range(N_PAIR):
                hout_ref[d, p * PAIR:(p + 1) * PAIR, :] = hst[d, p].T


def _ssd_mixer(z, xbc, dtr, prm, l, *, L, nb, blk0, h0=None, y_buf=None, st_buf=None, emit_state):
    seq = lambda w: pl.BlockSpec((L, w), lambda b: (blk0 + b, 0))
    st = pl.BlockSpec((None, None, N_DIR, SSD_HP, SSD_STATE), lambda b: (b, l, 0, 0, 0))
    args = [z, xbc, dtr, prm["conv_w"], prm["conv_b"], prm["dt_bias"], prm["a_log"], prm["d_vec"],
            prm["norm_g"]]
    in_specs = [seq(SSD_INNER), seq(SSD_CONV_DIM), seq(LANES),
                _layer_spec(l, (SSD_CONV, SSD_CONV_DIM)), _layer_spec(l, (1, SSD_CONV_DIM)),
                _layer_spec(l, (1, LANES)), _layer_spec(l, (1, LANES)),
                _layer_spec(l, (1, SSD_INNER)), _layer_spec(l, (1, SSD_INNER))]
    if h0 is not None:
        args.append(h0)
        in_specs.append(st)
    out_shape = [jax.ShapeDtypeStruct((N_TOK, SSD_INNER), F32)]
    out_specs = [seq(SSD_INNER)]
    aliases = {}
    if y_buf is not None:
        aliases[len(args)] = 0
        args.append(y_buf)
        in_specs.append(_any_spec())
    if emit_state:
        out_shape.append(jax.ShapeDtypeStruct((BATCH, DEPTH, N_DIR, SSD_HP, SSD_STATE), F32))
        out_specs.append(st)
        if st_buf is not None:
            aliases[len(args)] = 1
            args.append(st_buf)
            in_specs.append(_any_spec())
    res = pl.pallas_call(
        functools.partial(_ssd_kernel, L=L, has_h0=h0 is not None, n_alias=len(aliases),
                          emit_state=emit_state),
        grid=(nb,),
        in_specs=in_specs,
        out_specs=out_specs,
        out_shape=out_shape,
        input_output_aliases=aliases,
        scratch_shapes=[pltpu.VMEM((L + 2 * SUBLANES, SSD_CONV_DIM), F32),
                        pltpu.VMEM((L, SSD_CONV_DIM), F32),
                        pltpu.VMEM((L, LANES), F32),
                        pltpu.VMEM((L // CHUNK, N_DIR * SSD_HEADS, CHUNK), F32),
                        pltpu.VMEM((L // CHUNK, SSD_GROUPS, CHUNK, CHUNK), F32),
                        pltpu.VMEM((L // CHUNK, SSD_GROUPS, SSD_STATE, CHUNK), BF16),
                        pltpu.VMEM((L, LANES), F32),
                        pltpu.VMEM((N_DIR, L, SSD_INNER), F32),
                        pltpu.VMEM((N_DIR, N_PAIR, PAIR, SSD_STATE), F32)],
        compiler_params=_params(),
        name=f"ssd_mixer_L{L}",
    )(*args)
    return res if emit_state else (res[0], None)


_LOG2_CH = S5_GROUP_CH.bit_length() - 1
_LOG2_ST = S5_STATE.bit_length() - 1


def _cmul(ar, ai, br, bi):
    return ar * br - ai * bi, ar * bi + ai * br


def _s5prep_kernel(lre_ref, lim_ref, ldt_ref, btr_ref, bti_ref, ctr_ref, cti_ref,
                   bblk_ref, cblk_ref, pw_ref):
    brow = lax.broadcasted_iota(jnp.int32, (S5_DIM, S5_LANES), 0) >> _LOG2_CH
    bcol = lax.broadcasted_iota(jnp.int32, (S5_DIM, S5_LANES), 1) >> _LOG2_ST
    bmask = brow == bcol
    p1 = lax.broadcasted_iota(jnp.int32, (S5_SEG, S5_LANES), 0) + 1
    for d in range(N_DIR):
        lre = lre_ref[d]
        lim = lim_ref[d]
        step = jnp.exp(ldt_ref[d])
        mag = jnp.exp(lre * step)
        lbr = mag * jnp.cos(lim * step)
        lbi = mag * jnp.sin(lim * step)
        den = lre * lre + lim * lim
        nr = lbr - 1.0
        cr = (nr * lre + lbi * lim) / den
        ci = (lbi * lre - nr * lim) / den
        br, bi = _cmul(cr, ci, btr_ref[...], bti_ref[...])
        br = jnp.where(bmask, br, 0.0).astype(BF16)
        bi = jnp.where(bmask, bi, 0.0).astype(BF16)
        for sl in range(S5_NSLAB):
            bblk_ref[d, sl, :, 0:S5_SLAB] = br[:, sl * S5_SLAB:(sl + 1) * S5_SLAB]
            bblk_ref[d, sl, :, S5_SLAB:2 * S5_SLAB] = bi[:, sl * S5_SLAB:(sl + 1) * S5_SLAB]
        rr = jnp.ones((S5_SEG, S5_LANES), F32)
        ri = jnp.zeros((S5_SEG, S5_LANES), F32)
        sr, si = lbr, lbi
        for k in range(S5_SEG.bit_length()):
            bit = ((p1 >> k) & 1) == 1
            tr, ti = _cmul(rr, ri, sr, si)
            rr = jnp.where(bit, tr, rr)
            ri = jnp.where(bit, ti, ri)
            sr, si = _cmul(sr, si, sr, si)
        pw_ref[d, 0, 0:S5_SEG, :] = rr
        pw_ref[d, 1, 0:S5_SEG, :] = ri
        n = S5_SEG
        while n < S5_PW_ROWS:
            tr, ti = _cmul(pw_ref[d, 0, 0:n, :], pw_ref[d, 1, 0:n, :],
                           pw_ref[d, 0, n - 1:n, :], pw_ref[d, 1, n - 1:n, :])
            pw_ref[d, 0, n:2 * n, :] = tr
            pw_ref[d, 1, n:2 * n, :] = ti
            n *= 2
    crow = lax.broadcasted_iota(jnp.int32, (S5_LANES, S5_DIM), 0) >> _LOG2_ST
    ccol = lax.broadcasted_iota(jnp.int32, (S5_LANES, S5_DIM), 1) >> _LOG2_CH
    cmask = crow == ccol
    cr = jnp.where(cmask, ctr_ref[...], 0.0).astype(BF16)
    ci = jnp.where(cmask, -cti_ref[...], 0.0).astype(BF16)
    for sl in range(S5_NSLAB):
        cblk_ref[sl, 0:S5_SLAB, :] = cr[sl * S5_SLAB:(sl + 1) * S5_SLAB, :]
        cblk_ref[sl, S5_SLAB:2 * S5_SLAB, :] = ci[sl * S5_SLAB:(sl + 1) * S5_SLAB, :]


def _s5_prep(lam_re, lam_im, log_dt, b_re, b_im, c_re, c_im):
    row = lambda t: t.reshape(DEPTH, N_DIR, 1, S5_LANES)
    ldt = jnp.repeat(log_dt, S5_STATE, axis=-1)
    bt = lambda t: jnp.tile(t.transpose(0, 3, 1, 2).reshape(DEPTH, S5_GROUP_CH, S5_LANES),
                            (1, S5_GROUPS, 1))
    ct = lambda t: jnp.tile(t.transpose(0, 1, 3, 2).reshape(DEPTH, S5_LANES, S5_GROUP_CH),
                            (1, 1, S5_GROUPS))
    vec = pl.BlockSpec((None, N_DIR, 1, S5_LANES), lambda l: (l, 0, 0, 0))
    bsp = pl.BlockSpec((None, S5_DIM, S5_LANES), lambda l: (l, 0, 0))
    csp = pl.BlockSpec((None, S5_LANES, S5_DIM), lambda l: (l, 0, 0))
    return pl.pallas_call(
        _s5prep_kernel,
        grid=(DEPTH,),
        in_specs=[vec, vec, vec, bsp, bsp, csp, csp],
        out_specs=[pl.BlockSpec((None, N_DIR, S5_NSLAB, S5_DIM, 2 * S5_SLAB), lambda l: (l, 0, 0, 0, 0)),
                   pl.BlockSpec((None, S5_NSLAB, 2 * S5_SLAB, S5_DIM), lambda l: (l, 0, 0, 0)),
                   pl.BlockSpec((None, N_DIR, 2, S5_PW_ROWS, S5_LANES), lambda l: (l, 0, 0, 0, 0))],
        out_shape=[jax.ShapeDtypeStruct((DEPTH, N_DIR, S5_NSLAB, S5_DIM, 2 * S5_SLAB), BF16),
                   jax.ShapeDtypeStruct((DEPTH, S5_NSLAB, 2 * S5_SLAB, S5_DIM), BF16),
                   jax.ShapeDtypeStruct((DEPTH, N_DIR, 2, S5_PW_ROWS, S5_LANES), F32)],
        compiler_params=_params(),
        name="s5_prep",
    )(row(lam_re), row(lam_im), row(ldt), bt(b_re), bt(b_im), ct(c_re), ct(c_im))


def _s5_moves(n_seq, n_seg, seq_len, col_major):
    steps = seq_len // n_seg
    if not col_major:
        return [(q * seq_len + s * steps, steps, q * n_seg + s, S5_SEG)
                for q in range(n_seq) for s in range(n_seg)]
    assert n_seq == 1 and n_seg == S5_SEG
    grid_rows = seq_len // GRID_W
    wseg = GRID_W // n_seg
    return [(r * GRID_W + s * wseg, wseg, r * S5_SEG + s, grid_rows * S5_SEG)
            for r in range(grid_rows) for s in range(n_seg)]


def _s5_kernel(*refs, n_seq, n_seg, seq_len, col_major, has_h0, n_alias, emit_state):
    u_ref, bblk_ref, cblk_ref, pw_ref, dvec_ref, gw_ref, gb_ref = refs[:7]
    k = 7
    h0r_ref, h0i_ref = (refs[k], refs[k + 1]) if has_h0 else (None, None)
    k += 2 * int(has_h0) + n_alias
    y_ref = refs[k]
    fr_ref, fi_ref = (refs[k + 1], refs[k + 2]) if emit_state else (None, None)
    up, buf_f, buf_b, yacc, yp, cin, fin = refs[k + 1 + 2 * int(emit_state):]
    assert n_seq * n_seg == S5_SEG and not (has_h0 and n_seg == 1) and not (emit_state and n_seg > 1)
    n_rows = n_seq * seq_len
    steps = seq_len // n_seg
    n_planes = S5_DIM // LANES
    W = S5_SLAB
    moves = _s5_moves(n_seq, n_seg, seq_len, col_major)

    for src, n, dst, stride in moves:
        for t in range(n_planes):
            up[t, pl.ds(dst, n, stride=stride), :] = u_ref[src:src + n, t * LANES:(t + 1) * LANES]

    def u_rows(rs):
        return jnp.concatenate([up[t, rs, :] for t in range(n_planes)], axis=1)

    zero = jnp.zeros((S5_SEG, W), F32)
    zrow = jnp.zeros((1, W), F32)
    for sl in range(S5_NSLAB):
        ln = slice(sl * W, (sl + 1) * W)
        for r0 in range(0, n_rows, ROW_SLAB):
            rs = slice(r0, r0 + ROW_SLAB)
            ub = u_rows(rs).astype(BF16)
            buf_f[rs, :] = jnp.dot(ub, bblk_ref[0, sl], preferred_element_type=F32)
            buf_b[rs, :] = jnp.dot(ub, bblk_ref[1, sl], preferred_element_type=F32)
        lam = [[jnp.broadcast_to(pw_ref[d, c, 0:1, ln], (S5_SEG, W)) for c in range(2)]
               for d in range(N_DIR)]

        def body(t, carry):
            fr, fi, br, bi = carry
            rf = pl.ds(pl.multiple_of(t * S5_SEG, S5_SEG), S5_SEG)
            rb = pl.ds(pl.multiple_of((steps - 1 - t) * S5_SEG, S5_SEG), S5_SEG)
            nfr = lam[0][0] * fr - lam[0][1] * fi + buf_f[rf, 0:W]
            nfi = lam[0][0] * fi + lam[0][1] * fr + buf_f[rf, W:2 * W]
            nbr = lam[1][0] * br - lam[1][1] * bi + buf_b[rb, 0:W]
            nbi = lam[1][0] * bi + lam[1][1] * br + buf_b[rb, W:2 * W]
            buf_f[rf, 0:W] = nfr
            buf_f[rf, W:2 * W] = nfi
            buf_b[rb, 0:W] = nbr
            buf_b[rb, W:2 * W] = nbi
            return nfr, nfi, nbr, nbi

        ends = lax.fori_loop(0, steps, body, (zero, zero, zero, zero), unroll=2)

        if n_seg == 1:
            if emit_state:
                for d in range(N_DIR):
                    fin[d, 0, :, ln] = ends[2 * d]
                    fin[d, 1, :, ln] = ends[2 * d + 1]
        else:
            for d in range(N_DIR):
                er, ei = ends[2 * d], ends[2 * d + 1]
                lpr = pw_ref[d, 0, steps - 1:steps, ln]
                lpi = pw_ref[d, 1, steps - 1:steps, ln]
                pr = h0r_ref[d:d + 1, ln] if has_h0 else zrow
                pi = h0i_ref[d:d + 1, ln] if has_h0 else zrow
                order = range(n_seg) if d == 0 else range(n_seg - 1, -1, -1)
                for s in order:
                    cin[d, 0, s:s + 1, :] = pr
                    cin[d, 1, s:s + 1, :] = pi
                    tr, ti = _cmul(lpr, lpi, pr, pi)
                    pr = er[s:s + 1, :] + tr
                    pi = ei[s:s + 1, :] + ti
            cfr = cin[0, 0]
            cfi = cin[0, 1]
            cbr = cin[1, 0]
            cbi = cin[1, 1]

            def fix(p, carry):
                rows = pl.ds(pl.multiple_of(p * S5_SEG, S5_SEG), S5_SEG)
                q = steps - 1 - p
                tfr, tfi = _cmul(pw_ref[0, 0, pl.ds(p, 1), ln], pw_ref[0, 1, pl.ds(p, 1), ln], cfr, cfi)
                tbr, tbi = _cmul(pw_ref[1, 0, pl.ds(q, 1), ln], pw_ref[1, 1, pl.ds(q, 1), ln], cbr, cbi)
                buf_f[rows, 0:W] = buf_f[rows, 0:W] + buf_b[rows, 0:W] + tfr + tbr
                buf_f[rows, W:2 * W] = buf_f[rows, W:2 * W] + buf_b[rows, W:2 * W] + tfi + tbi
                return carry

            lax.fori_loop(0, steps, fix, 0, unroll=2)

        for r0 in range(0, n_rows, ROW_SLAB):
            rs = slice(r0, r0 + ROW_SLAB)
            hs = buf_f[rs, :] if n_seg > 1 else buf_f[rs, :] + buf_b[rs, :]
            part = jnp.dot(hs.astype(BF16), cblk_ref[sl], preferred_element_type=F32)
            if sl == 0:
                yacc[rs, :] = part
            else:
                yacc[rs, :] = yacc[rs, :] + part

    gwb = gw_ref[...].astype(BF16)
    for r0 in range(0, n_rows, ROW_SLAB):
        rs = slice(r0, r0 + ROW_SLAB)
        y = _gelu_tanh(yacc[rs, :] + dvec_ref[...] * u_rows(rs))
        gate = jnp.dot(y.astype(BF16), gwb, preferred_element_type=F32) + gb_ref[...]
        y = y * jax.nn.sigmoid(gate)
        for t in range(n_planes):
            yp[t, rs, :] = y[:, t * LANES:(t + 1) * LANES]
    for src, n, dst, stride in moves:
        for t in range(n_planes):
            y_ref[src:src + n, t * LANES:(t + 1) * LANES] = yp[t, pl.ds(dst, n, stride=stride), :]
    if emit_state:
        for q in range(n_seq):
            for d in range(N_DIR):
                fr_ref[q, d:d + 1, :] = fin[d, 0, q:q + 1, :]
                fi_ref[q, d:d + 1, :] = fin[d, 1, q:q + 1, :]


def _s5_mixer(u, bblk, cblk, pw, prm, l, *, seq_len, n_seq, nb, blk0, col_major, h0=None, y_buf=None,
              st_buf=None, emit_state):
    n_rows = n_seq * seq_len
    n_seg = S5_SEG // n_seq
    seq = pl.BlockSpec((n_rows, S5_DIM), lambda b: (blk0 + b, 0))
    st = pl.BlockSpec((n_seq, None, N_DIR, S5_LANES), lambda b: (b, l, 0, 0))
    once = dict(pipeline_mode=pl.Buffered(1))
    args = [u, bblk, cblk, pw, prm["d_vec"], prm["glu_w"], prm["glu_b"]]
    in_specs = [seq, _layer_spec(l, (N_DIR, S5_NSLAB, S5_DIM, 2 * S5_SLAB), **once),
                _layer_spec(l, (S5_NSLAB, 2 * S5_SLAB, S5_DIM), **once),
                _layer_spec(l, (N_DIR, 2, S5_PW_ROWS, S5_LANES), **once), _layer_spec(l, (1, S5_DIM)),
                _layer_spec(l, (S5_DIM, S5_DIM)), _layer_spec(l, (1, S5_DIM))]
    if h0 is not None:
        h0_spec = pl.BlockSpec((None, None, N_DIR, S5_LANES), lambda b: (b, l, 0, 0))
        args += list(h0)
        in_specs += [h0_spec, h0_spec]
    out_shape = [jax.ShapeDtypeStruct((N_TOK, S5_DIM), F32)]
    out_specs = [seq]
    aliases = {}
    if y_buf is not None:
        aliases[len(args)] = 0
        args.append(y_buf)
        in_specs.append(_any_spec())
    if emit_state:
        out_shape += [jax.ShapeDtypeStruct((BATCH, DEPTH, N_DIR, S5_LANES), F32)] * 2
        out_specs += [st, st]
        if st_buf is not None:
            for j, buf in enumerate(st_buf):
                aliases[len(args)] = 1 + j
                args.append(buf)
                in_specs.append(_any_spec())
    res = pl.pallas_call(
        functools.partial(_s5_kernel, n_seq=n_seq, n_seg=n_seg, seq_len=seq_len, col_major=col_major,
                          has_h0=h0 is not None, n_alias=len(aliases), emit_state=emit_state),
        grid=(nb,),
        in_specs=in_specs,
        out_specs=out_specs,
        out_shape=out_shape,
        input_output_aliases=aliases,
        scratch_shapes=[pltpu.VMEM((S5_DIM // LANES, n_rows, LANES), F32),
                        pltpu.VMEM((n_rows, 2 * S5_SLAB), F32),
                        pltpu.VMEM((n_rows, 2 * S5_SLAB), F32),
                        pltpu.VMEM((n_rows, S5_DIM), F32),
                        pltpu.VMEM((S5_DIM // LANES, n_rows, LANES), F32),
                        pltpu.VMEM((N_DIR, 2, S5_SEG, S5_SLAB), F32),
                        pltpu.VMEM((N_DIR, 2, S5_SEG, S5_LANES), F32)],
        compiler_params=_params(),
        name=f"s5_mixer_L{seq_len}",
    )(*args)
    return (res[0], res[1], res[2]) if emit_state else (res[0], None, None)


def _outffn_kernel(xa_ref, xb_ref, ys_ref, yg_ref, y5_ref, g1_ref, sh_ref, sc_ref, g2_ref, ng_ref,
                   fg_ref, wo_ref, w1_ref, w2_ref, *rest, final):
    n_out = 2 if final else 1
    o_refs = rest[:n_out]
    wob, w1b, w2b = rest[n_out:]
    i = pl.program_id(0)

    @pl.when(i < FFN_PREP)
    def _():
        wob[i] = wo_ref[...].astype(BF16)
        w1b[i] = w1_ref[...].astype(BF16)
        w2b[i] = w2_ref[...].astype(BF16)

    @pl.when(i >= FFN_PREP)
    def _():
        t = i - FFN_PREP
        n_ctx = N_CTX // FFN_TILE
        r = _mod_row(t, FFN_TILE)
        row = lambda ref: ref[pl.ds(r, 1), :]
        x = jnp.where(t < n_ctx, xa_ref[...], xb_ref[...])
        mixed = jnp.concatenate([ys_ref[...].astype(BF16), yg_ref[...].astype(BF16),
                                 y5_ref[...].astype(BF16)], axis=1)
        w_out = wob[...].reshape(D_MODEL, D_MODEL)
        x1 = x + row(g1_ref) * jnp.dot(mixed, w_out, preferred_element_type=F32)
        h = ((_rms(x1) * ng_ref[...]) * (1.0 + row(sc_ref)) + row(sh_ref)).astype(BF16)
        acc = jnp.zeros((FFN_TILE, D_MODEL), F32)
        for j in range(FFN_PREP):
            a = jnp.maximum(jnp.dot(h, w1b[j], preferred_element_type=F32), 0.0)
            acc = acc + jnp.dot((a * a).astype(BF16), w2b[j], preferred_element_type=F32)
        x2 = x1 + row(g2_ref) * acc
        if not final:
            o_refs[0][...] = x2
        else:
            y = _rms(x2) * fg_ref[...]

            @pl.when(t < n_ctx)
            def _():
                o_refs[0][...] = y

            @pl.when(t >= n_ctx)
            def _():
                o_refs[1][...] = y


def _out_ffn(xa, xb, b_is_stream, y_ssd, y_sgu, y_s5, l, mods, norm_g, final_g, w_out, w1, w2, *, final):
    tile_of = lambda i: jnp.maximum(i - FFN_PREP, 0)
    tok = lambda w: pl.BlockSpec((FFN_TILE, w), lambda i: (tile_of(i), 0))
    chunk_of = lambda i: jnp.minimum(i, FFN_PREP - 1)
    sa, sb = _split_tok_specs(FFN_TILE, D_MODEL, b_is_stream, tile_of)
    if final:
        out_specs = list(_split_tok_specs(FFN_TILE, D_MODEL, False, tile_of))
        out_shape = [jax.ShapeDtypeStruct((N_CTX, D_MODEL), F32), jax.ShapeDtypeStruct((N_LAT, D_MODEL), F32)]
    else:
        out_specs = [tok(D_MODEL)]
        out_shape = [jax.ShapeDtypeStruct((N_TOK, D_MODEL), F32)]
    ko, kf = D_MODEL // FFN_PREP, D_FF // FFN_PREP
    return pl.pallas_call(
        functools.partial(_outffn_kernel, final=final),
        grid=(FFN_PREP + N_TOK // FFN_TILE,),
        in_specs=[sa, sb, tok(SSD_INNER), tok(SGU_DIM), tok(S5_DIM),
                  _mod_spec(l, 2), _mod_spec(l, 3), _mod_spec(l, 4), _mod_spec(l, 5),
                  _layer_spec(l, (1, D_MODEL)), pl.BlockSpec((1, D_MODEL), lambda i: (0, 0)),
                  pl.BlockSpec((None, ko, D_MODEL), lambda i: (l, chunk_of(i), 0)),
                  pl.BlockSpec((None, D_MODEL, kf), lambda i: (l, 0, chunk_of(i))),
                  pl.BlockSpec((None, kf, D_MODEL), lambda i: (l, chunk_of(i), 0))],
        out_specs=out_specs,
        out_shape=out_shape,
        scratch_shapes=[pltpu.VMEM((FFN_PREP, ko, D_MODEL), BF16),
                        pltpu.VMEM((FFN_PREP, D_MODEL, kf), BF16),
                        pltpu.VMEM((FFN_PREP, kf, D_MODEL), BF16)],
        compiler_params=_params(),
        name="out_ffn_final" if final else "out_ffn",
    )(xa, xb, y_ssd, y_sgu, y_s5, mods, mods, mods, mods, norm_g, final_g, w_out, w1, w2)


def kernel(x_prompt, x_sample, state_ssd, state_s5_re, state_s5_im, c, c_ctx, ada_w, ada_b, norm1_g,
           norm2_g, w_in, ssd_conv_w, ssd_conv_b, ssd_dt_bias, ssd_a_log, ssd_d, ssd_norm_g,
           sgu_norm_g, sgu_w, sgu_b, s5_lambda_re, s5_lambda_im, s5_log_dt, s5_b_re, s5_b_im,
           s5_c_re, s5_c_im, s5_d, s5_glu_w, s5_glu_b, w_out, ffn_w1, ffn_w2, final_norm_g):
    cvec = jnp.concatenate([c_ctx[None, :], c,
                            jnp.zeros((MOD_ROWS - 1 - DEC_BATCH, D_MODEL), F32)], axis=0)
    mods = _adaln_mods(cvec, ada_w, ada_b)
    bblk, cblk, pw = _s5_prep(s5_lambda_re, s5_lambda_im, s5_log_dt, s5_b_re, s5_b_im,
                              s5_c_re, s5_c_im)

    w_in_t = w_in.transpose(0, 2, 1)
    vec =lambda t: t.reshape(DEPTH, 1, -1)
    lane_pad = lambda t: jnp.pad(vec(t), ((0, 0), (0, 0), (0, LANES - t[0].size)))
    ssd_prm = dict(conv_w=ssd_conv_w, conv_b=vec(ssd_conv_b), dt_bias=lane_pad(ssd_dt_bias),
                   a_log=lane_pad(ssd_a_log), d_vec=vec(jnp.repeat(ssd_d, SSD_HEAD_DIM, axis=-1)),
                   norm_g=vec(ssd_norm_g))
    s5_prm = dict(d_vec=vec(s5_d), glu_w=s5_glu_w, glu_b=vec(s5_glu_b))
    sgu_w_pair = sgu_w.reshape(DEPTH, SGU_HEADS // 2, 2, CHUNK, CHUNK).transpose(0, 1, 3, 2, 4)
    sgu_w_pair = sgu_w_pair.reshape(DEPTH, SGU_HEADS // 2, CHUNK, 2 * CHUNK).astype(BF16)
    sgu_b_full = jnp.repeat(sgu_b.transpose(0, 2, 1), SGU_DIM // SGU_HEADS, axis=2)
    norm1 = vec(norm1_g)
    norm2 = vec(norm2_g)
    sgu_g = vec(sgu_norm_g)
    final_g = final_norm_g.reshape(1, D_MODEL)

    lat_ssd = state_ssd.reshape(DEC_BATCH, DEPTH, N_DIR, SSD_HP, SSD_STATE)
    lat_s5 = (state_s5_re.reshape(DEC_BATCH, DEPTH, N_DIR, S5_LANES),
              state_s5_im.reshape(DEC_BATCH, DEPTH, N_DIR, S5_LANES))
    lat_blk = N_CTX // DEC_SEQ

    xa = x_prompt.reshape(N_CTX, D_MODEL)
    xb = x_sample.reshape(N_LAT, D_MODEL)
    b_is_stream = False
    st_ssd = None
    st_s5 = None
    for l in range(DEPTH):
        z, xbc, y_sgu, s5u, dtr = _in_proj(xa, xb, b_is_stream, l, mods, norm1, w_in_t, sgu_g,
                                           sgu_w_pair, sgu_b_full)
        y_ssd, st_ssd = _ssd_mixer(z, xbc, dtr, ssd_prm, l, L=SEQ, nb=BATCH, blk0=0, st_buf=st_ssd,
                                   emit_state=True)
        y_ssd, _ = _ssd_mixer(z, xbc, dtr, ssd_prm, l, L=DEC_SEQ, nb=DEC_BATCH, blk0=lat_blk,
                              h0=lat_ssd, y_buf=y_ssd, emit_state=False)
        y_s5, fr, fi = _s5_mixer(s5u, bblk, cblk, pw, s5_prm, l, seq_len=SEQ, n_seq=S5_SEG,
                                 nb=BATCH // S5_SEG, blk0=0, col_major=False, st_buf=st_s5,
                                 emit_state=True)
        st_s5 = (fr, fi)
        y_s5, _, _ = _s5_mixer(s5u, bblk, cblk, pw, s5_prm, l, seq_len=DEC_SEQ, n_seq=1, nb=DEC_BATCH,
                               blk0=lat_blk, col_major=True, h0=lat_s5, y_buf=y_s5, emit_state=False)
        final = l == DEPTH - 1
        out = _out_ffn(xa, xb, b_is_stream, y_ssd, y_sgu, y_s5, l, mods, norm2, final_g, w_out, ffn_w1,
                       ffn_w2, final=final)
        if not final:
            xa = xb = out[0]
            b_is_stream = True

    y_prompt = out[0].reshape(BATCH, SEQ, D_MODEL)
    y_sample = out[1].reshape(DEC_BATCH, DEC_SEQ, D_MODEL)
    new_state_ssd = st_ssd.reshape(BATCH, DEPTH, N_DIR, SSD_HEADS, SSD_HEAD_DIM, SSD_STATE)
    s5_shape = (BATCH, DEPTH, N_DIR, S5_GROUPS, S5_STATE)
    return (y_prompt, y_sample, new_state_ssd, st_s5[0].reshape(s5_shape), st_s5[1].reshape(s5_shape))
```

```python
import functools
import math

import jax
import jax.numpy as jnp
from jax import lax
from jax.experimental import pallas as pl
from jax.experimental.pallas import tpu as pltpu

F32 = jnp.float32
BF16 = jnp.bfloat16

D_MODEL = 1024
BATCH = 16
SEQ = 256
DEPTH = 2
DEC_BATCH = 2
DEC_SEQ = 1024
GRID_W = 64
CHUNK = 128
N_DIR = 2
EPS = 1e-6
SSD_INNER = 512
SSD_HEAD_DIM = 64
SSD_HEADS = 8
SSD_GROUPS = 2
SSD_STATE = 128
SSD_CONV = 5
SSD_CONV_DIM = SSD_INNER + 2 * SSD_GROUPS * SSD_STATE
SGU_DIM = 256
SGU_HEADS = 4
S5_DIM = 256
S5_GROUP_CH = 16
S5_GROUPS = 16
S5_STATE = 64
D_FF = 4 * D_MODEL
OFF_XBC = SSD_INNER
OFF_DT = OFF_XBC + SSD_CONV_DIM
OFF_SGU = OFF_DT + N_DIR * SSD_HEADS
OFF_S5 = OFF_SGU + 2 * SGU_DIM
IN_DIM = OFF_S5 + S5_DIM

N_CTX = BATCH * SEQ
N_LAT = DEC_BATCH * DEC_SEQ
N_TOK = N_CTX + N_LAT
LANES = 128
SUBLANES = 8
MOD_ROWS = SUBLANES
SSD_HP = SSD_HEADS * SSD_HEAD_DIM
S5_LANES = S5_GROUPS * S5_STATE
S5_SEG = SUBLANES
S5_PW_ROWS = DEC_SEQ // S5_SEG
S5_SLAB = 256
S5_NSLAB = S5_LANES // S5_SLAB
PAIR = 2 * SSD_HEAD_DIM
N_PAIR = SSD_HEADS // 2
TOK_TILE = 512
N_CTX_TILES = N_CTX // TOK_TILE
FFN_TILE = 512
FFN_PREP = 8
ROW_SLAB = 256
CONV_LANES = 256
SSD_CTX_SEQS = 2
VMEM_LIMIT = 56 * 1024 * 1024

_NT = (((1,), (1,)), ((), ()))


def _params(n_axes=1):
    return pltpu.CompilerParams(dimension_semantics=("arbitrary",) * n_axes,
                                vmem_limit_bytes=VMEM_LIMIT)


def _layer_spec(l, shape, **kw):
    return pl.BlockSpec((None,) + tuple(shape), lambda *_: (l,) + (0,) * len(shape), **kw)


def _mod_spec(l, k):
    return pl.BlockSpec((None, MOD_ROWS, D_MODEL), lambda *_: (l, 0, k))


def _any_spec():
    return pl.BlockSpec(memory_space=pl.ANY)


def _mod_row(i, tm):
    n_ctx = N_CTX // tm
    return jnp.where(i < n_ctx, 0, 1 + (i - n_ctx) // (DEC_SEQ // tm))


def _split_tok_specs(tm, width, b_has_ctx_rows, tile_of=lambda i: i):
    n_ctx = N_CTX // tm
    b_off = n_ctx if b_has_ctx_rows else 0
    a = pl.BlockSpec((tm, width), lambda i: (jnp.minimum(tile_of(i), n_ctx - 1), 0))
    b = pl.BlockSpec((tm, width), lambda i: (jnp.maximum(tile_of(i), n_ctx) - n_ctx + b_off, 0))
    return a, b


def _silu(x):
    return x * jax.nn.sigmoid(x)


def _gelu_tanh(x):
    c = math.sqrt(2.0 / math.pi)
    return 0.5 * x * (1.0 + jnp.tanh(c * (x + 0.044715 * (x * x * x))))


def _rms(x):
    return x * lax.rsqrt(jnp.mean(x * x, axis=-1, keepdims=True) + EPS)


def _bdot(a, b):
    return jnp.dot(a.astype(BF16), b.astype(BF16), preferred_element_type=F32)


def _split3(x):
    hi = x.astype(BF16)
    r = x - hi.astype(F32)
    mid = r.astype(BF16)
    lo = (r - mid.astype(F32)).astype(BF16)
    return jnp.concatenate([hi, mid, lo], axis=1)


def _mod_kernel(c_ref, w_ref, b_ref, o_ref):
    o_ref[...] = _bdot(_silu(c_ref[...]), w_ref[...]) + b_ref[...]


def _adaln_mods(cvec, ada_w, ada_b):
    n_blk = 6
    return pl.pallas_call(
        _mod_kernel,
        grid=(DEPTH, n_blk),
        in_specs=[pl.BlockSpec((MOD_ROWS, D_MODEL), lambda l, j: (0, 0)),
                  pl.BlockSpec((None, D_MODEL, D_MODEL), lambda l, j: (l, 0, j)),
                  pl.BlockSpec((None, 1, D_MODEL), lambda l, j: (l, 0, j))],
        out_specs=pl.BlockSpec((None, MOD_ROWS, D_MODEL), lambda l, j: (l, 0, j)),
        out_shape=jax.ShapeDtypeStruct((DEPTH, MOD_ROWS, 6 * D_MODEL), F32),
        compiler_params=_params(2),
        name="adaln_mod",
    )(cvec, ada_w, ada_b.reshape(DEPTH, 1, 6 * D_MODEL))


_C_Z = 0
_C_XBC = _C_Z + SSD_INNER
_C_SGU = _C_XBC + SSD_CONV_DIM
_C_S5 = _C_SGU + 2 * SGU_DIM
_C_DT = _C_S5 + S5_DIM
_C_END = _C_DT + LANES


def _inproj_kernel(xa_ref, xb_ref, sh_ref, sc_ref, g_ref, win_ref, sg_ref, sw_ref, sb_ref,
                   z_ref, xbc_ref, ysgu_ref, s5_ref, dt_ref, w_ref):
    i = pl.program_id(0)

    @pl.when(i == 0)
    def _():
        moves = ((0, OFF_DT, _C_Z), (OFF_SGU, IN_DIM, _C_SGU), (OFF_DT, OFF_SGU, _C_DT))
        for src0, src1, dst0 in moves:
            for r0 in range(src0, src1, ROW_SLAB):
                n = min(ROW_SLAB, src1 - r0)
                w_ref[dst0 + r0 - src0:dst0 + r0 - src0 + n, :] = win_ref[r0:r0 + n, :].astype(BF16)
        n_pad = LANES - N_DIR * SSD_HEADS
        w_ref[_C_END - n_pad:_C_END, :] = jnp.zeros((n_pad, D_MODEL), BF16)

    r = _mod_row(i, TOK_TILE)
    x = jnp.where(i < N_CTX_TILES, xa_ref[...], xb_ref[...])
    shift = sh_ref[pl.ds(r, 1), :]
    scale = sc_ref[pl.ds(r, 1), :]
    h = (_rms(x) * g_ref[...]) * (1.0 + scale) + shift
    hb = h.astype(BF16)
    proj = lambda c0, c1: lax.dot_general(hb, w_ref[c0:c1, :], _NT, preferred_element_type=F32)
    z_ref[...] = proj(_C_Z, _C_XBC)
    xbc_ref[...] = proj(_C_XBC, _C_SGU)
    s5_ref[...] = proj(_C_S5, _C_DT)
    dt_ref[...] = proj(_C_DT, _C_END)

    uv = _gelu_tanh(proj(_C_SGU, _C_S5))
    u = uv[:, :SGU_DIM]
    v = _rms(uv[:, SGU_DIM:]) * sg_ref[...]
    lo_lane = lax.broadcasted_iota(jnp.int32, (CHUNK, LANES), 1) < (LANES // 2)
    for c in range(TOK_TILE // CHUNK):
        rows = slice(c * CHUNK, (c + 1) * CHUNK)
        mix = []
        for pr in range(SGU_HEADS // 2):
            vp = v[rows, pr * LANES:(pr + 1) * LANES]
            rhs = jnp.concatenate([jnp.where(lo_lane, vp, 0.0).astype(BF16),
                                   jnp.where(lo_lane, 0.0, vp).astype(BF16)], axis=0)
            mix.append(jnp.dot(sw_ref[pr], rhs, preferred_element_type=F32))
        ysgu_ref[rows, :] = u[rows, :] * (jnp.concatenate(mix, axis=1) + sb_ref[...])


def _in_proj(xa, xb, b_is_stream, l, mods, norm_g, w_in, sgu_g, sgu_w_pair, sgu_b_full):
    tok = lambda w: pl.BlockSpec((TOK_TILE, w), lambda i: (i, 0))
    widths = (SSD_INNER, SSD_CONV_DIM, SGU_DIM, S5_DIM, LANES)
    sa, sb = _split_tok_specs(TOK_TILE, D_MODEL, b_is_stream)
    return pl.pallas_call(
        _inproj_kernel,
        grid=(N_TOK // TOK_TILE,),
        in_specs=[sa, sb, _mod_spec(l, 0), _mod_spec(l, 1), _layer_spec(l, (1, D_MODEL)),
                  _layer_spec(l, (IN_DIM, D_MODEL), pipeline_mode=pl.Buffered(1)),
                  _layer_spec(l, (1, SGU_DIM)), _layer_spec(l, (SGU_HEADS // 2, CHUNK, 2 * CHUNK)),
                  _layer_spec(l, (CHUNK, SGU_DIM))],
        out_specs=[tok(w) for w in widths],
        out_shape=[jax.ShapeDtypeStruct((N_TOK, w), F32) for w in widths],
        scratch_shapes=[pltpu.VMEM((_C_END, D_MODEL), BF16)],
        compiler_params=_params(),
        name="in_proj",
    )(xa, xb, mods, mods, norm_g, w_in, sgu_g, sgu_w_pair, sgu_b_full)


def _ssd_kernel(*refs, L, n_seq, has_h0, n_alias, emit_state):
    z_ref, xbc_ref, dt_ref, cw_ref, cb_ref, dtb_ref, alog_ref, dvec_ref, ng_ref = refs[:9]
    k = 9
    h0_ref = refs[k] if has_h0 else None
    k += int(has_h0) + n_alias
    y_ref = refs[k]
    hout_ref = refs[k + 1] if emit_state else None
    xpad, xc, acol, atr, cbs, bmt, dtsp, yacc, hst = refs[k + 1 + int(emit_state):]
    Q = CHUNK
    nc = L // Q
    halo = SUBLANES
    pad = (SSD_CONV - 1) // 2

    pitch = L + halo
    for q in range(n_seq + 1):
        xpad[q * pitch:q * pitch + halo, :] = jnp.zeros((halo, SSD_CONV_DIM), F32)
    for g in range(n_seq * nc):
        o = halo + (g // nc) * pitch + (g % nc) * Q
        xpad[o:o + Q, :] = xbc_ref[g * Q:(g + 1) * Q, :]
    win = Q + 2 * halo
    for c in range(n_seq * nc):
        for lb in range(0, SSD_CONV_DIM, CONV_LANES):
            ln = slice(lb, lb + CONV_LANES)
            o = (c // nc) * pitch + (c % nc) * Q
            xa = xpad[o:o + win, ln]
            acc = cb_ref[:, ln] + cw_ref[pad:pad + 1, ln] * xa[halo:halo + Q, :]
            for t in range(SSD_CONV):
                if t != pad:
                    rolled = pltpu.roll(xa, (pad - t) % win, 0)
                    acc = acc + cw_ref[t:t + 1, ln] * rolled[halo:halo + Q, :]
            xc[c * Q:(c + 1) * Q, ln] = _silu(acc)

    raw = dt_ref[...] + dtb_ref[...]
    dt = jnp.maximum(raw, 0.0) + jnp.log(1.0 + jnp.exp(-jnp.abs(raw)))
    dtsp[...] = dt
    a_neg = -jnp.exp(alog_ref[...])
    row = lax.broadcasted_iota(jnp.int32, (Q, Q), 0)
    col = lax.broadcasted_iota(jnp.int32, (Q, Q), 1)
    lower = row >= col
    upper = col >= row
    tri_l = lower.astype(F32)
    tri_u = upper.astype(F32)
    fwd_lane = lax.broadcasted_iota(jnp.int32, (Q, LANES), 1) < SSD_HEADS
    for c in range(n_seq * nc):
        dta = dtsp[c * Q:(c + 1) * Q, :] * a_neg
        pre = jnp.dot(tri_l, dta, precision=lax.Precision.HIGHEST, preferred_element_type=F32)
        suf = jnp.dot(tri_u, dta, precision=lax.Precision.HIGHEST, preferred_element_type=F32)
        a = jnp.where(fwd_lane, pre, suf)
        acol[c * Q:(c + 1) * Q, :] = a
        atr[c] = a.T[0:N_DIR * SSD_HEADS, :]
        for g in range(SSD_GROUPS):
            b0 = SSD_INNER + g * SSD_STATE
            c0 = SSD_INNER + (SSD_GROUPS + g) * SSD_STATE
            bm = xc[c * Q:(c + 1) * Q, b0:b0 + SSD_STATE]
            cbs[c, g] = lax.dot_general(xc[c * Q:(c + 1) * Q, c0:c0 + SSD_STATE].astype(BF16),
                                        bm.astype(BF16), _NT, preferred_element_type=F32)
            bmt[c, g] = bm.T.astype(BF16)

    sel_row = lax.broadcasted_iota(jnp.int32, (3 * LANES, SSD_INNER), 0) & (LANES - 1)
    sel_head = lax.broadcasted_iota(jnp.int32, (3 * LANES, SSD_INNER), 1) >> (SSD_HEAD_DIM.bit_length() - 1)
    sel = [(sel_row == sel_head + d * SSD_HEADS).astype(BF16) for d in range(N_DIR)]
    mask2 = [jnp.concatenate([m, m], axis=1) for m in (lower, upper)]

    for q in range(n_seq):
        for d in range(N_DIR):
            for p in range(N_PAIR):
                if has_h0:
                    hst[q, d, p] = h0_ref[q, d, p * PAIR:(p + 1) * PAIR, :].T
                else:
                    hst[q, d, p] = jnp.zeros((SSD_STATE, PAIR), F32)

    lo_lane = lax.broadcasted_iota(jnp.int32, (Q, PAIR), 1) < SSD_HEAD_DIM

    def chunk(c, d, q):
        r0 = pl.multiple_of(c * Q, Q)
        rows = pl.ds(r0, Q)
        a = acol[rows, :]
        a_t = atr[c]
        dtp = jnp.dot(_split3(dtsp[rows, :]), sel[d], preferred_element_type=F32)
        a_end = a[Q - 1:Q, :] if d == 0 else a[0:1, :]
        dec = jnp.exp(a_end)
        for g in range(SSD_GROUPS):
            c0 = SSD_INNER + (SSD_GROUPS + g) * SSD_STATE
            cmb = xc[rows, c0:c0 + SSD_STATE].astype(BF16)
            cb = cbs[c, g]
            cb2 = jnp.concatenate([cb, cb], axis=1)
            for pr in range(2):
                p = g * 2 + pr
                j0 = d * SSD_HEADS + 2 * p
                j1 = j0 + 1
                xs = xc[rows, p * PAIR:(p + 1) * PAIR]
                ab0 = jnp.broadcast_to(a[:, j0:j0 + 1], (Q, Q))
                ab1 = jnp.broadcast_to(a[:, j1:j1 + 1], (Q, Q))
                seg = (jnp.concatenate([ab0, ab1], axis=1)
                       - jnp.concatenate([a_t[j0:j0 + 1, :], a_t[j1:j1 + 1, :]], axis=1))
                m = (cb2 * jnp.exp(jnp.where(mask2[d], seg, -jnp.inf))).astype(BF16)
                xdt = xs * dtp[:, p * PAIR:(p + 1) * PAIR]
                rhs = jnp.concatenate([jnp.where(lo_lane, xdt, 0.0).astype(BF16),
                                       jnp.where(lo_lane, 0.0, xdt).astype(BF16)], axis=0)
                y_diag = jnp.dot(m, rhs, preferred_element_type=F32)
                hp = hst[q, d, p]
                y_off = jnp.dot(cmb, hp.astype(BF16), preferred_element_type=F32)
                a_pair = jnp.where(lo_lane, ab0, ab1)
                y = y_diag + jnp.exp(a_pair) * y_off
                a_end_pair = jnp.where(lo_lane[0:1, :], a_end[:, j0:j0 + 1], a_end[:, j1:j1 + 1])
                xw = (xdt * jnp.exp(a_end_pair - a_pair)).astype(BF16)
                s_new = jnp.dot(bmt[c, g], xw, preferred_element_type=F32)
                decp = jnp.where(lo_lane[0:1, :], dec[:, j0:j0 + 1], dec[:, j1:j1 + 1])
                hst[q, d, p] = decp * hp + s_new
                if d == 0:
                    y = y + dvec_ref[:, p * PAIR:(p + 1) * PAIR] * xs
                yacc[d, rows, p * PAIR:(p + 1) * PAIR] = y

    def step(t, carry):
        for q in range(n_seq):
            chunk(q * nc + t, 0, q)
            chunk(q * nc + nc - 1 - t, 1, q)
        return carry

    lax.fori_loop(0, nc, step, 0, unroll=2 if n_seq == 1 else 1)
    for c in range(n_seq * nc):
        rows = slice(c * Q, (c + 1) * Q)
        y = (yacc[0, rows, :] + yacc[1, rows, :]) * _silu(z_ref[rows, :])
        y_ref[rows, :] = _rms(y) * ng_ref[...]
    if emit_state:
        for q in range(n_seq):
            for d in range(N_DIR):
                for p in range(N_PAIR):
                    hout_ref[q, d, p * PAIR:(p + 1) * PAIR, :] = hst[q, d, p].T


def _ssd_mixer(z, xbc, dtr, prm, l, *, L, n_seq, nb, blk0, h0=None, y_buf=None, st_buf=None, emit_state):
    R = n_seq * L
    seq = lambda w: pl.BlockSpec((R, w), lambda b: (blk0 + b, 0))
    st = pl.BlockSpec((n_seq, None, N_DIR, SSD_HP, SSD_STATE), lambda b: (b, l, 0, 0, 0))
    args = [z, xbc, dtr, prm["conv_w"], prm["conv_b"], prm["dt_bias"], prm["a_log"], prm["d_vec"],
            prm["norm_g"]]
    in_specs = [seq(SSD_INNER), seq(SSD_CONV_DIM), seq(LANES),
                _layer_spec(l, (SSD_CONV, SSD_CONV_DIM)), _layer_spec(l, (1, SSD_CONV_DIM)),
                _layer_spec(l, (1, LANES)), _layer_spec(l, (1, LANES)),
                _layer_spec(l, (1, SSD_INNER)), _layer_spec(l, (1, SSD_INNER))]
    if h0 is not None:
        args.append(h0)
        in_specs.append(st)
    out_shape = [jax.ShapeDtypeStruct((N_TOK, SSD_INNER), F32)]
    out_specs = [seq(SSD_INNER)]
    aliases = {}
    if y_buf is not None:
        aliases[len(args)] = 0
        args.append(y_buf)
        in_specs.append(_any_spec())
    if emit_state:
        out_shape.append(jax.ShapeDtypeStruct((BATCH, DEPTH, N_DIR, SSD_HP, SSD_STATE), F32))
        out_specs.append(st)
        if st_buf is not None:
            aliases[len(args)] = 1
            args.append(st_buf)
            in_specs.append(_any_spec())
    res = pl.pallas_call(
        functools.partial(_ssd_kernel, L=L, n_seq=n_seq, has_h0=h0 is not None, n_alias=len(aliases),
                          emit_state=emit_state),
        grid=(nb,),
        in_specs=in_specs,
        out_specs=out_specs,
        out_shape=out_shape,
        input_output_aliases=aliases,
        scratch_shapes=[pltpu.VMEM((n_seq * (L + SUBLANES) + SUBLANES, SSD_CONV_DIM), F32),
                        pltpu.VMEM((R, SSD_CONV_DIM), F32),
                        pltpu.VMEM((R, LANES), F32),
                        pltpu.VMEM((R // CHUNK, N_DIR * SSD_HEADS, CHUNK), F32),
                        pltpu.VMEM((R // CHUNK, SSD_GROUPS, CHUNK, CHUNK), F32),
                        pltpu.VMEM((R // CHUNK, SSD_GROUPS, SSD_STATE, CHUNK), BF16),
                        pltpu.VMEM((R, LANES), F32),
                        pltpu.VMEM((N_DIR, R, SSD_INNER), F32),
                        pltpu.VMEM((n_seq, N_DIR, N_PAIR, PAIR, SSD_STATE), F32)],
        compiler_params=_params(),
        name=f"ssd_mixer_L{L}",
    )(*args)
    return res if emit_state else (res[0], None)


_LOG2_CH = S5_GROUP_CH.bit_length() - 1
_LOG2_ST = S5_STATE.bit_length() - 1


def _cmul(ar, ai, br, bi):
    return ar * br - ai * bi, ar * bi + ai * br


def _s5prep_kernel(lre_ref, lim_ref, ldt_ref, btr_ref, bti_ref, ctr_ref, cti_ref,
                   bblk_ref, cblk_ref, pw_ref):
    brow = lax.broadcasted_iota(jnp.int32, (S5_DIM, S5_LANES), 0) >> _LOG2_CH
    bcol = lax.broadcasted_iota(jnp.int32, (S5_DIM, S5_LANES), 1) >> _LOG2_ST
    bmask = brow == bcol
    p1 = lax.broadcasted_iota(jnp.int32, (S5_SEG, S5_LANES), 0) + 1
    for d in range(N_DIR):
        lre = lre_ref[d]
        lim = lim_ref[d]
        step = jnp.exp(ldt_ref[d])
        mag = jnp.exp(lre * step)
        lbr = mag * jnp.cos(lim * step)
        lbi = mag * jnp.sin(lim * step)
        den = lre * lre + lim * lim
        nr = lbr - 1.0
        cr = (nr * lre + lbi * lim) / den
        ci = (lbi * lre - nr * lim) / den
        br, bi = _cmul(cr, ci, btr_ref[...], bti_ref[...])
        br = jnp.where(bmask, br, 0.0).astype(BF16)
        bi = jnp.where(bmask, bi, 0.0).astype(BF16)
        for sl in range(S5_NSLAB):
            bblk_ref[d, sl, :, 0:S5_SLAB] = br[:, sl * S5_SLAB:(sl + 1) * S5_SLAB]
            bblk_ref[d, sl, :, S5_SLAB:2 * S5_SLAB] = bi[:, sl * S5_SLAB:(sl + 1) * S5_SLAB]
        rr = jnp.ones((S5_SEG, S5_LANES), F32)
        ri = jnp.zeros((S5_SEG, S5_LANES), F32)
        sr, si = lbr, lbi
        for k in range(S5_SEG.bit_length()):
            bit = ((p1 >> k) & 1) == 1
            tr, ti = _cmul(rr, ri, sr, si)
            rr = jnp.where(bit, tr, rr)
            ri = jnp.where(bit, ti, ri)
            sr, si = _cmul(sr, si, sr, si)
        pw_ref[d, 0, 0:S5_SEG, :] = rr
        pw_ref[d, 1, 0:S5_SEG, :] = ri
        n = S5_SEG
        while n < S5_PW_ROWS:
            tr, ti = _cmul(pw_ref[d, 0, 0:n, :], pw_ref[d, 1, 0:n, :],
                           pw_ref[d, 0, n - 1:n, :], pw_ref[d, 1, n - 1:n, :])
            pw_ref[d, 0, n:2 * n, :] = tr
            pw_ref[d, 1, n:2 * n, :] = ti
            n *= 2
    crow = lax.broadcasted_iota(jnp.int32, (S5_LANES, S5_DIM), 0) >> _LOG2_ST
    ccol = lax.broadcasted_iota(jnp.int32, (S5_LANES, S5_DIM), 1) >> _LOG2_CH
    cmask = crow == ccol
    cr = jnp.where(cmask, ctr_ref[...], 0.0).astype(BF16)
    ci = jnp.where(cmask, -cti_ref[...], 0.0).astype(BF16)
    for sl in range(S5_NSLAB):
        cblk_ref[sl, 0:S5_SLAB, :] = cr[sl * S5_SLAB:(sl + 1) * S5_SLAB, :]
        cblk_ref[sl, S5_SLAB:2 * S5_SLAB, :] = ci[sl * S5_SLAB:(sl + 1) * S5_SLAB, :]


def _s5_prep(lam_re, lam_im, log_dt, b_re, b_im, c_re, c_im):
    row = lambda t: t.reshape(DEPTH, N_DIR, 1, S5_LANES)
    ldt = jnp.repeat(log_dt, S5_STATE, axis=-1)
    bt = lambda t: jnp.tile(t.transpose(0, 3, 1, 2).reshape(DEPTH, S5_GROUP_CH, S5_LANES),
                            (1, S5_GROUPS, 1))
    ct = lambda t: jnp.tile(t.transpose(0, 1, 3, 2).reshape(DEPTH, S5_LANES, S5_GROUP_CH),
                            (1, 1, S5_GROUPS))
    vec = pl.BlockSpec((None, N_DIR, 1, S5_LANES), lambda l: (l, 0, 0, 0))
    bsp = pl.BlockSpec((None, S5_DIM, S5_LANES), lambda l: (l, 0, 0))
    csp = pl.BlockSpec((None, S5_LANES, S5_DIM), lambda l: (l, 0, 0))
    return pl.pallas_call(
        _s5prep_kernel,
        grid=(DEPTH,),
        in_specs=[vec, vec, vec, bsp, bsp, csp, csp],
        out_specs=[pl.BlockSpec((None, N_DIR, S5_NSLAB, S5_DIM, 2 * S5_SLAB), lambda l: (l, 0, 0, 0, 0)),
                   pl.BlockSpec((None, S5_NSLAB, 2 * S5_SLAB, S5_DIM), lambda l: (l, 0, 0, 0)),
                   pl.BlockSpec((None, N_DIR, 2, S5_PW_ROWS, S5_LANES), lambda l: (l, 0, 0, 0, 0))],
        out_shape=[jax.ShapeDtypeStruct((DEPTH, N_DIR, S5_NSLAB, S5_DIM, 2 * S5_SLAB), BF16),
                   jax.ShapeDtypeStruct((DEPTH, S5_NSLAB, 2 * S5_SLAB, S5_DIM), BF16),
                   jax.ShapeDtypeStruct((DEPTH, N_DIR, 2, S5_PW_ROWS, S5_LANES), F32)],
        compiler_params=_params(),
        name="s5_prep",
    )(row(lam_re), row(lam_im), row(ldt), bt(b_re), bt(b_im), ct(c_re), ct(c_im))


def _s5_moves(n_seq, n_seg, seq_len, col_major):
    steps = seq_len // n_seg
    if not col_major:
        return [(q * seq_len + s * steps, steps, q * n_seg + s, S5_SEG)
                for q in range(n_seq) for s in range(n_seg)]
    assert n_seq == 1 and n_seg == S5_SEG
    grid_rows = seq_len // GRID_W
    wseg = GRID_W // n_seg
    return [(r * GRID_W + s * wseg, wseg, r * S5_SEG + s, grid_rows * S5_SEG)
            for r in range(grid_rows) for s in range(n_seg)]


def _s5_kernel(*refs, n_seq, n_seg, seq_len, col_major, has_h0, n_alias, emit_state):
    u_ref, bblk_ref, cblk_ref, pw_ref, dvec_ref, gw_ref, gb_ref = refs[:7]
    k = 7
    h0r_ref, h0i_ref = (refs[k], refs[k + 1]) if has_h0 else (None, None)
    k += 2 * int(has_h0) + n_alias
    y_ref = refs[k]
    fr_ref, fi_ref = (refs[k + 1], refs[k + 2]) if emit_state else (None, None)
    up, buf_f, buf_b, yacc, yp, cin, fin = refs[k + 1 + 2 * int(emit_state):]
    assert n_seq * n_seg == S5_SEG and not (has_h0 and n_seg == 1) and not (emit_state and n_seg > 1)
    n_rows = n_seq * seq_len
    steps = seq_len // n_seg
    n_planes = S5_DIM // LANES
    W = S5_SLAB
    moves = _s5_moves(n_seq, n_seg, seq_len, col_major)

    for src, n, dst, stride in moves:
        for t in range(n_planes):
            up[t, pl.ds(dst, n, stride=stride), :] = u_ref[src:src + n, t * LANES:(t + 1) * LANES]

    def u_rows(rs):
        return jnp.concatenate([up[t, rs, :] for t in range(n_planes)], axis=1)

    zero = jnp.zeros((S5_SEG, W), F32)
    zrow = jnp.zeros((1, W), F32)
    for sl in range(S5_NSLAB):
        ln = slice(sl * W, (sl + 1) * W)
        for r0 in range(0, n_rows, ROW_SLAB):
            rs = slice(r0, r0 + ROW_SLAB)
            ub = u_rows(rs).astype(BF16)
            buf_f[rs, :] = jnp.dot(ub, bblk_ref[0, sl], preferred_element_type=F32)
            buf_b[rs, :] = jnp.dot(ub, bblk_ref[1, sl], preferred_element_type=F32)
        lam = [[jnp.broadcast_to(pw_ref[d, c, 0:1, ln], (S5_SEG, W)) for c in range(2)]
               for d in range(N_DIR)]

        def body(t, carry):
            fr, fi, br, bi = carry
            rf = pl.ds(pl.multiple_of(t * S5_SEG, S5_SEG), S5_SEG)
            rb = pl.ds(pl.multiple_of((steps - 1 - t) * S5_SEG, S5_SEG), S5_SEG)
            nfr = lam[0][0] * fr - lam[0][1] * fi + buf_f[rf, 0:W]
            nfi = lam[0][0] * fi + lam[0][1] * fr + buf_f[rf, W:2 * W]
            nbr = lam[1][0] * br - lam[1][1] * bi + buf_b[rb, 0:W]
            nbi = lam[1][0] * bi + lam[1][1] * br + buf_b[rb, W:2 * W]
            buf_f[rf, 0:W] = nfr
            buf_f[rf, W:2 * W] = nfi
            buf_b[rb, 0:W] = nbr
            buf_b[rb, W:2 * W] = nbi
            return nfr, nfi, nbr, nbi

        ends = lax.fori_loop(0, steps, body, (zero, zero, zero, zero), unroll=2)

        if n_seg == 1:
            if emit_state:
                for d in range(N_DIR):
                    fin[d, 0, :, ln] = ends[2 * d]
                    fin[d, 1, :, ln] = ends[2 * d + 1]
        else:
            for d in range(N_DIR):
                er, ei = ends[2 * d], ends[2 * d + 1]
                lpr = pw_ref[d, 0, steps - 1:steps, ln]
                lpi = pw_ref[d, 1, steps - 1:steps, ln]
                pr = h0r_ref[d:d + 1, ln] if has_h0 else zrow
                pi = h0i_ref[d:d + 1, ln] if has_h0 else zrow
                order = range(n_seg) if d == 0 else range(n_seg - 1, -1, -1)
                for s in order:
                    cin[d, 0, s:s + 1, :] = pr
                    cin[d, 1, s:s + 1, :] = pi
                    tr, ti = _cmul(lpr, lpi, pr, pi)
                    pr = er[s:s + 1, :] + tr
                    pi = ei[s:s + 1, :] + ti
            cfr = cin[0, 0]
            cfi = cin[0, 1]
            cbr = cin[1, 0]
            cbi = cin[1, 1]

            def fix(p, carry):
                rows = pl.ds(pl.multiple_of(p * S5_SEG, S5_SEG), S5_SEG)
                q = steps - 1 - p
                tfr, tfi = _cmul(pw_ref[0, 0, pl.ds(p, 1), ln], pw_ref[0, 1, pl.ds(p, 1), ln], cfr, cfi)
                tbr, tbi = _cmul(pw_ref[1, 0, pl.ds(q, 1), ln], pw_ref[1, 1, pl.ds(q, 1), ln], cbr, cbi)
                buf_f[rows, 0:W] = buf_f[rows, 0:W] + buf_b[rows, 0:W] + tfr + tbr
                buf_f[rows, W:2 * W] = buf_f[rows, W:2 * W] + buf_b[rows, W:2 * W] + tfi + tbi
                return carry

            lax.fori_loop(0, steps, fix, 0, unroll=2)

        for r0 in range(0, n_rows, ROW_SLAB):
            rs = slice(r0, r0 + ROW_SLAB)
            hs = buf_f[rs, :] if n_seg > 1 else buf_f[rs, :] + buf_b[rs, :]
            part = jnp.dot(hs.astype(BF16), cblk_ref[sl], preferred_element_type=F32)
            if sl == 0:
                yacc[rs, :] = part
            else:
                yacc[rs, :] = yacc[rs, :] + part

    gwb = gw_ref[...].astype(BF16)
    for r0 in range(0, n_rows, ROW_SLAB):
        rs = slice(r0, r0 + ROW_SLAB)
        y = _gelu_tanh(yacc[rs, :] + dvec_ref[...] * u_rows(rs))
        gate = jnp.dot(y.astype(BF16), gwb, preferred_element_type=F32) + gb_ref[...]
        y = y * jax.nn.sigmoid(gate)
        for t in range(n_planes):
            yp[t, rs, :] = y[:, t * LANES:(t + 1) * LANES]
    for src, n, dst, stride in moves:
        for t in range(n_planes):
            y_ref[src:src + n, t * LANES:(t + 1) * LANES] = yp[t, pl.ds(dst, n, stride=stride), :]
    if emit_state:
        for q in range(n_seq):
            for d in range(N_DIR):
                fr_ref[q, d:d + 1, :] = fin[d, 0, q:q + 1, :]
                fi_ref[q, d:d + 1, :] = fin[d, 1, q:q + 1, :]


def _s5_mixer(u, bblk, cblk, pw, prm, l, *, seq_len, n_seq, nb, blk0, col_major, h0=None, y_buf=None,
              st_buf=None, emit_state):
    n_rows = n_seq * seq_len
    n_seg = S5_SEG // n_seq
    seq = pl.BlockSpec((n_rows, S5_DIM), lambda b: (blk0 + b, 0))
    st = pl.BlockSpec((n_seq, None, N_DIR, S5_LANES), lambda b: (b, l, 0, 0))
    once = dict(pipeline_mode=pl.Buffered(1))
    args = [u, bblk, cblk, pw, prm["d_vec"], prm["glu_w"], prm["glu_b"]]
    in_specs = [seq, _layer_spec(l, (N_DIR, S5_NSLAB, S5_DIM, 2 * S5_SLAB), **once),
                _layer_spec(l, (S5_NSLAB, 2 * S5_SLAB, S5_DIM), **once),
                _layer_spec(l, (N_DIR, 2, S5_PW_ROWS if n_seg > 1 else S5_SEG, S5_LANES), **once),
                _layer_spec(l, (1, S5_DIM)),
                _layer_spec(l, (S5_DIM, S5_DIM)), _layer_spec(l, (1, S5_DIM))]
    if h0 is not None:
        h0_spec = pl.BlockSpec((None, None, N_DIR, S5_LANES), lambda b: (b, l, 0, 0))
        args += list(h0)
        in_specs += [h0_spec, h0_spec]
    out_shape = [jax.ShapeDtypeStruct((N_TOK, S5_DIM), F32)]
    out_specs = [seq]
    aliases = {}
    if y_buf is not None:
        aliases[len(args)] = 0
        args.append(y_buf)
        in_specs.append(_any_spec())
    if emit_state:
        out_shape += [jax.ShapeDtypeStruct((BATCH, DEPTH, N_DIR, S5_LANES), F32)] * 2
        out_specs += [st, st]
        if st_buf is not None:
            for j, buf in enumerate(st_buf):
                aliases[len(args)] = 1 + j
                args.append(buf)
                in_specs.append(_any_spec())
    res = pl.pallas_call(
        functools.partial(_s5_kernel, n_seq=n_seq, n_seg=n_seg, seq_len=seq_len, col_major=col_major,
                          has_h0=h0 is not None, n_alias=len(aliases), emit_state=emit_state),
        grid=(nb,),
        in_specs=in_specs,
        out_specs=out_specs,
        out_shape=out_shape,
        input_output_aliases=aliases,
        scratch_shapes=[pltpu.VMEM((S5_DIM // LANES, n_rows, LANES), F32),
                        pltpu.VMEM((n_rows, 2 * S5_SLAB), F32),
                        pltpu.VMEM((n_rows, 2 * S5_SLAB), F32),
                        pltpu.VMEM((n_rows, S5_DIM), F32),
                        pltpu.VMEM((S5_DIM // LANES, n_rows, LANES), F32),
                        pltpu.VMEM((N_DIR, 2, S5_SEG, S5_SLAB), F32),
                        pltpu.VMEM((N_DIR, 2, S5_SEG, S5_LANES), F32)],
        compiler_params=_params(),
        name=f"s5_mixer_L{seq_len}",
    )(*args)
    return (res[0], res[1], res[2]) if emit_state else (res[0], None, None)


def _outffn_kernel(xa_ref, xb_ref, ys_ref, yg_ref, y5_ref, g1_ref, sh_ref, sc_ref, g2_ref, ng_ref,
                   fg_ref, wo_ref, w1_ref, w2_ref, *rest, final):
    n_out = 2 if final else 1
    o_refs = rest[:n_out]
    wob, w1b, w2b = rest[n_out:]
    i = pl.program_id(0)

    @pl.when(i < FFN_PREP)
    def _():
        wob[i] = wo_ref[...].astype(BF16)
        w1b[i] = w1_ref[...].astype(BF16)
        w2b[i] = w2_ref[...].astype(BF16)

    @pl.when(i >= FFN_PREP)
    def _():
        t = i - FFN_PREP
        n_ctx = N_CTX // FFN_TILE
        r = _mod_row(t, FFN_TILE)
        row = lambda ref: ref[pl.ds(r, 1), :]
        x = jnp.where(t < n_ctx, xa_ref[...], xb_ref[...])
        mixed = jnp.concatenate([ys_ref[...].astype(BF16), yg_ref[...].astype(BF16),
                                 y5_ref[...].astype(BF16)], axis=1)
        w_out = wob[...].reshape(D_MODEL, D_MODEL)
        x1 = x + row(g1_ref) * jnp.dot(mixed, w_out, preferred_element_type=F32)
        h = ((_rms(x1) * ng_ref[...]) * (1.0 + row(sc_ref)) + row(sh_ref)).astype(BF16)
        acc = jnp.zeros((FFN_TILE, D_MODEL), F32)
        for j in range(FFN_PREP):
            a = jnp.maximum(jnp.dot(h, w1b[j], preferred_element_type=F32), 0.0)
            acc = acc + jnp.dot((a * a).astype(BF16), w2b[j], preferred_element_type=F32)
        x2 = x1 + row(g2_ref) * acc
        if not final:
            o_refs[0][...] = x2
        else:
            y = _rms(x2) * fg_ref[...]

            @pl.when(t < n_ctx)
            def _():
                o_refs[0][...] = y

            @pl.when(t >= n_ctx)
            def _():
                o_refs[1][...] = y


def _out_ffn(xa, xb, b_is_stream, y_ssd, y_sgu, y_s5, l, mods, norm_g, final_g, w_out, w1, w2, *, final):
    tile_of = lambda i: jnp.maximum(i - FFN_PREP, 0)
    tok = lambda w: pl.BlockSpec((FFN_TILE, w), lambda i: (tile_of(i), 0))
    chunk_of = lambda i: jnp.minimum(i, FFN_PREP - 1)
    sa, sb = _split_tok_specs(FFN_TILE, D_MODEL, b_is_stream, tile_of)
    if final:
        out_specs = list(_split_tok_specs(FFN_TILE, D_MODEL, False, tile_of))
        out_shape = [jax.ShapeDtypeStruct((N_CTX, D_MODEL), F32), jax.ShapeDtypeStruct((N_LAT, D_MODEL), F32)]
    else:
        out_specs = [tok(D_MODEL)]
        out_shape = [jax.ShapeDtypeStruct((N_TOK, D_MODEL), F32)]
    ko, kf = D_MODEL // FFN_PREP, D_FF // FFN_PREP
    return pl.pallas_call(
        functools.partial(_outffn_kernel, final=final),
        grid=(FFN_PREP + N_TOK // FFN_TILE,),
        in_specs=[sa, sb, tok(SSD_INNER), tok(SGU_DIM), tok(S5_DIM),
                  _mod_spec(l, 2), _mod_spec(l, 3), _mod_spec(l, 4), _mod_spec(l, 5),
                  _layer_spec(l, (1, D_MODEL)), pl.BlockSpec((1, D_MODEL), lambda i: (0, 0)),
                  pl.BlockSpec((None, ko, D_MODEL), lambda i: (l, chunk_of(i), 0)),
                  pl.BlockSpec((None, D_MODEL, kf), lambda i: (l, 0, chunk_of(i))),
                  pl.BlockSpec((None, kf, D_MODEL), lambda i: (l, chunk_of(i), 0))],
        out_specs=out_specs,
        out_shape=out_shape,
        scratch_shapes=[pltpu.VMEM((FFN_PREP, ko, D_MODEL), BF16),
                        pltpu.VMEM((FFN_PREP, D_MODEL, kf), BF16),
                        pltpu.VMEM((FFN_PREP, kf, D_MODEL), BF16)],
        compiler_params=_params(),
        name="out_ffn_final" if final else "out_ffn",
    )(xa, xb, y_ssd, y_sgu, y_s5, mods, mods, mods, mods, norm_g, final_g, w_out, w1, w2)


def kernel(x_prompt, x_sample, state_ssd, state_s5_re, state_s5_im, c, c_ctx, ada_w, ada_b, norm1_g,
           norm2_g, w_in, ssd_conv_w, ssd_conv_b, ssd_dt_bias, ssd_a_log, ssd_d, ssd_norm_g,
           sgu_norm_g, sgu_w, sgu_b, s5_lambda_re, s5_lambda_im, s5_log_dt, s5_b_re, s5_b_im,
           s5_c_re, s5_c_im, s5_d, s5_glu_w, s5_glu_b, w_out, ffn_w1, ffn_w2, final_norm_g):
    cvec = jnp.concatenate([c_ctx[None, :], c,
                            jnp.zeros((MOD_ROWS - 1 - DEC_BATCH, D_MODEL), F32)], axis=0)
    mods = _adaln_mods(cvec, ada_w, ada_b)
    bblk, cblk, pw = _s5_prep(s5_lambda_re, s5_lambda_im, s5_log_dt, s5_b_re, s5_b_im,
                              s5_c_re, s5_c_im)

    w_in_t = w_in.transpose(0, 2, 1)
    vec =lambda t: t.reshape(DEPTH, 1, -1)
    lane_pad = lambda t: jnp.pad(vec(t), ((0, 0), (0, 0), (0, LANES - t[0].size)))
    ssd_prm = dict(conv_w=ssd_conv_w, conv_b=vec(ssd_conv_b), dt_bias=lane_pad(ssd_dt_bias),
                   a_log=lane_pad(ssd_a_log), d_vec=vec(jnp.repeat(ssd_d, SSD_HEAD_DIM, axis=-1)),
                   norm_g=vec(ssd_norm_g))
    s5_prm = dict(d_vec=vec(s5_d), glu_w=s5_glu_w, glu_b=vec(s5_glu_b))
    sgu_w_pair = sgu_w.reshape(DEPTH, SGU_HEADS // 2, 2, CHUNK, CHUNK).transpose(0, 1, 3, 2, 4)
    sgu_w_pair = sgu_w_pair.reshape(DEPTH, SGU_HEADS // 2, CHUNK, 2 * CHUNK).astype(BF16)
    sgu_b_full = jnp.repeat(sgu_b.transpose(0, 2, 1), SGU_DIM // SGU_HEADS, axis=2)
    norm1 = vec(norm1_g)
    norm2 = vec(norm2_g)
    sgu_g = vec(sgu_norm_g)
    final_g = final_norm_g.reshape(1, D_MODEL)

    lat_ssd = state_ssd.reshape(DEC_BATCH, DEPTH, N_DIR, SSD_HP, SSD_STATE)
    lat_s5 = (state_s5_re.reshape(DEC_BATCH, DEPTH, N_DIR, S5_LANES),
              state_s5_im.reshape(DEC_BATCH, DEPTH, N_DIR, S5_LANES))
    lat_blk = N_CTX // DEC_SEQ

    xa = x_prompt.reshape(N_CTX, D_MODEL)
    xb = x_sample.reshape(N_LAT, D_MODEL)
    b_is_stream = False
    st_ssd = None
    st_s5 = None
    for l in range(DEPTH):
        z, xbc, y_sgu, s5u, dtr = _in_proj(xa, xb, b_is_stream, l, mods, norm1, w_in_t, sgu_g,
                                           sgu_w_pair, sgu_b_full)
        y_ssd, st_ssd = _ssd_mixer(z, xbc, dtr, ssd_prm, l, L=SEQ, n_seq=SSD_CTX_SEQS,
                                   nb=BATCH // SSD_CTX_SEQS, blk0=0, st_buf=st_ssd, emit_state=True)
        y_ssd, _ = _ssd_mixer(z, xbc, dtr, ssd_prm, l, L=DEC_SEQ, n_seq=1, nb=DEC_BATCH, blk0=lat_blk,
                              h0=lat_ssd, y_buf=y_ssd, emit_state=False)
        y_s5, fr, fi = _s5_mixer(s5u, bblk, cblk, pw, s5_prm, l, seq_len=SEQ, n_seq=S5_SEG,
                                 nb=BATCH // S5_SEG, blk0=0, col_major=False, st_buf=st_s5,
                                 emit_state=True)
        st_s5 = (fr, fi)
        y_s5, _, _ = _s5_mixer(s5u, bblk, cblk, pw, s5_prm, l, seq_len=DEC_SEQ, n_seq=1, nb=DEC_BATCH,
                               blk0=lat_blk, col_major=True, h0=lat_s5, y_buf=y_s5, emit_state=False)
        final = l == DEPTH - 1
        out = _out_ffn(xa, xb, b_is_stream, y_ssd, y_sgu, y_s5, l, mods, norm2, final_g, w_out, ffn_w1,
                       ffn_w2, final=final)
        if not final:
            xa = xb = out[0]
            b_is_stream = True

    y_prompt = out[0].reshape(BATCH, SEQ, D_MODEL)
    y_sample = out[1].reshape(DEC_BATCH, DEC_SEQ, D_MODEL)
    new_state_ssd = st_ssd.reshape(BATCH, DEPTH, N_DIR, SSD_HEADS, SSD_HEAD_DIM, SSD_STATE)
    s5_shape = (BATCH, DEPTH, N_DIR, S5_GROUPS, S5_STATE)
    return (y_prompt, y_sample, new_state_ssd, st_s5[0].reshape(s5_shape), st_s5[1].reshape(s5_shape))
```

```python
import functools
import math

import jax
import jax.numpy as jnp
from jax import lax
from jax.experimental import pallas as pl
from jax.experimental.pallas import tpu as pltpu

F32 = jnp.float32
BF16 = jnp.bfloat16

D_MODEL = 1024
BATCH = 16
SEQ = 256
DEPTH = 2
DEC_BATCH = 2
DEC_SEQ = 1024
GRID_W = 64
CHUNK = 128
N_DIR = 2
EPS = 1e-6
SSD_INNER = 512
SSD_HEAD_DIM = 64
SSD_HEADS = 8
SSD_GROUPS = 2
SSD_STATE = 128
SSD_CONV = 5
SSD_CONV_DIM = SSD_INNER + 2 * SSD_GROUPS * SSD_STATE
SGU_DIM = 256
SGU_HEADS = 4
S5_DIM = 256
S5_GROUP_CH = 16
S5_GROUPS = 16
S5_STATE = 64
D_FF = 4 * D_MODEL
OFF_XBC = SSD_INNER
OFF_DT = OFF_XBC + SSD_CONV_DIM
OFF_SGU = OFF_DT + N_DIR * SSD_HEADS
OFF_S5 = OFF_SGU + 2 * SGU_DIM
IN_DIM = OFF_S5 + S5_DIM

N_CTX = BATCH * SEQ
N_LAT = DEC_BATCH * DEC_SEQ
N_TOK = N_CTX + N_LAT
LANES = 128
SUBLANES = 8
MOD_ROWS = SUBLANES
SSD_HP = SSD_HEADS * SSD_HEAD_DIM
S5_LANES = S5_GROUPS * S5_STATE
S5_SEG = SUBLANES
S5_PW_ROWS = DEC_SEQ // S5_SEG
S5_SLAB = 256
S5_NSLAB = S5_LANES // S5_SLAB
PAIR = 2 * SSD_HEAD_DIM
N_PAIR = SSD_HEADS // 2
TOK_TILE = 512
N_CTX_TILES = N_CTX // TOK_TILE
FFN_TILE = 512
FFN_PREP = 8
ROW_SLAB = 256
CONV_LANES = 256
SSD_CTX_SEQS = 2
VMEM_LIMIT = 56 * 1024 * 1024

_NT = (((1,), (1,)), ((), ()))


def _params(n_axes=1):
    return pltpu.CompilerParams(dimension_semantics=("arbitrary",) * n_axes,
                                vmem_limit_bytes=VMEM_LIMIT)


def _layer_spec(l, shape, **kw):
    return pl.BlockSpec((None,) + tuple(shape), lambda *_: (l,) + (0,) * len(shape), **kw)


def _mod_spec(l, k):
    return pl.BlockSpec((None, MOD_ROWS, D_MODEL), lambda *_: (l, 0, k))


def _any_spec():
    return pl.BlockSpec(memory_space=pl.ANY)


def _mod_row(i, tm):
    n_ctx = N_CTX // tm
    return jnp.where(i < n_ctx, 0, 1 + (i - n_ctx) // (DEC_SEQ // tm))


def _split_tok_specs(tm, width, b_has_ctx_rows, tile_of=lambda i: i):
    n_ctx = N_CTX // tm
    b_off = n_ctx if b_has_ctx_rows else 0
    a = pl.BlockSpec((tm, width), lambda i: (jnp.minimum(tile_of(i), n_ctx - 1), 0))
    b = pl.BlockSpec((tm, width), lambda i: (jnp.maximum(tile_of(i), n_ctx) - n_ctx + b_off, 0))
    return a, b


def _silu(x):
    return x * jax.nn.sigmoid(x)


def _gelu_tanh(x):
    c = math.sqrt(2.0 / math.pi)
    return 0.5 * x * (1.0 + jnp.tanh(c * (x + 0.044715 * (x * x * x))))


def _rms(x):
    return x * lax.rsqrt(jnp.mean(x * x, axis=-1, keepdims=True) + EPS)


def _bdot(a, b):
    return jnp.dot(a.astype(BF16), b.astype(BF16), preferred_element_type=F32)


def _split3(x):
    hi = x.astype(BF16)
    r = x - hi.astype(F32)
    mid = r.astype(BF16)
    lo = (r - mid.astype(F32)).astype(BF16)
    return jnp.concatenate([hi, mid, lo], axis=1)


def _mod_kernel(c_ref, w_ref, b_ref, o_ref):
    o_ref[...] = _bdot(_silu(c_ref[...]), w_ref[...]) + b_ref[...]


def _adaln_mods(cvec, ada_w, ada_b):
    n_blk = 6
    return pl.pallas_call(
        _mod_kernel,
        grid=(DEPTH, n_blk),
        in_specs=[pl.BlockSpec((MOD_ROWS, D_MODEL), lambda l, j: (0, 0)),
                  pl.BlockSpec((None, D_MODEL, D_MODEL), lambda l, j: (l, 0, j)),
                  pl.BlockSpec((None, 1, D_MODEL), lambda l, j: (l, 0, j))],
        out_specs=pl.BlockSpec((None, MOD_ROWS, D_MODEL), lambda l, j: (l, 0, j)),
        out_shape=jax.ShapeDtypeStruct((DEPTH, MOD_ROWS, 6 * D_MODEL), F32),
        compiler_params=_params(2),
        name="adaln_mod",
    )(cvec, ada_w, ada_b.reshape(DEPTH, 1, 6 * D_MODEL))


_C_Z = 0
_C_XBC = _C_Z + SSD_INNER
_C_SGU = _C_XBC + SSD_CONV_DIM
_C_S5 = _C_SGU + 2 * SGU_DIM
_C_DT = _C_S5 + S5_DIM
_C_END = _C_DT + LANES


def _inproj_kernel(xa_ref, xb_ref, sh_ref, sc_ref, g_ref, win_ref, sg_ref, sw_ref, sb_ref,
                   z_ref, xbc_ref, ysgu_ref, s5_ref, dt_ref, w_ref):
    i = pl.program_id(0)

    @pl.when(i == 0)
    def _():
        moves = ((0, OFF_DT, _C_Z), (OFF_SGU, IN_DIM, _C_SGU), (OFF_DT, OFF_SGU, _C_DT))
        for src0, src1, dst0 in moves:
            for r0 in range(src0, src1, ROW_SLAB):
                n = min(ROW_SLAB, src1 - r0)
                w_ref[dst0 + r0 - src0:dst0 + r0 - src0 + n, :] = win_ref[r0:r0 + n, :].astype(BF16)
        n_pad = LANES - N_DIR * SSD_HEADS
        w_ref[_C_END - n_pad:_C_END, :] = jnp.zeros((n_pad, D_MODEL), BF16)

    r = _mod_row(i, TOK_TILE)
    x = jnp.where(i < N_CTX_TILES, xa_ref[...], xb_ref[...])
    shift = sh_ref[pl.ds(r, 1), :]
    scale = sc_ref[pl.ds(r, 1), :]
    h = (_rms(x) * g_ref[...]) * (1.0 + scale) + shift
    hb = h.astype(BF16)
    proj = lambda c0, c1: lax.dot_general(hb, w_ref[c0:c1, :], _NT, preferred_element_type=F32)
    z_ref[...] = proj(_C_Z, _C_XBC)
    xbc_ref[...] = proj(_C_XBC, _C_SGU)
    s5_ref[...] = proj(_C_S5, _C_DT)
    dt_ref[...] = proj(_C_DT, _C_END)

    uv = _gelu_tanh(proj(_C_SGU, _C_S5))
    u = uv[:, :SGU_DIM]
    v = _rms(uv[:, SGU_DIM:]) * sg_ref[...]
    lo_lane = lax.broadcasted_iota(jnp.int32, (CHUNK, LANES), 1) < (LANES // 2)
    for c in range(TOK_TILE // CHUNK):
        rows = slice(c * CHUNK, (c + 1) * CHUNK)
        mix = []
        for pr in range(SGU_HEADS // 2):
            vp = v[rows, pr * LANES:(pr + 1) * LANES]
            rhs = jnp.concatenate([jnp.where(lo_lane, vp, 0.0).astype(BF16),
                                   jnp.where(lo_lane, 0.0, vp).astype(BF16)], axis=0)
            mix.append(jnp.dot(sw_ref[pr], rhs, preferred_element_type=F32))
        ysgu_ref[rows, :] = u[rows, :] * (jnp.concatenate(mix, axis=1) + sb_ref[...])


def _in_proj(xa, xb, b_is_stream, l, mods, norm_g, w_in, sgu_g, sgu_w_pair, sgu_b_full):
    tok = lambda w: pl.BlockSpec((TOK_TILE, w), lambda i: (i, 0))
    widths = (SSD_INNER, SSD_CONV_DIM, SGU_DIM, S5_DIM, LANES)
    sa, sb = _split_tok_specs(TOK_TILE, D_MODEL, b_is_stream)
    return pl.pallas_call(
        _inproj_kernel,
        grid=(N_TOK // TOK_TILE,),
        in_specs=[sa, sb, _mod_spec(l, 0), _mod_spec(l, 1), _layer_spec(l, (1, D_MODEL)),
                  _layer_spec(l, (IN_DIM, D_MODEL), pipeline_mode=pl.Buffered(1)),
                  _layer_spec(l, (1, SGU_DIM)), _layer_spec(l, (SGU_HEADS // 2, CHUNK, 2 * CHUNK)),
                  _layer_spec(l, (CHUNK, SGU_DIM))],
        out_specs=[tok(w) for w in widths],
        out_shape=[jax.ShapeDtypeStruct((N_TOK, w), F32) for w in widths],
        scratch_shapes=[pltpu.VMEM((_C_END, D_MODEL), BF16)],
        compiler_params=_params(),
        name="in_proj",
    )(xa, xb, mods, mods, norm_g, w_in, sgu_g, sgu_w_pair, sgu_b_full)


def _ssd_kernel(*refs, L, n_seq, has_h0, n_alias, emit_state):
    z_ref, xbc_ref, dt_ref, cw_ref, cb_ref, dtb_ref, alog_ref, dvec_ref, ng_ref = refs[:9]
    k = 9
    h0_ref = refs[k] if has_h0 else None
    k += int(has_h0) + n_alias
    y_ref = refs[k]
    hout_ref = refs[k + 1] if emit_state else None
    xpad, xc, acol, atr, cbs, bmt, dtsp, yacc, hst = refs[k + 1 + int(emit_state):]
    Q = CHUNK
    nc = L // Q
    halo = SUBLANES
    pad = (SSD_CONV - 1) // 2

    pitch = L + halo
    for q in range(n_seq + 1):
        xpad[q * pitch:q * pitch + halo, :] = jnp.zeros((halo, SSD_CONV_DIM), F32)
    for g in range(n_seq * nc):
        o = halo + (g // nc) * pitch + (g % nc) * Q
        xpad[o:o + Q, :] = xbc_ref[g * Q:(g + 1) * Q, :]
    win = Q + 2 * halo
    for c in range(n_seq * nc):
        for lb in range(0, SSD_CONV_DIM, CONV_LANES):
            ln = slice(lb, lb + CONV_LANES)
            o = (c // nc) * pitch + (c % nc) * Q
            xa = xpad[o:o + win, ln]
            acc = cb_ref[:, ln] + cw_ref[pad:pad + 1, ln] * xa[halo:halo + Q, :]
            for t in range(SSD_CONV):
                if t != pad:
                    rolled = pltpu.roll(xa, (pad - t) % win, 0)
                    acc = acc + cw_ref[t:t + 1, ln] * rolled[halo:halo + Q, :]
            xc[c * Q:(c + 1) * Q, ln] = _silu(acc)

    raw = dt_ref[...] + dtb_ref[...]
    dt = jnp.maximum(raw, 0.0) + jnp.log(1.0 + jnp.exp(-jnp.abs(raw)))
    dtsp[...] = dt
    a_neg = -jnp.exp(alog_ref[...])
    row = lax.broadcasted_iota(jnp.int32, (Q, Q), 0)
    col = lax.broadcasted_iota(jnp.int32, (Q, Q), 1)
    lower = row >= col
    upper = col >= row
    tri_l = lower.astype(F32)
    tri_u = upper.astype(F32)
    fwd_lane = lax.broadcasted_iota(jnp.int32, (Q, LANES), 1) < SSD_HEADS
    for c in range(n_seq * nc):
        dta = dtsp[c * Q:(c + 1) * Q, :] * a_neg
        pre = jnp.dot(tri_l, dta, precision=lax.Precision.HIGHEST, preferred_element_type=F32)
        suf = jnp.dot(tri_u, dta, precision=lax.Precision.HIGHEST, preferred_element_type=F32)
        a = jnp.where(fwd_lane, pre, suf)
        acol[c * Q:(c + 1) * Q, :] = a
        atr[c] = a.T[0:N_DIR * SSD_HEADS, :]
        for g in range(SSD_GROUPS):
            b0 = SSD_INNER + g * SSD_STATE
            c0 = SSD_INNER + (SSD_GROUPS + g) * SSD_STATE
            bm = xc[c * Q:(c + 1) * Q, b0:b0 + SSD_STATE]
            cbs[c, g] = lax.dot_general(xc[c * Q:(c + 1) * Q, c0:c0 + SSD_STATE].astype(BF16),
                                        bm.astype(BF16), _NT, preferred_element_type=F32)
            bmt[c, g] = bm.T.astype(BF16)

    sel_row = lax.broadcasted_iota(jnp.int32, (3 * LANES, SSD_INNER), 0) & (LANES - 1)
    sel_head = lax.broadcasted_iota(jnp.int32, (3 * LANES, SSD_INNER), 1) >> (SSD_HEAD_DIM.bit_length() - 1)
    sel = [(sel_row == sel_head + d * SSD_HEADS).astype(BF16) for d in range(N_DIR)]
    mask2 = [jnp.concatenate([m, m], axis=1) for m in (lower, upper)]

    for q in range(n_seq):
        for d in range(N_DIR):
            for p in range(N_PAIR):
                if has_h0:
                    hst[q, d, p] = h0_ref[q, d, p * PAIR:(p + 1) * PAIR, :].T
                else:
                    hst[q, d, p] = jnp.zeros((SSD_STATE, PAIR), F32)

    lo_lane = lax.broadcasted_iota(jnp.int32, (Q, PAIR), 1) < SSD_HEAD_DIM

    def chunk(c, d, q):
        r0 = pl.multiple_of(c * Q, Q)
        rows = pl.ds(r0, Q)
        a = acol[rows, :]
        a_t = atr[c]
        dtp = jnp.dot(_split3(dtsp[rows, :]), sel[d], preferred_element_type=F32)
        a_end = a[Q - 1:Q, :] if d == 0 else a[0:1, :]
        dec = jnp.exp(a_end)
        for g in range(SSD_GROUPS):
            c0 = SSD_INNER + (SSD_GROUPS + g) * SSD_STATE
            cmb = xc[rows, c0:c0 + SSD_STATE].astype(BF16)
            cb = cbs[c, g]
            cb2 = jnp.concatenate([cb, cb], axis=1)
            for pr in range(2):
                p = g * 2 + pr
                j0 = d * SSD_HEADS + 2 * p
                j1 = j0 + 1
                xs = xc[rows, p * PAIR:(p + 1) * PAIR]
                ab0 = jnp.broadcast_to(a[:, j0:j0 + 1], (Q, Q))
                ab1 = jnp.broadcast_to(a[:, j1:j1 + 1], (Q, Q))
                seg = (jnp.concatenate([ab0, ab1], axis=1)
                       - jnp.concatenate([a_t[j0:j0 + 1, :], a_t[j1:j1 + 1, :]], axis=1))
                m = (cb2 * jnp.exp(jnp.where(mask2[d], seg, -jnp.inf))).astype(BF16)
                xdt = xs * dtp[:, p * PAIR:(p + 1) * PAIR]
                rhs = jnp.concatenate([jnp.where(lo_lane, xdt, 0.0).astype(BF16),
                                       jnp.where(lo_lane, 0.0, xdt).astype(BF16)], axis=0)
                y_diag = jnp.dot(m, rhs, preferred_element_type=F32)
                hp = hst[q, d, p]
                y_off = jnp.dot(cmb, hp.astype(BF16), preferred_element_type=F32)
                a_pair = jnp.where(lo_lane, ab0, ab1)
                y = y_diag + jnp.exp(a_pair) * y_off
                a_end_pair = jnp.where(lo_lane[0:1, :], a_end[:, j0:j0 + 1], a_end[:, j1:j1 + 1])
                xw = (xdt * jnp.exp(a_end_pair - a_pair)).astype(BF16)
                s_new = jnp.dot(bmt[c, g], xw, preferred_element_type=F32)
                decp = jnp.where(lo_lane[0:1, :], dec[:, j0:j0 + 1], dec[:, j1:j1 + 1])
                hst[q, d, p] = decp * hp + s_new
                if d == 0:
                    y = y + dvec_ref[:, p * PAIR:(p + 1) * PAIR] * xs
                yacc[d, rows, p * PAIR:(p + 1) * PAIR] = y

    def step(t, carry):
        for q in range(n_seq):
            chunk(q * nc + t, 0, q)
            chunk(q * nc + nc - 1 - t, 1, q)
        return carry

    lax.fori_loop(0, nc, step, 0, unroll=2 if n_seq == 1 else 1)
    for c in range(n_seq * nc):
        rows = slice(c * Q, (c + 1) * Q)
        y = (yacc[0, rows, :] + yacc[1, rows, :]) * _silu(z_ref[rows, :])
        y_ref[rows, :] = _rms(y) * ng_ref[...]
    if emit_state:
        for q in range(n_seq):
            for d in range(N_DIR):
                for p in range(N_PAIR):
                    hout_ref[q, d, p * PAIR:(p + 1) * PAIR, :] = hst[q, d, p].T


def _ssd_mixer(z, xbc, dtr, prm, l, *, L, n_seq, nb, blk0, h0=None, y_buf=None, st_buf=None, emit_state):
    R = n_seq * L
    seq = lambda w: pl.BlockSpec((R, w), lambda b: (blk0 + b, 0))
    st = pl.BlockSpec((n_seq, None, N_DIR, SSD_HP, SSD_STATE), lambda b: (b, l, 0, 0, 0))
    args = [z, xbc, dtr, prm["conv_w"], prm["conv_b"], prm["dt_bias"], prm["a_log"], prm["d_vec"],
            prm["norm_g"]]
    in_specs = [seq(SSD_INNER), seq(SSD_CONV_DIM), seq(LANES),
                _layer_spec(l, (SSD_CONV, SSD_CONV_DIM)), _layer_spec(l, (1, SSD_CONV_DIM)),
                _layer_spec(l, (1, LANES)), _layer_spec(l, (1, LANES)),
                _layer_spec(l, (1, SSD_INNER)), _layer_spec(l, (1, SSD_INNER))]
    if h0 is not None:
        args.append(h0)
        in_specs.append(st)
    out_shape = [jax.ShapeDtypeStruct((N_TOK, SSD_INNER), F32)]
    out_specs = [seq(SSD_INNER)]
    aliases = {}
    if y_buf is not None:
        aliases[len(args)] = 0
        args.append(y_buf)
        in_specs.append(_any_spec())
    if emit_state:
        out_shape.append(jax.ShapeDtypeStruct((BATCH, DEPTH, N_DIR, SSD_HP, SSD_STATE), F32))
        out_specs.append(st)
        if st_buf is not None:
            aliases[len(args)] = 1
            args.append(st_buf)
            in_specs.append(_any_spec())
    res = pl.pallas_call(
        functools.partial(_ssd_kernel, L=L, n_seq=n_seq, has_h0=h0 is not None, n_alias=len(aliases),
                          emit_state=emit_state),
        grid=(nb,),
        in_specs=in_specs,
        out_specs=out_specs,
        out_shape=out_shape,
        input_output_aliases=aliases,
        scratch_shapes=[pltpu.VMEM((n_seq * (L + SUBLANES) + SUBLANES, SSD_CONV_DIM), F32),
                        pltpu.VMEM((R, SSD_CONV_DIM), F32),
                        pltpu.VMEM((R, LANES), F32),
                        pltpu.VMEM((R // CHUNK, N_DIR * SSD_HEADS, CHUNK), F32),
                        pltpu.VMEM((R // CHUNK, SSD_GROUPS, CHUNK, CHUNK), F32),
                        pltpu.VMEM((R // CHUNK, SSD_GROUPS, SSD_STATE, CHUNK), BF16),
                        pltpu.VMEM((R, LANES), F32),
                        pltpu.VMEM((N_DIR, R, SSD_INNER), F32),
                        pltpu.VMEM((n_seq, N_DIR, N_PAIR, PAIR, SSD_STATE), F32)],
        compiler_params=_params(),
        name=f"ssd_mixer_L{L}",
    )(*args)
    return res if emit_state else (res[0], None)


_LOG2_CH = S5_GROUP_CH.bit_length() - 1
_LOG2_ST = S5_STATE.bit_length() - 1


def _cmul(ar, ai, br, bi):
    return ar * br - ai * bi, ar * bi + ai * br


def _s5prep_kernel(lre_ref, lim_ref, ldt_ref, btr_ref, bti_ref, ctr_ref, cti_ref,
                   bblk_ref, cblk_ref, pw_ref):
    brow = lax.broadcasted_iota(jnp.int32, (S5_DIM, S5_LANES), 0) >> _LOG2_CH
    bcol = lax.broadcasted_iota(jnp.int32, (S5_DIM, S5_LANES), 1) >> _LOG2_ST
    bmask = brow == bcol
    p1 = lax.broadcasted_iota(jnp.int32, (S5_SEG, S5_LANES), 0) + 1
    for d in range(N_DIR):
        lre = lre_ref[d]
        lim = lim_ref[d]
        step = jnp.exp(ldt_ref[d])
        mag = jnp.exp(lre * step)
        lbr = mag * jnp.cos(lim * step)
        lbi = mag * jnp.sin(lim * step)
        den = lre * lre + lim * lim
        nr = lbr - 1.0
        cr = (nr * lre + lbi * lim) / den
        ci = (lbi * lre - nr * lim) / den
        br, bi = _cmul(cr, ci, btr_ref[...], bti_ref[...])
        br = jnp.where(bmask, br, 0.0).astype(BF16)
        bi = jnp.where(bmask, bi, 0.0).astype(BF16)
        for sl in range(S5_NSLAB):
            bblk_ref[d, sl, :, 0:S5_SLAB] = br[:, sl * S5_SLAB:(sl + 1) * S5_SLAB]
            bblk_ref[d, sl, :, S5_SLAB:2 * S5_SLAB] = bi[:, sl * S5_SLAB:(sl + 1) * S5_SLAB]
        rr = jnp.ones((S5_SEG, S5_LANES), F32)
        ri = jnp.zeros((S5_SEG, S5_LANES), F32)
        sr, si = lbr, lbi
        for k in range(S5_SEG.bit_length()):
            bit = ((p1 >> k) & 1) == 1
            tr, ti = _cmul(rr, ri, sr, si)
            rr = jnp.where(bit, tr, rr)
            ri = jnp.where(bit, ti, ri)
            sr, si = _cmul(sr, si, sr, si)
        pw_ref[d, 0, 0:S5_SEG, :] = rr
        pw_ref[d, 1, 0:S5_SEG, :] = ri
        n = S5_SEG
        while n < S5_PW_ROWS:
            tr, ti = _cmul(pw_ref[d, 0, 0:n, :], pw_ref[d, 1, 0:n, :],
                           pw_ref[d, 0, n - 1:n, :], pw_ref[d, 1, n - 1:n, :])
            pw_ref[d, 0, n:2 * n, :] = tr
            pw_ref[d, 1, n:2 * n, :] = ti
            n *= 2
    crow = lax.broadcasted_iota(jnp.int32, (S5_LANES, S5_DIM), 0) >> _LOG2_ST
    ccol = lax.broadcasted_iota(jnp.int32, (S5_LANES, S5_DIM), 1) >> _LOG2_CH
    cmask = crow == ccol
    cr = jnp.where(cmask, ctr_ref[...], 0.0).astype(BF16)
    ci = jnp.where(cmask, -cti_ref[...], 0.0).astype(BF16)
    for sl in range(S5_NSLAB):
        cblk_ref[sl, 0:S5_SLAB, :] = cr[sl * S5_SLAB:(sl + 1) * S5_SLAB, :]
        cblk_ref[sl, S5_SLAB:2 * S5_SLAB, :] = ci[sl * S5_SLAB:(sl + 1) * S5_SLAB, :]


def _s5_prep(lam_re, lam_im, log_dt, b_re, b_im, c_re, c_im):
    row = lambda t: t.reshape(DEPTH, N_DIR, 1, S5_LANES)
    ldt = jnp.repeat(log_dt, S5_STATE, axis=-1)
    bt = lambda t: jnp.tile(t.transpose(0, 3, 1, 2).reshape(DEPTH, S5_GROUP_CH, S5_LANES),
                            (1, S5_GROUPS, 1))
    ct = lambda t: jnp.tile(t.transpose(0, 1, 3, 2).reshape(DEPTH, S5_LANES, S5_GROUP_CH),
                            (1, 1, S5_GROUPS))
    vec = pl.BlockSpec((None, N_DIR, 1, S5_LANES), lambda l: (l, 0, 0, 0))
    bsp = pl.BlockSpec((None, S5_DIM, S5_LANES), lambda l: (l, 0, 0))
    csp = pl.BlockSpec((None, S5_LANES, S5_DIM), lambda l: (l, 0, 0))
    return pl.pallas_call(
        _s5prep_kernel,
        grid=(DEPTH,),
        in_specs=[vec, vec, vec, bsp, bsp, csp, csp],
        out_specs=[pl.BlockSpec((None, N_DIR, S5_NSLAB, S5_DIM, 2 * S5_SLAB), lambda l: (l, 0, 0, 0, 0)),
                   pl.BlockSpec((None, S5_NSLAB, 2 * S5_SLAB, S5_DIM), lambda l: (l, 0, 0, 0)),
                   pl.BlockSpec((None, N_DIR, 2, S5_PW_ROWS, S5_LANES), lambda l: (l, 0, 0, 0, 0))],
        out_shape=[jax.ShapeDtypeStruct((DEPTH, N_DIR, S5_NSLAB, S5_DIM, 2 * S5_SLAB), BF16),
                   jax.ShapeDtypeStruct((DEPTH, S5_NSLAB, 2 * S5_SLAB, S5_DIM), BF16),
                   jax.ShapeDtypeStruct((DEPTH, N_DIR, 2, S5_PW_ROWS, S5_LANES), F32)],
        compiler_params=_params(),
        name="s5_prep",
    )(row(lam_re), row(lam_im), row(ldt), bt(b_re), bt(b_im), ct(c_re), ct(c_im))


def _s5_moves(n_seq, n_seg, seq_len, col_major):
    steps = seq_len // n_seg
    if not col_major:
        return [(q * seq_len + s * steps, steps, q * n_seg + s, S5_SEG)
                for q in range(n_seq) for s in range(n_seg)]
    assert n_seq == 1 and n_seg == S5_SEG
    grid_rows = seq_len // GRID_W
    wseg = GRID_W // n_seg
    return [(r * GRID_W + s * wseg, wseg, r * S5_SEG + s, grid_rows * S5_SEG)
            for r in range(grid_rows) for s in range(n_seg)]


def _s5_kernel(*refs, n_seq, n_seg, seq_len, col_major, has_h0, n_alias, emit_state):
    u_ref, bblk_ref, cblk_ref, pw_ref, dvec_ref, gw_ref, gb_ref = refs[:7]
    k = 7
    h0r_ref, h0i_ref = (refs[k], refs[k + 1]) if has_h0 else (None, None)
    k += 2 * int(has_h0) + n_alias
    y_ref = refs[k]
    fr_ref, fi_ref = (refs[k + 1], refs[k + 2]) if emit_state else (None, None)
    up, buf_f, buf_b, yacc, yp, cin, fin = refs[k + 1 + 2 * int(emit_state):]
    assert n_seq * n_seg == S5_SEG and not (has_h0 and n_seg == 1) and not (emit_state and n_seg > 1)
    n_rows = n_seq * seq_len
    steps = seq_len // n_seg
    n_planes = S5_DIM // LANES
    W = S5_SLAB
    moves = _s5_moves(n_seq, n_seg, seq_len, col_major)

    for src, n, dst, stride in moves:
        for t in range(n_planes):
            up[t, pl.ds(dst, n, stride=stride), :] = u_ref[src:src + n, t * LANES:(t + 1) * LANES]

    def u_rows(rs):
        return jnp.concatenate([up[t, rs, :] for t in range(n_planes)], axis=1)

    zero = jnp.zeros((S5_SEG, W), F32)
    zrow = jnp.zeros((1, W), F32)
    for sl in range(S5_NSLAB):
        ln = slice(sl * W, (sl + 1) * W)
        for r0 in range(0, n_rows, ROW_SLAB):
            rs = slice(r0, r0 + ROW_SLAB)
            ub = u_rows(rs).astype(BF16)
            buf_f[rs, :] = jnp.dot(ub, bblk_ref[0, sl], preferred_element_type=F32)
            buf_b[rs, :] = jnp.dot(ub, bblk_ref[1, sl], preferred_element_type=F32)
        lam = [[jnp.broadcast_to(pw_ref[d, c, 0:1, ln], (S5_SEG, W)) for c in range(2)]
               for d in range(N_DIR)]

        def body(t, carry):
            fr, fi, br, bi = carry
            rf = pl.ds(pl.multiple_of(t * S5_SEG, S5_SEG), S5_SEG)
            rb = pl.ds(pl.multiple_of((steps - 1 - t) * S5_SEG, S5_SEG), S5_SEG)
            nfr = lam[0][0] * fr - lam[0][1] * fi + buf_f[rf, 0:W]
            nfi = lam[0][0] * fi + lam[0][1] * fr + buf_f[rf, W:2 * W]
            nbr = lam[1][0] * br - lam[1][1] * bi + buf_b[rb, 0:W]
            nbi = lam[1][0] * bi + lam[1][1] * br + buf_b[rb, W:2 * W]
            buf_f[rf, 0:W] = nfr
            buf_f[rf, W:2 * W] = nfi
            buf_b[rb, 0:W] = nbr
            buf_b[rb, W:2 * W] = nbi
            return nfr, nfi, nbr, nbi

        ends = lax.fori_loop(0, steps, body, (zero, zero, zero, zero), unroll=2)

        if n_seg == 1:
            if emit_state:
                for d in range(N_DIR):
                    fin[d, 0, :, ln] = ends[2 * d]
                    fin[d, 1, :, ln] = ends[2 * d + 1]
        else:
            for d in range(N_DIR):
                er, ei = ends[2 * d], ends[2 * d + 1]
                lpr = pw_ref[d, 0, steps - 1:steps, ln]
                lpi = pw_ref[d, 1, steps - 1:steps, ln]
                pr = h0r_ref[d:d + 1, ln] if has_h0 else zrow
                pi = h0i_ref[d:d + 1, ln] if has_h0 else zrow
                order = range(n_seg) if d == 0 else range(n_seg - 1, -1, -1)
                for s in order:
                    cin[d, 0, s:s + 1, :] = pr
                    cin[d, 1, s:s + 1, :] = pi
                    tr, ti = _cmul(lpr, lpi, pr, pi)
                    pr = er[s:s + 1, :] + tr
                    pi = ei[s:s + 1, :] + ti
            cfr = cin[0, 0]
            cfi = cin[0, 1]
            cbr = cin[1, 0]
            cbi = cin[1, 1]

            def fix(p, carry):
                rows = pl.ds(pl.multiple_of(p * S5_SEG, S5_SEG), S5_SEG)
                q = steps - 1 - p
                tfr, tfi = _cmul(pw_ref[0, 0, pl.ds(p, 1), ln], pw_ref[0, 1, pl.ds(p, 1), ln], cfr, cfi)
                tbr, tbi = _cmul(pw_ref[1, 0, pl.ds(q, 1), ln], pw_ref[1, 1, pl.ds(q, 1), ln], cbr, cbi)
                buf_f[rows, 0:W] = buf_f[rows, 0:W] + buf_b[rows, 0:W] + tfr + tbr
                buf_f[rows, W:2 * W] = buf_f[rows, W:2 * W] + buf_b[rows, W:2 * W] + tfi + tbi
                return carry

            lax.fori_loop(0, steps, fix, 0, unroll=2)

        for r0 in range(0, n_rows, ROW_SLAB):
            rs = slice(r0, r0 + ROW_SLAB)
            hs = buf_f[rs, :] if n_seg > 1 else buf_f[rs, :] + buf_b[rs, :]
            part = jnp.dot(hs.astype(BF16), cblk_ref[sl], preferred_element_type=F32)
            if sl == 0:
                yacc[rs, :] = part
            else:
                yacc[rs, :] = yacc[rs, :] + part

    gwb = gw_ref[...].astype(BF16)
    for r0 in range(0, n_rows, ROW_SLAB):
        rs = slice(r0, r0 + ROW_SLAB)
        y = _gelu_tanh(yacc[rs, :] + dvec_ref[...] * u_rows(rs))
        gate = jnp.dot(y.astype(BF16), gwb, preferred_element_type=F32) + gb_ref[...]
        y = y * jax.nn.sigmoid(gate)
        for t in range(n_planes):
            yp[t, rs, :] = y[:, t * LANES:(t + 1) * LANES]
    for src, n, dst, stride in moves:
        for t in range(n_planes):
            y_ref[src:src + n, t * LANES:(t + 1) * LANES] = yp[t, pl.ds(dst, n, stride=stride), :]
    if emit_state:
        for q in range(n_seq):
            for d in range(N_DIR):
                fr_ref[q, d:d + 1, :] = fin[d, 0, q:q + 1, :]
                fi_ref[q, d:d + 1, :] = fin[d, 1, q:q + 1, :]


def _s5_mixer(u, bblk, cblk, pw, prm, l, *, seq_len, n_seq, nb, blk0, col_major, h0=None, y_buf=None,
              st_buf=None, emit_state):
    n_rows = n_seq * seq_len
    n_seg = S5_SEG // n_seq
    seq = pl.BlockSpec((n_rows, S5_DIM), lambda b: (blk0 + b, 0))
    st = pl.BlockSpec((n_seq, None, N_DIR, S5_LANES), lambda b: (b, l, 0, 0))
    once = dict(pipeline_mode=pl.Buffered(1))
    args = [u, bblk, cblk, pw, prm["d_vec"], prm["glu_w"], prm["glu_b"]]
    in_specs = [seq, _layer_spec(l, (N_DIR, S5_NSLAB, S5_DIM, 2 * S5_SLAB), **once),
                _layer_spec(l, (S5_NSLAB, 2 * S5_SLAB, S5_DIM), **once),
                _layer_spec(l, (N_DIR, 2, S5_PW_ROWS, S5_LANES), **once), _layer_spec(l, (1, S5_DIM)),
                _layer_spec(l, (S5_DIM, S5_DIM)), _layer_spec(l, (1, S5_DIM))]
    if h0 is not None:
        h0_spec = pl.BlockSpec((None, None, N_DIR, S5_LANES), lambda b: (b, l, 0, 0))
        args += list(h0)
        in_specs += [h0_spec, h0_spec]
    out_shape = [jax.ShapeDtypeStruct((N_TOK, S5_DIM), F32)]
    out_specs = [seq]
    aliases = {}
    if y_buf is not None:
        aliases[len(args)] = 0
        args.append(y_buf)
        in_specs.append(_any_spec())
    if emit_state:
        out_shape += [jax.ShapeDtypeStruct((BATCH, DEPTH, N_DIR, S5_LANES), F32)] * 2
        out_specs += [st, st]
        if st_buf is not None:
            for j, buf in enumerate(st_buf):
                aliases[len(args)] = 1 + j
                args.append(buf)
                in_specs.append(_any_spec())
    res = pl.pallas_call(
        functools.partial(_s5_kernel, n_seq=n_seq, n_seg=n_seg, seq_len=seq_len, col_major=col_major,
                          has_h0=h0 is not None, n_alias=len(aliases), emit_state=emit_state),
        grid=(nb,),
        in_specs=in_specs,
        out_specs=out_specs,
        out_shape=out_shape,
        input_output_aliases=aliases,
        scratch_shapes=[pltpu.VMEM((S5_DIM // LANES, n_rows, LANES), F32),
                        pltpu.VMEM((n_rows, 2 * S5_SLAB), F32),
                        pltpu.VMEM((n_rows, 2 * S5_SLAB), F32),
                        pltpu.VMEM((n_rows, S5_DIM), F32),
                        pltpu.VMEM((S5_DIM // LANES, n_rows, LANES), F32),
                        pltpu.VMEM((N_DIR, 2, S5_SEG, S5_SLAB), F32),
                        pltpu.VMEM((N_DIR, 2, S5_SEG, S5_LANES), F32)],
        compiler_params=_params(),
        name=f"s5_mixer_L{seq_len}",
    )(*args)
    return (res[0], res[1], res[2]) if emit_state else (res[0], None, None)


def _outffn_kernel(xa_ref, xb_ref, ys_ref, yg_ref, y5_ref, g1_ref, sh_ref, sc_ref, g2_ref, ng_ref,
                   fg_ref, wo_ref, w1_ref, w2_ref, *rest, final):
    n_out = 2 if final else 1
    o_refs = rest[:n_out]
    wob, w1b, w2b = rest[n_out:]
    i = pl.program_id(0)

    @pl.when(i < FFN_PREP)
    def _():
        wob[i] = wo_ref[...].astype(BF16)
        w1b[i] = w1_ref[...].astype(BF16)
        w2b[i] = w2_ref[...].astype(BF16)

    @pl.when(i >= FFN_PREP)
    def _():
        t = i - FFN_PREP
        n_ctx = N_CTX // FFN_TILE
        r = _mod_row(t, FFN_TILE)
        row = lambda ref: ref[pl.ds(r, 1), :]
        x = jnp.where(t < n_ctx, xa_ref[...], xb_ref[...])
        mixed = jnp.concatenate([ys_ref[...].astype(BF16), yg_ref[...].astype(BF16),
                                 y5_ref[...].astype(BF16)], axis=1)
        w_out = wob[...].reshape(D_MODEL, D_MODEL)
        x1 = x + row(g1_ref) * jnp.dot(mixed, w_out, preferred_element_type=F32)
        h = ((_rms(x1) * ng_ref[...]) * (1.0 + row(sc_ref)) + row(sh_ref)).astype(BF16)
        acc = jnp.zeros((FFN_TILE, D_MODEL), F32)
        for j in range(FFN_PREP):
            a = jnp.maximum(jnp.dot(h, w1b[j], preferred_element_type=F32), 0.0)
            acc = acc + jnp.dot((a * a).astype(BF16), w2b[j], preferred_element_type=F32)
        x2 = x1 + row(g2_ref) * acc
        if not final:
            o_refs[0][...] = x2
        else:
            y = _rms(x2) * fg_ref[...]

            @pl.when(t < n_ctx)
            def _():
                o_refs[0][...] = y

            @pl.when(t >= n_ctx)
            def _():
                o_refs[1][...] = y


def _out_ffn(xa, xb, b_is_stream, y_ssd, y_sgu, y_s5, l, mods, norm_g, final_g, w_out, w1, w2, *, final):
    tile_of = lambda i: jnp.maximum(i - FFN_PREP, 0)
    tok = lambda w: pl.BlockSpec((FFN_TILE, w), lambda i: (tile_of(i), 0))
    chunk_of = lambda i: jnp.minimum(i, FFN_PREP - 1)
    sa, sb = _split_tok_specs(FFN_TILE, D_MODEL, b_is_stream, tile_of)
    if final:
        out_specs = list(_split_tok_specs(FFN_TILE, D_MODEL, False, tile_of))
        out_shape = [jax.ShapeDtypeStruct((N_CTX, D_MODEL), F32), jax.ShapeDtypeStruct((N_LAT, D_MODEL), F32)]
    else:
        out_specs = [tok(D_MODEL)]
        out_shape = [jax.ShapeDtypeStruct((N_TOK, D_MODEL), F32)]
    ko, kf = D_MODEL // FFN_PREP, D_FF // FFN_PREP
    return pl.pallas_call(
        functools.partial(_outffn_kernel, final=final),
        grid=(FFN_PREP + N_TOK // FFN_TILE,),
        in_specs=[sa, sb, tok(SSD_INNER), tok(SGU_DIM), tok(S5_DIM),
                  _mod_spec(l, 2), _mod_spec(l, 3), _mod_spec(l, 4), _mod_spec(l, 5),
                  _layer_spec(l, (1, D_MODEL)), pl.BlockSpec((1, D_MODEL), lambda i: (0, 0)),
                  pl.BlockSpec((None, ko, D_MODEL), lambda i: (l, chunk_of(i), 0)),
                  pl.BlockSpec((None, D_MODEL, kf), lambda i: (l, 0, chunk_of(i))),
                  pl.BlockSpec((None, kf, D_MODEL), lambda i: (l, chunk_of(i), 0))],
        out_specs=out_specs,
        out_shape=out_shape,
        scratch_shapes=[pltpu.VMEM((FFN_PREP, ko, D_MODEL), BF16),
                        pltpu.VMEM((FFN_PREP, D_MODEL, kf), BF16),
                        pltpu.VMEM((FFN_PREP, kf, D_MODEL), BF16)],
        compiler_params=_params(),
        name="out_ffn_final" if final else "out_ffn",
    )(xa, xb, y_ssd, y_sgu, y_s5, mods, mods, mods, mods, norm_g, final_g, w_out, w1, w2)


def kernel(x_prompt, x_sample, state_ssd, state_s5_re, state_s5_im, c, c_ctx, ada_w, ada_b, norm1_g,
           norm2_g, w_in, ssd_conv_w, ssd_conv_b, ssd_dt_bias, ssd_a_log, ssd_d, ssd_norm_g,
           sgu_norm_g, sgu_w, sgu_b, s5_lambda_re, s5_lambda_im, s5_log_dt, s5_b_re, s5_b_im,
           s5_c_re, s5_c_im, s5_d, s5_glu_w, s5_glu_b, w_out, ffn_w1, ffn_w2, final_norm_g):
    cvec = jnp.concatenate([c_ctx[None, :], c,
                            jnp.zeros((MOD_ROWS - 1 - DEC_BATCH, D_MODEL), F32)], axis=0)
    mods = _adaln_mods(cvec, ada_w, ada_b)
    bblk, cblk, pw = _s5_prep(s5_lambda_re, s5_lambda_im, s5_log_dt, s5_b_re, s5_b_im,
                              s5_c_re, s5_c_im)

    w_in_t = w_in.transpose(0, 2, 1)
    vec =lambda t: t.reshape(DEPTH, 1, -1)
    lane_pad = lambda t: jnp.pad(vec(t), ((0, 0), (0, 0), (0, LANES - t[0].size)))
    ssd_prm = dict(conv_w=ssd_conv_w, conv_b=vec(ssd_conv_b), dt_bias=lane_pad(ssd_dt_bias),
                   a_log=lane_pad(ssd_a_log), d_vec=vec(jnp.repeat(ssd_d, SSD_HEAD_DIM, axis=-1)),
                   norm_g=vec(ssd_norm_g))
    s5_prm = dict(d_vec=vec(s5_d), glu_w=s5_glu_w, glu_b=vec(s5_glu_b))
    sgu_w_pair = sgu_w.reshape(DEPTH, SGU_HEADS // 2, 2, CHUNK, CHUNK).transpose(0, 1, 3, 2, 4)
    sgu_w_pair = sgu_w_pair.reshape(DEPTH, SGU_HEADS // 2, CHUNK, 2 * CHUNK).astype(BF16)
    sgu_b_full = jnp.repeat(sgu_b.transpose(0, 2, 1), SGU_DIM // SGU_HEADS, axis=2)
    norm1 = vec(norm1_g)
    norm2 = vec(norm2_g)
    sgu_g = vec(sgu_norm_g)
    final_g = final_norm_g.reshape(1, D_MODEL)

    lat_ssd = state_ssd.reshape(DEC_BATCH, DEPTH, N_DIR, SSD_HP, SSD_STATE)
    lat_s5 = (state_s5_re.reshape(DEC_BATCH, DEPTH, N_DIR, S5_LANES),
              state_s5_im.reshape(DEC_BATCH, DEPTH, N_DIR, S5_LANES))
    lat_blk = N_CTX // DEC_SEQ

    xa = x_prompt.reshape(N_CTX, D_MODEL)
    xb = x_sample.reshape(N_LAT, D_MODEL)
    b_is_stream = False
    st_ssd = None
    st_s5 = None
    for l in range(DEPTH):
        z, xbc, y_sgu, s5u, dtr = _in_proj(xa, xb, b_is_stream, l, mods, norm1, w_in_t, sgu_g,
                                           sgu_w_pair, sgu_b_full)
        y_ssd, st_ssd = _ssd_mixer(z, xbc, dtr, ssd_prm, l, L=SEQ, n_seq=SSD_CTX_SEQS,
                                   nb=BATCH // SSD_CTX_SEQS, blk0=0, st_buf=st_ssd, emit_state=True)
        y_ssd, _ = _ssd_mixer(z, xbc, dtr, ssd_prm, l, L=DEC_SEQ, n_seq=1, nb=DEC_BATCH, blk0=lat_blk,
                              h0=lat_ssd, y_buf=y_ssd, emit_state=False)
        y_s5, fr, fi = _s5_mixer(s5u, bblk, cblk, pw, s5_prm, l, seq_len=SEQ, n_seq=S5_SEG,
                                 nb=BATCH // S5_SEG, blk0=0, col_major=False, st_buf=st_s5,
                                 emit_state=True)
        st_s5 = (fr, fi)
        y_s5, _, _ = _s5_mixer(s5u, bblk, cblk, pw, s5_prm, l, seq_len=DEC_SEQ, n_seq=1, nb=DEC_BATCH,
                               blk0=lat_blk, col_major=True, h0=lat_s5, y_buf=y_s5, emit_state=False)
        final = l == DEPTH - 1
        out = _out_ffn(xa, xb, b_is_stream, y_ssd, y_sgu, y_s5, l, mods, norm2, final_g, w_out, ffn_w1,
                       ffn_w2, final=final)
        if not final:
            xa = xb = out[0]
            b_is_stream = True

    y_prompt = out[0].reshape(BATCH, SEQ, D_MODEL)
    y_sample = out[1].reshape(DEC_BATCH, DEC_SEQ, D_MODEL)
    new_state_ssd = st_ssd.reshape(BATCH, DEPTH, N_DIR, SSD_HEADS, SSD_HEAD_DIM, SSD_STATE)
    s5_shape = (BATCH, DEPTH, N_DIR, S5_GROUPS, S5_STATE)
    return (y_prompt, y_sample, new_state_ssd, st_s5[0].reshape(s5_shape), st_s5[1].reshape(s5_shape))
```

```python
import functools
import math

import jax
import jax.numpy as jnp
from jax import lax
from jax.experimental import pallas as pl
from jax.experimental.pallas import tpu as pltpu

F32 = jnp.float32
BF16 = jnp.bfloat16

D_MODEL = 1024
BATCH = 16
SEQ = 256
DEPTH = 2
DEC_BATCH = 2
DEC_SEQ = 1024
GRID_W = 64
CHUNK = 128
N_DIR = 2
EPS = 1e-6
SSD_INNER = 512
SSD_HEAD_DIM = 64
SSD_HEADS = 8
SSD_GROUPS = 2
SSD_STATE = 128
SSD_CONV = 5
SSD_CONV_DIM = SSD_INNER + 2 * SSD_GROUPS * SSD_STATE
SGU_DIM = 256
SGU_HEADS = 4
S5_DIM = 256
S5_GROUP_CH = 16
S5_GROUPS = 16
S5_STATE = 64
D_FF = 4 * D_MODEL
OFF_XBC = SSD_INNER
OFF_DT = OFF_XBC + SSD_CONV_DIM
OFF_SGU = OFF_DT + N_DIR * SSD_HEADS
OFF_S5 = OFF_SGU + 2 * SGU_DIM
IN_DIM = OFF_S5 + S5_DIM

N_CTX = BATCH * SEQ
N_LAT = DEC_BATCH * DEC_SEQ
N_TOK = N_CTX + N_LAT
LANES = 128
SUBLANES = 8
MOD_ROWS = SUBLANES
SSD_HP = SSD_HEADS * SSD_HEAD_DIM
S5_LANES = S5_GROUPS * S5_STATE
S5_SEG = SUBLANES
S5_PW_ROWS = DEC_SEQ // S5_SEG
S5_SLAB = 256
S5_NSLAB = S5_LANES // S5_SLAB
PAIR = 2 * SSD_HEAD_DIM
N_PAIR = SSD_HEADS // 2
TOK_TILE = 512
N_CTX_TILES = N_CTX // TOK_TILE
FFN_TILE = 512
FFN_PREP = 8
ROW_SLAB = 256
CONV_LANES = 256
SSD_CTX_SEQS = 2
VMEM_LIMIT = 56 * 1024 * 1024

_NT = (((1,), (1,)), ((), ()))


def _params(n_axes=1):
    return pltpu.CompilerParams(dimension_semantics=("arbitrary",) * n_axes,
                                vmem_limit_bytes=VMEM_LIMIT)


def _layer_spec(l, shape, **kw):
    return pl.BlockSpec((None,) + tuple(shape), lambda *_: (l,) + (0,) * len(shape), **kw)


def _mod_spec(l, k):
    return pl.BlockSpec((None, MOD_ROWS, D_MODEL), lambda *_: (l, 0, k))


def _any_spec():
    return pl.BlockSpec(memory_space=pl.ANY)


def _mod_row(i, tm):
    n_ctx = N_CTX // tm
    return jnp.where(i < n_ctx, 0, 1 + (i - n_ctx) // (DEC_SEQ // tm))


def _split_tok_specs(tm, width, b_has_ctx_rows, tile_of=lambda i: i):
    n_ctx = N_CTX // tm
    b_off = n_ctx if b_has_ctx_rows else 0
    a = pl.BlockSpec((tm, width), lambda i: (jnp.minimum(tile_of(i), n_ctx - 1), 0))
    b = pl.BlockSpec((tm, width), lambda i: (jnp.maximum(tile_of(i), n_ctx) - n_ctx + b_off, 0))
    return a, b


def _silu(x):
    return x * jax.nn.sigmoid(x)


def _gelu_tanh(x):
    c = math.sqrt(2.0 / math.pi)
    return 0.5 * x * (1.0 + jnp.tanh(c * (x + 0.044715 * (x * x * x))))


def _rms(x):
    return x * lax.rsqrt(jnp.mean(x * x, axis=-1, keepdims=True) + EPS)


def _bdot(a, b):
    return jnp.dot(a.astype(BF16), b.astype(BF16), preferred_element_type=F32)


def _split3(x):
    hi = x.astype(BF16)
    r = x - hi.astype(F32)
    mid = r.astype(BF16)
    lo = (r - mid.astype(F32)).astype(BF16)
    return jnp.concatenate([hi, mid, lo], axis=1)


def _mod_kernel(c_ref, w_ref, b_ref, o_ref):
    o_ref[...] = _bdot(_silu(c_ref[...]), w_ref[...]) + b_ref[...]


def _adaln_mods(cvec, ada_w, ada_b):
    n_blk = 6
    return pl.pallas_call(
        _mod_kernel,
        grid=(DEPTH, n_blk),
        in_specs=[pl.BlockSpec((MOD_ROWS, D_MODEL), lambda l, j: (0, 0)),
                  pl.BlockSpec((None, D_MODEL, D_MODEL), lambda l, j: (l, 0, j)),
                  pl.BlockSpec((None, 1, D_MODEL), lambda l, j: (l, 0, j))],
        out_specs=pl.BlockSpec((None, MOD_ROWS, D_MODEL), lambda l, j: (l, 0, j)),
        out_shape=jax.ShapeDtypeStruct((DEPTH, MOD_ROWS, 6 * D_MODEL), F32),
        compiler_params=_params(2),
        name="adaln_mod",
    )(cvec, ada_w, ada_b.reshape(DEPTH, 1, 6 * D_MODEL))


_C_Z = 0
_C_XBC = _C_Z + SSD_INNER
_C_SGU = _C_XBC + SSD_CONV_DIM
_C_S5 = _C_SGU + 2 * SGU_DIM
_C_DT = _C_S5 + S5_DIM
_C_END = _C_DT + LANES


def _inproj_kernel(xa_ref, xb_ref, sh_ref, sc_ref, g_ref, win_ref, sg_ref, sw_ref, sb_ref,
                   z_ref, xbc_ref, ysgu_ref, s5_ref, dt_ref, w_ref):
    i = pl.program_id(0)

    @pl.when(i == 0)
    def _():
        moves = ((0, OFF_DT, _C_Z), (OFF_SGU, IN_DIM, _C_SGU), (OFF_DT, OFF_SGU, _C_DT))
        for src0, src1, dst0 in moves:
            for r0 in range(src0, src1, ROW_SLAB):
                n = min(ROW_SLAB, src1 - r0)
                w_ref[dst0 + r0 - src0:dst0 + r0 - src0 + n, :] = win_ref[r0:r0 + n, :].astype(BF16)
        n_pad = LANES - N_DIR * SSD_HEADS
        w_ref[_C_END - n_pad:_C_END, :] = jnp.zeros((n_pad, D_MODEL), BF16)

    r = _mod_row(i, TOK_TILE)
    x = jnp.where(i < N_CTX_TILES, xa_ref[...], xb_ref[...])
    shift = sh_ref[pl.ds(r, 1), :]
    scale = sc_ref[pl.ds(r, 1), :]
    h = (_rms(x) * g_ref[...]) * (1.0 + scale) + shift
    hb = h.astype(BF16)
    proj = lambda c0, c1: lax.dot_general(hb, w_ref[c0:c1, :], _NT, preferred_element_type=F32)
    z_ref[...] = proj(_C_Z, _C_XBC)
    xbc_ref[...] = proj(_C_XBC, _C_SGU)
    s5_ref[...] = proj(_C_S5, _C_DT)
    dt_ref[...] = proj(_C_DT, _C_END)

    uv = _gelu_tanh(proj(_C_SGU, _C_S5))
    u = uv[:, :SGU_DIM]
    v = _rms(uv[:, SGU_DIM:]) * sg_ref[...]
    lo_lane = lax.broadcasted_iota(jnp.int32, (CHUNK, LANES), 1) < (LANES // 2)
    for c in range(TOK_TILE // CHUNK):
        rows = slice(c * CHUNK, (c + 1) * CHUNK)
        mix = []
        for pr in range(SGU_HEADS // 2):
            vp = v[rows, pr * LANES:(pr + 1) * LANES]
            rhs = jnp.concatenate([jnp.where(lo_lane, vp, 0.0).astype(BF16),
                                   jnp.where(lo_lane, 0.0, vp).astype(BF16)], axis=0)
            mix.append(jnp.dot(sw_ref[pr], rhs, preferred_element_type=F32))
        ysgu_ref[rows, :] = u[rows, :] * (jnp.concatenate(mix, axis=1) + sb_ref[...])


def _in_proj(xa, xb, b_is_stream, l, mods, norm_g, w_in, sgu_g, sgu_w_pair, sgu_b_full):
    tok = lambda w: pl.BlockSpec((TOK_TILE, w), lambda i: (i, 0))
    widths = (SSD_INNER, SSD_CONV_DIM, SGU_DIM, S5_DIM, LANES)
    sa, sb = _split_tok_specs(TOK_TILE, D_MODEL, b_is_stream)
    return pl.pallas_call(
        _inproj_kernel,
        grid=(N_TOK // TOK_TILE,),
        in_specs=[sa, sb, _mod_spec(l, 0), _mod_spec(l, 1), _layer_spec(l, (1, D_MODEL)),
                  _layer_spec(l, (IN_DIM, D_MODEL), pipeline_mode=pl.Buffered(1)),
                  _layer_spec(l, (1, SGU_DIM)), _layer_spec(l, (SGU_HEADS // 2, CHUNK, 2 * CHUNK)),
                  _layer_spec(l, (CHUNK, SGU_DIM))],
        out_specs=[tok(w) for w in widths],
        out_shape=[jax.ShapeDtypeStruct((N_TOK, w), F32) for w in widths],
        scratch_shapes=[pltpu.VMEM((_C_END, D_MODEL), BF16)],
        compiler_params=_params(),
        name="in_proj",
    )(xa, xb, mods, mods, norm_g, w_in, sgu_g, sgu_w_pair, sgu_b_full)


def _ssd_kernel(*refs, L, n_seq, has_h0, n_alias, emit_state):
    z_ref, xbc_ref, dt_ref, cw_ref, cb_ref, dtb_ref, alog_ref, dvec_ref, ng_ref = refs[:9]
    k = 9
    h0_ref = refs[k] if has_h0 else None
    k += int(has_h0) + n_alias
    y_ref = refs[k]
    hout_ref = refs[k + 1] if emit_state else None
    xpad, xc, acol, atr, cbs, bmt, dtsp, yacc, hst = refs[k + 1 + int(emit_state):]
    Q = CHUNK
    nc = L // Q
    halo = SUBLANES
    pad = (SSD_CONV - 1) // 2

    pitch = L + halo
    for q in range(n_seq + 1):
        xpad[q * pitch:q * pitch + halo, :] = jnp.zeros((halo, SSD_CONV_DIM), F32)
    for g in range(n_seq * nc):
        o = halo + (g // nc) * pitch + (g % nc) * Q
        xpad[o:o + Q, :] = xbc_ref[g * Q:(g + 1) * Q, :]
    win = Q + 2 * halo
    for c in range(n_seq * nc):
        for lb in range(0, SSD_CONV_DIM, CONV_LANES):
            ln = slice(lb, lb + CONV_LANES)
            o = (c // nc) * pitch + (c % nc) * Q
            xa = xpad[o:o + win, ln]
            acc = cb_ref[:, ln] + cw_ref[pad:pad + 1, ln] * xa[halo:halo + Q, :]
            for t in range(SSD_CONV):
                if t != pad:
                    rolled = pltpu.roll(xa, (pad - t) % win, 0)
                    acc = acc + cw_ref[t:t + 1, ln] * rolled[halo:halo + Q, :]
            xc[c * Q:(c + 1) * Q, ln] = _silu(acc)

    raw = dt_ref[...] + dtb_ref[...]
    dt = jnp.maximum(raw, 0.0) + jnp.log(1.0 + jnp.exp(-jnp.abs(raw)))
    dtsp[...] = dt
    a_neg = -jnp.exp(alog_ref[...])
    row = lax.broadcasted_iota(jnp.int32, (Q, Q), 0)
    col = lax.broadcasted_iota(jnp.int32, (Q, Q), 1)
    lower = row >= col
    upper = col >= row
    tri_l = lower.astype(F32)
    tri_u = upper.astype(F32)
    fwd_lane = lax.broadcasted_iota(jnp.int32, (Q, LANES), 1) < SSD_HEADS
    for c in range(n_seq * nc):
        dta = dtsp[c * Q:(c + 1) * Q, :] * a_neg
        pre = jnp.dot(tri_l, dta, precision=lax.Precision.HIGHEST, preferred_element_type=F32)
        suf = jnp.dot(tri_u, dta, precision=lax.Precision.HIGHEST, preferred_element_type=F32)
        a = jnp.where(fwd_lane, pre, suf)
        acol[c * Q:(c + 1) * Q, :] = a
        atr[c] = a.T[0:N_DIR * SSD_HEADS, :]
        for g in range(SSD_GROUPS):
            b0 = SSD_INNER + g * SSD_STATE
            c0 = SSD_INNER + (SSD_GROUPS + g) * SSD_STATE
            bm = xc[c * Q:(c + 1) * Q, b0:b0 + SSD_STATE]
            cbs[c, g] = lax.dot_general(xc[c * Q:(c + 1) * Q, c0:c0 + SSD_STATE].astype(BF16),
                                        bm.astype(BF16), _NT, preferred_element_type=F32)
            bmt[c, g] = bm.T.astype(BF16)

    sel_row = lax.broadcasted_iota(jnp.int32, (3 * LANES, SSD_INNER), 0) & (LANES - 1)
    sel_head = lax.broadcasted_iota(jnp.int32, (3 * LANES, SSD_INNER), 1) >> (SSD_HEAD_DIM.bit_length() - 1)
    sel = [(sel_row == sel_head + d * SSD_HEADS).astype(BF16) for d in range(N_DIR)]
    mask2 = [jnp.concatenate([m, m], axis=1) for m in (lower, upper)]

    for q in range(n_seq):
        for d in range(N_DIR):
            for p in range(N_PAIR):
                if has_h0:
                    hst[q, d, p] = h0_ref[q, d, p * PAIR:(p + 1) * PAIR, :].T
                else:
                    hst[q, d, p] = jnp.zeros((SSD_STATE, PAIR), F32)

    lo_lane = lax.broadcasted_iota(jnp.int32, (Q, PAIR), 1) < SSD_HEAD_DIM

    def chunk(c, d, q):
        r0 = pl.multiple_of(c * Q, Q)
        rows = pl.ds(r0, Q)
        a = acol[rows, :]
        a_t = atr[c]
        dtp = jnp.dot(_split3(dtsp[rows, :]), sel[d], preferred_element_type=F32)
        a_end = a[Q - 1:Q, :] if d == 0 else a[0:1, :]
        dec = jnp.exp(a_end)
        for g in range(SSD_GROUPS):
            c0 = SSD_INNER + (SSD_GROUPS + g) * SSD_STATE
            cmb = xc[rows, c0:c0 + SSD_STATE].astype(BF16)
            cb = cbs[c, g]
            cb2 = jnp.concatenate([cb, cb], axis=1)
            for pr in range(2):
                p = g * 2 + pr
                j0 = d * SSD_HEADS + 2 * p
                j1 = j0 + 1
                xs = xc[rows, p * PAIR:(p + 1) * PAIR]
                ab0 = jnp.broadcast_to(a[:, j0:j0 + 1], (Q, Q))
                ab1 = jnp.broadcast_to(a[:, j1:j1 + 1], (Q, Q))
                seg = (jnp.concatenate([ab0, ab1], axis=1)
                       - jnp.concatenate([a_t[j0:j0 + 1, :], a_t[j1:j1 + 1, :]], axis=1))
                m = (cb2 * jnp.exp(jnp.where(mask2[d], seg, -jnp.inf))).astype(BF16)
                xdt = xs * dtp[:, p * PAIR:(p + 1) * PAIR]
                rhs = jnp.concatenate([jnp.where(lo_lane, xdt, 0.0).astype(BF16),
                                       jnp.where(lo_lane, 0.0, xdt).astype(BF16)], axis=0)
                y_diag = jnp.dot(m, rhs, preferred_element_type=F32)
                hp = hst[q, d, p]
                y_off = jnp.dot(cmb, hp.astype(BF16), preferred_element_type=F32)
                a_pair = jnp.where(lo_lane, ab0, ab1)
                y = y_diag + jnp.exp(a_pair) * y_off
                a_end_pair = jnp.where(lo_lane[0:1, :], a_end[:, j0:j0 + 1], a_end[:, j1:j1 + 1])
                xw = (xdt * jnp.exp(a_end_pair - a_pair)).astype(BF16)
                s_new = jnp.dot(bmt[c, g], xw, preferred_element_type=F32)
                decp = jnp.where(lo_lane[0:1, :], dec[:, j0:j0 + 1], dec[:, j1:j1 + 1])
                hst[q, d, p] = decp * hp + s_new
                if d == 0:
                    y = y + dvec_ref[:, p * PAIR:(p + 1) * PAIR] * xs
                yacc[d, rows, p * PAIR:(p + 1) * PAIR] = y

    def step(t, carry):
        for q in range(n_seq):
            chunk(q * nc + t, 0, q)
            chunk(q * nc + nc - 1 - t, 1, q)
        return carry

    lax.fori_loop(0, nc, step, 0, unroll=2 if n_seq == 1 else 1)
    for c in range(n_seq * nc):
        rows = slice(c * Q, (c + 1) * Q)
        y = (yacc[0, rows, :] + yacc[1, rows, :]) * _silu(z_ref[rows, :])
        y_ref[rows, :] = _rms(y) * ng_ref[...]
    if emit_state:
        for q in range(n_seq):
            for d in range(N_DIR):
                for p in range(N_PAIR):
                    hout_ref[q, d, p * PAIR:(p + 1) * PAIR, :] = hst[q, d, p].T


def _ssd_mixer(z, xbc, dtr, prm, l, *, L, n_seq, nb, blk0, h0=None, y_buf=None, st_buf=None, after=None,
               emit_state):
    R = n_seq * L
    seq = lambda w: pl.BlockSpec((R, w), lambda b: (blk0 + b, 0))
    st = pl.BlockSpec((n_seq, None, N_DIR, SSD_HP, SSD_STATE), lambda b: (b, l, 0, 0, 0))
    args = [z, xbc, dtr, prm["conv_w"], prm["conv_b"], prm["dt_bias"], prm["a_log"], prm["d_vec"],
            prm["norm_g"]]
    in_specs = [seq(SSD_INNER), seq(SSD_CONV_DIM), seq(LANES),
                _layer_spec(l, (SSD_CONV, SSD_CONV_DIM)), _layer_spec(l, (1, SSD_CONV_DIM)),
                _layer_spec(l, (1, LANES)), _layer_spec(l, (1, LANES)),
                _layer_spec(l, (1, SSD_INNER)), _layer_spec(l, (1, SSD_INNER))]
    if h0 is not None:
        args.append(h0)
        in_specs.append(st)
    out_shape = [jax.ShapeDtypeStruct((N_TOK, SSD_INNER), F32)]
    out_specs = [seq(SSD_INNER)]
    aliases = {}
    if y_buf is not None:
        aliases[len(args)] = 0
        args.append(y_buf)
        in_specs.append(_any_spec())
    if emit_state:
        out_shape.append(jax.ShapeDtypeStruct((BATCH, DEPTH, N_DIR, SSD_HP, SSD_STATE), F32))
        out_specs.append(st)
        if st_buf is not None:
            aliases[len(args)] = 1
            args.append(st_buf)
            in_specs.append(_any_spec())
    if after is not None:
        args.append(after)
        in_specs.append(_any_spec())
    res = pl.pallas_call(
        functools.partial(_ssd_kernel, L=L, n_seq=n_seq, has_h0=h0 is not None,
                          n_alias=len(aliases) + int(after is not None), emit_state=emit_state),
        grid=(nb,),
        in_specs=in_specs,
        out_specs=out_specs,
        out_shape=out_shape,
        input_output_aliases=aliases,
        scratch_shapes=[pltpu.VMEM((n_seq * (L + SUBLANES) + SUBLANES, SSD_CONV_DIM), F32),
                        pltpu.VMEM((R, SSD_CONV_DIM), F32),
                        pltpu.VMEM((R, LANES), F32),
                        pltpu.VMEM((R // CHUNK, N_DIR * SSD_HEADS, CHUNK), F32),
                        pltpu.VMEM((R // CHUNK, SSD_GROUPS, CHUNK, CHUNK), F32),
                        pltpu.VMEM((R // CHUNK, SSD_GROUPS, SSD_STATE, CHUNK), BF16),
                        pltpu.VMEM((R, LANES), F32),
                        pltpu.VMEM((N_DIR, R, SSD_INNER), F32),
                        pltpu.VMEM((n_seq, N_DIR, N_PAIR, PAIR, SSD_STATE), F32)],
        compiler_params=_params(),
        name=f"ssd_mixer_L{L}",
    )(*args)
    return res if emit_state else (res[0], None)


_LOG2_CH = S5_GROUP_CH.bit_length() - 1
_LOG2_ST = S5_STATE.bit_length() - 1


def _cmul(ar, ai, br, bi):
    return ar * br - ai * bi, ar * bi + ai * br


def _s5prep_kernel(lre_ref, lim_ref, ldt_ref, btr_ref, bti_ref, ctr_ref, cti_ref,
                   bblk_ref, cblk_ref, pw_ref):
    brow = lax.broadcasted_iota(jnp.int32, (S5_DIM, S5_LANES), 0) >> _LOG2_CH
    bcol = lax.broadcasted_iota(jnp.int32, (S5_DIM, S5_LANES), 1) >> _LOG2_ST
    bmask = brow == bcol
    p1 = lax.broadcasted_iota(jnp.int32, (S5_SEG, S5_LANES), 0) + 1
    for d in range(N_DIR):
        lre = lre_ref[d]
        lim = lim_ref[d]
        step = jnp.exp(ldt_ref[d])
        mag = jnp.exp(lre * step)
        lbr = mag * jnp.cos(lim * step)
        lbi = mag * jnp.sin(lim * step)
        den = lre * lre + lim * lim
        nr = lbr - 1.0
        cr = (nr * lre + lbi * lim) / den
        ci = (lbi * lre - nr * lim) / den
        br, bi = _cmul(cr, ci, btr_ref[...], bti_ref[...])
        br = jnp.where(bmask, br, 0.0).astype(BF16)
        bi = jnp.where(bmask, bi, 0.0).astype(BF16)
        for sl in range(S5_NSLAB):
            bblk_ref[d, sl, :, 0:S5_SLAB] = br[:, sl * S5_SLAB:(sl + 1) * S5_SLAB]
            bblk_ref[d, sl, :, S5_SLAB:2 * S5_SLAB] = bi[:, sl * S5_SLAB:(sl + 1) * S5_SLAB]
        rr = jnp.ones((S5_SEG, S5_LANES), F32)
        ri = jnp.zeros((S5_SEG, S5_LANES), F32)
        sr, si = lbr, lbi
        for k in range(S5_SEG.bit_length()):
            bit = ((p1 >> k) & 1) == 1
            tr, ti = _cmul(rr, ri, sr, si)
            rr = jnp.where(bit, tr, rr)
            ri = jnp.where(bit, ti, ri)
            sr, si = _cmul(sr, si, sr, si)
        pw_ref[d, 0, 0:S5_SEG, :] = rr
        pw_ref[d, 1, 0:S5_SEG, :] = ri
        n = S5_SEG
        while n < S5_PW_ROWS:
            tr, ti = _cmul(pw_ref[d, 0, 0:n, :], pw_ref[d, 1, 0:n, :],
                           pw_ref[d, 0, n - 1:n, :], pw_ref[d, 1, n - 1:n, :])
            pw_ref[d, 0, n:2 * n, :] = tr
            pw_ref[d, 1, n:2 * n, :] = ti
            n *= 2
    crow = lax.broadcasted_iota(jnp.int32, (S5_LANES, S5_DIM), 0) >> _LOG2_ST
    ccol = lax.broadcasted_iota(jnp.int32, (S5_LANES, S5_DIM), 1) >> _LOG2_CH
    cmask = crow == ccol
    cr = jnp.where(cmask, ctr_ref[...], 0.0).astype(BF16)
    ci = jnp.where(cmask, -cti_ref[...], 0.0).astype(BF16)
    for sl in range(S5_NSLAB):
        cblk_ref[sl, 0:S5_SLAB, :] = cr[sl * S5_SLAB:(sl + 1) * S5_SLAB, :]
        cblk_ref[sl, S5_SLAB:2 * S5_SLAB, :] = ci[sl * S5_SLAB:(sl + 1) * S5_SLAB, :]


def _s5_prep(lam_re, lam_im, log_dt, b_re, b_im, c_re, c_im):
    row = lambda t: t.reshape(DEPTH, N_DIR, 1, S5_LANES)
    ldt = jnp.repeat(log_dt, S5_STATE, axis=-1)
    bt = lambda t: jnp.tile(t.transpose(0, 3, 1, 2).reshape(DEPTH, S5_GROUP_CH, S5_LANES),
                            (1, S5_GROUPS, 1))
    ct = lambda t: jnp.tile(t.transpose(0, 1, 3, 2).reshape(DEPTH, S5_LANES, S5_GROUP_CH),
                            (1, 1, S5_GROUPS))
    vec = pl.BlockSpec((None, N_DIR, 1, S5_LANES), lambda l: (l, 0, 0, 0))
    bsp = pl.BlockSpec((None, S5_DIM, S5_LANES), lambda l: (l, 0, 0))
    csp = pl.BlockSpec((None, S5_LANES, S5_DIM), lambda l: (l, 0, 0))
    return pl.pallas_call(
        _s5prep_kernel,
        grid=(DEPTH,),
        in_specs=[vec, vec, vec, bsp, bsp, csp, csp],
        out_specs=[pl.BlockSpec((None, N_DIR, S5_NSLAB, S5_DIM, 2 * S5_SLAB), lambda l: (l, 0, 0, 0, 0)),
                   pl.BlockSpec((None, S5_NSLAB, 2 * S5_SLAB, S5_DIM), lambda l: (l, 0, 0, 0)),
                   pl.BlockSpec((None, N_DIR, 2, S5_PW_ROWS, S5_LANES), lambda l: (l, 0, 0, 0, 0))],
        out_shape=[jax.ShapeDtypeStruct((DEPTH, N_DIR, S5_NSLAB, S5_DIM, 2 * S5_SLAB), BF16),
                   jax.ShapeDtypeStruct((DEPTH, S5_NSLAB, 2 * S5_SLAB, S5_DIM), BF16),
                   jax.ShapeDtypeStruct((DEPTH, N_DIR, 2, S5_PW_ROWS, S5_LANES), F32)],
        compiler_params=_params(),
        name="s5_prep",
    )(row(lam_re), row(lam_im), row(ldt), bt(b_re), bt(b_im), ct(c_re), ct(c_im))


def _s5_moves(n_seq, n_seg, seq_len, col_major):
    steps = seq_len // n_seg
    if not col_major:
        return [(q * seq_len + s * steps, steps, q * n_seg + s, S5_SEG)
                for q in range(n_seq) for s in range(n_seg)]
    assert n_seq == 1 and n_seg == S5_SEG
    grid_rows = seq_len // GRID_W
    wseg = GRID_W // n_seg
    return [(r * GRID_W + s * wseg, wseg, r * S5_SEG + s, grid_rows * S5_SEG)
            for r in range(grid_rows) for s in range(n_seg)]


def _s5_kernel(*refs, n_seq, n_seg, seq_len, col_major, has_h0, n_alias, emit_state):
    u_ref, bblk_ref, cblk_ref, pw_ref, dvec_ref, gw_ref, gb_ref = refs[:7]
    k = 7
    h0r_ref, h0i_ref = (refs[k], refs[k + 1]) if has_h0 else (None, None)
    k += 2 * int(has_h0) + n_alias
    y_ref = refs[k]
    fr_ref, fi_ref = (refs[k + 1], refs[k + 2]) if emit_state else (None, None)
    up, buf_f, buf_b, yacc, yp, cin, fin = refs[k + 1 + 2 * int(emit_state):]
    assert n_seq * n_seg == S5_SEG and not (has_h0 and n_seg == 1) and not (emit_state and n_seg > 1)
    n_rows = n_seq * seq_len
    steps = seq_len // n_seg
    n_planes = S5_DIM // LANES
    W = S5_SLAB
    moves = _s5_moves(n_seq, n_seg, seq_len, col_major)

    for src, n, dst, stride in moves:
        for t in range(n_planes):
            up[t, pl.ds(dst, n, stride=stride), :] = u_ref[src:src + n, t * LANES:(t + 1) * LANES]

    def u_rows(rs):
        return jnp.concatenate([up[t, rs, :] for t in range(n_planes)], axis=1)

    zero = jnp.zeros((S5_SEG, W), F32)
    zrow = jnp.zeros((1, W), F32)
    for sl in range(S5_NSLAB):
        ln = slice(sl * W, (sl + 1) * W)
        for r0 in range(0, n_rows, ROW_SLAB):
            rs = slice(r0, r0 + ROW_SLAB)
            ub = u_rows(rs).astype(BF16)
            buf_f[rs, :] = jnp.dot(ub, bblk_ref[0, sl], preferred_element_type=F32)
            buf_b[rs, :] = jnp.dot(ub, bblk_ref[1, sl], preferred_element_type=F32)
        lam = [[jnp.broadcast_to(pw_ref[d, c, 0:1, ln], (S5_SEG, W)) for c in range(2)]
               for d in range(N_DIR)]

        def body(t, carry):
            fr, fi, br, bi = carry
            rf = pl.ds(pl.multiple_of(t * S5_SEG, S5_SEG), S5_SEG)
            rb = pl.ds(pl.multiple_of((steps - 1 - t) * S5_SEG, S5_SEG), S5_SEG)
            nfr = lam[0][0] * fr - lam[0][1] * fi + buf_f[rf, 0:W]
            nfi = lam[0][0] * fi + lam[0][1] * fr + buf_f[rf, W:2 * W]
            nbr = lam[1][0] * br - lam[1][1] * bi + buf_b[rb, 0:W]
            nbi = lam[1][0] * bi + lam[1][1] * br + buf_b[rb, W:2 * W]
            buf_f[rf, 0:W] = nfr
            buf_f[rf, W:2 * W] = nfi
            buf_b[rb, 0:W] = nbr
            buf_b[rb, W:2 * W] = nbi
            return nfr, nfi, nbr, nbi

        ends = lax.fori_loop(0, steps, body, (zero, zero, zero, zero), unroll=2)

        if n_seg == 1:
            if emit_state:
                for d in range(N_DIR):
                    fin[d, 0, :, ln] = ends[2 * d]
                    fin[d, 1, :, ln] = ends[2 * d + 1]
        else:
            for d in range(N_DIR):
                er, ei = ends[2 * d], ends[2 * d + 1]
                lpr = pw_ref[d, 0, steps - 1:steps, ln]
                lpi = pw_ref[d, 1, steps - 1:steps, ln]
                pr = h0r_ref[d:d + 1, ln] if has_h0 else zrow
                pi = h0i_ref[d:d + 1, ln] if has_h0 else zrow
                order = range(n_seg) if d == 0 else range(n_seg - 1, -1, -1)
                for s in order:
                    cin[d, 0, s:s + 1, :] = pr
                    cin[d, 1, s:s + 1, :] = pi
                    tr, ti = _cmul(lpr, lpi, pr, pi)
                    pr = er[s:s + 1, :] + tr
                    pi = ei[s:s + 1, :] + ti
            cfr = cin[0, 0]
            cfi = cin[0, 1]
            cbr = cin[1, 0]
            cbi = cin[1, 1]

            def fix(p, carry):
                rows = pl.ds(pl.multiple_of(p * S5_SEG, S5_SEG), S5_SEG)
                q = steps - 1 - p
                tfr, tfi = _cmul(pw_ref[0, 0, pl.ds(p, 1), ln], pw_ref[0, 1, pl.ds(p, 1), ln], cfr, cfi)
                tbr, tbi = _cmul(pw_ref[1, 0, pl.ds(q, 1), ln], pw_ref[1, 1, pl.ds(q, 1), ln], cbr, cbi)
                buf_f[rows, 0:W] = buf_f[rows, 0:W] + buf_b[rows, 0:W] + tfr + tbr
                buf_f[rows, W:2 * W] = buf_f[rows, W:2 * W] + buf_b[rows, W:2 * W] + tfi + tbi
                return carry

            lax.fori_loop(0, steps, fix, 0, unroll=2)

        for r0 in range(0, n_rows, ROW_SLAB):
            rs = slice(r0, r0 + ROW_SLAB)
            hs = buf_f[rs, :] if n_seg > 1 else buf_f[rs, :] + buf_b[rs, :]
            part = jnp.dot(hs.astype(BF16), cblk_ref[sl], preferred_element_type=F32)
            if sl == 0:
                yacc[rs, :] = part
            else:
                yacc[rs, :] = yacc[rs, :] + part

    gwb = gw_ref[...].astype(BF16)
    for r0 in range(0, n_rows, ROW_SLAB):
        rs = slice(r0, r0 + ROW_SLAB)
        y = _gelu_tanh(yacc[rs, :] + dvec_ref[...] * u_rows(rs))
        gate = jnp.dot(y.astype(BF16), gwb, preferred_element_type=F32) + gb_ref[...]
        y = y * jax.nn.sigmoid(gate)
        for t in range(n_planes):
            yp[t, rs, :] = y[:, t * LANES:(t + 1) * LANES]
    for src, n, dst, stride in moves:
        for t in range(n_planes):
            y_ref[src:src + n, t * LANES:(t + 1) * LANES] = yp[t, pl.ds(dst, n, stride=stride), :]
    if emit_state:
        for q in range(n_seq):
            for d in range(N_DIR):
                fr_ref[q, d:d + 1, :] = fin[d, 0, q:q + 1, :]
                fi_ref[q, d:d + 1, :] = fin[d, 1, q:q + 1, :]


def _s5_mixer(u, bblk, cblk, pw, prm, l, *, seq_len, n_seq, nb, blk0, col_major, h0=None, y_buf=None,
              st_buf=None, emit_state):
    n_rows = n_seq * seq_len
    n_seg = S5_SEG // n_seq
    seq = pl.BlockSpec((n_rows, S5_DIM), lambda b: (blk0 + b, 0))
    st = pl.BlockSpec((n_seq, None, N_DIR, S5_LANES), lambda b: (b, l, 0, 0))
    once = dict(pipeline_mode=pl.Buffered(1))
    args = [u, bblk, cblk, pw, prm["d_vec"], prm["glu_w"], prm["glu_b"]]
    in_specs = [seq, _layer_spec(l, (N_DIR, S5_NSLAB, S5_DIM, 2 * S5_SLAB), **once),
                _layer_spec(l, (S5_NSLAB, 2 * S5_SLAB, S5_DIM), **once),
                _layer_spec(l, (N_DIR, 2, S5_PW_ROWS, S5_LANES), **once), _layer_spec(l, (1, S5_DIM)),
                _layer_spec(l, (S5_DIM, S5_DIM)), _layer_spec(l, (1, S5_DIM))]
    if h0 is not None:
        h0_spec = pl.BlockSpec((None, None, N_DIR, S5_LANES), lambda b: (b, l, 0, 0))
        args += list(h0)
        in_specs += [h0_spec, h0_spec]
    out_shape = [jax.ShapeDtypeStruct((N_TOK, S5_DIM), F32)]
    out_specs = [seq]
    aliases = {}
    if y_buf is not None:
        aliases[len(args)] = 0
        args.append(y_buf)
        in_specs.append(_any_spec())
    if emit_state:
        out_shape += [jax.ShapeDtypeStruct((BATCH, DEPTH, N_DIR, S5_LANES), F32)] * 2
        out_specs += [st, st]
        if st_buf is not None:
            for j, buf in enumerate(st_buf):
                aliases[len(args)] = 1 + j
                args.append(buf)
                in_specs.append(_any_spec())
    res = pl.pallas_call(
        functools.partial(_s5_kernel, n_seq=n_seq, n_seg=n_seg, seq_len=seq_len, col_major=col_major,
                          has_h0=h0 is not None, n_alias=len(aliases), emit_state=emit_state),
        grid=(nb,),
        in_specs=in_specs,
        out_specs=out_specs,
        out_shape=out_shape,
        input_output_aliases=aliases,
        scratch_shapes=[pltpu.VMEM((S5_DIM // LANES, n_rows, LANES), F32),
                        pltpu.VMEM((n_rows, 2 * S5_SLAB), F32),
                        pltpu.VMEM((n_rows, 2 * S5_SLAB), F32),
                        pltpu.VMEM((n_rows, S5_DIM), F32),
                        pltpu.VMEM((S5_DIM // LANES, n_rows, LANES), F32),
                        pltpu.VMEM((N_DIR, 2, S5_SEG, S5_SLAB), F32),
                        pltpu.VMEM((N_DIR, 2, S5_SEG, S5_LANES), F32)],
        compiler_params=_params(),
        name=f"s5_mixer_L{seq_len}",
    )(*args)
    return (res[0], res[1], res[2]) if emit_state else (res[0], None, None)


def _outffn_kernel(xa_ref, xb_ref, ys_ref, yg_ref, y5_ref, g1_ref, sh_ref, sc_ref, g2_ref, ng_ref,
                   fg_ref, wo_ref, w1_ref, w2_ref, *rest, final):
    n_out = 2 if final else 1
    o_refs = rest[:n_out]
    wob, w1b, w2b = rest[n_out:]
    i = pl.program_id(0)

    @pl.when(i < FFN_PREP)
    def _():
        wob[i] = wo_ref[...].astype(BF16)
        w1b[i] = w1_ref[...].astype(BF16)
        w2b[i] = w2_ref[...].astype(BF16)

    @pl.when(i >= FFN_PREP)
    def _():
        t = i - FFN_PREP
        n_ctx = N_CTX // FFN_TILE
        r = _mod_row(t, FFN_TILE)
        row = lambda ref: ref[pl.ds(r, 1), :]
        x = jnp.where(t < n_ctx, xa_ref[...], xb_ref[...])
        mixed = jnp.concatenate([ys_ref[...].astype(BF16), yg_ref[...].astype(BF16),
                                 y5_ref[...].astype(BF16)], axis=1)
        w_out = wob[...].reshape(D_MODEL, D_MODEL)
        x1 = x + row(g1_ref) * jnp.dot(mixed, w_out, preferred_element_type=F32)
        h = ((_rms(x1) * ng_ref[...]) * (1.0 + row(sc_ref)) + row(sh_ref)).astype(BF16)
        acc = jnp.zeros((FFN_TILE, D_MODEL), F32)
        for j in range(FFN_PREP):
            a = jnp.maximum(jnp.dot(h, w1b[j], preferred_element_type=F32), 0.0)
            acc = acc + jnp.dot((a * a).astype(BF16), w2b[j], preferred_element_type=F32)
        x2 = x1 + row(g2_ref) * acc
        if not final:
            o_refs[0][...] = x2
        else:
            y = _rms(x2) * fg_ref[...]

            @pl.when(t < n_ctx)
            def _():
                o_refs[0][...] = y

            @pl.when(t >= n_ctx)
            def _():
                o_refs[1][...] = y


def _out_ffn(xa, xb, b_is_stream, y_ssd, y_sgu, y_s5, l, mods, norm_g, final_g, w_out, w1, w2, *, final):
    tile_of = lambda i: jnp.maximum(i - FFN_PREP, 0)
    tok = lambda w: pl.BlockSpec((FFN_TILE, w), lambda i: (tile_of(i), 0))
    chunk_of = lambda i: jnp.minimum(i, FFN_PREP - 1)
    sa, sb = _split_tok_specs(FFN_TILE, D_MODEL, b_is_stream, tile_of)
    if final:
        out_specs = list(_split_tok_specs(FFN_TILE, D_MODEL, False, tile_of))
        out_shape = [jax.ShapeDtypeStruct((N_CTX, D_MODEL), F32), jax.ShapeDtypeStruct((N_LAT, D_MODEL), F32)]
    else:
        out_specs = [tok(D_MODEL)]
        out_shape = [jax.ShapeDtypeStruct((N_TOK, D_MODEL), F32)]
    ko, kf = D_MODEL // FFN_PREP, D_FF // FFN_PREP
    return pl.pallas_call(
        functools.partial(_outffn_kernel, final=final),
        grid=(FFN_PREP + N_TOK // FFN_TILE,),
        in_specs=[sa, sb, tok(SSD_INNER), tok(SGU_DIM), tok(S5_DIM),
                  _mod_spec(l, 2), _mod_spec(l, 3), _mod_spec(l, 4), _mod_spec(l, 5),
                  _layer_spec(l, (1, D_MODEL)), pl.BlockSpec((1, D_MODEL), lambda i: (0, 0)),
                  pl.BlockSpec((None, ko, D_MODEL), lambda i: (l, chunk_of(i), 0)),
                  pl.BlockSpec((None, D_MODEL, kf), lambda i: (l, 0, chunk_of(i))),
                  pl.BlockSpec((None, kf, D_MODEL), lambda i: (l, chunk_of(i), 0))],
        out_specs=out_specs,
        out_shape=out_shape,
        scratch_shapes=[pltpu.VMEM((FFN_PREP, ko, D_MODEL), BF16),
                        pltpu.VMEM((FFN_PREP, D_MODEL, kf), BF16),
                        pltpu.VMEM((FFN_PREP, kf, D_MODEL), BF16)],
        compiler_params=_params(),
        name="out_ffn_final" if final else "out_ffn",
    )(xa, xb, y_ssd, y_sgu, y_s5, mods, mods, mods, mods, norm_g, final_g, w_out, w1, w2)


def kernel(x_prompt, x_sample, state_ssd, state_s5_re, state_s5_im, c, c_ctx, ada_w, ada_b, norm1_g,
           norm2_g, w_in, ssd_conv_w, ssd_conv_b, ssd_dt_bias, ssd_a_log, ssd_d, ssd_norm_g,
           sgu_norm_g, sgu_w, sgu_b, s5_lambda_re, s5_lambda_im, s5_log_dt, s5_b_re, s5_b_im,
           s5_c_re, s5_c_im, s5_d, s5_glu_w, s5_glu_b, w_out, ffn_w1, ffn_w2, final_norm_g):
    cvec = jnp.concatenate([c_ctx[None, :], c,
                            jnp.zeros((MOD_ROWS - 1 - DEC_BATCH, D_MODEL), F32)], axis=0)
    mods = _adaln_mods(cvec, ada_w, ada_b)
    bblk, cblk, pw = _s5_prep(s5_lambda_re, s5_lambda_im, s5_log_dt, s5_b_re, s5_b_im,
                              s5_c_re, s5_c_im)

    w_in_t = w_in.transpose(0, 2, 1)
    vec =lambda t: t.reshape(DEPTH, 1, -1)
    lane_pad = lambda t: jnp.pad(vec(t), ((0, 0), (0, 0), (0, LANES - t[0].size)))
    ssd_prm = dict(conv_w=ssd_conv_w, conv_b=vec(ssd_conv_b), dt_bias=lane_pad(ssd_dt_bias),
                   a_log=lane_pad(ssd_a_log), d_vec=vec(jnp.repeat(ssd_d, SSD_HEAD_DIM, axis=-1)),
                   norm_g=vec(ssd_norm_g))
    s5_prm = dict(d_vec=vec(s5_d), glu_w=s5_glu_w, glu_b=vec(s5_glu_b))
    sgu_w_pair = sgu_w.reshape(DEPTH, SGU_HEADS // 2, 2, CHUNK, CHUNK).transpose(0, 1, 3, 2, 4)
    sgu_w_pair = sgu_w_pair.reshape(DEPTH, SGU_HEADS // 2, CHUNK, 2 * CHUNK).astype(BF16)
    sgu_b_full = jnp.repeat(sgu_b.transpose(0, 2, 1), SGU_DIM // SGU_HEADS, axis=2)
    norm1 = vec(norm1_g)
    norm2 = vec(norm2_g)
    sgu_g = vec(sgu_norm_g)
    final_g = final_norm_g.reshape(1, D_MODEL)

    lat_ssd = state_ssd.reshape(DEC_BATCH, DEPTH, N_DIR, SSD_HP, SSD_STATE)
    lat_s5 = (state_s5_re.reshape(DEC_BATCH, DEPTH, N_DIR, S5_LANES),
              state_s5_im.reshape(DEC_BATCH, DEPTH, N_DIR, S5_LANES))
    lat_blk = N_CTX // DEC_SEQ

    xa = x_prompt.reshape(N_CTX, D_MODEL)
    xb = x_sample.reshape(N_LAT, D_MODEL)
    b_is_stream = False
    st_ssd = None
    st_s5 = None
    for l in range(DEPTH):
        z, xbc, y_sgu, s5u, dtr = _in_proj(xa, xb, b_is_stream, l, mods, norm1, w_in_t, sgu_g,
                                           sgu_w_pair, sgu_b_full)
        y_s5, fr, fi = _s5_mixer(s5u, bblk, cblk, pw, s5_prm, l, seq_len=SEQ, n_seq=S5_SEG,
                                 nb=BATCH // S5_SEG, blk0=0, col_major=False, st_buf=st_s5,
                                 emit_state=True)
        st_s5 = (fr, fi)
        y_s5, _, _ = _s5_mixer(s5u, bblk, cblk, pw, s5_prm, l, seq_len=DEC_SEQ, n_seq=1, nb=DEC_BATCH,
                               blk0=lat_blk, col_major=True, h0=lat_s5, y_buf=y_s5, emit_state=False)
        y_ssd, st_ssd = _ssd_mixer(z, xbc, dtr, ssd_prm, l, L=SEQ, n_seq=SSD_CTX_SEQS,
                                   nb=BATCH // SSD_CTX_SEQS, blk0=0, st_buf=st_ssd, after=y_s5,
                                   emit_state=True)
        y_ssd, _ = _ssd_mixer(z, xbc, dtr, ssd_prm, l, L=DEC_SEQ, n_seq=1, nb=DEC_BATCH, blk0=lat_blk,
                              h0=lat_ssd, y_buf=y_ssd, emit_state=False)
        final = l == DEPTH - 1
        out = _out_ffn(xa, xb, b_is_stream, y_ssd, y_sgu, y_s5, l, mods, norm2, final_g, w_out, ffn_w1,
                       ffn_w2, final=final)
        if not final:
            xa = xb = out[0]
            b_is_stream = True

    y_prompt = out[0].reshape(BATCH, SEQ, D_MODEL)
    y_sample = out[1].reshape(DEC_BATCH, DEC_SEQ, D_MODEL)
    new_state_ssd = st_ssd.reshape(BATCH, DEPTH, N_DIR, SSD_HEADS, SSD_HEAD_DIM, SSD_STATE)
    s5_shape = (BATCH, DEPTH, N_DIR, S5_GROUPS, S5_STATE)
    return (y_prompt, y_sample, new_state_ssd, st_s5[0].reshape(s5_shape), st_s5[1].reshape(s5_shape))
```

```python
import functools
import math

import jax
import jax.numpy as jnp
from jax import lax
from jax.experimental import pallas as pl
from jax.experimental.pallas import tpu as pltpu

F32 = jnp.float32
BF16 = jnp.bfloat16

D_MODEL = 1024
BATCH = 16
SEQ = 256
DEPTH = 2
DEC_BATCH = 2
DEC_SEQ = 1024
GRID_W = 64
CHUNK = 128
N_DIR = 2
EPS = 1e-6
SSD_INNER = 512
SSD_HEAD_DIM = 64
SSD_HEADS = 8
SSD_GROUPS = 2
SSD_STATE = 128
SSD_CONV = 5
SSD_CONV_DIM = SSD_INNER + 2 * SSD_GROUPS * SSD_STATE
SGU_DIM = 256
SGU_HEADS = 4
S5_DIM = 256
S5_GROUP_CH = 16
S5_GROUPS = 16
S5_STATE = 64
D_FF = 4 * D_MODEL
OFF_XBC = SSD_INNER
OFF_DT = OFF_XBC + SSD_CONV_DIM
OFF_SGU = OFF_DT + N_DIR * SSD_HEADS
OFF_S5 = OFF_SGU + 2 * SGU_DIM
IN_DIM = OFF_S5 + S5_DIM

N_CTX = BATCH * SEQ
N_LAT = DEC_BATCH * DEC_SEQ
N_TOK = N_CTX + N_LAT
LANES = 128
SUBLANES = 8
MOD_ROWS = SUBLANES
SSD_HP = SSD_HEADS * SSD_HEAD_DIM
S5_LANES = S5_GROUPS * S5_STATE
S5_SEG = SUBLANES
S5_PW_ROWS = DEC_SEQ // S5_SEG
S5_SLAB = 256
S5_NSLAB = S5_LANES // S5_SLAB
PAIR = 2 * SSD_HEAD_DIM
N_PAIR = SSD_HEADS // 2
TOK_TILE = 512
N_CTX_TILES = N_CTX // TOK_TILE
FFN_TILE = 512
FFN_PREP = 8
ROW_SLAB = 256
CONV_LANES = 256
SSD_CTX_SEQS = 2
VMEM_LIMIT = 56 * 1024 * 1024

_NT = (((1,), (1,)), ((), ()))


def _params(n_axes=1):
    return pltpu.CompilerParams(dimension_semantics=("arbitrary",) * n_axes,
                                vmem_limit_bytes=VMEM_LIMIT)


def _layer_spec(l, shape, **kw):
    return pl.BlockSpec((None,) + tuple(shape), lambda *_: (l,) + (0,) * len(shape), **kw)


def _mod_spec(l, k):
    return pl.BlockSpec((None, MOD_ROWS, D_MODEL), lambda *_: (l, 0, k))


def _any_spec():
    return pl.BlockSpec(memory_space=pl.ANY)


def _mod_row(i, tm):
    n_ctx = N_CTX // tm
    return jnp.where(i < n_ctx, 0, 1 + (i - n_ctx) // (DEC_SEQ // tm))


def _split_tok_specs(tm, width, b_has_ctx_rows, tile_of=lambda i: i):
    n_ctx = N_CTX // tm
    b_off = n_ctx if b_has_ctx_rows else 0
    a = pl.BlockSpec((tm, width), lambda i: (jnp.minimum(tile_of(i), n_ctx - 1), 0))
    b = pl.BlockSpec((tm, width), lambda i: (jnp.maximum(tile_of(i), n_ctx) - n_ctx + b_off, 0))
    return a, b


def _silu(x):
    return x * jax.nn.sigmoid(x)


def _gelu_tanh(x):
    c = math.sqrt(2.0 / math.pi)
    return 0.5 * x * (1.0 + jnp.tanh(c * (x + 0.044715 * (x * x * x))))


def _rms(x):
    return x * lax.rsqrt(jnp.mean(x * x, axis=-1, keepdims=True) + EPS)


def _bdot(a, b):
    return jnp.dot(a.astype(BF16), b.astype(BF16), preferred_element_type=F32)


def _split3(x):
    hi = x.astype(BF16)
    r = x - hi.astype(F32)
    mid = r.astype(BF16)
    lo = (r - mid.astype(F32)).astype(BF16)
    return jnp.concatenate([hi, mid, lo], axis=1)


def _mod_kernel(c_ref, w_ref, b_ref, o_ref):
    o_ref[...] = _bdot(_silu(c_ref[...]), w_ref[...]) + b_ref[...]


def _adaln_mods(cvec, ada_w, ada_b):
    n_blk = 6
    return pl.pallas_call(
        _mod_kernel,
        grid=(DEPTH, n_blk),
        in_specs=[pl.BlockSpec((MOD_ROWS, D_MODEL), lambda l, j: (0, 0)),
                  pl.BlockSpec((None, D_MODEL, D_MODEL), lambda l, j: (l, 0, j)),
                  pl.BlockSpec((None, 1, D_MODEL), lambda l, j: (l, 0, j))],
        out_specs=pl.BlockSpec((None, MOD_ROWS, D_MODEL), lambda l, j: (l, 0, j)),
        out_shape=jax.ShapeDtypeStruct((DEPTH, MOD_ROWS, 6 * D_MODEL), F32),
        compiler_params=_params(2),
        name="adaln_mod",
    )(cvec, ada_w, ada_b.reshape(DEPTH, 1, 6 * D_MODEL))


_C_Z = 0
_C_XBC = _C_Z + SSD_INNER
_C_SGU = _C_XBC + SSD_CONV_DIM
_C_S5 = _C_SGU + 2 * SGU_DIM
_C_DT = _C_S5 + S5_DIM
_C_END = _C_DT + LANES


def _inproj_kernel(xa_ref, xb_ref, sh_ref, sc_ref, g_ref, win_ref, sg_ref, sw_ref, sb_ref,
                   z_ref, xbc_ref, ysgu_ref, s5_ref, dt_ref, w_ref):
    i = pl.program_id(0)

    @pl.when(i == 0)
    def _():
        moves = ((0, OFF_DT, _C_Z), (OFF_SGU, IN_DIM, _C_SGU), (OFF_DT, OFF_SGU, _C_DT))
        for src0, src1, dst0 in moves:
            for r0 in range(src0, src1, ROW_SLAB):
                n = min(ROW_SLAB, src1 - r0)
                w_ref[dst0 + r0 - src0:dst0 + r0 - src0 + n, :] = win_ref[r0:r0 + n, :].astype(BF16)
        n_pad = LANES - N_DIR * SSD_HEADS
        w_ref[_C_END - n_pad:_C_END, :] = jnp.zeros((n_pad, D_MODEL), BF16)

    r = _mod_row(i, TOK_TILE)
    x = jnp.where(i < N_CTX_TILES, xa_ref[...], xb_ref[...])
    shift = sh_ref[pl.ds(r, 1), :]
    scale = sc_ref[pl.ds(r, 1), :]
    h = (_rms(x) * g_ref[...]) * (1.0 + scale) + shift
    hb = h.astype(BF16)
    proj = lambda c0, c1: lax.dot_general(hb, w_ref[c0:c1, :], _NT, preferred_element_type=F32)
    uv = _gelu_tanh(proj(_C_SGU, _C_S5))
    z_ref[...] = proj(_C_Z, _C_XBC)
    xbc_ref[...] = proj(_C_XBC, _C_SGU)
    s5_ref[...] = proj(_C_S5, _C_DT)
    dt_ref[...] = proj(_C_DT, _C_END)

    u = uv[:, :SGU_DIM]
    v = _rms(uv[:, SGU_DIM:]) * sg_ref[...]
    lo_lane = lax.broadcasted_iota(jnp.int32, (CHUNK, LANES), 1) < (LANES // 2)
    for c in range(TOK_TILE // CHUNK):
        rows = slice(c * CHUNK, (c + 1) * CHUNK)
        mix = []
        for pr in range(SGU_HEADS // 2):
            vp = v[rows, pr * LANES:(pr + 1) * LANES]
            rhs = jnp.concatenate([jnp.where(lo_lane, vp, 0.0).astype(BF16),
                                   jnp.where(lo_lane, 0.0, vp).astype(BF16)], axis=0)
            mix.append(jnp.dot(sw_ref[pr], rhs, preferred_element_type=F32))
        ysgu_ref[rows, :] = u[rows, :] * (jnp.concatenate(mix, axis=1) + sb_ref[...])


def _in_proj(xa, xb, b_is_stream, l, mods, norm_g, w_in, sgu_g, sgu_w_pair, sgu_b_full):
    tok = lambda w: pl.BlockSpec((TOK_TILE, w), lambda i: (i, 0))
    widths = (SSD_INNER, SSD_CONV_DIM, SGU_DIM, S5_DIM, LANES)
    sa, sb = _split_tok_specs(TOK_TILE, D_MODEL, b_is_stream)
    return pl.pallas_call(
        _inproj_kernel,
        grid=(N_TOK // TOK_TILE,),
        in_specs=[sa, sb, _mod_spec(l, 0), _mod_spec(l, 1), _layer_spec(l, (1, D_MODEL)),
                  _layer_spec(l, (IN_DIM, D_MODEL), pipeline_mode=pl.Buffered(1)),
                  _layer_spec(l, (1, SGU_DIM)), _layer_spec(l, (SGU_HEADS // 2, CHUNK, 2 * CHUNK)),
                  _layer_spec(l, (CHUNK, SGU_DIM))],
        out_specs=[tok(w) for w in widths],
        out_shape=[jax.ShapeDtypeStruct((N_TOK, w), F32) for w in widths],
        scratch_shapes=[pltpu.VMEM((_C_END, D_MODEL), BF16)],
        compiler_params=_params(),
        name="in_proj",
    )(xa, xb, mods, mods, norm_g, w_in, sgu_g, sgu_w_pair, sgu_b_full)


def _ssd_kernel(*refs, L, n_seq, has_h0, n_alias, emit_state):
    z_ref, xbc_ref, dt_ref, cw_ref, cb_ref, dtb_ref, alog_ref, dvec_ref, ng_ref = refs[:9]
    k = 9
    h0_ref = refs[k] if has_h0 else None
    k += int(has_h0) + n_alias
    y_ref = refs[k]
    hout_ref = refs[k + 1] if emit_state else None
    xpad, xc, acol, atr, cbs, bmt, dtsp, yacc, hst = refs[k + 1 + int(emit_state):]
    Q = CHUNK
    nc = L // Q
    halo = SUBLANES
    pad = (SSD_CONV - 1) // 2

    pitch = L + halo
    for q in range(n_seq + 1):
        xpad[q * pitch:q * pitch + halo, :] = jnp.zeros((halo, SSD_CONV_DIM), F32)
    for g in range(n_seq * nc):
        o = halo + (g // nc) * pitch + (g % nc) * Q
        xpad[o:o + Q, :] = xbc_ref[g * Q:(g + 1) * Q, :]
    win = Q + 2 * halo
    for c in range(n_seq * nc):
        for lb in range(0, SSD_CONV_DIM, CONV_LANES):
            ln = slice(lb, lb + CONV_LANES)
            o = (c // nc) * pitch + (c % nc) * Q
            xa = xpad[o:o + win, ln]
            acc = cb_ref[:, ln] + cw_ref[pad:pad + 1, ln] * xa[halo:halo + Q, :]
            for t in range(SSD_CONV):
                if t != pad:
                    rolled = pltpu.roll(xa, (pad - t) % win, 0)
                    acc = acc + cw_ref[t:t + 1, ln] * rolled[halo:halo + Q, :]
            xc[c * Q:(c + 1) * Q, ln] = _silu(acc)

    raw = dt_ref[...] + dtb_ref[...]
    dt = jnp.maximum(raw, 0.0) + jnp.log(1.0 + jnp.exp(-jnp.abs(raw)))
    dtsp[...] = dt
    a_neg = -jnp.exp(alog_ref[...])
    row = lax.broadcasted_iota(jnp.int32, (Q, Q), 0)
    col = lax.broadcasted_iota(jnp.int32, (Q, Q), 1)
    lower = row >= col
    upper = col >= row
    tri3 = jnp.concatenate([lower.astype(BF16)] * 3, axis=1)
    fwd_lane = lax.broadcasted_iota(jnp.int32, (Q, LANES), 1) < SSD_HEADS
    for c in range(n_seq * nc):
        dta = dtsp[c * Q:(c + 1) * Q, :] * a_neg
        hi = dta.astype(BF16)
        rem = dta - hi.astype(F32)
        mid = rem.astype(BF16)
        lo = (rem - mid.astype(F32)).astype(BF16)
        pre = jnp.dot(tri3, jnp.concatenate([hi, mid, lo], axis=0), preferred_element_type=F32)
        suf = pre[Q - 1:Q, :] - pre + dta
        a = jnp.where(fwd_lane, pre, suf)
        acol[c * Q:(c + 1) * Q, :] = a
        atr[c] = a.T[0:N_DIR * SSD_HEADS, :]
        for g in range(SSD_GROUPS):
            b0 = SSD_INNER + g * SSD_STATE
            c0 = SSD_INNER + (SSD_GROUPS + g) * SSD_STATE
            bm = xc[c * Q:(c + 1) * Q, b0:b0 + SSD_STATE]
            cbs[c, g] = lax.dot_general(xc[c * Q:(c + 1) * Q, c0:c0 + SSD_STATE].astype(BF16),
                                        bm.astype(BF16), _NT, preferred_element_type=F32)
            bmt[c, g] = bm.T.astype(BF16)

    sel_row = lax.broadcasted_iota(jnp.int32, (3 * LANES, SSD_INNER), 0) & (LANES - 1)
    sel_head = lax.broadcasted_iota(jnp.int32, (3 * LANES, SSD_INNER), 1) >> (SSD_HEAD_DIM.bit_length() - 1)
    sel = [(sel_row == sel_head + d * SSD_HEADS).astype(BF16) for d in range(N_DIR)]
    mask2 = [jnp.concatenate([m, m], axis=1) for m in (lower, upper)]

    for q in range(n_seq):
        for d in range(N_DIR):
            for p in range(N_PAIR):
                if has_h0:
                    hst[q, d, p] = h0_ref[q, d, p * PAIR:(p + 1) * PAIR, :].T
                else:
                    hst[q, d, p] = jnp.zeros((SSD_STATE, PAIR), F32)

    lo_lane = lax.broadcasted_iota(jnp.int32, (Q, PAIR), 1) < SSD_HEAD_DIM

    def chunk(c, d, q):
        r0 = pl.multiple_of(c * Q, Q)
        rows = pl.ds(r0, Q)
        a = acol[rows, :]
        a_t = atr[c]
        dtp = jnp.dot(_split3(dtsp[rows, :]), sel[d], preferred_element_type=F32)
        a_end = a[Q - 1:Q, :] if d == 0 else a[0:1, :]
        dec = jnp.exp(a_end)
        for g in range(SSD_GROUPS):
            c0 = SSD_INNER + (SSD_GROUPS + g) * SSD_STATE
            cmb = xc[rows, c0:c0 + SSD_STATE].astype(BF16)
            cb = cbs[c, g]
            cb2 = jnp.concatenate([cb, cb], axis=1)
            for pr in range(2):
                p = g * 2 + pr
                j0 = d * SSD_HEADS + 2 * p
                j1 = j0 + 1
                xs = xc[rows, p * PAIR:(p + 1) * PAIR]
                ab0 = jnp.broadcast_to(a[:, j0:j0 + 1], (Q, Q))
                ab1 = jnp.broadcast_to(a[:, j1:j1 + 1], (Q, Q))
                seg = (jnp.concatenate([ab0, ab1], axis=1)
                       - jnp.concatenate([a_t[j0:j0 + 1, :], a_t[j1:j1 + 1, :]], axis=1))
                m = (cb2 * jnp.exp(jnp.where(mask2[d], seg, -jnp.inf))).astype(BF16)
                xdt = xs * dtp[:, p * PAIR:(p + 1) * PAIR]
                rhs = jnp.concatenate([jnp.where(lo_lane, xdt, 0.0).astype(BF16),
                                       jnp.where(lo_lane, 0.0, xdt).astype(BF16)], axis=0)
                y_diag = jnp.dot(m, rhs, preferred_element_type=F32)
                hp = hst[q, d, p]
                y_off = jnp.dot(cmb, hp.astype(BF16), preferred_element_type=F32)
                a_pair = jnp.where(lo_lane, ab0, ab1)
                y = y_diag + jnp.exp(a_pair) * y_off
                a_end_pair = jnp.where(lo_lane[0:1, :], a_end[:, j0:j0 + 1], a_end[:, j1:j1 + 1])
                xw = (xdt * jnp.exp(a_end_pair - a_pair)).astype(BF16)
                s_new = jnp.dot(bmt[c, g], xw, preferred_element_type=F32)
                decp = jnp.where(lo_lane[0:1, :], dec[:, j0:j0 + 1], dec[:, j1:j1 + 1])
                hst[q, d, p] = decp * hp + s_new
                if d == 0:
                    y = y + dvec_ref[:, p * PAIR:(p + 1) * PAIR] * xs
                yacc[d, rows, p * PAIR:(p + 1) * PAIR] = y

    def step(t, carry):
        for q in range(n_seq):
            chunk(q * nc + t, 0, q)
            chunk(q * nc + nc - 1 - t, 1, q)
        return carry

    lax.fori_loop(0, nc, step, 0, unroll=2 if n_seq == 1 else 1)
    for c in range(n_seq * nc):
        rows = slice(c * Q, (c + 1) * Q)
        y = (yacc[0, rows, :] + yacc[1, rows, :]) * _silu(z_ref[rows, :])
        y_ref[rows, :] = _rms(y) * ng_ref[...]
    if emit_state:
        for q in range(n_seq):
            for d in range(N_DIR):
                for p in range(N_PAIR):
                    hout_ref[q, d, p * PAIR:(p + 1) * PAIR, :] = hst[q, d, p].T


def _ssd_mixer(z, xbc, dtr, prm, l, *, L, n_seq, nb, blk0, h0=None, y_buf=None, st_buf=None, after=None,
               emit_state):
    R = n_seq * L
    seq = lambda w: pl.BlockSpec((R, w), lambda b: (blk0 + b, 0))
    st = pl.BlockSpec((n_seq, None, N_DIR, SSD_HP, SSD_STATE), lambda b: (b, l, 0, 0, 0))
    args = [z, xbc, dtr, prm["conv_w"], prm["conv_b"], prm["dt_bias"], prm["a_log"], prm["d_vec"],
            prm["norm_g"]]
    in_specs = [seq(SSD_INNER), seq(SSD_CONV_DIM), seq(LANES),
                _layer_spec(l, (SSD_CONV, SSD_CONV_DIM)), _layer_spec(l, (1, SSD_CONV_DIM)),
                _layer_spec(l, (1, LANES)), _layer_spec(l, (1, LANES)),
                _layer_spec(l, (1, SSD_INNER)), _layer_spec(l, (1, SSD_INNER))]
    if h0 is not None:
        args.append(h0)
        in_specs.append(st)
    out_shape = [jax.ShapeDtypeStruct((N_TOK, SSD_INNER), F32)]
    out_specs = [seq(SSD_INNER)]
    aliases = {}
    if y_buf is not None:
        aliases[len(args)] = 0
        args.append(y_buf)
        in_specs.append(_any_spec())
    if emit_state:
        out_shape.append(jax.ShapeDtypeStruct((BATCH, DEPTH, N_DIR, SSD_HP, SSD_STATE), F32))
        out_specs.append(st)
        if st_buf is not None:
            aliases[len(args)] = 1
            args.append(st_buf)
            in_specs.append(_any_spec())
    if after is not None:
        args.append(after)
        in_specs.append(_any_spec())
    res = pl.pallas_call(
        functools.partial(_ssd_kernel, L=L, n_seq=n_seq, has_h0=h0 is not None,
                          n_alias=len(aliases) + int(after is not None), emit_state=emit_state),
        grid=(nb,),
        in_specs=in_specs,
        out_specs=out_specs,
        out_shape=out_shape,
        input_output_aliases=aliases,
        scratch_shapes=[pltpu.VMEM((n_seq * (L + SUBLANES) + SUBLANES, SSD_CONV_DIM), F32),
                        pltpu.VMEM((R, SSD_CONV_DIM), F32),
                        pltpu.VMEM((R, LANES), F32),
                        pltpu.VMEM((R // CHUNK, N_DIR * SSD_HEADS, CHUNK), F32),
                        pltpu.VMEM((R // CHUNK, SSD_GROUPS, CHUNK, CHUNK), F32),
                        pltpu.VMEM((R // CHUNK, SSD_GROUPS, SSD_STATE, CHUNK), BF16),
                        pltpu.VMEM((R, LANES), F32),
                        pltpu.VMEM((N_DIR, R, SSD_INNER), F32),
                        pltpu.VMEM((n_seq, N_DIR, N_PAIR, PAIR, SSD_STATE), F32)],
        compiler_params=_params(),
        name=f"ssd_mixer_L{L}",
    )(*args)
    return res if emit_state else (res[0], None)


_LOG2_CH = S5_GROUP_CH.bit_length() - 1
_LOG2_ST = S5_STATE.bit_length() - 1


def _cmul(ar, ai, br, bi):
    return ar * br - ai * bi, ar * bi + ai * br


def _s5prep_kernel(lre_ref, lim_ref, ldt_ref, btr_ref, bti_ref, ctr_ref, cti_ref,
                   bblk_ref, cblk_ref, pw_ref):
    brow = lax.broadcasted_iota(jnp.int32, (S5_DIM, S5_LANES), 0) >> _LOG2_CH
    bcol = lax.broadcasted_iota(jnp.int32, (S5_DIM, S5_LANES), 1) >> _LOG2_ST
    bmask = brow == bcol
    p1 = lax.broadcasted_iota(jnp.int32, (S5_SEG, S5_LANES), 0) + 1
    for d in range(N_DIR):
        lre = lre_ref[d]
        lim = lim_ref[d]
        step = jnp.exp(ldt_ref[d])
        mag = jnp.exp(lre * step)
        lbr = mag * jnp.cos(lim * step)
        lbi = mag * jnp.sin(lim * step)
        den = lre * lre + lim * lim
        nr = lbr - 1.0
        cr = (nr * lre + lbi * lim) / den
        ci = (lbi * lre - nr * lim) / den
        br, bi = _cmul(cr, ci, btr_ref[...], bti_ref[...])
        br = jnp.where(bmask, br, 0.0).astype(BF16)
        bi = jnp.where(bmask, bi, 0.0).astype(BF16)
        for sl in range(S5_NSLAB):
            bblk_ref[d, sl, :, 0:S5_SLAB] = br[:, sl * S5_SLAB:(sl + 1) * S5_SLAB]
            bblk_ref[d, sl, :, S5_SLAB:2 * S5_SLAB] = bi[:, sl * S5_SLAB:(sl + 1) * S5_SLAB]
        rr = jnp.ones((S5_SEG, S5_LANES), F32)
        ri = jnp.zeros((S5_SEG, S5_LANES), F32)
        sr, si = lbr, lbi
        for k in range(S5_SEG.bit_length()):
            bit = ((p1 >> k) & 1) == 1
            tr, ti = _cmul(rr, ri, sr, si)
            rr = jnp.where(bit, tr, rr)
            ri = jnp.where(bit, ti, ri)
            sr, si = _cmul(sr, si, sr, si)
        pw_ref[d, 0, 0:S5_SEG, :] = rr
        pw_ref[d, 1, 0:S5_SEG, :] = ri
        n = S5_SEG
        while n < S5_PW_ROWS:
            tr, ti = _cmul(pw_ref[d, 0, 0:n, :], pw_ref[d, 1, 0:n, :],
                           pw_ref[d, 0, n - 1:n, :], pw_ref[d, 1, n - 1:n, :])
            pw_ref[d, 0, n:2 * n, :] = tr
            pw_ref[d, 1, n:2 * n, :] = ti
            n *= 2
    crow = lax.broadcasted_iota(jnp.int32, (S5_LANES, S5_DIM), 0) >> _LOG2_ST
    ccol = lax.broadcasted_iota(jnp.int32, (S5_LANES, S5_DIM), 1) >> _LOG2_CH
    cmask = crow == ccol
    cr = jnp.where(cmask, ctr_ref[...], 0.0).astype(BF16)
    ci = jnp.where(cmask, -cti_ref[...], 0.0).astype(BF16)
    for sl in range(S5_NSLAB):
        cblk_ref[sl, 0:S5_SLAB, :] = cr[sl * S5_SLAB:(sl + 1) * S5_SLAB, :]
        cblk_ref[sl, S5_SLAB:2 * S5_SLAB, :] = ci[sl * S5_SLAB:(sl + 1) * S5_SLAB, :]


def _s5_prep(lam_re, lam_im, log_dt, b_re, b_im, c_re, c_im):
    row = lambda t: t.reshape(DEPTH, N_DIR, 1, S5_LANES)
    ldt = jnp.repeat(log_dt, S5_STATE, axis=-1)
    bt = lambda t: jnp.tile(t.transpose(0, 3, 1, 2).reshape(DEPTH, S5_GROUP_CH, S5_LANES),
                            (1, S5_GROUPS, 1))
    ct = lambda t: jnp.tile(t.transpose(0, 1, 3, 2).reshape(DEPTH, S5_LANES, S5_GROUP_CH),
                            (1, 1, S5_GROUPS))
    vec = pl.BlockSpec((None, N_DIR, 1, S5_LANES), lambda l: (l, 0, 0, 0))
    bsp = pl.BlockSpec((None, S5_DIM, S5_LANES), lambda l: (l, 0, 0))
    csp = pl.BlockSpec((None, S5_LANES, S5_DIM), lambda l: (l, 0, 0))
    return pl.pallas_call(
        _s5prep_kernel,
        grid=(DEPTH,),
        in_specs=[vec, vec, vec, bsp, bsp, csp, csp],
        out_specs=[pl.BlockSpec((None, N_DIR, S5_NSLAB, S5_DIM, 2 * S5_SLAB), lambda l: (l, 0, 0, 0, 0)),
                   pl.BlockSpec((None, S5_NSLAB, 2 * S5_SLAB, S5_DIM), lambda l: (l, 0, 0, 0)),
                   pl.BlockSpec((None, N_DIR, 2, S5_PW_ROWS, S5_LANES), lambda l: (l, 0, 0, 0, 0))],
        out_shape=[jax.ShapeDtypeStruct((DEPTH, N_DIR, S5_NSLAB, S5_DIM, 2 * S5_SLAB), BF16),
                   jax.ShapeDtypeStruct((DEPTH, S5_NSLAB, 2 * S5_SLAB, S5_DIM), BF16),
                   jax.ShapeDtypeStruct((DEPTH, N_DIR, 2, S5_PW_ROWS, S5_LANES), F32)],
        compiler_params=_params(),
        name="s5_prep",
    )(row(lam_re), row(lam_im), row(ldt), bt(b_re), bt(b_im), ct(c_re), ct(c_im))


def _s5_moves(n_seq, n_seg, seq_len, col_major):
    steps = seq_len // n_seg
    if not col_major:
        return [(q * seq_len + s * steps, steps, q * n_seg + s, S5_SEG)
                for q in range(n_seq) for s in range(n_seg)]
    assert n_seq == 1 and n_seg == S5_SEG
    grid_rows = seq_len // GRID_W
    wseg = GRID_W // n_seg
    return [(r * GRID_W + s * wseg, wseg, r * S5_SEG + s, grid_rows * S5_SEG)
            for r in range(grid_rows) for s in range(n_seg)]


def _s5_kernel(*refs, n_seq, n_seg, seq_len, col_major, has_h0, n_alias, emit_state):
    u_ref, bblk_ref, cblk_ref, pw_ref, dvec_ref, gw_ref, gb_ref = refs[:7]
    k = 7
    h0r_ref, h0i_ref = (refs[k], refs[k + 1]) if has_h0 else (None, None)
    k += 2 * int(has_h0) + n_alias
    y_ref = refs[k]
    fr_ref, fi_ref = (refs[k + 1], refs[k + 2]) if emit_state else (None, None)
    up, buf_f, buf_b, yacc, yp, cin, fin = refs[k + 1 + 2 * int(emit_state):]
    assert n_seq * n_seg == S5_SEG and not (has_h0 and n_seg == 1) and not (emit_state and n_seg > 1)
    n_rows = n_seq * seq_len
    steps = seq_len // n_seg
    n_planes = S5_DIM // LANES
    W = S5_SLAB
    moves = _s5_moves(n_seq, n_seg, seq_len, col_major)

    for src, n, dst, stride in moves:
        for t in range(n_planes):
            up[t, pl.ds(dst, n, stride=stride), :] = u_ref[src:src + n, t * LANES:(t + 1) * LANES]

    def u_rows(rs):
        return jnp.concatenate([up[t, rs, :] for t in range(n_planes)], axis=1)

    zero = jnp.zeros((S5_SEG, W), F32)
    zrow = jnp.zeros((1, W), F32)
    for sl in range(S5_NSLAB):
        ln = slice(sl * W, (sl + 1) * W)
        for r0 in range(0, n_rows, ROW_SLAB):
            rs = slice(r0, r0 + ROW_SLAB)
            ub = u_rows(rs).astype(BF16)
            buf_f[rs, :] = jnp.dot(ub, bblk_ref[0, sl], preferred_element_type=F32)
            buf_b[rs, :] = jnp.dot(ub, bblk_ref[1, sl], preferred_element_type=F32)
        lam = [[jnp.broadcast_to(pw_ref[d, c, 0:1, ln], (S5_SEG, W)) for c in range(2)]
               for d in range(N_DIR)]

        def body(t, carry):
            fr, fi, br, bi = carry
            rf = pl.ds(pl.multiple_of(t * S5_SEG, S5_SEG), S5_SEG)
            rb = pl.ds(pl.multiple_of((steps - 1 - t) * S5_SEG, S5_SEG), S5_SEG)
            nfr = lam[0][0] * fr - lam[0][1] * fi + buf_f[rf, 0:W]
            nfi = lam[0][0] * fi + lam[0][1] * fr + buf_f[rf, W:2 * W]
            nbr = lam[1][0] * br - lam[1][1] * bi + buf_b[rb, 0:W]
            nbi = lam[1][0] * bi + lam[1][1] * br + buf_b[rb, W:2 * W]
            buf_f[rf, 0:W] = nfr
            buf_f[rf, W:2 * W] = nfi
            buf_b[rb, 0:W] = nbr
            buf_b[rb, W:2 * W] = nbi
            return nfr, nfi, nbr, nbi

        ends = lax.fori_loop(0, steps, body, (zero, zero, zero, zero), unroll=2)

        if n_seg == 1:
            if emit_state:
                for d in range(N_DIR):
                    fin[d, 0, :, ln] = ends[2 * d]
                    fin[d, 1, :, ln] = ends[2 * d + 1]
        else:
            for d in range(N_DIR):
                er, ei = ends[2 * d], ends[2 * d + 1]
                lpr = pw_ref[d, 0, steps - 1:steps, ln]
                lpi = pw_ref[d, 1, steps - 1:steps, ln]
                pr = h0r_ref[d:d + 1, ln] if has_h0 else zrow
                pi = h0i_ref[d:d + 1, ln] if has_h0 else zrow
                order = range(n_seg) if d == 0 else range(n_seg - 1, -1, -1)
                for s in order:
                    cin[d, 0, s:s + 1, :] = pr
                    cin[d, 1, s:s + 1, :] = pi
                    tr, ti = _cmul(lpr, lpi, pr, pi)
                    pr = er[s:s + 1, :] + tr
                    pi = ei[s:s + 1, :] + ti
            cfr = cin[0, 0]
            cfi = cin[0, 1]
            cbr = cin[1, 0]
            cbi = cin[1, 1]

            def fix(p, carry):
                rows = pl.ds(pl.multiple_of(p * S5_SEG, S5_SEG), S5_SEG)
                q = steps - 1 - p
                tfr, tfi = _cmul(pw_ref[0, 0, pl.ds(p, 1), ln], pw_ref[0, 1, pl.ds(p, 1), ln], cfr, cfi)
                tbr, tbi = _cmul(pw_ref[1, 0, pl.ds(q, 1), ln], pw_ref[1, 1, pl.ds(q, 1), ln], cbr, cbi)
                buf_f[rows, 0:W] = buf_f[rows, 0:W] + buf_b[rows, 0:W] + tfr + tbr
                buf_f[rows, W:2 * W] = buf_f[rows, W:2 * W] + buf_b[rows, W:2 * W] + tfi + tbi
                return carry

            lax.fori_loop(0, steps, fix, 0, unroll=2)

        for r0 in range(0, n_rows, ROW_SLAB):
            rs = slice(r0, r0 + ROW_SLAB)
            hs = buf_f[rs, :] if n_seg > 1 else buf_f[rs, :] + buf_b[rs, :]
            part = jnp.dot(hs.astype(BF16), cblk_ref[sl], preferred_element_type=F32)
            if sl == 0:
                yacc[rs, :] = part
            else:
                yacc[rs, :] = yacc[rs, :] + part

    gwb = gw_ref[...].astype(BF16)
    for r0 in range(0, n_rows, ROW_SLAB):
        rs = slice(r0, r0 + ROW_SLAB)
        y = _gelu_tanh(yacc[rs, :] + dvec_ref[...] * u_rows(rs))
        gate = jnp.dot(y.astype(BF16), gwb, preferred_element_type=F32) + gb_ref[...]
        y = y * jax.nn.sigmoid(gate)
        for t in range(n_planes):
            yp[t, rs, :] = y[:, t * LANES:(t + 1) * LANES]
    for src, n, dst, stride in moves:
        for t in range(n_planes):
            y_ref[src:src + n, t * LANES:(t + 1) * LANES] = yp[t, pl.ds(dst, n, stride=stride), :]
    if emit_state:
        for q in range(n_seq):
            for d in range(N_DIR):
                fr_ref[q, d:d + 1, :] = fin[d, 0, q:q + 1, :]
                fi_ref[q, d:d + 1, :] = fin[d, 1, q:q + 1, :]


def _s5_mixer(u, bblk, cblk, pw, prm, l, *, seq_len, n_seq, nb, blk0, col_major, h0=None, y_buf=None,
              st_buf=None, emit_state):
    n_rows = n_seq * seq_len
    n_seg = S5_SEG // n_seq
    seq = pl.BlockSpec((n_rows, S5_DIM), lambda b: (blk0 + b, 0))
    st = pl.BlockSpec((n_seq, None, N_DIR, S5_LANES), lambda b: (b, l, 0, 0))
    once = dict(pipeline_mode=pl.Buffered(1))
    args = [u, bblk, cblk, pw, prm["d_vec"], prm["glu_w"], prm["glu_b"]]
    in_specs = [seq, _layer_spec(l, (N_DIR, S5_NSLAB, S5_DIM, 2 * S5_SLAB), **once),
                _layer_spec(l, (S5_NSLAB, 2 * S5_SLAB, S5_DIM), **once),
                _layer_spec(l, (N_DIR, 2, S5_PW_ROWS, S5_LANES), **once), _layer_spec(l, (1, S5_DIM)),
                _layer_spec(l, (S5_DIM, S5_DIM)), _layer_spec(l, (1, S5_DIM))]
    if h0 is not None:
        h0_spec = pl.BlockSpec((None, None, N_DIR, S5_LANES), lambda b: (b, l, 0, 0))
        args += list(h0)
        in_specs += [h0_spec, h0_spec]
    out_shape = [jax.ShapeDtypeStruct((N_TOK, S5_DIM), F32)]
    out_specs = [seq]
    aliases = {}
    if y_buf is not None:
        aliases[len(args)] = 0
        args.append(y_buf)
        in_specs.append(_any_spec())
    if emit_state:
        out_shape += [jax.ShapeDtypeStruct((BATCH, DEPTH, N_DIR, S5_LANES), F32)] * 2
        out_specs += [st, st]
        if st_buf is not None:
            for j, buf in enumerate(st_buf):
                aliases[len(args)] = 1 + j
                args.append(buf)
                in_specs.append(_any_spec())
    res = pl.pallas_call(
        functools.partial(_s5_kernel, n_seq=n_seq, n_seg=n_seg, seq_len=seq_len, col_major=col_major,
                          has_h0=h0 is not None, n_alias=len(aliases), emit_state=emit_state),
        grid=(nb,),
        in_specs=in_specs,
        out_specs=out_specs,
        out_shape=out_shape,
        input_output_aliases=aliases,
        scratch_shapes=[pltpu.VMEM((S5_DIM // LANES, n_rows, LANES), F32),
                        pltpu.VMEM((n_rows, 2 * S5_SLAB), F32),
                        pltpu.VMEM((n_rows, 2 * S5_SLAB), F32),
                        pltpu.VMEM((n_rows, S5_DIM), F32),
                        pltpu.VMEM((S5_DIM // LANES, n_rows, LANES), F32),
                        pltpu.VMEM((N_DIR, 2, S5_SEG, S5_SLAB), F32),
                        pltpu.VMEM((N_DIR, 2, S5_SEG, S5_LANES), F32)],
        compiler_params=_params(),
        name=f"s5_mixer_L{seq_len}",
    )(*args)
    return (res[0], res[1], res[2]) if emit_state else (res[0], None, None)


def _outffn_kernel(xa_ref, xb_ref, ys_ref, yg_ref, y5_ref, g1_ref, sh_ref, sc_ref, g2_ref, ng_ref,
                   fg_ref, wo_ref, w1_ref, w2_ref, *rest, final):
    n_out = 2 if final else 1
    o_refs = rest[:n_out]
    wob, w1b, w2b = rest[n_out:]
    i = pl.program_id(0)

    @pl.when(i < FFN_PREP)
    def _():
        wob[i] = wo_ref[...].astype(BF16)
        w1b[i] = w1_ref[...].astype(BF16)
        w2b[i] = w2_ref[...].astype(BF16)

    @pl.when(i >= FFN_PREP)
    def _():
        t = i - FFN_PREP
        n_ctx = N_CTX // FFN_TILE
        r = _mod_row(t, FFN_TILE)
        row = lambda ref: ref[pl.ds(r, 1), :]
        x = jnp.where(t < n_ctx, xa_ref[...], xb_ref[...])
        mixed = jnp.concatenate([ys_ref[...].astype(BF16), yg_ref[...].astype(BF16),
                                 y5_ref[...].astype(BF16)], axis=1)
        w_out = wob[...].reshape(D_MODEL, D_MODEL)
        x1 = x + row(g1_ref) * jnp.dot(mixed, w_out, preferred_element_type=F32)
        h = ((_rms(x1) * ng_ref[...]) * (1.0 + row(sc_ref)) + row(sh_ref)).astype(BF16)
        acc = jnp.zeros((FFN_TILE, D_MODEL), F32)
        for j in range(FFN_PREP):
            a = jnp.maximum(jnp.dot(h, w1b[j], preferred_element_type=F32), 0.0)
            acc = acc + jnp.dot((a * a).astype(BF16), w2b[j], preferred_element_type=F32)
        x2 = x1 + row(g2_ref) * acc
        if not final:
            o_refs[0][...] = x2
        else:
            y = _rms(x2) * fg_ref[...]

            @pl.when(t < n_ctx)
            def _():
                o_refs[0][...] = y

            @pl.when(t >= n_ctx)
            def _():
                o_refs[1][...] = y


def _out_ffn(xa, xb, b_is_stream, y_ssd, y_sgu, y_s5, l, mods, norm_g, final_g, w_out, w1, w2, *, final):
    tile_of = lambda i: jnp.maximum(i - FFN_PREP, 0)
    tok = lambda w: pl.BlockSpec((FFN_TILE, w), lambda i: (tile_of(i), 0))
    chunk_of = lambda i: jnp.minimum(i, FFN_PREP - 1)
    sa, sb = _split_tok_specs(FFN_TILE, D_MODEL, b_is_stream, tile_of)
    if final:
        out_specs = list(_split_tok_specs(FFN_TILE, D_MODEL, False, tile_of))
        out_shape = [jax.ShapeDtypeStruct((N_CTX, D_MODEL), F32), jax.ShapeDtypeStruct((N_LAT, D_MODEL), F32)]
    else:
        out_specs = [tok(D_MODEL)]
        out_shape = [jax.ShapeDtypeStruct((N_TOK, D_MODEL), F32)]
    ko, kf = D_MODEL // FFN_PREP, D_FF // FFN_PREP
    return pl.pallas_call(
        functools.partial(_outffn_kernel, final=final),
        grid=(FFN_PREP + N_TOK // FFN_TILE,),
        in_specs=[sa, sb, tok(SSD_INNER), tok(SGU_DIM), tok(S5_DIM),
                  _mod_spec(l, 2), _mod_spec(l, 3), _mod_spec(l, 4), _mod_spec(l, 5),
                  _layer_spec(l, (1, D_MODEL)), pl.BlockSpec((1, D_MODEL), lambda i: (0, 0)),
                  pl.BlockSpec((None, ko, D_MODEL), lambda i: (l, chunk_of(i), 0)),
                  pl.BlockSpec((None, D_MODEL, kf), lambda i: (l, 0, chunk_of(i))),
                  pl.BlockSpec((None, kf, D_MODEL), lambda i: (l, chunk_of(i), 0))],
        out_specs=out_specs,
        out_shape=out_shape,
        scratch_shapes=[pltpu.VMEM((FFN_PREP, ko, D_MODEL), BF16),
                        pltpu.VMEM((FFN_PREP, D_MODEL, kf), BF16),
                        pltpu.VMEM((FFN_PREP, kf, D_MODEL), BF16)],
        compiler_params=_params(),
        name="out_ffn_final" if final else "out_ffn",
    )(xa, xb, y_ssd, y_sgu, y_s5, mods, mods, mods, mods, norm_g, final_g, w_out, w1, w2)


def kernel(x_prompt, x_sample, state_ssd, state_s5_re, state_s5_im, c, c_ctx, ada_w, ada_b, norm1_g,
           norm2_g, w_in, ssd_conv_w, ssd_conv_b, ssd_dt_bias, ssd_a_log, ssd_d, ssd_norm_g,
           sgu_norm_g, sgu_w, sgu_b, s5_lambda_re, s5_lambda_im, s5_log_dt, s5_b_re, s5_b_im,
           s5_c_re, s5_c_im, s5_d, s5_glu_w, s5_glu_b, w_out, ffn_w1, ffn_w2, final_norm_g):
    cvec = jnp.concatenate([c_ctx[None, :], c,
                            jnp.zeros((MOD_ROWS - 1 - DEC_BATCH, D_MODEL), F32)], axis=0)
    mods = _adaln_mods(cvec, ada_w, ada_b)
    bblk, cblk, pw = _s5_prep(s5_lambda_re, s5_lambda_im, s5_log_dt, s5_b_re, s5_b_im,
                              s5_c_re, s5_c_im)

    w_in_t = w_in.transpose(0, 2, 1)
    vec =lambda t: t.reshape(DEPTH, 1, -1)
    lane_pad = lambda t: jnp.pad(vec(t), ((0, 0), (0, 0), (0, LANES - t[0].size)))
    ssd_prm = dict(conv_w=ssd_conv_w, conv_b=vec(ssd_conv_b), dt_bias=lane_pad(ssd_dt_bias),
                   a_log=lane_pad(ssd_a_log), d_vec=vec(jnp.repeat(ssd_d, SSD_HEAD_DIM, axis=-1)),
                   norm_g=vec(ssd_norm_g))
    s5_prm = dict(d_vec=vec(s5_d), glu_w=s5_glu_w, glu_b=vec(s5_glu_b))
    sgu_w_pair = sgu_w.reshape(DEPTH, SGU_HEADS // 2, 2, CHUNK, CHUNK).transpose(0, 1, 3, 2, 4)
    sgu_w_pair = sgu_w_pair.reshape(DEPTH, SGU_HEADS // 2, CHUNK, 2 * CHUNK).astype(BF16)
    sgu_b_full = jnp.repeat(sgu_b.transpose(0, 2, 1), SGU_DIM // SGU_HEADS, axis=2)
    norm1 = vec(norm1_g)
    norm2 = vec(norm2_g)
    sgu_g = vec(sgu_norm_g)
    final_g = final_norm_g.reshape(1, D_MODEL)

    lat_ssd = state_ssd.reshape(DEC_BATCH, DEPTH, N_DIR, SSD_HP, SSD_STATE)
    lat_s5 = (state_s5_re.reshape(DEC_BATCH, DEPTH, N_DIR, S5_LANES),
              state_s5_im.reshape(DEC_BATCH, DEPTH, N_DIR, S5_LANES))
    lat_blk = N_CTX // DEC_SEQ

    xa = x_prompt.reshape(N_CTX, D_MODEL)
    xb = x_sample.reshape(N_LAT, D_MODEL)
    b_is_stream = False
    st_ssd = None
    st_s5 = None
    for l in range(DEPTH):
        z, xbc, y_sgu, s5u, dtr = _in_proj(xa, xb, b_is_stream, l, mods, norm1, w_in_t, sgu_g,
                                           sgu_w_pair, sgu_b_full)
        y_s5, fr, fi = _s5_mixer(s5u, bblk, cblk, pw, s5_prm, l, seq_len=SEQ, n_seq=S5_SEG,
                                 nb=BATCH // S5_SEG, blk0=0, col_major=False, st_buf=st_s5,
                                 emit_state=True)
        st_s5 = (fr, fi)
        y_s5, _, _ = _s5_mixer(s5u, bblk, cblk, pw, s5_prm, l, seq_len=DEC_SEQ, n_seq=1, nb=DEC_BATCH,
                               blk0=lat_blk, col_major=True, h0=lat_s5, y_buf=y_s5, emit_state=False)
        y_ssd, st_ssd = _ssd_mixer(z, xbc, dtr, ssd_prm, l, L=SEQ, n_seq=SSD_CTX_SEQS,
                                   nb=BATCH // SSD_CTX_SEQS, blk0=0, st_buf=st_ssd, after=y_s5,
                                   emit_state=True)
        y_ssd, _ = _ssd_mixer(z, xbc, dtr, ssd_prm, l, L=DEC_SEQ, n_seq=1, nb=DEC_BATCH, blk0=lat_blk,
                              h0=lat_ssd, y_buf=y_ssd, emit_state=False)
        final = l == DEPTH - 1
        out = _out_ffn(xa, xb, b_is_stream, y_ssd, y_sgu, y_s5, l, mods, norm2, final_g, w_out, ffn_w1,
                       ffn_w2, final=final)
        if not final:
            xa = xb = out[0]
            b_is_stream = True

    y_prompt = out[0].reshape(BATCH, SEQ, D_MODEL)
    y_sample = out[1].reshape(DEC_BATCH, DEC_SEQ, D_MODEL)
    new_state_ssd = st_ssd.reshape(BATCH, DEPTH, N_DIR, SSD_HEADS, SSD_HEAD_DIM, SSD_STATE)
    s5_shape = (BATCH, DEPTH, N_DIR, S5_GROUPS, S5_STATE)
    return (y_prompt, y_sample, new_state_ssd, st_s5[0].reshape(s5_shape), st_s5[1].reshape(s5_shape))
```

```python
import functools
import math

import jax
import jax.numpy as jnp
from jax import lax
from jax.experimental import pallas as pl
from jax.experimental.pallas import tpu as pltpu

F32 = jnp.float32
BF16 = jnp.bfloat16

D_MODEL = 1024
BATCH = 16
SEQ = 256
DEPTH = 2
DEC_BATCH = 2
DEC_SEQ = 1024
GRID_W = 64
CHUNK = 128
N_DIR = 2
EPS = 1e-6
SSD_INNER = 512
SSD_HEAD_DIM = 64
SSD_HEADS = 8
SSD_GROUPS = 2
SSD_STATE = 128
SSD_CONV = 5
SSD_CONV_DIM = SSD_INNER + 2 * SSD_GROUPS * SSD_STATE
SGU_DIM = 256
SGU_HEADS = 4
S5_DIM = 256
S5_GROUP_CH = 16
S5_GROUPS = 16
S5_STATE = 64
D_FF = 4 * D_MODEL
OFF_XBC = SSD_INNER
OFF_DT = OFF_XBC + SSD_CONV_DIM
OFF_SGU = OFF_DT + N_DIR * SSD_HEADS
OFF_S5 = OFF_SGU + 2 * SGU_DIM
IN_DIM = OFF_S5 + S5_DIM

N_CTX = BATCH * SEQ
N_LAT = DEC_BATCH * DEC_SEQ
N_TOK = N_CTX + N_LAT
LANES = 128
SUBLANES = 8
MOD_ROWS = SUBLANES
SSD_HP = SSD_HEADS * SSD_HEAD_DIM
S5_LANES = S5_GROUPS * S5_STATE
S5_SEG = SUBLANES
S5_PW_ROWS = DEC_SEQ // S5_SEG
S5_SLAB = 256
S5_NSLAB = S5_LANES // S5_SLAB
S5_UNROLL = 8
PAIR = 2 * SSD_HEAD_DIM
N_PAIR = SSD_HEADS // 2
TOK_TILE = 512
N_CTX_TILES = N_CTX // TOK_TILE
FFN_TILE = 512
FFN_PREP = 8
ROW_SLAB = 256
CONV_LANES = 256
SSD_CTX_SEQS = 2
VMEM_LIMIT = 56 * 1024 * 1024

_NT = (((1,), (1,)), ((), ()))


def _params(n_axes=1):
    return pltpu.CompilerParams(dimension_semantics=("arbitrary",) * n_axes,
                                vmem_limit_bytes=VMEM_LIMIT)


def _layer_spec(l, shape, **kw):
    return pl.BlockSpec((None,) + tuple(shape), lambda *_: (l,) + (0,) * len(shape), **kw)


def _mod_spec(l, k):
    return pl.BlockSpec((None, MOD_ROWS, D_MODEL), lambda *_: (l, 0, k))


def _any_spec():
    return pl.BlockSpec(memory_space=pl.ANY)


def _mod_row(i, tm):
    n_ctx = N_CTX // tm
    return jnp.where(i < n_ctx, 0, 1 + (i - n_ctx) // (DEC_SEQ // tm))


def _split_tok_specs(tm, width, b_has_ctx_rows, tile_of=lambda i: i):
    n_ctx = N_CTX // tm
    b_off = n_ctx if b_has_ctx_rows else 0
    a = pl.BlockSpec((tm, width), lambda i: (jnp.minimum(tile_of(i), n_ctx - 1), 0))
    b = pl.BlockSpec((tm, width), lambda i: (jnp.maximum(tile_of(i), n_ctx) - n_ctx + b_off, 0))
    return a, b


def _silu(x):
    return x * jax.nn.sigmoid(x)


def _gelu_tanh(x):
    c = math.sqrt(2.0 / math.pi)
    return 0.5 * x * (1.0 + jnp.tanh(c * (x + 0.044715 * (x * x * x))))


def _rms(x):
    return x * lax.rsqrt(jnp.mean(x * x, axis=-1, keepdims=True) + EPS)


def _bdot(a, b):
    return jnp.dot(a.astype(BF16), b.astype(BF16), preferred_element_type=F32)


def _split3(x):
    hi = x.astype(BF16)
    r = x - hi.astype(F32)
    mid = r.astype(BF16)
    lo = (r - mid.astype(F32)).astype(BF16)
    return jnp.concatenate([hi, mid, lo], axis=1)


def _mod_kernel(c_ref, w_ref, b_ref, o_ref):
    o_ref[...] = _bdot(_silu(c_ref[...]), w_ref[...]) + b_ref[...]


def _adaln_mods(cvec, ada_w, ada_b):
    n_blk = 6
    return pl.pallas_call(
        _mod_kernel,
        grid=(DEPTH, n_blk),
        in_specs=[pl.BlockSpec((MOD_ROWS, D_MODEL), lambda l, j: (0, 0)),
                  pl.BlockSpec((None, D_MODEL, D_MODEL), lambda l, j: (l, 0, j)),
                  pl.BlockSpec((None, 1, D_MODEL), lambda l, j: (l, 0, j))],
        out_specs=pl.BlockSpec((None, MOD_ROWS, D_MODEL), lambda l, j: (l, 0, j)),
        out_shape=jax.ShapeDtypeStruct((DEPTH, MOD_ROWS, 6 * D_MODEL), F32),
        compiler_params=_params(2),
        name="adaln_mod",
    )(cvec, ada_w, ada_b.reshape(DEPTH, 1, 6 * D_MODEL))


_C_Z = 0
_C_XBC = _C_Z + SSD_INNER
_C_SGU = _C_XBC + SSD_CONV_DIM
_C_S5 = _C_SGU + 2 * SGU_DIM
_C_DT = _C_S5 + S5_DIM
_C_END = _C_DT + LANES


def _inproj_kernel(xa_ref, xb_ref, sh_ref, sc_ref, g_ref, win_ref, sg_ref, sw_ref, sb_ref,
                   z_ref, xbc_ref, ysgu_ref, s5_ref, dt_ref, w_ref):
    i = pl.program_id(0)

    @pl.when(i == 0)
    def _():
        moves = ((0, OFF_DT, _C_Z), (OFF_SGU, IN_DIM, _C_SGU), (OFF_DT, OFF_SGU, _C_DT))
        for src0, src1, dst0 in moves:
            for r0 in range(src0, src1, ROW_SLAB):
                n = min(ROW_SLAB, src1 - r0)
                w_ref[dst0 + r0 - src0:dst0 + r0 - src0 + n, :] = win_ref[r0:r0 + n, :].astype(BF16)
        n_pad = LANES - N_DIR * SSD_HEADS
        w_ref[_C_END - n_pad:_C_END, :] = jnp.zeros((n_pad, D_MODEL), BF16)

    r = _mod_row(i, TOK_TILE)
    x = jnp.where(i < N_CTX_TILES, xa_ref[...], xb_ref[...])
    shift = sh_ref[pl.ds(r, 1), :]
    scale = sc_ref[pl.ds(r, 1), :]
    h = (_rms(x) * g_ref[...]) * (1.0 + scale) + shift
    hb = h.astype(BF16)
    proj = lambda c0, c1: lax.dot_general(hb, w_ref[c0:c1, :], _NT, preferred_element_type=F32)
    uv = _gelu_tanh(proj(_C_SGU, _C_S5))
    z_ref[...] = proj(_C_Z, _C_XBC)
    xbc_ref[...] = proj(_C_XBC, _C_SGU)
    s5_ref[...] = proj(_C_S5, _C_DT)
    dt_ref[...] = proj(_C_DT, _C_END)

    u = uv[:, :SGU_DIM]
    v = _rms(uv[:, SGU_DIM:]) * sg_ref[...]
    lo_lane = lax.broadcasted_iota(jnp.int32, (CHUNK, LANES), 1) < (LANES // 2)
    for c in range(TOK_TILE // CHUNK):
        rows = slice(c * CHUNK, (c + 1) * CHUNK)
        mix = []
        for pr in range(SGU_HEADS // 2):
            vp = v[rows, pr * LANES:(pr + 1) * LANES]
            rhs = jnp.concatenate([jnp.where(lo_lane, vp, 0.0).astype(BF16),
                                   jnp.where(lo_lane, 0.0, vp).astype(BF16)], axis=0)
            mix.append(jnp.dot(sw_ref[pr], rhs, preferred_element_type=F32))
        ysgu_ref[rows, :] = u[rows, :] * (jnp.concatenate(mix, axis=1) + sb_ref[...])


def _in_proj(xa, xb, b_is_stream, l, mods, norm_g, w_in, sgu_g, sgu_w_pair, sgu_b_full):
    tok = lambda w: pl.BlockSpec((TOK_TILE, w), lambda i: (i, 0))
    widths = (SSD_INNER, SSD_CONV_DIM, SGU_DIM, S5_DIM, LANES)
    sa, sb = _split_tok_specs(TOK_TILE, D_MODEL, b_is_stream)
    return pl.pallas_call(
        _inproj_kernel,
        grid=(N_TOK // TOK_TILE,),
        in_specs=[sa, sb, _mod_spec(l, 0), _mod_spec(l, 1), _layer_spec(l, (1, D_MODEL)),
                  _layer_spec(l, (IN_DIM, D_MODEL), pipeline_mode=pl.Buffered(1)),
                  _layer_spec(l, (1, SGU_DIM)), _layer_spec(l, (SGU_HEADS // 2, CHUNK, 2 * CHUNK)),
                  _layer_spec(l, (CHUNK, SGU_DIM))],
        out_specs=[tok(w) for w in widths],
        out_shape=[jax.ShapeDtypeStruct((N_TOK, w), F32) for w in widths],
        scratch_shapes=[pltpu.VMEM((_C_END, D_MODEL), BF16)],
        compiler_params=_params(),
        name="in_proj",
    )(xa, xb, mods, mods, norm_g, w_in, sgu_g, sgu_w_pair, sgu_b_full)


def _ssd_kernel(*refs, L, n_seq, has_h0, n_alias, emit_state):
    z_ref, xbc_ref, dt_ref, cw_ref, cb_ref, dtb_ref, alog_ref, dvec_ref, ng_ref = refs[:9]
    k = 9
    h0_ref = refs[k] if has_h0 else None
    k += int(has_h0) + n_alias
    y_ref = refs[k]
    hout_ref = refs[k + 1] if emit_state else None
    xpad, xc, acol, atr, cbs, bmt, dtsp, yacc, hst = refs[k + 1 + int(emit_state):]
    Q = CHUNK
    nc = L // Q
    halo = SUBLANES
    pad = (SSD_CONV - 1) // 2

    pitch = L + halo
    for q in range(n_seq + 1):
        xpad[q * pitch:q * pitch + halo, :] = jnp.zeros((halo, SSD_CONV_DIM), F32)
    for g in range(n_seq * nc):
        o = halo + (g // nc) * pitch + (g % nc) * Q
        xpad[o:o + Q, :] = xbc_ref[g * Q:(g + 1) * Q, :]
    win = Q + 2 * halo
    for c in range(n_seq * nc):
        for lb in range(0, SSD_CONV_DIM, CONV_LANES):
            ln = slice(lb, lb + CONV_LANES)
            o = (c // nc) * pitch + (c % nc) * Q
            xa = xpad[o:o + win, ln]
            acc = cb_ref[:, ln] + cw_ref[pad:pad + 1, ln] * xa[halo:halo + Q, :]
            for t in range(SSD_CONV):
                if t != pad:
                    rolled = pltpu.roll(xa, (pad - t) % win, 0)
                    acc = acc + cw_ref[t:t + 1, ln] * rolled[halo:halo + Q, :]
            xc[c * Q:(c + 1) * Q, ln] = _silu(acc)

    raw = dt_ref[...] + dtb_ref[...]
    dt = jnp.maximum(raw, 0.0) + jnp.log(1.0 + jnp.exp(-jnp.abs(raw)))
    dtsp[...] = dt
    a_neg = -jnp.exp(alog_ref[...])
    row = lax.broadcasted_iota(jnp.int32, (Q, Q), 0)
    col = lax.broadcasted_iota(jnp.int32, (Q, Q), 1)
    lower = row >= col
    upper = col >= row
    tri3 = jnp.concatenate([lower.astype(BF16)] * 3, axis=1)
    fwd_lane = lax.broadcasted_iota(jnp.int32, (Q, LANES), 1) < SSD_HEADS
    for c in range(n_seq * nc):
        dta = dtsp[c * Q:(c + 1) * Q, :] * a_neg
        hi = dta.astype(BF16)
        rem = dta - hi.astype(F32)
        mid = rem.astype(BF16)
        lo = (rem - mid.astype(F32)).astype(BF16)
        pre = jnp.dot(tri3, jnp.concatenate([hi, mid, lo], axis=0), preferred_element_type=F32)
        suf = pre[Q - 1:Q, :] - pre + dta
        a = jnp.where(fwd_lane, pre, suf)
        acol[c * Q:(c + 1) * Q, :] = a
        atr[c] = a.T[0:N_DIR * SSD_HEADS, :]
        for g in range(SSD_GROUPS):
            b0 = SSD_INNER + g * SSD_STATE
            c0 = SSD_INNER + (SSD_GROUPS + g) * SSD_STATE
            bm = xc[c * Q:(c + 1) * Q, b0:b0 + SSD_STATE]
            cbs[c, g] = lax.dot_general(xc[c * Q:(c + 1) * Q, c0:c0 + SSD_STATE].astype(BF16),
                                        bm.astype(BF16), _NT, preferred_element_type=F32)
            bmt[c, g] = bm.T.astype(BF16)

    sel_row = lax.broadcasted_iota(jnp.int32, (3 * LANES, SSD_INNER), 0) & (LANES - 1)
    sel_head = lax.broadcasted_iota(jnp.int32, (3 * LANES, SSD_INNER), 1) >> (SSD_HEAD_DIM.bit_length() - 1)
    sel = [(sel_row == sel_head + d * SSD_HEADS).astype(BF16) for d in range(N_DIR)]
    mask2 = [jnp.concatenate([m, m], axis=1) for m in (lower, upper)]

    for q in range(n_seq):
        for d in range(N_DIR):
            for p in range(N_PAIR):
                if has_h0:
                    hst[q, d, p] = h0_ref[q, d, p * PAIR:(p + 1) * PAIR, :].T
                else:
                    hst[q, d, p] = jnp.zeros((SSD_STATE, PAIR), F32)

    lo_lane = lax.broadcasted_iota(jnp.int32, (Q, PAIR), 1) < SSD_HEAD_DIM

    def chunk(c, d, q):
        r0 = pl.multiple_of(c * Q, Q)
        rows = pl.ds(r0, Q)
        a = acol[rows, :]
        a_t = atr[c]
        dtp = jnp.dot(_split3(dtsp[rows, :]), sel[d], preferred_element_type=F32)
        a_end = a[Q - 1:Q, :] if d == 0 else a[0:1, :]
        dec = jnp.exp(a_end)
        for g in range(SSD_GROUPS):
            c0 = SSD_INNER + (SSD_GROUPS + g) * SSD_STATE
            cmb = xc[rows, c0:c0 + SSD_STATE].astype(BF16)
            cb = cbs[c, g]
            cb2 = jnp.concatenate([cb, cb], axis=1)
            for pr in range(2):
                p = g * 2 + pr
                j0 = d * SSD_HEADS + 2 * p
                j1 = j0 + 1
                xs = xc[rows, p * PAIR:(p + 1) * PAIR]
                ab0 = jnp.broadcast_to(a[:, j0:j0 + 1], (Q, Q))
                ab1 = jnp.broadcast_to(a[:, j1:j1 + 1], (Q, Q))
                seg = (jnp.concatenate([ab0, ab1], axis=1)
                       - jnp.concatenate([a_t[j0:j0 + 1, :], a_t[j1:j1 + 1, :]], axis=1))
                m = (cb2 * jnp.exp(jnp.where(mask2[d], seg, -jnp.inf))).astype(BF16)
                xdt = xs * dtp[:, p * PAIR:(p + 1) * PAIR]
                rhs = jnp.concatenate([jnp.where(lo_lane, xdt, 0.0).astype(BF16),
                                       jnp.where(lo_lane, 0.0, xdt).astype(BF16)], axis=0)
                y_diag = jnp.dot(m, rhs, preferred_element_type=F32)
                hp = hst[q, d, p]
                y_off = jnp.dot(cmb, hp.astype(BF16), preferred_element_type=F32)
                a_pair = jnp.where(lo_lane, ab0, ab1)
                y = y_diag + jnp.exp(a_pair) * y_off
                a_end_pair = jnp.where(lo_lane[0:1, :], a_end[:, j0:j0 + 1], a_end[:, j1:j1 + 1])
                xw = (xdt * jnp.exp(a_end_pair - a_pair)).astype(BF16)
                s_new = jnp.dot(bmt[c, g], xw, preferred_element_type=F32)
                decp = jnp.where(lo_lane[0:1, :], dec[:, j0:j0 + 1], dec[:, j1:j1 + 1])
                hst[q, d, p] = decp * hp + s_new
                if d == 0:
                    y = y + dvec_ref[:, p * PAIR:(p + 1) * PAIR] * xs
                yacc[d, rows, p * PAIR:(p + 1) * PAIR] = y

    def step(t, carry):
        for q in range(n_seq):
            chunk(q * nc + t, 0, q)
            chunk(q * nc + nc - 1 - t, 1, q)
        return carry

    lax.fori_loop(0, nc, step, 0, unroll=2 if n_seq == 1 else 1)
    for c in range(n_seq * nc):
        rows = slice(c * Q, (c + 1) * Q)
        y = (yacc[0, rows, :] + yacc[1, rows, :]) * _silu(z_ref[rows, :])
        y_ref[rows, :] = _rms(y) * ng_ref[...]
    if emit_state:
        for q in range(n_seq):
            for d in range(N_DIR):
                for p in range(N_PAIR):
                    hout_ref[q, d, p * PAIR:(p + 1) * PAIR, :] = hst[q, d, p].T


def _ssd_mixer(z, xbc, dtr, prm, l, *, L, n_seq, nb, blk0, h0=None, y_buf=None, st_buf=None, after=None,
               emit_state):
    R = n_seq * L
    seq = lambda w: pl.BlockSpec((R, w), lambda b: (blk0 + b, 0))
    st = pl.BlockSpec((n_seq, None, N_DIR, SSD_HP, SSD_STATE), lambda b: (b, l, 0, 0, 0))
    args = [z, xbc, dtr, prm["conv_w"], prm["conv_b"], prm["dt_bias"], prm["a_log"], prm["d_vec"],
            prm["norm_g"]]
    in_specs = [seq(SSD_INNER), seq(SSD_CONV_DIM), seq(LANES),
                _layer_spec(l, (SSD_CONV, SSD_CONV_DIM)), _layer_spec(l, (1, SSD_CONV_DIM)),
                _layer_spec(l, (1, LANES)), _layer_spec(l, (1, LANES)),
                _layer_spec(l, (1, SSD_INNER)), _layer_spec(l, (1, SSD_INNER))]
    if h0 is not None:
        args.append(h0)
        in_specs.append(st)
    out_shape = [jax.ShapeDtypeStruct((N_TOK, SSD_INNER), F32)]
    out_specs = [seq(SSD_INNER)]
    aliases = {}
    if y_buf is not None:
        aliases[len(args)] = 0
        args.append(y_buf)
        in_specs.append(_any_spec())
    if emit_state:
        out_shape.append(jax.ShapeDtypeStruct((BATCH, DEPTH, N_DIR, SSD_HP, SSD_STATE), F32))
        out_specs.append(st)
        if st_buf is not None:
            aliases[len(args)] = 1
            args.append(st_buf)
            in_specs.append(_any_spec())
    if after is not None:
        args.append(after)
        in_specs.append(_any_spec())
    res = pl.pallas_call(
        functools.partial(_ssd_kernel, L=L, n_seq=n_seq, has_h0=h0 is not None,
                          n_alias=len(aliases) + int(after is not None), emit_state=emit_state),
        grid=(nb,),
        in_specs=in_specs,
        out_specs=out_specs,
        out_shape=out_shape,
        input_output_aliases=aliases,
        scratch_shapes=[pltpu.VMEM((n_seq * (L + SUBLANES) + SUBLANES, SSD_CONV_DIM), F32),
                        pltpu.VMEM((R, SSD_CONV_DIM), F32),
                        pltpu.VMEM((R, LANES), F32),
                        pltpu.VMEM((R // CHUNK, N_DIR * SSD_HEADS, CHUNK), F32),
                        pltpu.VMEM((R // CHUNK, SSD_GROUPS, CHUNK, CHUNK), F32),
                        pltpu.VMEM((R // CHUNK, SSD_GROUPS, SSD_STATE, CHUNK), BF16),
                        pltpu.VMEM((R, LANES), F32),
                        pltpu.VMEM((N_DIR, R, SSD_INNER), F32),
                        pltpu.VMEM((n_seq, N_DIR, N_PAIR, PAIR, SSD_STATE), F32)],
        compiler_params=_params(),
        name=f"ssd_mixer_L{L}",
    )(*args)
    return res if emit_state else (res[0], None)


_LOG2_CH = S5_GROUP_CH.bit_length() - 1
_LOG2_ST = S5_STATE.bit_length() - 1


def _cmul(ar, ai, br, bi):
    return ar * br - ai * bi, ar * bi + ai * br


def _s5prep_kernel(lre_ref, lim_ref, ldt_ref, btr_ref, bti_ref, ctr_ref, cti_ref,
                   bblk_ref, cblk_ref, pw_ref):
    brow = lax.broadcasted_iota(jnp.int32, (S5_DIM, S5_LANES), 0) >> _LOG2_CH
    bcol = lax.broadcasted_iota(jnp.int32, (S5_DIM, S5_LANES), 1) >> _LOG2_ST
    bmask = brow == bcol
    p1 = lax.broadcasted_iota(jnp.int32, (S5_SEG, S5_LANES), 0) + 1
    for d in range(N_DIR):
        lre = lre_ref[d]
        lim = lim_ref[d]
        step = jnp.exp(ldt_ref[d])
        mag = jnp.exp(lre * step)
        lbr = mag * jnp.cos(lim * step)
        lbi = mag * jnp.sin(lim * step)
        den = lre * lre + lim * lim
        nr = lbr - 1.0
        cr = (nr * lre + lbi * lim) / den
        ci = (lbi * lre - nr * lim) / den
        br, bi = _cmul(cr, ci, btr_ref[...], bti_ref[...])
        br = jnp.where(bmask, br, 0.0).astype(BF16)
        bi = jnp.where(bmask, bi, 0.0).astype(BF16)
        for sl in range(S5_NSLAB):
            bblk_ref[d, sl, :, 0:S5_SLAB] = br[:, sl * S5_SLAB:(sl + 1) * S5_SLAB]
            bblk_ref[d, sl, :, S5_SLAB:2 * S5_SLAB] = bi[:, sl * S5_SLAB:(sl + 1) * S5_SLAB]
        rr = jnp.ones((S5_SEG, S5_LANES), F32)
        ri = jnp.zeros((S5_SEG, S5_LANES), F32)
        sr, si = lbr, lbi
        for k in range(S5_SEG.bit_length()):
            bit = ((p1 >> k) & 1) == 1
            tr, ti = _cmul(rr, ri, sr, si)
            rr = jnp.where(bit, tr, rr)
            ri = jnp.where(bit, ti, ri)
            sr, si = _cmul(sr, si, sr, si)
        pw_ref[d, 0, 0:S5_SEG, :] = rr
        pw_ref[d, 1, 0:S5_SEG, :] = ri
        n = S5_SEG
        while n < S5_PW_ROWS:
            tr, ti = _cmul(pw_ref[d, 0, 0:n, :], pw_ref[d, 1, 0:n, :],
                           pw_ref[d, 0, n - 1:n, :], pw_ref[d, 1, n - 1:n, :])
            pw_ref[d, 0, n:2 * n, :] = tr
            pw_ref[d, 1, n:2 * n, :] = ti
            n *= 2
    crow = lax.broadcasted_iota(jnp.int32, (S5_LANES, S5_DIM), 0) >> _LOG2_ST
    ccol = lax.broadcasted_iota(jnp.int32, (S5_LANES, S5_DIM), 1) >> _LOG2_CH
    cmask = crow == ccol
    cr = jnp.where(cmask, ctr_ref[...], 0.0).astype(BF16)
    ci = jnp.where(cmask, -cti_ref[...], 0.0).astype(BF16)
    for sl in range(S5_NSLAB):
        cblk_ref[sl, 0:S5_SLAB, :] = cr[sl * S5_SLAB:(sl + 1) * S5_SLAB, :]
        cblk_ref[sl, S5_SLAB:2 * S5_SLAB, :] = ci[sl * S5_SLAB:(sl + 1) * S5_SLAB, :]


def _s5_prep(lam_re, lam_im, log_dt, b_re, b_im, c_re, c_im):
    row = lambda t: t.reshape(DEPTH, N_DIR, 1, S5_LANES)
    ldt = jnp.repeat(log_dt, S5_STATE, axis=-1)
    bt = lambda t: jnp.tile(t.transpose(0, 3, 1, 2).reshape(DEPTH, S5_GROUP_CH, S5_LANES),
                            (1, S5_GROUPS, 1))
    ct = lambda t: jnp.tile(t.transpose(0, 1, 3, 2).reshape(DEPTH, S5_LANES, S5_GROUP_CH),
                            (1, 1, S5_GROUPS))
    vec = pl.BlockSpec((None, N_DIR, 1, S5_LANES), lambda l: (l, 0, 0, 0))
    bsp = pl.BlockSpec((None, S5_DIM, S5_LANES), lambda l: (l, 0, 0))
    csp = pl.BlockSpec((None, S5_LANES, S5_DIM), lambda l: (l, 0, 0))
    return pl.pallas_call(
        _s5prep_kernel,
        grid=(DEPTH,),
        in_specs=[vec, vec, vec, bsp, bsp, csp, csp],
        out_specs=[pl.BlockSpec((None, N_DIR, S5_NSLAB, S5_DIM, 2 * S5_SLAB), lambda l: (l, 0, 0, 0, 0)),
                   pl.BlockSpec((None, S5_NSLAB, 2 * S5_SLAB, S5_DIM), lambda l: (l, 0, 0, 0)),
                   pl.BlockSpec((None, N_DIR, 2, S5_PW_ROWS, S5_LANES), lambda l: (l, 0, 0, 0, 0))],
        out_shape=[jax.ShapeDtypeStruct((DEPTH, N_DIR, S5_NSLAB, S5_DIM, 2 * S5_SLAB), BF16),
                   jax.ShapeDtypeStruct((DEPTH, S5_NSLAB, 2 * S5_SLAB, S5_DIM), BF16),
                   jax.ShapeDtypeStruct((DEPTH, N_DIR, 2, S5_PW_ROWS, S5_LANES), F32)],
        compiler_params=_params(),
        name="s5_prep",
    )(row(lam_re), row(lam_im), row(ldt), bt(b_re), bt(b_im), ct(c_re), ct(c_im))


def _s5_moves(n_seq, n_seg, seq_len, col_major):
    steps = seq_len // n_seg
    if not col_major:
        return [(q * seq_len + s * steps, steps, q * n_seg + s, S5_SEG)
                for q in range(n_seq) for s in range(n_seg)]
    assert n_seq == 1 and n_seg == S5_SEG
    grid_rows = seq_len // GRID_W
    wseg = GRID_W // n_seg
    return [(r * GRID_W + s * wseg, wseg, r * S5_SEG + s, grid_rows * S5_SEG)
            for r in range(grid_rows) for s in range(n_seg)]


def _s5_kernel(*refs, n_seq, n_seg, seq_len, col_major, has_h0, n_alias, emit_state):
    u_ref, bblk_ref, cblk_ref, pw_ref, dvec_ref, gw_ref, gb_ref = refs[:7]
    k = 7
    h0r_ref, h0i_ref = (refs[k], refs[k + 1]) if has_h0 else (None, None)
    k += 2 * int(has_h0) + n_alias
    y_ref = refs[k]
    fr_ref, fi_ref = (refs[k + 1], refs[k + 2]) if emit_state else (None, None)
    up, buf_f, buf_b, yacc, yp, cin, fin = refs[k + 1 + 2 * int(emit_state):]
    assert n_seq * n_seg == S5_SEG and not (has_h0 and n_seg == 1) and not (emit_state and n_seg > 1)
    n_rows = n_seq * seq_len
    steps = seq_len // n_seg
    n_planes = S5_DIM // LANES
    W = S5_SLAB
    moves = _s5_moves(n_seq, n_seg, seq_len, col_major)

    for src, n, dst, stride in moves:
        for t in range(n_planes):
            up[t, pl.ds(dst, n, stride=stride), :] = u_ref[src:src + n, t * LANES:(t + 1) * LANES]

    def u_rows(rs):
        return jnp.concatenate([up[t, rs, :] for t in range(n_planes)], axis=1)

    zero = jnp.zeros((S5_SEG, W), F32)
    zrow = jnp.zeros((1, W), F32)
    for sl in range(S5_NSLAB):
        ln = slice(sl * W, (sl + 1) * W)
        for r0 in range(0, n_rows, ROW_SLAB):
            rs = slice(r0, r0 + ROW_SLAB)
            ub = u_rows(rs).astype(BF16)
            buf_f[rs, :] = jnp.dot(ub, bblk_ref[0, sl], preferred_element_type=F32)
            buf_b[rs, :] = jnp.dot(ub, bblk_ref[1, sl], preferred_element_type=F32)
        lam = [[jnp.broadcast_to(pw_ref[d, c, 0:1, ln], (S5_SEG, W)) for c in range(2)]
               for d in range(N_DIR)]

        def body(t, carry):
            fr, fi, br, bi = carry
            rf = pl.ds(pl.multiple_of(t * S5_SEG, S5_SEG), S5_SEG)
            rb = pl.ds(pl.multiple_of((steps - 1 - t) * S5_SEG, S5_SEG), S5_SEG)
            nfr = lam[0][0] * fr - lam[0][1] * fi + buf_f[rf, 0:W]
            nfi = lam[0][0] * fi + lam[0][1] * fr + buf_f[rf, W:2 * W]
            nbr = lam[1][0] * br - lam[1][1] * bi + buf_b[rb, 0:W]
            nbi = lam[1][0] * bi + lam[1][1] * br + buf_b[rb, W:2 * W]
            buf_f[rf, 0:W] = nfr
            buf_f[rf, W:2 * W] = nfi
            buf_b[rb, 0:W] = nbr
            buf_b[rb, W:2 * W] = nbi
            return nfr, nfi, nbr, nbi

        ends = lax.fori_loop(0, steps, body, (zero, zero, zero, zero), unroll=S5_UNROLL)

        if n_seg == 1:
            if emit_state:
                for d in range(N_DIR):
                    fin[d, 0, :, ln] = ends[2 * d]
                    fin[d, 1, :, ln] = ends[2 * d + 1]
        else:
            for d in range(N_DIR):
                er, ei = ends[2 * d], ends[2 * d + 1]
                lpr = pw_ref[d, 0, steps - 1:steps, ln]
                lpi = pw_ref[d, 1, steps - 1:steps, ln]
                pr = h0r_ref[d:d + 1, ln] if has_h0 else zrow
                pi = h0i_ref[d:d + 1, ln] if has_h0 else zrow
                order = range(n_seg) if d == 0 else range(n_seg - 1, -1, -1)
                for s in order:
                    cin[d, 0, s:s + 1, :] = pr
                    cin[d, 1, s:s + 1, :] = pi
                    tr, ti = _cmul(lpr, lpi, pr, pi)
                    pr = er[s:s + 1, :] + tr
                    pi = ei[s:s + 1, :] + ti
            cfr = cin[0, 0]
            cfi = cin[0, 1]
            cbr = cin[1, 0]
            cbi = cin[1, 1]

            def fix(p, carry):
                rows = pl.ds(pl.multiple_of(p * S5_SEG, S5_SEG), S5_SEG)
                q = steps - 1 - p
                tfr, tfi = _cmul(pw_ref[0, 0, pl.ds(p, 1), ln], pw_ref[0, 1, pl.ds(p, 1), ln], cfr, cfi)
                tbr, tbi = _cmul(pw_ref[1, 0, pl.ds(q, 1), ln], pw_ref[1, 1, pl.ds(q, 1), ln], cbr, cbi)
                buf_f[rows, 0:W] = buf_f[rows, 0:W] + buf_b[rows, 0:W] + tfr + tbr
                buf_f[rows, W:2 * W] = buf_f[rows, W:2 * W] + buf_b[rows, W:2 * W] + tfi + tbi
                return carry

            lax.fori_loop(0, steps, fix, 0, unroll=S5_UNROLL)

        for r0 in range(0, n_rows, ROW_SLAB):
            rs = slice(r0, r0 + ROW_SLAB)
            hs = buf_f[rs, :] if n_seg > 1 else buf_f[rs, :] + buf_b[rs, :]
            part = jnp.dot(hs.astype(BF16), cblk_ref[sl], preferred_element_type=F32)
            if sl == 0:
                yacc[rs, :] = part
            else:
                yacc[rs, :] = yacc[rs, :] + part

    gwb = gw_ref[...].astype(BF16)
    for r0 in range(0, n_rows, ROW_SLAB):
        rs = slice(r0, r0 + ROW_SLAB)
        y = _gelu_tanh(yacc[rs, :] + dvec_ref[...] * u_rows(rs))
        gate = jnp.dot(y.astype(BF16), gwb, preferred_element_type=F32) + gb_ref[...]
        y = y * jax.nn.sigmoid(gate)
        for t in range(n_planes):
            yp[t, rs, :] = y[:, t * LANES:(t + 1) * LANES]
    for src, n, dst, stride in moves:
        for t in range(n_planes):
            y_ref[src:src + n, t * LANES:(t + 1) * LANES] = yp[t, pl.ds(dst, n, stride=stride), :]
    if emit_state:
        for q in range(n_seq):
            for d in range(N_DIR):
                fr_ref[q, d:d + 1, :] = fin[d, 0, q:q + 1, :]
                fi_ref[q, d:d + 1, :] = fin[d, 1, q:q + 1, :]


def _s5_mixer(u, bblk, cblk, pw, prm, l, *, seq_len, n_seq, nb, blk0, col_major, h0=None, y_buf=None,
              st_buf=None, emit_state):
    n_rows = n_seq * seq_len
    n_seg = S5_SEG // n_seq
    seq = pl.BlockSpec((n_rows, S5_DIM), lambda b: (blk0 + b, 0))
    st = pl.BlockSpec((n_seq, None, N_DIR, S5_LANES), lambda b: (b, l, 0, 0))
    once = dict(pipeline_mode=pl.Buffered(1))
    args = [u, bblk, cblk, pw, prm["d_vec"], prm["glu_w"], prm["glu_b"]]
    in_specs = [seq, _layer_spec(l, (N_DIR, S5_NSLAB, S5_DIM, 2 * S5_SLAB), **once),
                _layer_spec(l, (S5_NSLAB, 2 * S5_SLAB, S5_DIM), **once),
                _layer_spec(l, (N_DIR, 2, S5_PW_ROWS if n_seg > 1 else S5_SEG, S5_LANES), **once),
                _layer_spec(l, (1, S5_DIM)),
                _layer_spec(l, (S5_DIM, S5_DIM)), _layer_spec(l, (1, S5_DIM))]
    if h0 is not None:
        h0_spec = pl.BlockSpec((None, None, N_DIR, S5_LANES), lambda b: (b, l, 0, 0))
        args += list(h0)
        in_specs += [h0_spec, h0_spec]
    out_shape = [jax.ShapeDtypeStruct((N_TOK, S5_DIM), F32)]
    out_specs = [seq]
    aliases = {}
    if y_buf is not None:
        aliases[len(args)] = 0
        args.append(y_buf)
        in_specs.append(_any_spec())
    if emit_state:
        out_shape += [jax.ShapeDtypeStruct((BATCH, DEPTH, N_DIR, S5_LANES), F32)] * 2
        out_specs += [st, st]
        if st_buf is not None:
            for j, buf in enumerate(st_buf):
                aliases[len(args)] = 1 + j
                args.append(buf)
                in_specs.append(_any_spec())
    res = pl.pallas_call(
        functools.partial(_s5_kernel, n_seq=n_seq, n_seg=n_seg, seq_len=seq_len, col_major=col_major,
                          has_h0=h0 is not None, n_alias=len(aliases), emit_state=emit_state),
        grid=(nb,),
        in_specs=in_specs,
        out_specs=out_specs,
        out_shape=out_shape,
        input_output_aliases=aliases,
        scratch_shapes=[pltpu.VMEM((S5_DIM // LANES, n_rows, LANES), F32),
                        pltpu.VMEM((n_rows, 2 * S5_SLAB), F32),
                        pltpu.VMEM((n_rows, 2 * S5_SLAB), F32),
                        pltpu.VMEM((n_rows, S5_DIM), F32),
                        pltpu.VMEM((S5_DIM // LANES, n_rows, LANES), F32),
                        pltpu.VMEM((N_DIR, 2, S5_SEG, S5_SLAB), F32),
                        pltpu.VMEM((N_DIR, 2, S5_SEG, S5_LANES), F32)],
        compiler_params=_params(),
        name=f"s5_mixer_L{seq_len}",
    )(*args)
    return (res[0], res[1], res[2]) if emit_state else (res[0], None, None)


def _outffn_kernel(xa_ref, xb_ref, ys_ref, yg_ref, y5_ref, g1_ref, sh_ref, sc_ref, g2_ref, ng_ref,
                   fg_ref, wo_ref, w1_ref, w2_ref, *rest, final):
    n_out = 2 if final else 1
    o_refs = rest[:n_out]
    wob, w1b, w2b = rest[n_out:]
    i = pl.program_id(0)

    @pl.when(i < FFN_PREP)
    def _():
        wob[i] = wo_ref[...].astype(BF16)
        w1b[i] = w1_ref[...].astype(BF16)
        w2b[i] = w2_ref[...].astype(BF16)

    @pl.when(i >= FFN_PREP)
    def _():
        t = i - FFN_PREP
        n_ctx = N_CTX // FFN_TILE
        r = _mod_row(t, FFN_TILE)
        row = lambda ref: ref[pl.ds(r, 1), :]
        x = jnp.where(t < n_ctx, xa_ref[...], xb_ref[...])
        mixed = jnp.concatenate([ys_ref[...].astype(BF16), yg_ref[...].astype(BF16),
                                 y5_ref[...].astype(BF16)], axis=1)
        w_out = wob[...].reshape(D_MODEL, D_MODEL)
        x1 = x + row(g1_ref) * jnp.dot(mixed, w_out, preferred_element_type=F32)
        h = ((_rms(x1) * ng_ref[...]) * (1.0 + row(sc_ref)) + row(sh_ref)).astype(BF16)
        acc = jnp.zeros((FFN_TILE, D_MODEL), F32)
        for j in range(FFN_PREP):
            a = jnp.maximum(jnp.dot(h, w1b[j], preferred_element_type=F32), 0.0)
            acc = acc + jnp.dot((a * a).astype(BF16), w2b[j], preferred_element_type=F32)
        x2 = x1 + row(g2_ref) * acc
        if not final:
            o_refs[0][...] = x2
        else:
            y = _rms(x2) * fg_ref[...]

            @pl.when(t < n_ctx)
            def _():
                o_refs[0][...] = y

            @pl.when(t >= n_ctx)
            def _():
                o_refs[1][...] = y


def _out_ffn(xa, xb, b_is_stream, y_ssd, y_sgu, y_s5, l, mods, norm_g, final_g, w_out, w1, w2, *, final):
    tile_of = lambda i: jnp.maximum(i - FFN_PREP, 0)
    tok = lambda w: pl.BlockSpec((FFN_TILE, w), lambda i: (tile_of(i), 0))
    chunk_of = lambda i: jnp.minimum(i, FFN_PREP - 1)
    sa, sb = _split_tok_specs(FFN_TILE, D_MODEL, b_is_stream, tile_of)
    if final:
        out_specs = list(_split_tok_specs(FFN_TILE, D_MODEL, False, tile_of))
        out_shape = [jax.ShapeDtypeStruct((N_CTX, D_MODEL), F32), jax.ShapeDtypeStruct((N_LAT, D_MODEL), F32)]
    else:
        out_specs = [tok(D_MODEL)]
        out_shape = [jax.ShapeDtypeStruct((N_TOK, D_MODEL), F32)]
    ko, kf = D_MODEL // FFN_PREP, D_FF // FFN_PREP
    return pl.pallas_call(
        functools.partial(_outffn_kernel, final=final),
        grid=(FFN_PREP + N_TOK // FFN_TILE,),
        in_specs=[sa, sb, tok(SSD_INNER), tok(SGU_DIM), tok(S5_DIM),
                  _mod_spec(l, 2), _mod_spec(l, 3), _mod_spec(l, 4), _mod_spec(l, 5),
                  _layer_spec(l, (1, D_MODEL)), pl.BlockSpec((1, D_MODEL), lambda i: (0, 0)),
                  pl.BlockSpec((None, ko, D_MODEL), lambda i: (l, chunk_of(i), 0)),
                  pl.BlockSpec((None, D_MODEL, kf), lambda i: (l, 0, chunk_of(i))),
                  pl.BlockSpec((None, kf, D_MODEL), lambda i: (l, chunk_of(i), 0))],
        out_specs=out_specs,
        out_shape=out_shape,
        scratch_shapes=[pltpu.VMEM((FFN_PREP, ko, D_MODEL), BF16),
                        pltpu.VMEM((FFN_PREP, D_MODEL, kf), BF16),
                        pltpu.VMEM((FFN_PREP, kf, D_MODEL), BF16)],
        compiler_params=_params(),
        name="out_ffn_final" if final else "out_ffn",
    )(xa, xb, y_ssd, y_sgu, y_s5, mods, mods, mods, mods, norm_g, final_g, w_out, w1, w2)


def kernel(x_prompt, x_sample, state_ssd, state_s5_re, state_s5_im, c, c_ctx, ada_w, ada_b, norm1_g,
           norm2_g, w_in, ssd_conv_w, ssd_conv_b, ssd_dt_bias, ssd_a_log, ssd_d, ssd_norm_g,
           sgu_norm_g, sgu_w, sgu_b, s5_lambda_re, s5_lambda_im, s5_log_dt, s5_b_re, s5_b_im,
           s5_c_re, s5_c_im, s5_d, s5_glu_w, s5_glu_b, w_out, ffn_w1, ffn_w2, final_norm_g):
    cvec = jnp.concatenate([c_ctx[None, :], c,
                            jnp.zeros((MOD_ROWS - 1 - DEC_BATCH, D_MODEL), F32)], axis=0)
    mods = _adaln_mods(cvec, ada_w, ada_b)
    bblk, cblk, pw = _s5_prep(s5_lambda_re, s5_lambda_im, s5_log_dt, s5_b_re, s5_b_im,
                              s5_c_re, s5_c_im)

    w_in_t = w_in.transpose(0, 2, 1)
    vec =lambda t: t.reshape(DEPTH, 1, -1)
    lane_pad = lambda t: jnp.pad(vec(t), ((0, 0), (0, 0), (0, LANES - t[0].size)))
    ssd_prm = dict(conv_w=ssd_conv_w, conv_b=vec(ssd_conv_b), dt_bias=lane_pad(ssd_dt_bias),
                   a_log=lane_pad(ssd_a_log), d_vec=vec(jnp.repeat(ssd_d, SSD_HEAD_DIM, axis=-1)),
                   norm_g=vec(ssd_norm_g))
    s5_prm = dict(d_vec=vec(s5_d), glu_w=s5_glu_w, glu_b=vec(s5_glu_b))
    sgu_w_pair = sgu_w.reshape(DEPTH, SGU_HEADS // 2, 2, CHUNK, CHUNK).transpose(0, 1, 3, 2, 4)
    sgu_w_pair = sgu_w_pair.reshape(DEPTH, SGU_HEADS // 2, CHUNK, 2 * CHUNK).astype(BF16)
    sgu_b_full = jnp.repeat(sgu_b.transpose(0, 2, 1), SGU_DIM // SGU_HEADS, axis=2)
    norm1 = vec(norm1_g)
    norm2 = vec(norm2_g)
    sgu_g = vec(sgu_norm_g)
    final_g = final_norm_g.reshape(1, D_MODEL)

    lat_ssd = state_ssd.reshape(DEC_BATCH, DEPTH, N_DIR, SSD_HP, SSD_STATE)
    lat_s5 = (state_s5_re.reshape(DEC_BATCH, DEPTH, N_DIR, S5_LANES),
              state_s5_im.reshape(DEC_BATCH, DEPTH, N_DIR, S5_LANES))
    lat_blk = N_CTX // DEC_SEQ

    xa = x_prompt.reshape(N_CTX, D_MODEL)
    xb = x_sample.reshape(N_LAT, D_MODEL)
    b_is_stream = False
    st_ssd = None
    st_s5 = None
    for l in range(DEPTH):
        z, xbc, y_sgu, s5u, dtr = _in_proj(xa, xb, b_is_stream, l, mods, norm1, w_in_t, sgu_g,
                                           sgu_w_pair, sgu_b_full)
        y_s5, fr, fi = _s5_mixer(s5u, bblk, cblk, pw, s5_prm, l, seq_len=SEQ, n_seq=S5_SEG,
                                 nb=BATCH // S5_SEG, blk0=0, col_major=False, st_buf=st_s5,
                                 emit_state=True)
        st_s5 = (fr, fi)
        y_s5, _, _ = _s5_mixer(s5u, bblk, cblk, pw, s5_prm, l, seq_len=DEC_SEQ, n_seq=1, nb=DEC_BATCH,
                               blk0=lat_blk, col_major=True, h0=lat_s5, y_buf=y_s5, emit_state=False)
        y_ssd, st_ssd = _ssd_mixer(z, xbc, dtr, ssd_prm, l, L=SEQ, n_seq=SSD_CTX_SEQS,
                                   nb=BATCH // SSD_CTX_SEQS, blk0=0, st_buf=st_ssd, after=y_s5,
                                   emit_state=True)
        y_ssd, _ = _ssd_mixer(z, xbc, dtr, ssd_prm, l, L=DEC_SEQ, n_seq=1, nb=DEC_BATCH, blk0=lat_blk,
                              h0=lat_ssd, y_buf=y_ssd, emit_state=False)
        final = l == DEPTH - 1
        out = _out_ffn(xa, xb, b_is_stream, y_ssd, y_sgu, y_s5, l, mods, norm2, final_g, w_out, ffn_w1,
                       ffn_w2, final=final)
        if not final:
            xa = xb = out[0]
            b_is_stream = True

    y_prompt = out[0].reshape(BATCH, SEQ, D_MODEL)
    y_sample = out[1].reshape(DEC_BATCH, DEC_SEQ, D_MODEL)
    new_state_ssd = st_ssd.reshape(BATCH, DEPTH, N_DIR, SSD_HEADS, SSD_HEAD_DIM, SSD_STATE)
    s5_shape = (BATCH, DEPTH, N_DIR, S5_GROUPS, S5_STATE)
    return (y_prompt, y_sample, new_state_ssd, st_s5[0].reshape(s5_shape), st_s5[1].reshape(s5_shape))
```

```python
import functools
import math

import jax
import jax.numpy as jnp
from jax import lax
from jax.experimental import pallas as pl
from jax.experimental.pallas import tpu as pltpu

F32 = jnp.float32
BF16 = jnp.bfloat16

D_MODEL = 1024
BATCH = 16
SEQ = 256
DEPTH = 2
DEC_BATCH = 2
DEC_SEQ = 1024
GRID_W = 64
CHUNK = 128
N_DIR = 2
EPS = 1e-6
SSD_INNER = 512
SSD_HEAD_DIM = 64
SSD_HEADS = 8
SSD_GROUPS = 2
SSD_STATE = 128
SSD_CONV = 5
SSD_CONV_DIM = SSD_INNER + 2 * SSD_GROUPS * SSD_STATE
SGU_DIM = 256
SGU_HEADS = 4
S5_DIM = 256
S5_GROUP_CH = 16
S5_GROUPS = 16
S5_STATE = 64
D_FF = 4 * D_MODEL
OFF_XBC = SSD_INNER
OFF_DT = OFF_XBC + SSD_CONV_DIM
OFF_SGU = OFF_DT + N_DIR * SSD_HEADS
OFF_S5 = OFF_SGU + 2 * SGU_DIM
IN_DIM = OFF_S5 + S5_DIM

N_CTX = BATCH * SEQ
N_LAT = DEC_BATCH * DEC_SEQ
N_TOK = N_CTX + N_LAT
LANES = 128
SUBLANES = 8
MOD_ROWS = SUBLANES
SSD_HP = SSD_HEADS * SSD_HEAD_DIM
S5_LANES = S5_GROUPS * S5_STATE
S5_SEG = SUBLANES
S5_PW_ROWS = DEC_SEQ // S5_SEG
S5_SLAB = 256
S5_NSLAB = S5_LANES // S5_SLAB
S5_UNROLL = 16
PAIR = 2 * SSD_HEAD_DIM
N_PAIR = SSD_HEADS // 2
TOK_TILE = 512
N_CTX_TILES = N_CTX // TOK_TILE
FFN_TILE = 512
FFN_PREP = 8
ROW_SLAB = 256
CONV_LANES = 256
SSD_CTX_SEQS = 2
SSD_UNROLL = 4
VMEM_LIMIT = 56 * 1024 * 1024

_NT = (((1,), (1,)), ((), ()))


def _params(n_axes=1):
    return pltpu.CompilerParams(dimension_semantics=("arbitrary",) * n_axes,
                                vmem_limit_bytes=VMEM_LIMIT)


def _layer_spec(l, shape, **kw):
    return pl.BlockSpec((None,) + tuple(shape), lambda *_: (l,) + (0,) * len(shape), **kw)


def _mod_spec(l, k):
    return pl.BlockSpec((None, MOD_ROWS, D_MODEL), lambda *_: (l, 0, k))


def _any_spec():
    return pl.BlockSpec(memory_space=pl.ANY)


def _mod_row(i, tm):
    n_ctx = N_CTX // tm
    return jnp.where(i < n_ctx, 0, 1 + (i - n_ctx) // (DEC_SEQ // tm))


def _split_tok_specs(tm, width, b_has_ctx_rows, tile_of=lambda i: i):
    n_ctx = N_CTX // tm
    b_off = n_ctx if b_has_ctx_rows else 0
    a = pl.BlockSpec((tm, width), lambda i: (jnp.minimum(tile_of(i), n_ctx - 1), 0))
    b = pl.BlockSpec((tm, width), lambda i: (jnp.maximum(tile_of(i), n_ctx) - n_ctx + b_off, 0))
    return a, b


def _silu(x):
    return x * jax.nn.sigmoid(x)


def _gelu_tanh(x):
    c = math.sqrt(2.0 / math.pi)
    return 0.5 * x * (1.0 + jnp.tanh(c * (x + 0.044715 * (x * x * x))))


def _rms(x):
    return x * lax.rsqrt(jnp.mean(x * x, axis=-1, keepdims=True) + EPS)


def _bdot(a, b):
    return jnp.dot(a.astype(BF16), b.astype(BF16), preferred_element_type=F32)


def _split3(x):
    hi = x.astype(BF16)
    r = x - hi.astype(F32)
    mid = r.astype(BF16)
    lo = (r - mid.astype(F32)).astype(BF16)
    return jnp.concatenate([hi, mid, lo], axis=1)


def _mod_kernel(c_ref, w_ref, b_ref, o_ref):
    o_ref[...] = _bdot(_silu(c_ref[...]), w_ref[...]) + b_ref[...]


def _adaln_mods(cvec, ada_w, ada_b):
    n_blk = 6
    return pl.pallas_call(
        _mod_kernel,
        grid=(DEPTH, n_blk),
        in_specs=[pl.BlockSpec((MOD_ROWS, D_MODEL), lambda l, j: (0, 0)),
                  pl.BlockSpec((None, D_MODEL, D_MODEL), lambda l, j: (l, 0, j)),
                  pl.BlockSpec((None, 1, D_MODEL), lambda l, j: (l, 0, j))],
        out_specs=pl.BlockSpec((None, MOD_ROWS, D_MODEL), lambda l, j: (l, 0, j)),
        out_shape=jax.ShapeDtypeStruct((DEPTH, MOD_ROWS, 6 * D_MODEL), F32),
        compiler_params=_params(2),
        name="adaln_mod",
    )(cvec, ada_w, ada_b.reshape(DEPTH, 1, 6 * D_MODEL))


_C_Z = 0
_C_XBC = _C_Z + SSD_INNER
_C_SGU = _C_XBC + SSD_CONV_DIM
_C_S5 = _C_SGU + 2 * SGU_DIM
_C_DT = _C_S5 + S5_DIM
_C_END = _C_DT + LANES


def _inproj_kernel(xa_ref, xb_ref, sh_ref, sc_ref, g_ref, win_ref, sg_ref, sw_ref, sb_ref,
                   z_ref, xbc_ref, ysgu_ref, s5_ref, dt_ref, w_ref):
    i = pl.program_id(0)

    @pl.when(i == 0)
    def _():
        moves = ((0, OFF_DT, _C_Z), (OFF_SGU, IN_DIM, _C_SGU), (OFF_DT, OFF_SGU, _C_DT))
        for src0, src1, dst0 in moves:
            for r0 in range(src0, src1, ROW_SLAB):
                n = min(ROW_SLAB, src1 - r0)
                w_ref[dst0 + r0 - src0:dst0 + r0 - src0 + n, :] = win_ref[r0:r0 + n, :].astype(BF16)
        n_pad = LANES - N_DIR * SSD_HEADS
        w_ref[_C_END - n_pad:_C_END, :] = jnp.zeros((n_pad, D_MODEL), BF16)

    r = _mod_row(i, TOK_TILE)
    x = jnp.where(i < N_CTX_TILES, xa_ref[...], xb_ref[...])
    shift = sh_ref[pl.ds(r, 1), :]
    scale = sc_ref[pl.ds(r, 1), :]
    h = (_rms(x) * g_ref[...]) * (1.0 + scale) + shift
    hb = h.astype(BF16)
    proj = lambda c0, c1: lax.dot_general(hb, w_ref[c0:c1, :], _NT, preferred_element_type=F32)
    uv = _gelu_tanh(proj(_C_SGU, _C_S5))
    z_ref[...] = proj(_C_Z, _C_XBC)
    xbc_ref[...] = proj(_C_XBC, _C_SGU)
    s5_ref[...] = proj(_C_S5, _C_DT)
    dt_ref[...] = proj(_C_DT, _C_END)

    u = uv[:, :SGU_DIM]
    v = _rms(uv[:, SGU_DIM:]) * sg_ref[...]
    lo_lane = lax.broadcasted_iota(jnp.int32, (CHUNK, LANES), 1) < (LANES // 2)
    for c in range(TOK_TILE // CHUNK):
        rows = slice(c * CHUNK, (c + 1) * CHUNK)
        mix = []
        for pr in range(SGU_HEADS // 2):
            vp = v[rows, pr * LANES:(pr + 1) * LANES]
            rhs = jnp.concatenate([jnp.where(lo_lane, vp, 0.0).astype(BF16),
                                   jnp.where(lo_lane, 0.0, vp).astype(BF16)], axis=0)
            mix.append(jnp.dot(sw_ref[pr], rhs, preferred_element_type=F32))
        ysgu_ref[rows, :] = u[rows, :] * (jnp.concatenate(mix, axis=1) + sb_ref[...])


def _in_proj(xa, xb, b_is_stream, l, mods, norm_g, w_in, sgu_g, sgu_w_pair, sgu_b_full):
    tok = lambda w: pl.BlockSpec((TOK_TILE, w), lambda i: (i, 0))
    widths = (SSD_INNER, SSD_CONV_DIM, SGU_DIM, S5_DIM, LANES)
    sa, sb = _split_tok_specs(TOK_TILE, D_MODEL, b_is_stream)
    return pl.pallas_call(
        _inproj_kernel,
        grid=(N_TOK // TOK_TILE,),
        in_specs=[sa, sb, _mod_spec(l, 0), _mod_spec(l, 1), _layer_spec(l, (1, D_MODEL)),
                  _layer_spec(l, (IN_DIM, D_MODEL), pipeline_mode=pl.Buffered(1)),
                  _layer_spec(l, (1, SGU_DIM)), _layer_spec(l, (SGU_HEADS // 2, CHUNK, 2 * CHUNK)),
                  _layer_spec(l, (CHUNK, SGU_DIM))],
        out_specs=[tok(w) for w in widths],
        out_shape=[jax.ShapeDtypeStruct((N_TOK, w), F32) for w in widths],
        scratch_shapes=[pltpu.VMEM((_C_END, D_MODEL), BF16)],
        compiler_params=_params(),
        name="in_proj",
    )(xa, xb, mods, mods, norm_g, w_in, sgu_g, sgu_w_pair, sgu_b_full)


def _ssd_kernel(*refs, L, n_seq, has_h0, n_alias, emit_state):
    z_ref, xbc_ref, dt_ref, cw_ref, cb_ref, dtb_ref, alog_ref, dvec_ref, ng_ref = refs[:9]
    k = 9
    h0_ref = refs[k] if has_h0 else None
    k += int(has_h0) + n_alias
    y_ref = refs[k]
    hout_ref = refs[k + 1] if emit_state else None
    xpad, xc, acol, atr, cbs, bmt, dtsp, yacc, hst = refs[k + 1 + int(emit_state):]
    Q = CHUNK
    nc = L // Q
    halo = SUBLANES
    pad = (SSD_CONV - 1) // 2

    pitch = L + halo
    for q in range(n_seq + 1):
        xpad[q * pitch:q * pitch + halo, :] = jnp.zeros((halo, SSD_CONV_DIM), F32)
    for g in range(n_seq * nc):
        o = halo + (g // nc) * pitch + (g % nc) * Q
        xpad[o:o + Q, :] = xbc_ref[g * Q:(g + 1) * Q, :]
    win = Q + 2 * halo
    for c in range(n_seq * nc):
        for lb in range(0, SSD_CONV_DIM, CONV_LANES):
            ln = slice(lb, lb + CONV_LANES)
            o = (c // nc) * pitch + (c % nc) * Q
            xa = xpad[o:o + win, ln]
            acc = cb_ref[:, ln] + cw_ref[pad:pad + 1, ln] * xa[halo:halo + Q, :]
            for t in range(SSD_CONV):
                if t != pad:
                    rolled = pltpu.roll(xa, (pad - t) % win, 0)
                    acc = acc + cw_ref[t:t + 1, ln] * rolled[halo:halo + Q, :]
            xc[c * Q:(c + 1) * Q, ln] = _silu(acc)

    raw = dt_ref[...] + dtb_ref[...]
    dt = jnp.maximum(raw, 0.0) + jnp.log(1.0 + jnp.exp(-jnp.abs(raw)))
    dtsp[...] = dt
    a_neg = -jnp.exp(alog_ref[...])
    row = lax.broadcasted_iota(jnp.int32, (Q, Q), 0)
    col = lax.broadcasted_iota(jnp.int32, (Q, Q), 1)
    lower = row >= col
    upper = col >= row
    tri3 = jnp.concatenate([lower.astype(BF16)] * 3, axis=1)
    fwd_lane = lax.broadcasted_iota(jnp.int32, (Q, LANES), 1) < SSD_HEADS
    for c in range(n_seq * nc):
        dta = dtsp[c * Q:(c + 1) * Q, :] * a_neg
        hi = dta.astype(BF16)
        rem = dta - hi.astype(F32)
        mid = rem.astype(BF16)
        lo = (rem - mid.astype(F32)).astype(BF16)
        pre = jnp.dot(tri3, jnp.concatenate([hi, mid, lo], axis=0), preferred_element_type=F32)
        suf = pre[Q - 1:Q, :] - pre + dta
        a = jnp.where(fwd_lane, pre, suf)
        acol[c * Q:(c + 1) * Q, :] = a
        atr[c] = a.T[0:N_DIR * SSD_HEADS, :]
        for g in range(SSD_GROUPS):
            b0 = SSD_INNER + g * SSD_STATE
            c0 = SSD_INNER + (SSD_GROUPS + g) * SSD_STATE
            bm = xc[c * Q:(c + 1) * Q, b0:b0 + SSD_STATE]
            cbs[c, g] = lax.dot_general(xc[c * Q:(c + 1) * Q, c0:c0 + SSD_STATE].astype(BF16),
                                        bm.astype(BF16), _NT, preferred_element_type=F32)
            bmt[c, g] = bm.T.astype(BF16)

    sel_row = lax.broadcasted_iota(jnp.int32, (3 * LANES, SSD_INNER), 0) & (LANES - 1)
    sel_head = lax.broadcasted_iota(jnp.int32, (3 * LANES, SSD_INNER), 1) >> (SSD_HEAD_DIM.bit_length() - 1)
    sel = [(sel_row == sel_head + d * SSD_HEADS).astype(BF16) for d in range(N_DIR)]
    mask2 = [jnp.concatenate([m, m], axis=1) for m in (lower, upper)]

    for q in range(n_seq):
        for d in range(N_DIR):
            for p in range(N_PAIR):
                if has_h0:
                    hst[q, d, p] = h0_ref[q, d, p * PAIR:(p + 1) * PAIR, :].T
                else:
                    hst[q, d, p] = jnp.zeros((SSD_STATE, PAIR), F32)

    lo_lane = lax.broadcasted_iota(jnp.int32, (Q, PAIR), 1) < SSD_HEAD_DIM

    def chunk(c, d, q):
        r0 = pl.multiple_of(c * Q, Q)
        rows = pl.ds(r0, Q)
        a = acol[rows, :]
        a_t = atr[c]
        dtp = jnp.dot(_split3(dtsp[rows, :]), sel[d], preferred_element_type=F32)
        a_end = a[Q - 1:Q, :] if d == 0 else a[0:1, :]
        dec = jnp.exp(a_end)
        for g in range(SSD_GROUPS):
            c0 = SSD_INNER + (SSD_GROUPS + g) * SSD_STATE
            cmb = xc[rows, c0:c0 + SSD_STATE].astype(BF16)
            cb = cbs[c, g]
            cb2 = jnp.concatenate([cb, cb], axis=1)
            for pr in range(2):
                p = g * 2 + pr
                j0 = d * SSD_HEADS + 2 * p
                j1 = j0 + 1
                xs = xc[rows, p * PAIR:(p + 1) * PAIR]
                ab0 = jnp.broadcast_to(a[:, j0:j0 + 1], (Q, Q))
                ab1 = jnp.broadcast_to(a[:, j1:j1 + 1], (Q, Q))
                seg = (jnp.concatenate([ab0, ab1], axis=1)
                       - jnp.concatenate([a_t[j0:j0 + 1, :], a_t[j1:j1 + 1, :]], axis=1))
                m = (cb2 * jnp.exp(jnp.where(mask2[d], seg, -jnp.inf))).astype(BF16)
                xdt = xs * dtp[:, p * PAIR:(p + 1) * PAIR]
                rhs = jnp.concatenate([jnp.where(lo_lane, xdt, 0.0).astype(BF16),
                                       jnp.where(lo_lane, 0.0, xdt).astype(BF16)], axis=0)
                y_diag = jnp.dot(m, rhs, preferred_element_type=F32)
                hp = hst[q, d, p]
                y_off = jnp.dot(cmb, hp.astype(BF16), preferred_element_type=F32)
                a_pair = jnp.where(lo_lane, ab0, ab1)
                y = y_diag + jnp.exp(a_pair) * y_off
                a_end_pair = jnp.where(lo_lane[0:1, :], a_end[:, j0:j0 + 1], a_end[:, j1:j1 + 1])
                xw = (xdt * jnp.exp(a_end_pair - a_pair)).astype(BF16)
                s_new = jnp.dot(bmt[c, g], xw, preferred_element_type=F32)
                decp = jnp.where(lo_lane[0:1, :], dec[:, j0:j0 + 1], dec[:, j1:j1 + 1])
                hst[q, d, p] = decp * hp + s_new
                if d == 0:
                    y = y + dvec_ref[:, p * PAIR:(p + 1) * PAIR] * xs
                yacc[d, rows, p * PAIR:(p + 1) * PAIR] = y

    def step(t, carry):
        for q in range(n_seq):
            chunk(q * nc + t, 0, q)
            chunk(q * nc + nc - 1 - t, 1, q)
        return carry

    lax.fori_loop(0, nc, step, 0, unroll=min(nc, SSD_UNROLL))
    for c in range(n_seq * nc):
        rows = slice(c * Q, (c + 1) * Q)
        y = (yacc[0, rows, :] + yacc[1, rows, :]) * _silu(z_ref[rows, :])
        y_ref[rows, :] = _rms(y) * ng_ref[...]
    if emit_state:
        for q in range(n_seq):
            for d in range(N_DIR):
                for p in range(N_PAIR):
                    hout_ref[q, d, p * PAIR:(p + 1) * PAIR, :] = hst[q, d, p].T


def _ssd_mixer(z, xbc, dtr, prm, l, *, L, n_seq, nb, blk0, h0=None, y_buf=None, st_buf=None, after=None,
               emit_state):
    R = n_seq * L
    seq = lambda w: pl.BlockSpec((R, w), lambda b: (blk0 + b, 0))
    st = pl.BlockSpec((n_seq, None, N_DIR, SSD_HP, SSD_STATE), lambda b: (b, l, 0, 0, 0))
    args = [z, xbc, dtr, prm["conv_w"], prm["conv_b"], prm["dt_bias"], prm["a_log"], prm["d_vec"],
            prm["norm_g"]]
    in_specs = [seq(SSD_INNER), seq(SSD_CONV_DIM), seq(LANES),
                _layer_spec(l, (SSD_CONV, SSD_CONV_DIM)), _layer_spec(l, (1, SSD_CONV_DIM)),
                _layer_spec(l, (1, LANES)), _layer_spec(l, (1, LANES)),
                _layer_spec(l, (1, SSD_INNER)), _layer_spec(l, (1, SSD_INNER))]
    if h0 is not None:
        args.append(h0)
        in_specs.append(st)
    out_shape = [jax.ShapeDtypeStruct((N_TOK, SSD_INNER), F32)]
    out_specs = [seq(SSD_INNER)]
    aliases = {}
    if y_buf is not None:
        aliases[len(args)] = 0
        args.append(y_buf)
        in_specs.append(_any_spec())
    if emit_state:
        out_shape.append(jax.ShapeDtypeStruct((BATCH, DEPTH, N_DIR, SSD_HP, SSD_STATE), F32))
        out_specs.append(st)
        if st_buf is not None:
            aliases[len(args)] = 1
            args.append(st_buf)
            in_specs.append(_any_spec())
    if after is not None:
        args.append(after)
        in_specs.append(_any_spec())
    res = pl.pallas_call(
        functools.partial(_ssd_kernel, L=L, n_seq=n_seq, has_h0=h0 is not None,
                          n_alias=len(aliases) + int(after is not None), emit_state=emit_state),
        grid=(nb,),
        in_specs=in_specs,
        out_specs=out_specs,
        out_shape=out_shape,
        input_output_aliases=aliases,
        scratch_shapes=[pltpu.VMEM((n_seq * (L + SUBLANES) + SUBLANES, SSD_CONV_DIM), F32),
                        pltpu.VMEM((R, SSD_CONV_DIM), F32),
                        pltpu.VMEM((R, LANES), F32),
                        pltpu.VMEM((R // CHUNK, N_DIR * SSD_HEADS, CHUNK), F32),
                        pltpu.VMEM((R // CHUNK, SSD_GROUPS, CHUNK, CHUNK), F32),
                        pltpu.VMEM((R // CHUNK, SSD_GROUPS, SSD_STATE, CHUNK), BF16),
                        pltpu.VMEM((R, LANES), F32),
                        pltpu.VMEM((N_DIR, R, SSD_INNER), F32),
                        pltpu.VMEM((n_seq, N_DIR, N_PAIR, PAIR, SSD_STATE), F32)],
        compiler_params=_params(),
        name=f"ssd_mixer_L{L}",
    )(*args)
    return res if emit_state else (res[0], None)


_LOG2_CH = S5_GROUP_CH.bit_length() - 1
_LOG2_ST = S5_STATE.bit_length() - 1


def _cmul(ar, ai, br, bi):
    return ar * br - ai * bi, ar * bi + ai * br


def _s5prep_kernel(lre_ref, lim_ref, ldt_ref, btr_ref, bti_ref, ctr_ref, cti_ref,
                   bblk_ref, cblk_ref, pw_ref):
    brow = lax.broadcasted_iota(jnp.int32, (S5_DIM, S5_LANES), 0) >> _LOG2_CH
    bcol = lax.broadcasted_iota(jnp.int32, (S5_DIM, S5_LANES), 1) >> _LOG2_ST
    bmask = brow == bcol
    p1 = lax.broadcasted_iota(jnp.int32, (S5_SEG, S5_LANES), 0) + 1
    for d in range(N_DIR):
        lre = lre_ref[d]
        lim = lim_ref[d]
        step = jnp.exp(ldt_ref[d])
        mag = jnp.exp(lre * step)
        lbr = mag * jnp.cos(lim * step)
        lbi = mag * jnp.sin(lim * step)
        den = lre * lre + lim * lim
        nr = lbr - 1.0
        cr = (nr * lre + lbi * lim) / den
        ci = (lbi * lre - nr * lim) / den
        br, bi = _cmul(cr, ci, btr_ref[...], bti_ref[...])
        br = jnp.where(bmask, br, 0.0).astype(BF16)
        bi = jnp.where(bmask, bi, 0.0).astype(BF16)
        for sl in range(S5_NSLAB):
            bblk_ref[d, sl, :, 0:S5_SLAB] = br[:, sl * S5_SLAB:(sl + 1) * S5_SLAB]
            bblk_ref[d, sl, :, S5_SLAB:2 * S5_SLAB] = bi[:, sl * S5_SLAB:(sl + 1) * S5_SLAB]
        rr = jnp.ones((S5_SEG, S5_LANES), F32)
        ri = jnp.zeros((S5_SEG, S5_LANES), F32)
        sr, si = lbr, lbi
        for k in range(S5_SEG.bit_length()):
            bit = ((p1 >> k) & 1) == 1
            tr, ti = _cmul(rr, ri, sr, si)
            rr = jnp.where(bit, tr, rr)
            ri = jnp.where(bit, ti, ri)
            sr, si = _cmul(sr, si, sr, si)
        pw_ref[d, 0, 0:S5_SEG, :] = rr
        pw_ref[d, 1, 0:S5_SEG, :] = ri
        n = S5_SEG
        while n < S5_PW_ROWS:
            tr, ti = _cmul(pw_ref[d, 0, 0:n, :], pw_ref[d, 1, 0:n, :],
                           pw_ref[d, 0, n - 1:n, :], pw_ref[d, 1, n - 1:n, :])
            pw_ref[d, 0, n:2 * n, :] = tr
            pw_ref[d, 1, n:2 * n, :] = ti
            n *= 2
    crow = lax.broadcasted_iota(jnp.int32, (S5_LANES, S5_DIM), 0) >> _LOG2_ST
    ccol = lax.broadcasted_iota(jnp.int32, (S5_LANES, S5_DIM), 1) >> _LOG2_CH
    cmask = crow == ccol
    cr = jnp.where(cmask, ctr_ref[...], 0.0).astype(BF16)
    ci = jnp.where(cmask, -cti_ref[...], 0.0).astype(BF16)
    for sl in range(S5_NSLAB):
        cblk_ref[sl, 0:S5_SLAB, :] = cr[sl * S5_SLAB:(sl + 1) * S5_SLAB, :]
        cblk_ref[sl, S5_SLAB:2 * S5_SLAB, :] = ci[sl * S5_SLAB:(sl + 1) * S5_SLAB, :]


def _s5_prep(lam_re, lam_im, log_dt, b_re, b_im, c_re, c_im):
    row = lambda t: t.reshape(DEPTH, N_DIR, 1, S5_LANES)
    ldt = jnp.repeat(log_dt, S5_STATE, axis=-1)
    bt = lambda t: jnp.tile(t.transpose(0, 3, 1, 2).reshape(DEPTH, S5_GROUP_CH, S5_LANES),
                            (1, S5_GROUPS, 1))
    ct = lambda t: jnp.tile(t.transpose(0, 1, 3, 2).reshape(DEPTH, S5_LANES, S5_GROUP_CH),
                            (1, 1, S5_GROUPS))
    vec = pl.BlockSpec((None, N_DIR, 1, S5_LANES), lambda l: (l, 0, 0, 0))
    bsp = pl.BlockSpec((None, S5_DIM, S5_LANES), lambda l: (l, 0, 0))
    csp = pl.BlockSpec((None, S5_LANES, S5_DIM), lambda l: (l, 0, 0))
    return pl.pallas_call(
        _s5prep_kernel,
        grid=(DEPTH,),
        in_specs=[vec, vec, vec, bsp, bsp, csp, csp],
        out_specs=[pl.BlockSpec((None, N_DIR, S5_NSLAB, S5_DIM, 2 * S5_SLAB), lambda l: (l, 0, 0, 0, 0)),
                   pl.BlockSpec((None, S5_NSLAB, 2 * S5_SLAB, S5_DIM), lambda l: (l, 0, 0, 0)),
                   pl.BlockSpec((None, N_DIR, 2, S5_PW_ROWS, S5_LANES), lambda l: (l, 0, 0, 0, 0))],
        out_shape=[jax.ShapeDtypeStruct((DEPTH, N_DIR, S5_NSLAB, S5_DIM, 2 * S5_SLAB), BF16),
                   jax.ShapeDtypeStruct((DEPTH, S5_NSLAB, 2 * S5_SLAB, S5_DIM), BF16),
                   jax.ShapeDtypeStruct((DEPTH, N_DIR, 2, S5_PW_ROWS, S5_LANES), F32)],
        compiler_params=_params(),
        name="s5_prep",
    )(row(lam_re), row(lam_im), row(ldt), bt(b_re), bt(b_im), ct(c_re), ct(c_im))


def _s5_moves(n_seq, n_seg, seq_len, col_major):
    steps = seq_len // n_seg
    if not col_major:
        return [(q * seq_len + s * steps, steps, q * n_seg + s, S5_SEG)
                for q in range(n_seq) for s in range(n_seg)]
    assert n_seq == 1 and n_seg == S5_SEG
    grid_rows = seq_len // GRID_W
    wseg = GRID_W // n_seg
    return [(r * GRID_W + s * wseg, wseg, r * S5_SEG + s, grid_rows * S5_SEG)
            for r in range(grid_rows) for s in range(n_seg)]


def _s5_kernel(*refs, n_seq, n_seg, seq_len, col_major, has_h0, n_alias, emit_state):
    u_ref, bblk_ref, cblk_ref, pw_ref, dvec_ref, gw_ref, gb_ref = refs[:7]
    k = 7
    h0r_ref, h0i_ref = (refs[k], refs[k + 1]) if has_h0 else (None, None)
    k += 2 * int(has_h0) + n_alias
    y_ref = refs[k]
    fr_ref, fi_ref = (refs[k + 1], refs[k + 2]) if emit_state else (None, None)
    up, buf_f, buf_b, yacc, yp, cin, fin = refs[k + 1 + 2 * int(emit_state):]
    assert n_seq * n_seg == S5_SEG and not (has_h0 and n_seg == 1) and not (emit_state and n_seg > 1)
    n_rows = n_seq * seq_len
    steps = seq_len // n_seg
    n_planes = S5_DIM // LANES
    W = S5_SLAB
    moves = _s5_moves(n_seq, n_seg, seq_len, col_major)

    for src, n, dst, stride in moves:
        for t in range(n_planes):
            up[t, pl.ds(dst, n, stride=stride), :] = u_ref[src:src + n, t * LANES:(t + 1) * LANES]

    def u_rows(rs):
        return jnp.concatenate([up[t, rs, :] for t in range(n_planes)], axis=1)

    zero = jnp.zeros((S5_SEG, W), F32)
    zrow = jnp.zeros((1, W), F32)
    for sl in range(S5_NSLAB):
        ln = slice(sl * W, (sl + 1) * W)
        for r0 in range(0, n_rows, ROW_SLAB):
            rs = slice(r0, r0 + ROW_SLAB)
            ub = u_rows(rs).astype(BF16)
            buf_f[rs, :] = jnp.dot(ub, bblk_ref[0, sl], preferred_element_type=F32)
            buf_b[rs, :] = jnp.dot(ub, bblk_ref[1, sl], preferred_element_type=F32)
        lam = [[jnp.broadcast_to(pw_ref[d, c, 0:1, ln], (S5_SEG, W)) for c in range(2)]
               for d in range(N_DIR)]

        def body(t, carry):
            fr, fi, br, bi = carry
            rf = pl.ds(pl.multiple_of(t * S5_SEG, S5_SEG), S5_SEG)
            rb = pl.ds(pl.multiple_of((steps - 1 - t) * S5_SEG, S5_SEG), S5_SEG)
            nfr = lam[0][0] * fr - lam[0][1] * fi + buf_f[rf, 0:W]
            nfi = lam[0][0] * fi + lam[0][1] * fr + buf_f[rf, W:2 * W]
            nbr = lam[1][0] * br - lam[1][1] * bi + buf_b[rb, 0:W]
            nbi = lam[1][0] * bi + lam[1][1] * br + buf_b[rb, W:2 * W]
            buf_f[rf, 0:W] = nfr
            buf_f[rf, W:2 * W] = nfi
            buf_b[rb, 0:W] = nbr
            buf_b[rb, W:2 * W] = nbi
            return nfr, nfi, nbr, nbi

        ends = lax.fori_loop(0, steps, body, (zero, zero, zero, zero), unroll=S5_UNROLL)

        if n_seg == 1:
            if emit_state:
                for d in range(N_DIR):
                    fin[d, 0, :, ln] = ends[2 * d]
                    fin[d, 1, :, ln] = ends[2 * d + 1]
        else:
            for d in range(N_DIR):
                er, ei = ends[2 * d], ends[2 * d + 1]
                lpr = pw_ref[d, 0, steps - 1:steps, ln]
                lpi = pw_ref[d, 1, steps - 1:steps, ln]
                pr = h0r_ref[d:d + 1, ln] if has_h0 else zrow
                pi = h0i_ref[d:d + 1, ln] if has_h0 else zrow
                order = range(n_seg) if d == 0 else range(n_seg - 1, -1, -1)
                for s in order:
                    cin[d, 0, s:s + 1, :] = pr
                    cin[d, 1, s:s + 1, :] = pi
                    tr, ti = _cmul(lpr, lpi, pr, pi)
                    pr = er[s:s + 1, :] + tr
                    pi = ei[s:s + 1, :] + ti
            cfr = cin[0, 0]
            cfi = cin[0, 1]
            cbr = cin[1, 0]
            cbi = cin[1, 1]

            def fix(p, carry):
                rows = pl.ds(pl.multiple_of(p * S5_SEG, S5_SEG), S5_SEG)
                q = steps - 1 - p
                tfr, tfi = _cmul(pw_ref[0, 0, pl.ds(p, 1), ln], pw_ref[0, 1, pl.ds(p, 1), ln], cfr, cfi)
                tbr, tbi = _cmul(pw_ref[1, 0, pl.ds(q, 1), ln], pw_ref[1, 1, pl.ds(q, 1), ln], cbr, cbi)
                buf_f[rows, 0:W] = buf_f[rows, 0:W] + buf_b[rows, 0:W] + tfr + tbr
                buf_f[rows, W:2 * W] = buf_f[rows, W:2 * W] + buf_b[rows, W:2 * W] + tfi + tbi
                return carry

            lax.fori_loop(0, steps, fix, 0, unroll=S5_UNROLL)

        for r0 in range(0, n_rows, ROW_SLAB):
            rs = slice(r0, r0 + ROW_SLAB)
            hs = buf_f[rs, :] if n_seg > 1 else buf_f[rs, :] + buf_b[rs, :]
            part = jnp.dot(hs.astype(BF16), cblk_ref[sl], preferred_element_type=F32)
            if sl == 0:
                yacc[rs, :] = part
            else:
                yacc[rs, :] = yacc[rs, :] + part

    gwb = gw_ref[...].astype(BF16)
    for r0 in range(0, n_rows, ROW_SLAB):
        rs = slice(r0, r0 + ROW_SLAB)
        y = _gelu_tanh(yacc[rs, :] + dvec_ref[...] * u_rows(rs))
        gate = jnp.dot(y.astype(BF16), gwb, preferred_element_type=F32) + gb_ref[...]
        y = y * jax.nn.sigmoid(gate)
        for t in range(n_planes):
            yp[t, rs, :] = y[:, t * LANES:(t + 1) * LANES]
    for src, n, dst, stride in moves:
        for t in range(n_planes):
            y_ref[src:src + n, t * LANES:(t + 1) * LANES] = yp[t, pl.ds(dst, n, stride=stride), :]
    if emit_state:
        for q in range(n_seq):
            for d in range(N_DIR):
                fr_ref[q, d:d + 1, :] = fin[d, 0, q:q + 1, :]
                fi_ref[q, d:d + 1, :] = fin[d, 1, q:q + 1, :]


def _s5_mixer(u, bblk, cblk, pw, prm, l, *, seq_len, n_seq, nb, blk0, col_major, h0=None, y_buf=None,
              st_buf=None, emit_state):
    n_rows = n_seq * seq_len
    n_seg = S5_SEG // n_seq
    seq = pl.BlockSpec((n_rows, S5_DIM), lambda b: (blk0 + b, 0))
    st = pl.BlockSpec((n_seq, None, N_DIR, S5_LANES), lambda b: (b, l, 0, 0))
    once = dict(pipeline_mode=pl.Buffered(1))
    args = [u, bblk, cblk, pw, prm["d_vec"], prm["glu_w"], prm["glu_b"]]
    in_specs = [seq, _layer_spec(l, (N_DIR, S5_NSLAB, S5_DIM, 2 * S5_SLAB), **once),
                _layer_spec(l, (S5_NSLAB, 2 * S5_SLAB, S5_DIM), **once),
                _layer_spec(l, (N_DIR, 2, S5_PW_ROWS if n_seg > 1 else S5_SEG, S5_LANES), **once),
                _layer_spec(l, (1, S5_DIM)),
                _layer_spec(l, (S5_DIM, S5_DIM)), _layer_spec(l, (1, S5_DIM))]
    if h0 is not None:
        h0_spec = pl.BlockSpec((None, None, N_DIR, S5_LANES), lambda b: (b, l, 0, 0))
        args += list(h0)
        in_specs += [h0_spec, h0_spec]
    out_shape = [jax.ShapeDtypeStruct((N_TOK, S5_DIM), F32)]
    out_specs = [seq]
    aliases = {}
    if y_buf is not None:
        aliases[len(args)] = 0
        args.append(y_buf)
        in_specs.append(_any_spec())
    if emit_state:
        out_shape += [jax.ShapeDtypeStruct((BATCH, DEPTH, N_DIR, S5_LANES), F32)] * 2
        out_specs += [st, st]
        if st_buf is not None:
            for j, buf in enumerate(st_buf):
                aliases[len(args)] = 1 + j
                args.append(buf)
                in_specs.append(_any_spec())
    res = pl.pallas_call(
        functools.partial(_s5_kernel, n_seq=n_seq, n_seg=n_seg, seq_len=seq_len, col_major=col_major,
                          has_h0=h0 is not None, n_alias=len(aliases), emit_state=emit_state),
        grid=(nb,),
        in_specs=in_specs,
        out_specs=out_specs,
        out_shape=out_shape,
        input_output_aliases=aliases,
        scratch_shapes=[pltpu.VMEM((S5_DIM // LANES, n_rows, LANES), F32),
                        pltpu.VMEM((n_rows, 2 * S5_SLAB), F32),
                        pltpu.VMEM((n_rows, 2 * S5_SLAB), F32),
                        pltpu.VMEM((n_rows, S5_DIM), F32),
                        pltpu.VMEM((S5_DIM // LANES, n_rows, LANES), F32),
                        pltpu.VMEM((N_DIR, 2, S5_SEG, S5_SLAB), F32),
                        pltpu.VMEM((N_DIR, 2, S5_SEG, S5_LANES), F32)],
        compiler_params=_params(),
        name=f"s5_mixer_L{seq_len}",
    )(*args)
    return (res[0], res[1], res[2]) if emit_state else (res[0], None, None)


def _outffn_kernel(xa_ref, xb_ref, ys_ref, yg_ref, y5_ref, g1_ref, sh_ref, sc_ref, g2_ref, ng_ref,
                   fg_ref, wo_ref, w1_ref, w2_ref, *rest, final):
    n_out = 2 if final else 1
    o_refs = rest[:n_out]
    wob, w1b, w2b = rest[n_out:]
    i = pl.program_id(0)

    @pl.when(i < FFN_PREP)
    def _():
        wob[i] = wo_ref[...].astype(BF16)
        w1b[i] = w1_ref[...].astype(BF16)
        w2b[i] = w2_ref[...].astype(BF16)

    @pl.when(i >= FFN_PREP)
    def _():
        t = i - FFN_PREP
        n_ctx = N_CTX // FFN_TILE
        r = _mod_row(t, FFN_TILE)
        row = lambda ref: ref[pl.ds(r, 1), :]
        x = jnp.where(t < n_ctx, xa_ref[...], xb_ref[...])
        mixed = jnp.concatenate([ys_ref[...].astype(BF16), yg_ref[...].astype(BF16),
                                 y5_ref[...].astype(BF16)], axis=1)
        w_out = wob[...].reshape(D_MODEL, D_MODEL)
        x1 = x + row(g1_ref) * jnp.dot(mixed, w_out, preferred_element_type=F32)
        h = ((_rms(x1) * ng_ref[...]) * (1.0 + row(sc_ref)) + row(sh_ref)).astype(BF16)
        acc = jnp.zeros((FFN_TILE, D_MODEL), F32)
        for j in range(FFN_PREP):
            a = jnp.maximum(jnp.dot(h, w1b[j], preferred_element_type=F32), 0.0)
            acc = acc + jnp.dot((a * a).astype(BF16), w2b[j], preferred_element_type=F32)
        x2 = x1 + row(g2_ref) * acc
        if not final:
            o_refs[0][...] = x2
        else:
            y = _rms(x2) * fg_ref[...]

            @pl.when(t < n_ctx)
            def _():
                o_refs[0][...] = y

            @pl.when(t >= n_ctx)
            def _():
                o_refs[1][...] = y


def _out_ffn(xa, xb, b_is_stream, y_ssd, y_sgu, y_s5, l, mods, norm_g, final_g, w_out, w1, w2, *, final):
    tile_of = lambda i: jnp.maximum(i - FFN_PREP, 0)
    tok = lambda w: pl.BlockSpec((FFN_TILE, w), lambda i: (tile_of(i), 0))
    chunk_of = lambda i: jnp.minimum(i, FFN_PREP - 1)
    sa, sb = _split_tok_specs(FFN_TILE, D_MODEL, b_is_stream, tile_of)
    if final:
        out_specs = list(_split_tok_specs(FFN_TILE, D_MODEL, False, tile_of))
        out_shape = [jax.ShapeDtypeStruct((N_CTX, D_MODEL), F32), jax.ShapeDtypeStruct((N_LAT, D_MODEL), F32)]
    else:
        out_specs = [tok(D_MODEL)]
        out_shape = [jax.ShapeDtypeStruct((N_TOK, D_MODEL), F32)]
    ko, kf = D_MODEL // FFN_PREP, D_FF // FFN_PREP
    return pl.pallas_call(
        functools.partial(_outffn_kernel, final=final),
        grid=(FFN_PREP + N_TOK // FFN_TILE,),
        in_specs=[sa, sb, tok(SSD_INNER), tok(SGU_DIM), tok(S5_DIM),
                  _mod_spec(l, 2), _mod_spec(l, 3), _mod_spec(l, 4), _mod_spec(l, 5),
                  _layer_spec(l, (1, D_MODEL)), pl.BlockSpec((1, D_MODEL), lambda i: (0, 0)),
                  pl.BlockSpec((None, ko, D_MODEL), lambda i: (l, chunk_of(i), 0)),
                  pl.BlockSpec((None, D_MODEL, kf), lambda i: (l, 0, chunk_of(i))),
                  pl.BlockSpec((None, kf, D_MODEL), lambda i: (l, chunk_of(i), 0))],
        out_specs=out_specs,
        out_shape=out_shape,
        scratch_shapes=[pltpu.VMEM((FFN_PREP, ko, D_MODEL), BF16),
                        pltpu.VMEM((FFN_PREP, D_MODEL, kf), BF16),
                        pltpu.VMEM((FFN_PREP, kf, D_MODEL), BF16)],
        compiler_params=_params(),
        name="out_ffn_final" if final else "out_ffn",
    )(xa, xb, y_ssd, y_sgu, y_s5, mods, mods, mods, mods, norm_g, final_g, w_out, w1, w2)


def kernel(x_prompt, x_sample, state_ssd, state_s5_re, state_s5_im, c, c_ctx, ada_w, ada_b, norm1_g,
           norm2_g, w_in, ssd_conv_w, ssd_conv_b, ssd_dt_bias, ssd_a_log, ssd_d, ssd_norm_g,
           sgu_norm_g, sgu_w, sgu_b, s5_lambda_re, s5_lambda_im, s5_log_dt, s5_b_re, s5_b_im,
           s5_c_re, s5_c_im, s5_d, s5_glu_w, s5_glu_b, w_out, ffn_w1, ffn_w2, final_norm_g):
    cvec = jnp.concatenate([c_ctx[None, :], c,
                            jnp.zeros((MOD_ROWS - 1 - DEC_BATCH, D_MODEL), F32)], axis=0)
    mods = _adaln_mods(cvec, ada_w, ada_b)
    bblk, cblk, pw = _s5_prep(s5_lambda_re, s5_lambda_im, s5_log_dt, s5_b_re, s5_b_im,
                              s5_c_re, s5_c_im)

    w_in_t = w_in.transpose(0, 2, 1)
    vec =lambda t: t.reshape(DEPTH, 1, -1)
    lane_pad = lambda t: jnp.pad(vec(t), ((0, 0), (0, 0), (0, LANES - t[0].size)))
    ssd_prm = dict(conv_w=ssd_conv_w, conv_b=vec(ssd_conv_b), dt_bias=lane_pad(ssd_dt_bias),
                   a_log=lane_pad(ssd_a_log), d_vec=vec(jnp.repeat(ssd_d, SSD_HEAD_DIM, axis=-1)),
                   norm_g=vec(ssd_norm_g))
    s5_prm = dict(d_vec=vec(s5_d), glu_w=s5_glu_w, glu_b=vec(s5_glu_b))
    sgu_w_pair = sgu_w.reshape(DEPTH, SGU_HEADS // 2, 2, CHUNK, CHUNK).transpose(0, 1, 3, 2, 4)
    sgu_w_pair = sgu_w_pair.reshape(DEPTH, SGU_HEADS // 2, CHUNK, 2 * CHUNK).astype(BF16)
    sgu_b_full = jnp.repeat(sgu_b.transpose(0, 2, 1), SGU_DIM // SGU_HEADS, axis=2)
    norm1 = vec(norm1_g)
    norm2 = vec(norm2_g)
    sgu_g = vec(sgu_norm_g)
    final_g = final_norm_g.reshape(1, D_MODEL)

    lat_ssd = state_ssd.reshape(DEC_BATCH, DEPTH, N_DIR, SSD_HP, SSD_STATE)
    lat_s5 = (state_s5_re.reshape(DEC_BATCH, DEPTH, N_DIR, S5_LANES),
              state_s5_im.reshape(DEC_BATCH, DEPTH, N_DIR, S5_LANES))
    lat_blk = N_CTX // DEC_SEQ

    xa = x_prompt.reshape(N_CTX, D_MODEL)
    xb = x_sample.reshape(N_LAT, D_MODEL)
    b_is_stream = False
    st_ssd = None
    st_s5 = None
    for l in range(DEPTH):
        z, xbc, y_sgu, s5u, dtr = _in_proj(xa, xb, b_is_stream, l, mods, norm1, w_in_t, sgu_g,
                                           sgu_w_pair, sgu_b_full)
        y_s5, fr, fi = _s5_mixer(s5u, bblk, cblk, pw, s5_prm, l, seq_len=SEQ, n_seq=S5_SEG,
                                 nb=BATCH // S5_SEG, blk0=0, col_major=False, st_buf=st_s5,
                                 emit_state=True)
        st_s5 = (fr, fi)
        y_s5, _, _ = _s5_mixer(s5u, bblk, cblk, pw, s5_prm, l, seq_len=DEC_SEQ, n_seq=1, nb=DEC_BATCH,
                               blk0=lat_blk, col_major=True, h0=lat_s5, y_buf=y_s5, emit_state=False)
        y_ssd, st_ssd = _ssd_mixer(z, xbc, dtr, ssd_prm, l, L=SEQ, n_seq=SSD_CTX_SEQS,
                                   nb=BATCH // SSD_CTX_SEQS, blk0=0, st_buf=st_ssd, after=y_s5,
                                   emit_state=True)
        y_ssd, _ = _ssd_mixer(z, xbc, dtr, ssd_prm, l, L=DEC_SEQ, n_seq=1, nb=DEC_BATCH, blk0=lat_blk,
                              h0=lat_ssd, y_buf=y_ssd, emit_state=False)
        final = l == DEPTH - 1
        out = _out_ffn(xa, xb, b_is_stream, y_ssd, y_sgu, y_s5, l, mods, norm2, final_g, w_out, ffn_w1,
                       ffn_w2, final=final)
        if not final:
            xa = xb = out[0]
            b_is_stream = True

    y_prompt = out[0].reshape(BATCH, SEQ, D_MODEL)
    y_sample = out[1].reshape(DEC_BATCH, DEC_SEQ, D_MODEL)
    new_state_ssd = st_ssd.reshape(BATCH, DEPTH, N_DIR, SSD_HEADS, SSD_HEAD_DIM, SSD_STATE)
    s5_shape = (BATCH, DEPTH, N_DIR, S5_GROUPS, S5_STATE)
    return (y_prompt, y_sample, new_state_ssd, st_s5[0].reshape(s5_shape), st_s5[1].reshape(s5_shape))
```

```python
import functools
import math

import jax
import jax.numpy as jnp
from jax import lax
from jax.experimental import pallas as pl
from jax.experimental.pallas import tpu as pltpu

F32 = jnp.float32
BF16 = jnp.bfloat16

D_MODEL = 1024
BATCH = 16
SEQ = 256
DEPTH = 2
DEC_BATCH = 2
DEC_SEQ = 1024
GRID_W = 64
CHUNK = 128
N_DIR = 2
EPS = 1e-6
SSD_INNER = 512
SSD_HEAD_DIM = 64
SSD_HEADS = 8
SSD_GROUPS = 2
SSD_STATE = 128
SSD_CONV = 5
SSD_CONV_DIM = SSD_INNER + 2 * SSD_GROUPS * SSD_STATE
SGU_DIM = 256
SGU_HEADS = 4
S5_DIM = 256
S5_GROUP_CH = 16
S5_GROUPS = 16
S5_STATE = 64
D_FF = 4 * D_MODEL
OFF_XBC = SSD_INNER
OFF_DT = OFF_XBC + SSD_CONV_DIM
OFF_SGU = OFF_DT + N_DIR * SSD_HEADS
OFF_S5 = OFF_SGU + 2 * SGU_DIM
IN_DIM = OFF_S5 + S5_DIM

N_CTX = BATCH * SEQ
N_LAT = DEC_BATCH * DEC_SEQ
N_TOK = N_CTX + N_LAT
LANES = 128
SUBLANES = 8
MOD_ROWS = SUBLANES
SSD_HP = SSD_HEADS * SSD_HEAD_DIM
S5_LANES = S5_GROUPS * S5_STATE
S5_SEG = SUBLANES
S5_PW_ROWS = DEC_SEQ // S5_SEG
S5_SLAB = 256
S5_NSLAB = S5_LANES // S5_SLAB
S5_UNROLL = 16
PAIR = 2 * SSD_HEAD_DIM
N_PAIR = SSD_HEADS // 2
TOK_TILE = 512
N_CTX_TILES = N_CTX // TOK_TILE
FFN_TILE = 512
FFN_PREP = 8
ROW_SLAB = 256
CONV_LANES = 256
SSD_CTX_SEQS = 2
SSD_UNROLL = 4
VMEM_LIMIT = 56 * 1024 * 1024

_NT = (((1,), (1,)), ((), ()))


def _params(n_axes=1):
    return pltpu.CompilerParams(dimension_semantics=("arbitrary",) * n_axes,
                                vmem_limit_bytes=VMEM_LIMIT)


def _layer_spec(l, shape, **kw):
    return pl.BlockSpec((None,) + tuple(shape), lambda *_: (l,) + (0,) * len(shape), **kw)


def _mod_spec(l, k):
    return pl.BlockSpec((None, MOD_ROWS, D_MODEL), lambda *_: (l, 0, k))


def _rows_spec(n):
    return pl.BlockSpec((DEPTH, n), lambda *_: (0, 0))


def _any_spec():
    return pl.BlockSpec(memory_space=pl.ANY)


def _mod_row(i, tm):
    n_ctx = N_CTX // tm
    return jnp.where(i < n_ctx, 0, 1 + (i - n_ctx) // (DEC_SEQ // tm))


def _split_tok_specs(tm, width, b_has_ctx_rows, tile_of=lambda i: i):
    n_ctx = N_CTX // tm
    b_off = n_ctx if b_has_ctx_rows else 0
    a = pl.BlockSpec((tm, width), lambda i: (jnp.minimum(tile_of(i), n_ctx - 1), 0))
    b = pl.BlockSpec((tm, width), lambda i: (jnp.maximum(tile_of(i), n_ctx) - n_ctx + b_off, 0))
    return a, b


def _silu(x):
    return x * jax.nn.sigmoid(x)


def _gelu_tanh(x):
    c = math.sqrt(2.0 / math.pi)
    return 0.5 * x * (1.0 + jnp.tanh(c * (x + 0.044715 * (x * x * x))))


def _rms(x):
    return x * lax.rsqrt(jnp.mean(x * x, axis=-1, keepdims=True) + EPS)


def _bdot(a, b):
    return jnp.dot(a.astype(BF16), b.astype(BF16), preferred_element_type=F32)


def _split3(x):
    hi = x.astype(BF16)
    r = x - hi.astype(F32)
    mid = r.astype(BF16)
    lo = (r - mid.astype(F32)).astype(BF16)
    return jnp.concatenate([hi, mid, lo], axis=1)


def _mod_kernel(c_ref, w_ref, b_ref, o_ref):
    bias = b_ref[pl.ds(pl.program_id(0), 1), :]
    o_ref[...] = _bdot(_silu(c_ref[...]), w_ref[...]) + bias


def _adaln_mods(cvec, ada_w, ada_b):
    n_blk = 6
    return pl.pallas_call(
        _mod_kernel,
        grid=(DEPTH, n_blk),
        in_specs=[pl.BlockSpec((MOD_ROWS, D_MODEL), lambda l, j: (0, 0)),
                  pl.BlockSpec((None, D_MODEL, D_MODEL), lambda l, j: (l, 0, j)),
                  pl.BlockSpec((DEPTH, D_MODEL), lambda l, j: (0, j))],
        out_specs=pl.BlockSpec((None, MOD_ROWS, D_MODEL), lambda l, j: (l, 0, j)),
        out_shape=jax.ShapeDtypeStruct((DEPTH, MOD_ROWS, 6 * D_MODEL), F32),
        compiler_params=_params(2),
        name="adaln_mod",
    )(cvec, ada_w, ada_b)


_C_Z = 0
_C_XBC = _C_Z + SSD_INNER
_C_SGU = _C_XBC + SSD_CONV_DIM
_C_S5 = _C_SGU + 2 * SGU_DIM
_C_DT = _C_S5 + S5_DIM
_C_END = _C_DT + LANES


def _inproj_kernel(xa_ref, xb_ref, sh_ref, sc_ref, g_ref, win_ref, sg_ref, sw_ref, sb_ref,
                   z_ref, xbc_ref, ysgu_ref, s5_ref, dt_ref, w_ref, *, l):
    i = pl.program_id(0)

    @pl.when(i == 0)
    def _():
        moves = ((0, OFF_DT, _C_Z), (OFF_SGU, IN_DIM, _C_SGU), (OFF_DT, OFF_SGU, _C_DT))
        for src0, src1, dst0 in moves:
            for r0 in range(src0, src1, ROW_SLAB):
                n = min(ROW_SLAB, src1 - r0)
                w_ref[dst0 + r0 - src0:dst0 + r0 - src0 + n, :] = win_ref[r0:r0 + n, :].astype(BF16)
        n_pad = LANES - N_DIR * SSD_HEADS
        w_ref[_C_END - n_pad:_C_END, :] = jnp.zeros((n_pad, D_MODEL), BF16)

    r = _mod_row(i, TOK_TILE)
    x = jnp.where(i < N_CTX_TILES, xa_ref[...], xb_ref[...])
    shift = sh_ref[pl.ds(r, 1), :]
    scale = sc_ref[pl.ds(r, 1), :]
    h = (_rms(x) * g_ref[l:l + 1, :]) * (1.0 + scale) + shift
    hb = h.astype(BF16)
    proj = lambda c0, c1: lax.dot_general(hb, w_ref[c0:c1, :], _NT, preferred_element_type=F32)
    uv = _gelu_tanh(proj(_C_SGU, _C_S5))
    z_ref[...] = proj(_C_Z, _C_XBC)
    xbc_ref[...] = proj(_C_XBC, _C_SGU)
    s5_ref[...] = proj(_C_S5, _C_DT)
    dt_ref[...] = proj(_C_DT, _C_END)

    u = uv[:, :SGU_DIM]
    v = _rms(uv[:, SGU_DIM:]) * sg_ref[l:l + 1, :]
    lo_lane = lax.broadcasted_iota(jnp.int32, (CHUNK, LANES), 1) < (LANES // 2)
    for c in range(TOK_TILE // CHUNK):
        rows = slice(c * CHUNK, (c + 1) * CHUNK)
        mix = []
        for pr in range(SGU_HEADS // 2):
            vp = v[rows, pr * LANES:(pr + 1) * LANES]
            rhs = jnp.concatenate([jnp.where(lo_lane, vp, 0.0).astype(BF16),
                                   jnp.where(lo_lane, 0.0, vp).astype(BF16)], axis=0)
            mix.append(jnp.dot(sw_ref[pr], rhs, preferred_element_type=F32))
        ysgu_ref[rows, :] = u[rows, :] * (jnp.concatenate(mix, axis=1) + sb_ref[...])


def _in_proj(xa, xb, b_is_stream, l, mods, norm_g, w_in, sgu_g, sgu_w_pair, sgu_b_full):
    tok = lambda w: pl.BlockSpec((TOK_TILE, w), lambda i: (i, 0))
    widths = (SSD_INNER, SSD_CONV_DIM, SGU_DIM, S5_DIM, LANES)
    sa, sb = _split_tok_specs(TOK_TILE, D_MODEL, b_is_stream)
    return pl.pallas_call(
        functools.partial(_inproj_kernel, l=l),
        grid=(N_TOK // TOK_TILE,),
        in_specs=[sa, sb, _mod_spec(l, 0), _mod_spec(l, 1), _rows_spec(D_MODEL),
                  _layer_spec(l, (IN_DIM, D_MODEL), pipeline_mode=pl.Buffered(1)),
                  _rows_spec(SGU_DIM), _layer_spec(l, (SGU_HEADS // 2, CHUNK, 2 * CHUNK)),
                  _layer_spec(l, (CHUNK, SGU_DIM))],
        out_specs=[tok(w) for w in widths],
        out_shape=[jax.ShapeDtypeStruct((N_TOK, w), F32) for w in widths],
        scratch_shapes=[pltpu.VMEM((_C_END, D_MODEL), BF16)],
        compiler_params=_params(),
        name="in_proj",
    )(xa, xb, mods, mods, norm_g, w_in, sgu_g, sgu_w_pair, sgu_b_full)


def _ssd_kernel(*refs, l, L, n_seq, has_h0, n_alias, emit_state):
    z_ref, xbc_ref, dt_ref, cw_ref, cb_ref, dtb_ref, alog_ref, dvec_ref, ng_ref = refs[:9]
    lrow = slice(l, l + 1)
    lane_fill = jnp.zeros((1, LANES - N_DIR * SSD_HEADS), F32)
    k = 9
    h0_ref = refs[k] if has_h0 else None
    k += int(has_h0) + n_alias
    y_ref = refs[k]
    hout_ref = refs[k + 1] if emit_state else None
    xpad, xc, acol, atr, cbs, bmt, dtsp, yacc, hst = refs[k + 1 + int(emit_state):]
    Q = CHUNK
    nc = L // Q
    halo = SUBLANES
    pad = (SSD_CONV - 1) // 2

    pitch = L + halo
    for q in range(n_seq + 1):
        xpad[q * pitch:q * pitch + halo, :] = jnp.zeros((halo, SSD_CONV_DIM), F32)
    for g in range(n_seq * nc):
        o = halo + (g // nc) * pitch + (g % nc) * Q
        xpad[o:o + Q, :] = xbc_ref[g * Q:(g + 1) * Q, :]
    win = Q + 2 * halo
    for c in range(n_seq * nc):
        for lb in range(0, SSD_CONV_DIM, CONV_LANES):
            ln = slice(lb, lb + CONV_LANES)
            o = (c // nc) * pitch + (c % nc) * Q
            xa = xpad[o:o + win, ln]
            acc = cb_ref[lrow, ln] + cw_ref[pad:pad + 1, ln] * xa[halo:halo + Q, :]
            for t in range(SSD_CONV):
                if t != pad:
                    rolled = pltpu.roll(xa, (pad - t) % win, 0)
                    acc = acc + cw_ref[t:t + 1, ln] * rolled[halo:halo + Q, :]
            xc[c * Q:(c + 1) * Q, ln] = _silu(acc)

    raw = dt_ref[...] + jnp.concatenate([dtb_ref[lrow, :], lane_fill], axis=1)
    dt = jnp.maximum(raw, 0.0) + jnp.log(1.0 + jnp.exp(-jnp.abs(raw)))
    dtsp[...] = dt
    a_neg = -jnp.exp(jnp.concatenate([alog_ref[lrow, :], lane_fill], axis=1))
    row = lax.broadcasted_iota(jnp.int32, (Q, Q), 0)
    col = lax.broadcasted_iota(jnp.int32, (Q, Q), 1)
    lower = row >= col
    upper = col >= row
    tri3 = jnp.concatenate([lower.astype(BF16)] * 3, axis=1)
    fwd_lane = lax.broadcasted_iota(jnp.int32, (Q, LANES), 1) < SSD_HEADS
    for c in range(n_seq * nc):
        dta = dtsp[c * Q:(c + 1) * Q, :] * a_neg
        hi = dta.astype(BF16)
        rem = dta - hi.astype(F32)
        mid = rem.astype(BF16)
        lo = (rem - mid.astype(F32)).astype(BF16)
        pre = jnp.dot(tri3, jnp.concatenate([hi, mid, lo], axis=0), preferred_element_type=F32)
        suf = pre[Q - 1:Q, :] - pre + dta
        a = jnp.where(fwd_lane, pre, suf)
        acol[c * Q:(c + 1) * Q, :] = a
        atr[c] = a.T[0:N_DIR * SSD_HEADS, :]
        for g in range(SSD_GROUPS):
            b0 = SSD_INNER + g * SSD_STATE
            c0 = SSD_INNER + (SSD_GROUPS + g) * SSD_STATE
            bm = xc[c * Q:(c + 1) * Q, b0:b0 + SSD_STATE]
            cbs[c, g] = lax.dot_general(xc[c * Q:(c + 1) * Q, c0:c0 + SSD_STATE].astype(BF16),
                                        bm.astype(BF16), _NT, preferred_element_type=F32)
            bmt[c, g] = bm.T.astype(BF16)

    sel_row = lax.broadcasted_iota(jnp.int32, (3 * LANES, SSD_INNER), 0) & (LANES - 1)
    sel_head = lax.broadcasted_iota(jnp.int32, (3 * LANES, SSD_INNER), 1) >> (SSD_HEAD_DIM.bit_length() - 1)
    sel = [(sel_row == sel_head + d * SSD_HEADS).astype(BF16) for d in range(N_DIR)]
    mask2 = [jnp.concatenate([m, m], axis=1) for m in (lower, upper)]

    for q in range(n_seq):
        for d in range(N_DIR):
            for p in range(N_PAIR):
                if has_h0:
                    hst[q, d, p] = h0_ref[q, d, p * PAIR:(p + 1) * PAIR, :].T
                else:
                    hst[q, d, p] = jnp.zeros((SSD_STATE, PAIR), F32)

    lo_lane = lax.broadcasted_iota(jnp.int32, (Q, PAIR), 1) < SSD_HEAD_DIM

    def chunk(c, d, q):
        r0 = pl.multiple_of(c * Q, Q)
        rows = pl.ds(r0, Q)
        a = acol[rows, :]
        a_t = atr[c]
        dtp = jnp.dot(_split3(dtsp[rows, :]), sel[d], preferred_element_type=F32)
        a_end = a[Q - 1:Q, :] if d == 0 else a[0:1, :]
        dec = jnp.exp(a_end)
        for g in range(SSD_GROUPS):
            c0 = SSD_INNER + (SSD_GROUPS + g) * SSD_STATE
            cmb = xc[rows, c0:c0 + SSD_STATE].astype(BF16)
            cb = cbs[c, g]
            cb2 = jnp.concatenate([cb, cb], axis=1)
            for pr in range(2):
                p = g * 2 + pr
                j0 = d * SSD_HEADS + 2 * p
                j1 = j0 + 1
                xs = xc[rows, p * PAIR:(p + 1) * PAIR]
                ab0 = jnp.broadcast_to(a[:, j0:j0 + 1], (Q, Q))
                ab1 = jnp.broadcast_to(a[:, j1:j1 + 1], (Q, Q))
                seg = (jnp.concatenate([ab0, ab1], axis=1)
                       - jnp.concatenate([a_t[j0:j0 + 1, :], a_t[j1:j1 + 1, :]], axis=1))
                m = (cb2 * jnp.exp(jnp.where(mask2[d], seg, -jnp.inf))).astype(BF16)
                xdt = xs * dtp[:, p * PAIR:(p + 1) * PAIR]
                rhs = jnp.concatenate([jnp.where(lo_lane, xdt, 0.0).astype(BF16),
                                       jnp.where(lo_lane, 0.0, xdt).astype(BF16)], axis=0)
                y_diag = jnp.dot(m, rhs, preferred_element_type=F32)
                hp = hst[q, d, p]
                y_off = jnp.dot(cmb, hp.astype(BF16), preferred_element_type=F32)
                a_pair = jnp.where(lo_lane, ab0, ab1)
                y = y_diag + jnp.exp(a_pair) * y_off
                a_end_pair = jnp.where(lo_lane[0:1, :], a_end[:, j0:j0 + 1], a_end[:, j1:j1 + 1])
                xw = (xdt * jnp.exp(a_end_pair - a_pair)).astype(BF16)
                s_new = jnp.dot(bmt[c, g], xw, preferred_element_type=F32)
                decp = jnp.where(lo_lane[0:1, :], dec[:, j0:j0 + 1], dec[:, j1:j1 + 1])
                hst[q, d, p] = decp * hp + s_new
                if d == 0:
                    y = y + dvec_ref[lrow, p * PAIR:(p + 1) * PAIR] * xs
                yacc[d, rows, p * PAIR:(p + 1) * PAIR] = y

    def step(t, carry):
        for q in range(n_seq):
            chunk(q * nc + t, 0, q)
            chunk(q * nc + nc - 1 - t, 1, q)
        return carry

    lax.fori_loop(0, nc, step, 0, unroll=min(nc, SSD_UNROLL))
    for c in range(n_seq * nc):
        rows = slice(c * Q, (c + 1) * Q)
        y = (yacc[0, rows, :] + yacc[1, rows, :]) * _silu(z_ref[rows, :])
        y_ref[rows, :] = _rms(y) * ng_ref[lrow, :]
    if emit_state:
        for q in range(n_seq):
            for d in range(N_DIR):
                for p in range(N_PAIR):
                    hout_ref[q, d, p * PAIR:(p + 1) * PAIR, :] = hst[q, d, p].T


def _ssd_mixer(z, xbc, dtr, prm, l, *, L, n_seq, nb, blk0, h0=None, y_buf=None, st_buf=None, after=None,
               emit_state):
    R = n_seq * L
    seq = lambda w: pl.BlockSpec((R, w), lambda b: (blk0 + b, 0))
    st = pl.BlockSpec((n_seq, None, N_DIR, SSD_HP, SSD_STATE), lambda b: (b, l, 0, 0, 0))
    args = [z, xbc, dtr, prm["conv_w"], prm["conv_b"], prm["dt_bias"], prm["a_log"], prm["d_vec"],
            prm["norm_g"]]
    in_specs = [seq(SSD_INNER), seq(SSD_CONV_DIM), seq(LANES),
                _layer_spec(l, (SSD_CONV, SSD_CONV_DIM)), _rows_spec(SSD_CONV_DIM),
                _rows_spec(N_DIR * SSD_HEADS), _rows_spec(N_DIR * SSD_HEADS),
                _rows_spec(SSD_INNER), _rows_spec(SSD_INNER)]
    if h0 is not None:
        args.append(h0)
        in_specs.append(st)
    out_shape = [jax.ShapeDtypeStruct((N_TOK, SSD_INNER), F32)]
    out_specs = [seq(SSD_INNER)]
    aliases = {}
    if y_buf is not None:
        aliases[len(args)] = 0
        args.append(y_buf)
        in_specs.append(_any_spec())
    if emit_state:
        out_shape.append(jax.ShapeDtypeStruct((BATCH, DEPTH, N_DIR, SSD_HP, SSD_STATE), F32))
        out_specs.append(st)
        if st_buf is not None:
            aliases[len(args)] = 1
            args.append(st_buf)
            in_specs.append(_any_spec())
    if after is not None:
        args.append(after)
        in_specs.append(_any_spec())
    res = pl.pallas_call(
        functools.partial(_ssd_kernel, l=l, L=L, n_seq=n_seq, has_h0=h0 is not None,
                          n_alias=len(aliases) + int(after is not None), emit_state=emit_state),
        grid=(nb,),
        in_specs=in_specs,
        out_specs=out_specs,
        out_shape=out_shape,
        input_output_aliases=aliases,
        scratch_shapes=[pltpu.VMEM((n_seq * (L + SUBLANES) + SUBLANES, SSD_CONV_DIM), F32),
                        pltpu.VMEM((R, SSD_CONV_DIM), F32),
                        pltpu.VMEM((R, LANES), F32),
                        pltpu.VMEM((R // CHUNK, N_DIR * SSD_HEADS, CHUNK), F32),
                        pltpu.VMEM((R // CHUNK, SSD_GROUPS, CHUNK, CHUNK), F32),
                        pltpu.VMEM((R // CHUNK, SSD_GROUPS, SSD_STATE, CHUNK), BF16),
                        pltpu.VMEM((R, LANES), F32),
                        pltpu.VMEM((N_DIR, R, SSD_INNER), F32),
                        pltpu.VMEM((n_seq, N_DIR, N_PAIR, PAIR, SSD_STATE), F32)],
        compiler_params=_params(),
        name=f"ssd_mixer_L{L}",
    )(*args)
    return res if emit_state else (res[0], None)


_LOG2_CH = S5_GROUP_CH.bit_length() - 1
_LOG2_ST = S5_STATE.bit_length() - 1


def _cmul(ar, ai, br, bi):
    return ar * br - ai * bi, ar * bi + ai * br


def _s5prep_kernel(lre_ref, lim_ref, ldt_ref, btr_ref, bti_ref, ctr_ref, cti_ref,
                   bblk_ref, cblk_ref, pw_ref):
    brow = lax.broadcasted_iota(jnp.int32, (S5_DIM, S5_LANES), 0) >> _LOG2_CH
    bcol = lax.broadcasted_iota(jnp.int32, (S5_DIM, S5_LANES), 1) >> _LOG2_ST
    bmask = brow == bcol
    p1 = lax.broadcasted_iota(jnp.int32, (S5_SEG, S5_LANES), 0) + 1
    for d in range(N_DIR):
        lre = lre_ref[d]
        lim = lim_ref[d]
        step = jnp.exp(ldt_ref[d])
        mag = jnp.exp(lre * step)
        lbr = mag * jnp.cos(lim * step)
        lbi = mag * jnp.sin(lim * step)
        den = lre * lre + lim * lim
        nr = lbr - 1.0
        cr = (nr * lre + lbi * lim) / den
        ci = (lbi * lre - nr * lim) / den
        br, bi = _cmul(cr, ci, btr_ref[...], bti_ref[...])
        br = jnp.where(bmask, br, 0.0).astype(BF16)
        bi = jnp.where(bmask, bi, 0.0).astype(BF16)
        for sl in range(S5_NSLAB):
            bblk_ref[d, sl, :, 0:S5_SLAB] = br[:, sl * S5_SLAB:(sl + 1) * S5_SLAB]
            bblk_ref[d, sl, :, S5_SLAB:2 * S5_SLAB] = bi[:, sl * S5_SLAB:(sl + 1) * S5_SLAB]
        rr = jnp.ones((S5_SEG, S5_LANES), F32)
        ri = jnp.zeros((S5_SEG, S5_LANES), F32)
        sr, si = lbr, lbi
        for k in range(S5_SEG.bit_length()):
            bit = ((p1 >> k) & 1) == 1
            tr, ti = _cmul(rr, ri, sr, si)
            rr = jnp.where(bit, tr, rr)
            ri = jnp.where(bit, ti, ri)
            sr, si = _cmul(sr, si, sr, si)
        pw_ref[d, 0, 0:S5_SEG, :] = rr
        pw_ref[d, 1, 0:S5_SEG, :] = ri
        n = S5_SEG
        while n < S5_PW_ROWS:
            tr, ti = _cmul(pw_ref[d, 0, 0:n, :], pw_ref[d, 1, 0:n, :],
                           pw_ref[d, 0, n - 1:n, :], pw_ref[d, 1, n - 1:n, :])
            pw_ref[d, 0, n:2 * n, :] = tr
            pw_ref[d, 1, n:2 * n, :] = ti
            n *= 2
    crow = lax.broadcasted_iota(jnp.int32, (S5_LANES, S5_DIM), 0) >> _LOG2_ST
    ccol = lax.broadcasted_iota(jnp.int32, (S5_LANES, S5_DIM), 1) >> _LOG2_CH
    cmask = crow == ccol
    cr = jnp.where(cmask, ctr_ref[...], 0.0).astype(BF16)
    ci = jnp.where(cmask, -cti_ref[...], 0.0).astype(BF16)
    for sl in range(S5_NSLAB):
        cblk_ref[sl, 0:S5_SLAB, :] = cr[sl * S5_SLAB:(sl + 1) * S5_SLAB, :]
        cblk_ref[sl, S5_SLAB:2 * S5_SLAB, :] = ci[sl * S5_SLAB:(sl + 1) * S5_SLAB, :]


def _s5_prep(lam_re, lam_im, log_dt, b_re, b_im, c_re, c_im):
    row = lambda t: t.reshape(DEPTH, N_DIR, 1, S5_LANES)
    ldt = jnp.repeat(log_dt, S5_STATE, axis=-1)
    bt = lambda t: jnp.tile(t.transpose(0, 3, 1, 2).reshape(DEPTH, S5_GROUP_CH, S5_LANES),
                            (1, S5_GROUPS, 1))
    ct = lambda t: jnp.tile(t.transpose(0, 1, 3, 2).reshape(DEPTH, S5_LANES, S5_GROUP_CH),
                            (1, 1, S5_GROUPS))
    vec = pl.BlockSpec((None, N_DIR, 1, S5_LANES), lambda l: (l, 0, 0, 0))
    bsp = pl.BlockSpec((None, S5_DIM, S5_LANES), lambda l: (l, 0, 0))
    csp = pl.BlockSpec((None, S5_LANES, S5_DIM), lambda l: (l, 0, 0))
    return pl.pallas_call(
        _s5prep_kernel,
        grid=(DEPTH,),
        in_specs=[vec, vec, vec, bsp, bsp, csp, csp],
        out_specs=[pl.BlockSpec((None, N_DIR, S5_NSLAB, S5_DIM, 2 * S5_SLAB), lambda l: (l, 0, 0, 0, 0)),
                   pl.BlockSpec((None, S5_NSLAB, 2 * S5_SLAB, S5_DIM), lambda l: (l, 0, 0, 0)),
                   pl.BlockSpec((None, N_DIR, 2, S5_PW_ROWS, S5_LANES), lambda l: (l, 0, 0, 0, 0))],
        out_shape=[jax.ShapeDtypeStruct((DEPTH, N_DIR, S5_NSLAB, S5_DIM, 2 * S5_SLAB), BF16),
                   jax.ShapeDtypeStruct((DEPTH, S5_NSLAB, 2 * S5_SLAB, S5_DIM), BF16),
                   jax.ShapeDtypeStruct((DEPTH, N_DIR, 2, S5_PW_ROWS, S5_LANES), F32)],
        compiler_params=_params(),
        name="s5_prep",
    )(row(lam_re), row(lam_im), row(ldt), bt(b_re), bt(b_im), ct(c_re), ct(c_im))


def _s5_moves(n_seq, n_seg, seq_len, col_major):
    steps = seq_len // n_seg
    if not col_major:
        return [(q * seq_len + s * steps, steps, q * n_seg + s, S5_SEG)
                for q in range(n_seq) for s in range(n_seg)]
    assert n_seq == 1 and n_seg == S5_SEG
    grid_rows = seq_len // GRID_W
    wseg = GRID_W // n_seg
    return [(r * GRID_W + s * wseg, wseg, r * S5_SEG + s, grid_rows * S5_SEG)
            for r in range(grid_rows) for s in range(n_seg)]


def _s5_kernel(*refs, l, n_seq, n_seg, seq_len, col_major, has_h0, n_alias, emit_state):
    u_ref, bblk_ref, cblk_ref, pw_ref, dvec_ref, gw_ref, gb_ref = refs[:7]
    k = 7
    h0r_ref, h0i_ref = (refs[k], refs[k + 1]) if has_h0 else (None, None)
    k += 2 * int(has_h0) + n_alias
    y_ref = refs[k]
    fr_ref, fi_ref = (refs[k + 1], refs[k + 2]) if emit_state else (None, None)
    up, buf_f, buf_b, yacc, yp, cin, fin = refs[k + 1 + 2 * int(emit_state):]
    assert n_seq * n_seg == S5_SEG and not (has_h0 and n_seg == 1) and not (emit_state and n_seg > 1)
    n_rows = n_seq * seq_len
    steps = seq_len // n_seg
    n_planes = S5_DIM // LANES
    W = S5_SLAB
    moves = _s5_moves(n_seq, n_seg, seq_len, col_major)

    for src, n, dst, stride in moves:
        for t in range(n_planes):
            up[t, pl.ds(dst, n, stride=stride), :] = u_ref[src:src + n, t * LANES:(t + 1) * LANES]

    def u_rows(rs):
        return jnp.concatenate([up[t, rs, :] for t in range(n_planes)], axis=1)

    zero = jnp.zeros((S5_SEG, W), F32)
    zrow = jnp.zeros((1, W), F32)
    for sl in range(S5_NSLAB):
        ln = slice(sl * W, (sl + 1) * W)
        for r0 in range(0, n_rows, ROW_SLAB):
            rs = slice(r0, r0 + ROW_SLAB)
            ub = u_rows(rs).astype(BF16)
            buf_f[rs, :] = jnp.dot(ub, bblk_ref[0, sl], preferred_element_type=F32)
            buf_b[rs, :] = jnp.dot(ub, bblk_ref[1, sl], preferred_element_type=F32)
        lam = [[jnp.broadcast_to(pw_ref[d, c, 0:1, ln], (S5_SEG, W)) for c in range(2)]
               for d in range(N_DIR)]

        def body(t, carry):
            fr, fi, br, bi = carry
            rf = pl.ds(pl.multiple_of(t * S5_SEG, S5_SEG), S5_SEG)
            rb = pl.ds(pl.multiple_of((steps - 1 - t) * S5_SEG, S5_SEG), S5_SEG)
            nfr = lam[0][0] * fr - lam[0][1] * fi + buf_f[rf, 0:W]
            nfi = lam[0][0] * fi + lam[0][1] * fr + buf_f[rf, W:2 * W]
            nbr = lam[1][0] * br - lam[1][1] * bi + buf_b[rb, 0:W]
            nbi = lam[1][0] * bi + lam[1][1] * br + buf_b[rb, W:2 * W]
            buf_f[rf, 0:W] = nfr
            buf_f[rf, W:2 * W] = nfi
            buf_b[rb, 0:W] = nbr
            buf_b[rb, W:2 * W] = nbi
            return nfr, nfi, nbr, nbi

        ends = lax.fori_loop(0, steps, body, (zero, zero, zero, zero), unroll=S5_UNROLL)

        if n_seg == 1:
            if emit_state:
                for d in range(N_DIR):
                    fin[d, 0, :, ln] = ends[2 * d]
                    fin[d, 1, :, ln] = ends[2 * d + 1]
        else:
            for d in range(N_DIR):
                er, ei = ends[2 * d], ends[2 * d + 1]
                lpr = pw_ref[d, 0, steps - 1:steps, ln]
                lpi = pw_ref[d, 1, steps - 1:steps, ln]
                pr = h0r_ref[d:d + 1, ln] if has_h0 else zrow
                pi = h0i_ref[d:d + 1, ln] if has_h0 else zrow
                order = range(n_seg) if d == 0 else range(n_seg - 1, -1, -1)
                for s in order:
                    cin[d, 0, s:s + 1, :] = pr
                    cin[d, 1, s:s + 1, :] = pi
                    tr, ti = _cmul(lpr, lpi, pr, pi)
                    pr = er[s:s + 1, :] + tr
                    pi = ei[s:s + 1, :] + ti
            cfr = cin[0, 0]
            cfi = cin[0, 1]
            cbr = cin[1, 0]
            cbi = cin[1, 1]

            def fix(p, carry):
                rows = pl.ds(pl.multiple_of(p * S5_SEG, S5_SEG), S5_SEG)
                q = steps - 1 - p
                tfr, tfi = _cmul(pw_ref[0, 0, pl.ds(p, 1), ln], pw_ref[0, 1, pl.ds(p, 1), ln], cfr, cfi)
                tbr, tbi = _cmul(pw_ref[1, 0, pl.ds(q, 1), ln], pw_ref[1, 1, pl.ds(q, 1), ln], cbr, cbi)
                buf_f[rows, 0:W] = buf_f[rows, 0:W] + buf_b[rows, 0:W] + tfr + tbr
                buf_f[rows, W:2 * W] = buf_f[rows, W:2 * W] + buf_b[rows, W:2 * W] + tfi + tbi
                return carry

            lax.fori_loop(0, steps, fix, 0, unroll=S5_UNROLL)

        for r0 in range(0, n_rows, ROW_SLAB):
            rs = slice(r0, r0 + ROW_SLAB)
            hs = buf_f[rs, :] if n_seg > 1 else buf_f[rs, :] + buf_b[rs, :]
            part = jnp.dot(hs.astype(BF16), cblk_ref[sl], preferred_element_type=F32)
            if sl == 0:
                yacc[rs, :] = part
            else:
                yacc[rs, :] = yacc[rs, :] + part

    gwb = gw_ref[...].astype(BF16)
    for r0 in range(0, n_rows, ROW_SLAB):
        rs = slice(r0, r0 + ROW_SLAB)
        y = _gelu_tanh(yacc[rs, :] + dvec_ref[l:l + 1, :] * u_rows(rs))
        gate = jnp.dot(y.astype(BF16), gwb, preferred_element_type=F32) + gb_ref[l:l + 1, :]
        y = y * jax.nn.sigmoid(gate)
        for t in range(n_planes):
            yp[t, rs, :] = y[:, t * LANES:(t + 1) * LANES]
    for src, n, dst, stride in moves:
        for t in range(n_planes):
            y_ref[src:src + n, t * LANES:(t + 1) * LANES] = yp[t, pl.ds(dst, n, stride=stride), :]
    if emit_state:
        for q in range(n_seq):
            for d in range(N_DIR):
                fr_ref[q, d:d + 1, :] = fin[d, 0, q:q + 1, :]
                fi_ref[q, d:d + 1, :] = fin[d, 1, q:q + 1, :]


def _s5_mixer(u, bblk, cblk, pw, prm, l, *, seq_len, n_seq, nb, blk0, col_major, h0=None, y_buf=None,
              st_buf=None, emit_state):
    n_rows = n_seq * seq_len
    n_seg = S5_SEG // n_seq
    seq = pl.BlockSpec((n_rows, S5_DIM), lambda b: (blk0 + b, 0))
    st = pl.BlockSpec((n_seq, None, N_DIR, S5_LANES), lambda b: (b, l, 0, 0))
    once = dict(pipeline_mode=pl.Buffered(1))
    args = [u, bblk, cblk, pw, prm["d_vec"], prm["glu_w"], prm["glu_b"]]
    in_specs = [seq, _layer_spec(l, (N_DIR, S5_NSLAB, S5_DIM, 2 * S5_SLAB), **once),
                _layer_spec(l, (S5_NSLAB, 2 * S5_SLAB, S5_DIM), **once),
                _layer_spec(l, (N_DIR, 2, S5_PW_ROWS if n_seg > 1 else S5_SEG, S5_LANES), **once),
                _rows_spec(S5_DIM), _layer_spec(l, (S5_DIM, S5_DIM)), _rows_spec(S5_DIM)]
    if h0 is not None:
        h0_spec = pl.BlockSpec((None, None, N_DIR, S5_LANES), lambda b: (b, l, 0, 0))
        args += list(h0)
        in_specs += [h0_spec, h0_spec]
    out_shape = [jax.ShapeDtypeStruct((N_TOK, S5_DIM), F32)]
    out_specs = [seq]
    aliases = {}
    if y_buf is not None:
        aliases[len(args)] = 0
        args.append(y_buf)
        in_specs.append(_any_spec())
    if emit_state:
        out_shape += [jax.ShapeDtypeStruct((BATCH, DEPTH, N_DIR, S5_LANES), F32)] * 2
        out_specs += [st, st]
        if st_buf is not None:
            for j, buf in enumerate(st_buf):
                aliases[len(args)] = 1 + j
                args.append(buf)
                in_specs.append(_any_spec())
    res = pl.pallas_call(
        functools.partial(_s5_kernel, l=l, n_seq=n_seq, n_seg=n_seg, seq_len=seq_len, col_major=col_major,
                          has_h0=h0 is not None, n_alias=len(aliases), emit_state=emit_state),
        grid=(nb,),
        in_specs=in_specs,
        out_specs=out_specs,
        out_shape=out_shape,
        input_output_aliases=aliases,
        scratch_shapes=[pltpu.VMEM((S5_DIM // LANES, n_rows, LANES), F32),
                        pltpu.VMEM((n_rows, 2 * S5_SLAB), F32),
                        pltpu.VMEM((n_rows, 2 * S5_SLAB), F32),
                        pltpu.VMEM((n_rows, S5_DIM), F32),
                        pltpu.VMEM((S5_DIM // LANES, n_rows, LANES), F32),
                        pltpu.VMEM((N_DIR, 2, S5_SEG, S5_SLAB), F32),
                        pltpu.VMEM((N_DIR, 2, S5_SEG, S5_LANES), F32)],
        compiler_params=_params(),
        name=f"s5_mixer_L{seq_len}",
    )(*args)
    return (res[0], res[1], res[2]) if emit_state else (res[0], None, None)


def _outffn_kernel(xa_ref, xb_ref, ys_ref, yg_ref, y5_ref, g1_ref, sh_ref, sc_ref, g2_ref, ng_ref,
                   fg_ref, wo_ref, w1_ref, w2_ref, *rest, l, final):
    n_out = 2 if final else 1
    o_refs = rest[:n_out]
    wob, w1b, w2b = rest[n_out:]
    i = pl.program_id(0)

    @pl.when(i < FFN_PREP)
    def _():
        wob[i] = wo_ref[...].astype(BF16)
        w1b[i] = w1_ref[...].astype(BF16)
        w2b[i] = w2_ref[...].astype(BF16)

    @pl.when(i >= FFN_PREP)
    def _():
        t = i - FFN_PREP
        n_ctx = N_CTX // FFN_TILE
        r = _mod_row(t, FFN_TILE)
        row = lambda ref: ref[pl.ds(r, 1), :]
        x = jnp.where(t < n_ctx, xa_ref[...], xb_ref[...])
        mixed = jnp.concatenate([ys_ref[...].astype(BF16), yg_ref[...].astype(BF16),
                                 y5_ref[...].astype(BF16)], axis=1)
        w_out = wob[...].reshape(D_MODEL, D_MODEL)
        x1 = x + row(g1_ref) * jnp.dot(mixed, w_out, preferred_element_type=F32)
        h = ((_rms(x1) * ng_ref[l:l + 1, :]) * (1.0 + row(sc_ref)) + row(sh_ref)).astype(BF16)
        acc = jnp.zeros((FFN_TILE, D_MODEL), F32)
        for j in range(FFN_PREP):
            a = jnp.maximum(jnp.dot(h, w1b[j], preferred_element_type=F32), 0.0)
            acc = acc + jnp.dot((a * a).astype(BF16), w2b[j], preferred_element_type=F32)
        x2 = x1 + row(g2_ref) * acc
        if not final:
            o_refs[0][...] = x2
        else:
            y = _rms(x2) * fg_ref[...]

            @pl.when(t < n_ctx)
            def _():
                o_refs[0][...] = y

            @pl.when(t >= n_ctx)
            def _():
                o_refs[1][...] = y


def _out_ffn(xa, xb, b_is_stream, y_ssd, y_sgu, y_s5, l, mods, norm_g, final_g, w_out, w1, w2, *, final):
    tile_of = lambda i: jnp.maximum(i - FFN_PREP, 0)
    tok = lambda w: pl.BlockSpec((FFN_TILE, w), lambda i: (tile_of(i), 0))
    chunk_of = lambda i: jnp.minimum(i, FFN_PREP - 1)
    sa, sb = _split_tok_specs(FFN_TILE, D_MODEL, b_is_stream, tile_of)
    if final:
        out_specs = list(_split_tok_specs(FFN_TILE, D_MODEL, False, tile_of))
        out_shape = [jax.ShapeDtypeStruct((N_CTX, D_MODEL), F32), jax.ShapeDtypeStruct((N_LAT, D_MODEL), F32)]
    else:
        out_specs = [tok(D_MODEL)]
        out_shape = [jax.ShapeDtypeStruct((N_TOK, D_MODEL), F32)]
    ko, kf = D_MODEL // FFN_PREP, D_FF // FFN_PREP
    return pl.pallas_call(
        functools.partial(_outffn_kernel, l=l, final=final),
        grid=(FFN_PREP + N_TOK // FFN_TILE,),
        in_specs=[sa, sb, tok(SSD_INNER), tok(SGU_DIM), tok(S5_DIM),
                  _mod_spec(l, 2), _mod_spec(l, 3), _mod_spec(l, 4), _mod_spec(l, 5),
                  _rows_spec(D_MODEL), pl.BlockSpec((1, D_MODEL), lambda i: (0, 0)),
                  pl.BlockSpec((None, ko, D_MODEL), lambda i: (l, chunk_of(i), 0)),
                  pl.BlockSpec((None, D_MODEL, kf), lambda i: (l, 0, chunk_of(i))),
                  pl.BlockSpec((None, kf, D_MODEL), lambda i: (l, chunk_of(i), 0))],
        out_specs=out_specs,
        out_shape=out_shape,
        scratch_shapes=[pltpu.VMEM((FFN_PREP, ko, D_MODEL), BF16),
                        pltpu.VMEM((FFN_PREP, D_MODEL, kf), BF16),
                        pltpu.VMEM((FFN_PREP, kf, D_MODEL), BF16)],
        compiler_params=_params(),
        name="out_ffn_final" if final else "out_ffn",
    )(xa, xb, y_ssd, y_sgu, y_s5, mods, mods, mods, mods, norm_g, final_g, w_out, w1, w2)


def kernel(x_prompt, x_sample, state_ssd, state_s5_re, state_s5_im, c, c_ctx, ada_w, ada_b, norm1_g,
           norm2_g, w_in, ssd_conv_w, ssd_conv_b, ssd_dt_bias, ssd_a_log, ssd_d, ssd_norm_g,
           sgu_norm_g, sgu_w, sgu_b, s5_lambda_re, s5_lambda_im, s5_log_dt, s5_b_re, s5_b_im,
           s5_c_re, s5_c_im, s5_d, s5_glu_w, s5_glu_b, w_out, ffn_w1, ffn_w2, final_norm_g):
    cvec = jnp.concatenate([c_ctx[None, :], c,
                            jnp.zeros((MOD_ROWS - 1 - DEC_BATCH, D_MODEL), F32)], axis=0)
    mods = _adaln_mods(cvec, ada_w, ada_b)
    bblk, cblk, pw = _s5_prep(s5_lambda_re, s5_lambda_im, s5_log_dt, s5_b_re, s5_b_im,
                              s5_c_re, s5_c_im)

    w_in_t = w_in.transpose(0, 2, 1)
    ssd_prm = dict(conv_w=ssd_conv_w, conv_b=ssd_conv_b,
                   dt_bias=ssd_dt_bias.reshape(DEPTH, N_DIR * SSD_HEADS),
                   a_log=ssd_a_log.reshape(DEPTH, N_DIR * SSD_HEADS),
                   d_vec=jnp.repeat(ssd_d, SSD_HEAD_DIM, axis=-1), norm_g=ssd_norm_g)
    s5_prm = dict(d_vec=s5_d, glu_w=s5_glu_w, glu_b=s5_glu_b)
    sgu_w_pair = sgu_w.reshape(DEPTH, SGU_HEADS // 2, 2, CHUNK, CHUNK).transpose(0, 1, 3, 2, 4)
    sgu_w_pair = sgu_w_pair.reshape(DEPTH, SGU_HEADS // 2, CHUNK, 2 * CHUNK).astype(BF16)
    sgu_b_full = jnp.repeat(sgu_b.transpose(0, 2, 1), SGU_DIM // SGU_HEADS, axis=2)
    norm1 = norm1_g
    norm2 = norm2_g
    sgu_g = sgu_norm_g
    final_g = final_norm_g.reshape(1, D_MODEL)

    lat_ssd = state_ssd.reshape(DEC_BATCH, DEPTH, N_DIR, SSD_HP, SSD_STATE)
    lat_s5 = (state_s5_re.reshape(DEC_BATCH, DEPTH, N_DIR, S5_LANES),
              state_s5_im.reshape(DEC_BATCH, DEPTH, N_DIR, S5_LANES))
    lat_blk = N_CTX // DEC_SEQ

    xa = x_prompt.reshape(N_CTX, D_MODEL)
    xb = x_sample.reshape(N_LAT, D_MODEL)
    b_is_stream = False
    st_ssd = None
    st_s5 = None
    for l in range(DEPTH):
        z, xbc, y_sgu, s5u, dtr = _in_proj(xa, xb, b_is_stream, l, mods, norm1, w_in_t, sgu_g,
                                           sgu_w_pair, sgu_b_full)
        y_s5, fr, fi = _s5_mixer(s5u, bblk, cblk, pw, s5_prm, l, seq_len=SEQ, n_seq=S5_SEG,
                                 nb=BATCH // S5_SEG, blk0=0, col_major=False, st_buf=st_s5,
                                 emit_state=True)
        st_s5 = (fr, fi)
        y_s5, _, _ = _s5_mixer(s5u, bblk, cblk, pw, s5_prm, l, seq_len=DEC_SEQ, n_seq=1, nb=DEC_BATCH,
                               blk0=lat_blk, col_major=True, h0=lat_s5, y_buf=y_s5, emit_state=False)
        y_ssd, st_ssd = _ssd_mixer(z, xbc, dtr, ssd_prm, l, L=SEQ, n_seq=SSD_CTX_SEQS,
                                   nb=BATCH // SSD_CTX_SEQS, blk0=0, st_buf=st_ssd, after=y_s5,
                                   emit_state=True)
        y_ssd, _ = _ssd_mixer(z, xbc, dtr, ssd_prm, l, L=DEC_SEQ, n_seq=1, nb=DEC_BATCH, blk0=lat_blk,
                              h0=lat_ssd, y_buf=y_ssd, emit_state=False)
        final = l == DEPTH - 1
        out = _out_ffn(xa, xb, b_is_stream, y_ssd, y_sgu, y_s5, l, mods, norm2, final_g, w_out, ffn_w1,
                       ffn_w2, final=final)
        if not final:
            xa = xb = out[0]
            b_is_stream = True

    y_prompt = out[0].reshape(BATCH, SEQ, D_MODEL)
    y_sample = out[1].reshape(DEC_BATCH, DEC_SEQ, D_MODEL)
    new_state_ssd = st_ssd.reshape(BATCH, DEPTH, N_DIR, SSD_HEADS, SSD_HEAD_DIM, SSD_STATE)
    s5_shape = (BATCH, DEPTH, N_DIR, S5_GROUPS, S5_STATE)
    return (y_prompt, y_sample, new_state_ssd, st_s5[0].reshape(s5_shape), st_s5[1].reshape(s5_shape))
```

```python
import functools
import math

import jax
import jax.numpy as jnp
from jax import lax
from jax.experimental import pallas as pl
from jax.experimental.pallas import tpu as pltpu

F32 = jnp.float32
BF16 = jnp.bfloat16

D_MODEL = 1024
BATCH = 16
SEQ = 256
DEPTH = 2
DEC_BATCH = 2
DEC_SEQ = 1024
GRID_W = 64
CHUNK = 128
N_DIR = 2
EPS = 1e-6
SSD_INNER = 512
SSD_HEAD_DIM = 64
SSD_HEADS = 8
SSD_GROUPS = 2
SSD_STATE = 128
SSD_CONV = 5
SSD_CONV_DIM = SSD_INNER + 2 * SSD_GROUPS * SSD_STATE
SGU_DIM = 256
SGU_HEADS = 4
S5_DIM = 256
S5_GROUP_CH = 16
S5_GROUPS = 16
S5_STATE = 64
D_FF = 4 * D_MODEL
OFF_XBC = SSD_INNER
OFF_DT = OFF_XBC + SSD_CONV_DIM
OFF_SGU = OFF_DT + N_DIR * SSD_HEADS
OFF_S5 = OFF_SGU + 2 * SGU_DIM
IN_DIM = OFF_S5 + S5_DIM

N_CTX = BATCH * SEQ
N_LAT = DEC_BATCH * DEC_SEQ
N_TOK = N_CTX + N_LAT
LANES = 128
SUBLANES = 8
MOD_ROWS = SUBLANES
SSD_HP = SSD_HEADS * SSD_HEAD_DIM
S5_LANES = S5_GROUPS * S5_STATE
S5_SEG = SUBLANES
S5_PW_ROWS = DEC_SEQ // S5_SEG
S5_SLAB = 256
S5_NSLAB = S5_LANES // S5_SLAB
S5_UNROLL = 16
PAIR = 2 * SSD_HEAD_DIM
N_PAIR = SSD_HEADS // 2
TOK_TILE = 512
N_CTX_TILES = N_CTX // TOK_TILE
FFN_TILE = 512
FFN_PREP = 8
ROW_SLAB = 256
CONV_LANES = 256
SSD_CTX_SEQS = 2
SSD_UNROLL = 8
VMEM_LIMIT = 56 * 1024 * 1024

_NT = (((1,), (1,)), ((), ()))


def _params(n_axes=1):
    return pltpu.CompilerParams(dimension_semantics=("arbitrary",) * n_axes,
                                vmem_limit_bytes=VMEM_LIMIT)


def _layer_spec(l, shape, **kw):
    return pl.BlockSpec((None,) + tuple(shape), lambda *_: (l,) + (0,) * len(shape), **kw)


def _mod_spec(l, k):
    return pl.BlockSpec((None, MOD_ROWS, D_MODEL), lambda *_: (l, 0, k))


def _rows_spec(n):
    return pl.BlockSpec((DEPTH, n), lambda *_: (0, 0))


def _any_spec():
    return pl.BlockSpec(memory_space=pl.ANY)


def _mod_row(i, tm):
    n_ctx = N_CTX // tm
    return jnp.where(i < n_ctx, 0, 1 + (i - n_ctx) // (DEC_SEQ // tm))


def _split_tok_specs(tm, width, b_has_ctx_rows, tile_of=lambda i: i):
    n_ctx = N_CTX // tm
    b_off = n_ctx if b_has_ctx_rows else 0
    a = pl.BlockSpec((tm, width), lambda i: (jnp.minimum(tile_of(i), n_ctx - 1), 0))
    b = pl.BlockSpec((tm, width), lambda i: (jnp.maximum(tile_of(i), n_ctx) - n_ctx + b_off, 0))
    return a, b


def _silu(x):
    return x * jax.nn.sigmoid(x)


def _gelu_tanh(x):
    c = math.sqrt(2.0 / math.pi)
    return 0.5 * x * (1.0 + jnp.tanh(c * (x + 0.044715 * (x * x * x))))


def _rms(x):
    return x * lax.rsqrt(jnp.mean(x * x, axis=-1, keepdims=True) + EPS)


def _bdot(a, b):
    return jnp.dot(a.astype(BF16), b.astype(BF16), preferred_element_type=F32)


def _split3(x):
    hi = x.astype(BF16)
    r = x - hi.astype(F32)
    mid = r.astype(BF16)
    lo = (r - mid.astype(F32)).astype(BF16)
    return jnp.concatenate([hi, mid, lo], axis=1)


N_COND = 1 + DEC_BATCH


def _mod_kernel(ct_ref, w_ref, b_ref, o_ref, srep):
    l = pl.program_id(0)

    @pl.when((l == 0) & (pl.program_id(1) == 0))
    def _():
        s = _silu(ct_ref[...])
        for r in range(N_COND):
            srep[r] = jnp.broadcast_to(s[:, r:r + 1], (D_MODEL, LANES))

    n_tiles = D_MODEL // LANES

    def body(kb, accs):
        rows = pl.ds(pl.multiple_of(kb * SUBLANES, SUBLANES), SUBLANES)
        wk = w_ref[rows, :]
        return tuple(acc + wk * jnp.concatenate([srep[r, rows, :]] * n_tiles, axis=1)
                     for r, acc in enumerate(accs))

    zero = jnp.zeros((SUBLANES, D_MODEL), F32)
    accs = lax.fori_loop(0, D_MODEL // SUBLANES, body, (zero,) * N_COND, unroll=8)
    rows = [jnp.sum(acc, axis=0, keepdims=True) for acc in accs]
    rows.append(jnp.zeros((MOD_ROWS - N_COND, D_MODEL), F32))
    o_ref[...] = jnp.concatenate(rows, axis=0) + b_ref[pl.ds(l, 1), :]


def _adaln_mods(cvec_t, ada_w, ada_b):
    n_blk = 6
    return pl.pallas_call(
        _mod_kernel,
        grid=(DEPTH, n_blk),
        in_specs=[pl.BlockSpec((D_MODEL, MOD_ROWS), lambda l, j: (0, 0)),
                  pl.BlockSpec((None, D_MODEL, D_MODEL), lambda l, j: (l, 0, j)),
                  pl.BlockSpec((DEPTH, D_MODEL), lambda l, j: (0, j))],
        out_specs=pl.BlockSpec((None, MOD_ROWS, D_MODEL), lambda l, j: (l, 0, j)),
        out_shape=jax.ShapeDtypeStruct((DEPTH, MOD_ROWS, 6 * D_MODEL), F32),
        scratch_shapes=[pltpu.VMEM((N_COND, D_MODEL, LANES), F32)],
        compiler_params=_params(2),
        name="adaln_mod",
    )(cvec_t, ada_w, ada_b)


_C_Z = 0
_C_XBC = _C_Z + SSD_INNER
_C_SGU = _C_XBC + SSD_CONV_DIM
_C_S5 = _C_SGU + 2 * SGU_DIM
_C_DT = _C_S5 + S5_DIM
_C_END = _C_DT + LANES


def _inproj_kernel(xa_ref, xb_ref, sh_ref, sc_ref, g_ref, win_ref, sg_ref, sw_ref, sb_ref,
                   z_ref, xbc_ref, ysgu_ref, s5_ref, dt_ref, w_ref, *, l):
    i = pl.program_id(0)

    @pl.when(i == 0)
    def _():
        moves = ((0, OFF_DT, _C_Z), (OFF_SGU, IN_DIM, _C_SGU), (OFF_DT, OFF_SGU, _C_DT))
        for src0, src1, dst0 in moves:
            for r0 in range(src0, src1, ROW_SLAB):
                n = min(ROW_SLAB, src1 - r0)
                w_ref[dst0 + r0 - src0:dst0 + r0 - src0 + n, :] = win_ref[r0:r0 + n, :].astype(BF16)
        n_pad = LANES - N_DIR * SSD_HEADS
        w_ref[_C_END - n_pad:_C_END, :] = jnp.zeros((n_pad, D_MODEL), BF16)

    r = _mod_row(i, TOK_TILE)
    x = jnp.where(i < N_CTX_TILES, xa_ref[...], xb_ref[...])
    shift = sh_ref[pl.ds(r, 1), :]
    scale = sc_ref[pl.ds(r, 1), :]
    h = (_rms(x) * g_ref[l:l + 1, :]) * (1.0 + scale) + shift
    hb = h.astype(BF16)
    proj = lambda c0, c1: lax.dot_general(hb, w_ref[c0:c1, :], _NT, preferred_element_type=F32)
    uv = _gelu_tanh(proj(_C_SGU, _C_S5))
    z_ref[...] = proj(_C_Z, _C_XBC)
    xbc_ref[...] = proj(_C_XBC, _C_SGU)
    s5_ref[...] = proj(_C_S5, _C_DT)
    dt_ref[...] = proj(_C_DT, _C_END)

    u = uv[:, :SGU_DIM]
    v = _rms(uv[:, SGU_DIM:]) * sg_ref[l:l + 1, :]
    lo_lane = lax.broadcasted_iota(jnp.int32, (CHUNK, LANES), 1) < (LANES // 2)
    for c in range(TOK_TILE // CHUNK):
        rows = slice(c * CHUNK, (c + 1) * CHUNK)
        mix = []
        for pr in range(SGU_HEADS // 2):
            vp = v[rows, pr * LANES:(pr + 1) * LANES]
            rhs = jnp.concatenate([jnp.where(lo_lane, vp, 0.0).astype(BF16),
                                   jnp.where(lo_lane, 0.0, vp).astype(BF16)], axis=0)
            mix.append(jnp.dot(sw_ref[pr], rhs, preferred_element_type=F32))
        ysgu_ref[rows, :] = u[rows, :] * (jnp.concatenate(mix, axis=1) + sb_ref[...])


def _in_proj(xa, xb, b_is_stream, l, mods, norm_g, w_in, sgu_g, sgu_w_pair, sgu_b_full):
    tok = lambda w: pl.BlockSpec((TOK_TILE, w), lambda i: (i, 0))
    widths = (SSD_INNER, SSD_CONV_DIM, SGU_DIM, S5_DIM, LANES)
    sa, sb = _split_tok_specs(TOK_TILE, D_MODEL, b_is_stream)
    return pl.pallas_call(
        functools.partial(_inproj_kernel, l=l),
        grid=(N_TOK // TOK_TILE,),
        in_specs=[sa, sb, _mod_spec(l, 0), _mod_spec(l, 1), _rows_spec(D_MODEL),
                  _layer_spec(l, (IN_DIM, D_MODEL), pipeline_mode=pl.Buffered(1)),
                  _rows_spec(SGU_DIM), _layer_spec(l, (SGU_HEADS // 2, CHUNK, 2 * CHUNK)),
                  _layer_spec(l, (CHUNK, SGU_DIM))],
        out_specs=[tok(w) for w in widths],
        out_shape=[jax.ShapeDtypeStruct((N_TOK, w), F32) for w in widths],
        scratch_shapes=[pltpu.VMEM((_C_END, D_MODEL), BF16)],
        compiler_params=_params(),
        name="in_proj",
    )(xa, xb, mods, mods, norm_g, w_in, sgu_g, sgu_w_pair, sgu_b_full)


def _ssd_kernel(*refs, l, L, n_seq, has_h0, n_alias, emit_state):
    z_ref, xbc_ref, dt_ref, cw_ref, cb_ref, dtb_ref, alog_ref, dvec_ref, ng_ref = refs[:9]
    lrow = slice(l, l + 1)
    lane_fill = jnp.zeros((1, LANES - N_DIR * SSD_HEADS), F32)
    k = 9
    h0_ref = refs[k] if has_h0 else None
    k += int(has_h0) + n_alias
    y_ref = refs[k]
    hout_ref = refs[k + 1] if emit_state else None
    xpad, xc, acol, atr, cbs, bmt, dtsp, yacc, hst = refs[k + 1 + int(emit_state):]
    Q = CHUNK
    nc = L // Q
    halo = SUBLANES
    pad = (SSD_CONV - 1) // 2

    pitch = L + halo
    for q in range(n_seq + 1):
        xpad[q * pitch:q * pitch + halo, :] = jnp.zeros((halo, SSD_CONV_DIM), F32)
    for g in range(n_seq * nc):
        o = halo + (g // nc) * pitch + (g % nc) * Q
        xpad[o:o + Q, :] = xbc_ref[g * Q:(g + 1) * Q, :]
    win = Q + 2 * halo
    for c in range(n_seq * nc):
        for lb in range(0, SSD_CONV_DIM, CONV_LANES):
            ln = slice(lb, lb + CONV_LANES)
            o = (c // nc) * pitch + (c % nc) * Q
            xa = xpad[o:o + win, ln]
            acc = cb_ref[lrow, ln] + cw_ref[pad:pad + 1, ln] * xa[halo:halo + Q, :]
            for t in range(SSD_CONV):
                if t != pad:
                    rolled = pltpu.roll(xa, (pad - t) % win, 0)
                    acc = acc + cw_ref[t:t + 1, ln] * rolled[halo:halo + Q, :]
            xc[c * Q:(c + 1) * Q, ln] = _silu(acc)

    raw = dt_ref[...] + jnp.concatenate([dtb_ref[lrow, :], lane_fill], axis=1)
    dt = jnp.maximum(raw, 0.0) + jnp.log(1.0 + jnp.exp(-jnp.abs(raw)))
    dtsp[...] = dt
    a_neg = -jnp.exp(jnp.concatenate([alog_ref[lrow, :], lane_fill], axis=1))
    row = lax.broadcasted_iota(jnp.int32, (Q, Q), 0)
    col = lax.broadcasted_iota(jnp.int32, (Q, Q), 1)
    lower = row >= col
    upper = col >= row
    tri3 = jnp.concatenate([lower.astype(BF16)] * 3, axis=1)
    fwd_lane = lax.broadcasted_iota(jnp.int32, (Q, LANES), 1) < SSD_HEADS
    for c in range(n_seq * nc):
        dta = dtsp[c * Q:(c + 1) * Q, :] * a_neg
        hi = dta.astype(BF16)
        rem = dta - hi.astype(F32)
        mid = rem.astype(BF16)
        lo = (rem - mid.astype(F32)).astype(BF16)
        pre = jnp.dot(tri3, jnp.concatenate([hi, mid, lo], axis=0), preferred_element_type=F32)
        suf = pre[Q - 1:Q, :] - pre + dta
        a = jnp.where(fwd_lane, pre, suf)
        acol[c * Q:(c + 1) * Q, :] = a
        atr[c] = a.T[0:N_DIR * SSD_HEADS, :]
        for g in range(SSD_GROUPS):
            b0 = SSD_INNER + g * SSD_STATE
            c0 = SSD_INNER + (SSD_GROUPS + g) * SSD_STATE
            bm = xc[c * Q:(c + 1) * Q, b0:b0 + SSD_STATE]
            cbs[c, g] = lax.dot_general(xc[c * Q:(c + 1) * Q, c0:c0 + SSD_STATE].astype(BF16),
                                        bm.astype(BF16), _NT, preferred_element_type=F32)
            bmt[c, g] = bm.T.astype(BF16)

    sel_row = lax.broadcasted_iota(jnp.int32, (3 * LANES, SSD_INNER), 0) & (LANES - 1)
    sel_head = lax.broadcasted_iota(jnp.int32, (3 * LANES, SSD_INNER), 1) >> (SSD_HEAD_DIM.bit_length() - 1)
    sel = [(sel_row == sel_head + d * SSD_HEADS).astype(BF16) for d in range(N_DIR)]
    mask2 = [jnp.concatenate([m, m], axis=1) for m in (lower, upper)]

    for q in range(n_seq):
        for d in range(N_DIR):
            for p in range(N_PAIR):
                if has_h0:
                    hst[q, d, p] = h0_ref[q, d, p * PAIR:(p + 1) * PAIR, :].T
                else:
                    hst[q, d, p] = jnp.zeros((SSD_STATE, PAIR), F32)

    lo_lane = lax.broadcasted_iota(jnp.int32, (Q, PAIR), 1) < SSD_HEAD_DIM

    def chunk(c, d, q):
        r0 = pl.multiple_of(c * Q, Q)
        rows = pl.ds(r0, Q)
        a = acol[rows, :]
        a_t = atr[c]
        dtp = jnp.dot(_split3(dtsp[rows, :]), sel[d], preferred_element_type=F32)
        a_end = a[Q - 1:Q, :] if d == 0 else a[0:1, :]
        dec = jnp.exp(a_end)
        for g in range(SSD_GROUPS):
            c0 = SSD_INNER + (SSD_GROUPS + g) * SSD_STATE
            cmb = xc[rows, c0:c0 + SSD_STATE].astype(BF16)
            cb = cbs[c, g]
            cb2 = jnp.concatenate([cb, cb], axis=1)
            for pr in range(2):
                p = g * 2 + pr
                j0 = d * SSD_HEADS + 2 * p
                j1 = j0 + 1
                xs = xc[rows, p * PAIR:(p + 1) * PAIR]
                ab0 = jnp.broadcast_to(a[:, j0:j0 + 1], (Q, Q))
                ab1 = jnp.broadcast_to(a[:, j1:j1 + 1], (Q, Q))
                seg = (jnp.concatenate([ab0, ab1], axis=1)
                       - jnp.concatenate([a_t[j0:j0 + 1, :], a_t[j1:j1 + 1, :]], axis=1))
                m = (cb2 * jnp.exp(jnp.where(mask2[d], seg, -jnp.inf))).astype(BF16)
                xdt = xs * dtp[:, p * PAIR:(p + 1) * PAIR]
                rhs = jnp.concatenate([jnp.where(lo_lane, xdt, 0.0).astype(BF16),
                                       jnp.where(lo_lane, 0.0, xdt).astype(BF16)], axis=0)
                y_diag = jnp.dot(m, rhs, preferred_element_type=F32)
                hp = hst[q, d, p]
                y_off = jnp.dot(cmb, hp.astype(BF16), preferred_element_type=F32)
                a_pair = jnp.where(lo_lane, ab0, ab1)
                y = y_diag + jnp.exp(a_pair) * y_off
                a_end_pair = jnp.where(lo_lane[0:1, :], a_end[:, j0:j0 + 1], a_end[:, j1:j1 + 1])
                xw = (xdt * jnp.exp(a_end_pair - a_pair)).astype(BF16)
                s_new = jnp.dot(bmt[c, g], xw, preferred_element_type=F32)
                decp = jnp.where(lo_lane[0:1, :], dec[:, j0:j0 + 1], dec[:, j1:j1 + 1])
                hst[q, d, p] = decp * hp + s_new
                if d == 0:
                    y = y + dvec_ref[lrow, p * PAIR:(p + 1) * PAIR] * xs
                yacc[d, rows, p * PAIR:(p + 1) * PAIR] = y

    def step(t, carry):
        for q in range(n_seq):
            chunk(q * nc + t, 0, q)
            chunk(q * nc + nc - 1 - t, 1, q)
        return carry

    lax.fori_loop(0, nc, step, 0, unroll=min(nc, SSD_UNROLL))
    for c in range(n_seq * nc):
        rows = slice(c * Q, (c + 1) * Q)
        y = (yacc[0, rows, :] + yacc[1, rows, :]) * _silu(z_ref[rows, :])
        y_ref[rows, :] = _rms(y) * ng_ref[lrow, :]
    if emit_state:
        for q in range(n_seq):
            for d in range(N_DIR):
                for p in range(N_PAIR):
                    hout_ref[q, d, p * PAIR:(p + 1) * PAIR, :] = hst[q, d, p].T


def _ssd_mixer(z, xbc, dtr, prm, l, *, L, n_seq, nb, blk0, h0=None, y_buf=None, st_buf=None, after=None,
               emit_state):
    R = n_seq * L
    seq = lambda w: pl.BlockSpec((R, w), lambda b: (blk0 + b, 0))
    st = pl.BlockSpec((n_seq, None, N_DIR, SSD_HP, SSD_STATE), lambda b: (b, l, 0, 0, 0))
    args = [z, xbc, dtr, prm["conv_w"], prm["conv_b"], prm["dt_bias"], prm["a_log"], prm["d_vec"],
            prm["norm_g"]]
    in_specs = [seq(SSD_INNER), seq(SSD_CONV_DIM), seq(LANES),
                _layer_spec(l, (SSD_CONV, SSD_CONV_DIM)), _rows_spec(SSD_CONV_DIM),
                _rows_spec(N_DIR * SSD_HEADS), _rows_spec(N_DIR * SSD_HEADS),
                _rows_spec(SSD_INNER), _rows_spec(SSD_INNER)]
    if h0 is not None:
        args.append(h0)
        in_specs.append(st)
    out_shape = [jax.ShapeDtypeStruct((N_TOK, SSD_INNER), F32)]
    out_specs = [seq(SSD_INNER)]
    aliases = {}
    if y_buf is not None:
        aliases[len(args)] = 0
        args.append(y_buf)
        in_specs.append(_any_spec())
    if emit_state:
        out_shape.append(jax.ShapeDtypeStruct((BATCH, DEPTH, N_DIR, SSD_HP, SSD_STATE), F32))
        out_specs.append(st)
        if st_buf is not None:
            aliases[len(args)] = 1
            args.append(st_buf)
            in_specs.append(_any_spec())
    if after is not None:
        args.append(after)
        in_specs.append(_any_spec())
    res = pl.pallas_call(
        functools.partial(_ssd_kernel, l=l, L=L, n_seq=n_seq, has_h0=h0 is not None,
                          n_alias=len(aliases) + int(after is not None), emit_state=emit_state),
        grid=(nb,),
        in_specs=in_specs,
        out_specs=out_specs,
        out_shape=out_shape,
        input_output_aliases=aliases,
        scratch_shapes=[pltpu.VMEM((n_seq * (L + SUBLANES) + SUBLANES, SSD_CONV_DIM), F32),
                        pltpu.VMEM((R, SSD_CONV_DIM), F32),
                        pltpu.VMEM((R, LANES), F32),
                        pltpu.VMEM((R // CHUNK, N_DIR * SSD_HEADS, CHUNK), F32),
                        pltpu.VMEM((R // CHUNK, SSD_GROUPS, CHUNK, CHUNK), F32),
                        pltpu.VMEM((R // CHUNK, SSD_GROUPS, SSD_STATE, CHUNK), BF16),
                        pltpu.VMEM((R, LANES), F32),
                        pltpu.VMEM((N_DIR, R, SSD_INNER), F32),
                        pltpu.VMEM((n_seq, N_DIR, N_PAIR, PAIR, SSD_STATE), F32)],
        compiler_params=_params(),
        name=f"ssd_mixer_L{L}",
    )(*args)
    return res if emit_state else (res[0], None)


_LOG2_CH = S5_GROUP_CH.bit_length() - 1
_LOG2_ST = S5_STATE.bit_length() - 1


def _cmul(ar, ai, br, bi):
    return ar * br - ai * bi, ar * bi + ai * br


def _s5prep_kernel(lre_ref, lim_ref, ldt_ref, btr_ref, bti_ref, ctr_ref, cti_ref,
                   bblk_ref, cblk_ref, pw_ref):
    brow = lax.broadcasted_iota(jnp.int32, (S5_DIM, S5_LANES), 0) >> _LOG2_CH
    bcol = lax.broadcasted_iota(jnp.int32, (S5_DIM, S5_LANES), 1) >> _LOG2_ST
    bmask = brow == bcol
    p1 = lax.broadcasted_iota(jnp.int32, (S5_SEG, S5_LANES), 0) + 1
    for d in range(N_DIR):
        lre = lre_ref[d]
        lim = lim_ref[d]
        step = jnp.exp(ldt_ref[d])
        mag = jnp.exp(lre * step)
        lbr = mag * jnp.cos(lim * step)
        lbi = mag * jnp.sin(lim * step)
        den = lre * lre + lim * lim
        nr = lbr - 1.0
        cr = (nr * lre + lbi * lim) / den
        ci = (lbi * lre - nr * lim) / den
        br, bi = _cmul(cr, ci, btr_ref[...], bti_ref[...])
        br = jnp.where(bmask, br, 0.0).astype(BF16)
        bi = jnp.where(bmask, bi, 0.0).astype(BF16)
        for sl in range(S5_NSLAB):
            bblk_ref[d, sl, :, 0:S5_SLAB] = br[:, sl * S5_SLAB:(sl + 1) * S5_SLAB]
            bblk_ref[d, sl, :, S5_SLAB:2 * S5_SLAB] = bi[:, sl * S5_SLAB:(sl + 1) * S5_SLAB]
        rr = jnp.ones((S5_SEG, S5_LANES), F32)
        ri = jnp.zeros((S5_SEG, S5_LANES), F32)
        sr, si = lbr, lbi
        for k in range(S5_SEG.bit_length()):
            bit = ((p1 >> k) & 1) == 1
            tr, ti = _cmul(rr, ri, sr, si)
            rr = jnp.where(bit, tr, rr)
            ri = jnp.where(bit, ti, ri)
            sr, si = _cmul(sr, si, sr, si)
        pw_ref[d, 0, 0:S5_SEG, :] = rr
        pw_ref[d, 1, 0:S5_SEG, :] = ri
        n = S5_SEG
        while n < S5_PW_ROWS:
            tr, ti = _cmul(pw_ref[d, 0, 0:n, :], pw_ref[d, 1, 0:n, :],
                           pw_ref[d, 0, n - 1:n, :], pw_ref[d, 1, n - 1:n, :])
            pw_ref[d, 0, n:2 * n, :] = tr
            pw_ref[d, 1, n:2 * n, :] = ti
            n *= 2
    crow = lax.broadcasted_iota(jnp.int32, (S5_LANES, S5_DIM), 0) >> _LOG2_ST
    ccol = lax.broadcasted_iota(jnp.int32, (S5_LANES, S5_DIM), 1) >> _LOG2_CH
    cmask = crow == ccol
    cr = jnp.where(cmask, ctr_ref[...], 0.0).astype(BF16)
    ci = jnp.where(cmask, -cti_ref[...], 0.0).astype(BF16)
    for sl in range(S5_NSLAB):
        cblk_ref[sl, 0:S5_SLAB, :] = cr[sl * S5_SLAB:(sl + 1) * S5_SLAB, :]
        cblk_ref[sl, S5_SLAB:2 * S5_SLAB, :] = ci[sl * S5_SLAB:(sl + 1) * S5_SLAB, :]


def _s5_prep(lam_re, lam_im, log_dt, b_re, b_im, c_re, c_im):
    row = lambda t: t.reshape(DEPTH, N_DIR, 1, S5_LANES)
    ldt = jnp.repeat(log_dt, S5_STATE, axis=-1)
    bt = lambda t: jnp.tile(t.transpose(0, 3, 1, 2).reshape(DEPTH, S5_GROUP_CH, S5_LANES),
                            (1, S5_GROUPS, 1))
    ct = lambda t: jnp.tile(t.transpose(0, 1, 3, 2).reshape(DEPTH, S5_LANES, S5_GROUP_CH),
                            (1, 1, S5_GROUPS))
    vec = pl.BlockSpec((None, N_DIR, 1, S5_LANES), lambda l: (l, 0, 0, 0))
    bsp = pl.BlockSpec((None, S5_DIM, S5_LANES), lambda l: (l, 0, 0))
    csp = pl.BlockSpec((None, S5_LANES, S5_DIM), lambda l: (l, 0, 0))
    return pl.pallas_call(
        _s5prep_kernel,
        grid=(DEPTH,),
        in_specs=[vec, vec, vec, bsp, bsp, csp, csp],
        out_specs=[pl.BlockSpec((None, N_DIR, S5_NSLAB, S5_DIM, 2 * S5_SLAB), lambda l: (l, 0, 0, 0, 0)),
                   pl.BlockSpec((None, S5_NSLAB, 2 * S5_SLAB, S5_DIM), lambda l: (l, 0, 0, 0)),
                   pl.BlockSpec((None, N_DIR, 2, S5_PW_ROWS, S5_LANES), lambda l: (l, 0, 0, 0, 0))],
        out_shape=[jax.ShapeDtypeStruct((DEPTH, N_DIR, S5_NSLAB, S5_DIM, 2 * S5_SLAB), BF16),
                   jax.ShapeDtypeStruct((DEPTH, S5_NSLAB, 2 * S5_SLAB, S5_DIM), BF16),
                   jax.ShapeDtypeStruct((DEPTH, N_DIR, 2, S5_PW_ROWS, S5_LANES), F32)],
        compiler_params=_params(),
        name="s5_prep",
    )(row(lam_re), row(lam_im), row(ldt), bt(b_re), bt(b_im), ct(c_re), ct(c_im))


def _s5_moves(n_seq, n_seg, seq_len, col_major):
    steps = seq_len // n_seg
    if not col_major:
        return [(q * seq_len + s * steps, steps, q * n_seg + s, S5_SEG)
                for q in range(n_seq) for s in range(n_seg)]
    assert n_seq == 1 and n_seg == S5_SEG
    grid_rows = seq_len // GRID_W
    wseg = GRID_W // n_seg
    return [(r * GRID_W + s * wseg, wseg, r * S5_SEG + s, grid_rows * S5_SEG)
            for r in range(grid_rows) for s in range(n_seg)]


def _s5_kernel(*refs, l, n_seq, n_seg, seq_len, col_major, has_h0, n_alias, emit_state):
    u_ref, bblk_ref, cblk_ref, pw_ref, dvec_ref, gw_ref, gb_ref = refs[:7]
    k = 7
    h0r_ref, h0i_ref = (refs[k], refs[k + 1]) if has_h0 else (None, None)
    k += 2 * int(has_h0) + n_alias
    y_ref = refs[k]
    fr_ref, fi_ref = (refs[k + 1], refs[k + 2]) if emit_state else (None, None)
    up, buf_f, buf_b, yacc, yp, cin, fin = refs[k + 1 + 2 * int(emit_state):]
    assert n_seq * n_seg == S5_SEG and not (has_h0 and n_seg == 1) and not (emit_state and n_seg > 1)
    n_rows = n_seq * seq_len
    steps = seq_len // n_seg
    n_planes = S5_DIM // LANES
    W = S5_SLAB
    moves = _s5_moves(n_seq, n_seg, seq_len, col_major)

    for src, n, dst, stride in moves:
        for t in range(n_planes):
            up[t, pl.ds(dst, n, stride=stride), :] = u_ref[src:src + n, t * LANES:(t + 1) * LANES]

    def u_rows(rs):
        return jnp.concatenate([up[t, rs, :] for t in range(n_planes)], axis=1)

    zero = jnp.zeros((S5_SEG, W), F32)
    zrow = jnp.zeros((1, W), F32)
    for sl in range(S5_NSLAB):
        ln = slice(sl * W, (sl + 1) * W)
        for r0 in range(0, n_rows, ROW_SLAB):
            rs = slice(r0, r0 + ROW_SLAB)
            ub = u_rows(rs).astype(BF16)
            buf_f[rs, :] = jnp.dot(ub, bblk_ref[0, sl], preferred_element_type=F32)
            buf_b[rs, :] = jnp.dot(ub, bblk_ref[1, sl], preferred_element_type=F32)
        lam = [[jnp.broadcast_to(pw_ref[d, c, 0:1, ln], (S5_SEG, W)) for c in range(2)]
               for d in range(N_DIR)]

        def body(t, carry):
            fr, fi, br, bi = carry
            rf = pl.ds(pl.multiple_of(t * S5_SEG, S5_SEG), S5_SEG)
            rb = pl.ds(pl.multiple_of((steps - 1 - t) * S5_SEG, S5_SEG), S5_SEG)
            nfr = lam[0][0] * fr - lam[0][1] * fi + buf_f[rf, 0:W]
            nfi = lam[0][0] * fi + lam[0][1] * fr + buf_f[rf, W:2 * W]
            nbr = lam[1][0] * br - lam[1][1] * bi + buf_b[rb, 0:W]
            nbi = lam[1][0] * bi + lam[1][1] * br + buf_b[rb, W:2 * W]
            buf_f[rf, 0:W] = nfr
            buf_f[rf, W:2 * W] = nfi
            buf_b[rb, 0:W] = nbr
            buf_b[rb, W:2 * W] = nbi
            return nfr, nfi, nbr, nbi

        ends = lax.fori_loop(0, steps, body, (zero, zero, zero, zero), unroll=S5_UNROLL)

        if n_seg == 1:
            if emit_state:
                for d in range(N_DIR):
                    fin[d, 0, :, ln] = ends[2 * d]
                    fin[d, 1, :, ln] = ends[2 * d + 1]
        else:
            for d in range(N_DIR):
                er, ei = ends[2 * d], ends[2 * d + 1]
                lpr = pw_ref[d, 0, steps - 1:steps, ln]
                lpi = pw_ref[d, 1, steps - 1:steps, ln]
                pr = h0r_ref[d:d + 1, ln] if has_h0 else zrow
                pi = h0i_ref[d:d + 1, ln] if has_h0 else zrow
                order = range(n_seg) if d == 0 else range(n_seg - 1, -1, -1)
                for s in order:
                    cin[d, 0, s:s + 1, :] = pr
                    cin[d, 1, s:s + 1, :] = pi
                    tr, ti = _cmul(lpr, lpi, pr, pi)
                    pr = er[s:s + 1, :] + tr
                    pi = ei[s:s + 1, :] + ti
            cfr = cin[0, 0]
            cfi = cin[0, 1]
            cbr = cin[1, 0]
            cbi = cin[1, 1]

            def fix(p, carry):
                rows = pl.ds(pl.multiple_of(p * S5_SEG, S5_SEG), S5_SEG)
                q = steps - 1 - p
                tfr, tfi = _cmul(pw_ref[0, 0, pl.ds(p, 1), ln], pw_ref[0, 1, pl.ds(p, 1), ln], cfr, cfi)
                tbr, tbi = _cmul(pw_ref[1, 0, pl.ds(q, 1), ln], pw_ref[1, 1, pl.ds(q, 1), ln], cbr, cbi)
                buf_f[rows, 0:W] = buf_f[rows, 0:W] + buf_b[rows, 0:W] + tfr + tbr
                buf_f[rows, W:2 * W] = buf_f[rows, W:2 * W] + buf_b[rows, W:2 * W] + tfi + tbi
                return carry

            lax.fori_loop(0, steps, fix, 0, unroll=S5_UNROLL)

        for r0 in range(0, n_rows, ROW_SLAB):
            rs = slice(r0, r0 + ROW_SLAB)
            hs = buf_f[rs, :] if n_seg > 1 else buf_f[rs, :] + buf_b[rs, :]
            part = jnp.dot(hs.astype(BF16), cblk_ref[sl], preferred_element_type=F32)
            if sl == 0:
                yacc[rs, :] = part
            else:
                yacc[rs, :] = yacc[rs, :] + part

    gwb = gw_ref[...].astype(BF16)
    for r0 in range(0, n_rows, ROW_SLAB):
        rs = slice(r0, r0 + ROW_SLAB)
        y = _gelu_tanh(yacc[rs, :] + dvec_ref[l:l + 1, :] * u_rows(rs))
        gate = jnp.dot(y.astype(BF16), gwb, preferred_element_type=F32) + gb_ref[l:l + 1, :]
        y = y * jax.nn.sigmoid(gate)
        for t in range(n_planes):
            yp[t, rs, :] = y[:, t * LANES:(t + 1) * LANES]
    for src, n, dst, stride in moves:
        for t in range(n_planes):
            y_ref[src:src + n, t * LANES:(t + 1) * LANES] = yp[t, pl.ds(dst, n, stride=stride), :]
    if emit_state:
        for q in range(n_seq):
            for d in range(N_DIR):
                fr_ref[q, d:d + 1, :] = fin[d, 0, q:q + 1, :]
                fi_ref[q, d:d + 1, :] = fin[d, 1, q:q + 1, :]


def _s5_mixer(u, bblk, cblk, pw, prm, l, *, seq_len, n_seq, nb, blk0, col_major, h0=None, y_buf=None,
              st_buf=None, emit_state):
    n_rows = n_seq * seq_len
    n_seg = S5_SEG // n_seq
    seq = pl.BlockSpec((n_rows, S5_DIM), lambda b: (blk0 + b, 0))
    st = pl.BlockSpec((n_seq, None, N_DIR, S5_LANES), lambda b: (b, l, 0, 0))
    once = dict(pipeline_mode=pl.Buffered(1))
    args = [u, bblk, cblk, pw, prm["d_vec"], prm["glu_w"], prm["glu_b"]]
    in_specs = [seq, _layer_spec(l, (N_DIR, S5_NSLAB, S5_DIM, 2 * S5_SLAB), **once),
                _layer_spec(l, (S5_NSLAB, 2 * S5_SLAB, S5_DIM), **once),
                _layer_spec(l, (N_DIR, 2, S5_PW_ROWS if n_seg > 1 else S5_SEG, S5_LANES), **once),
                _rows_spec(S5_DIM), _layer_spec(l, (S5_DIM, S5_DIM)), _rows_spec(S5_DIM)]
    if h0 is not None:
        h0_spec = pl.BlockSpec((None, None, N_DIR, S5_LANES), lambda b: (b, l, 0, 0))
        args += list(h0)
        in_specs += [h0_spec, h0_spec]
    out_shape = [jax.ShapeDtypeStruct((N_TOK, S5_DIM), F32)]
    out_specs = [seq]
    aliases = {}
    if y_buf is not None:
        aliases[len(args)] = 0
        args.append(y_buf)
        in_specs.append(_any_spec())
    if emit_state:
        out_shape += [jax.ShapeDtypeStruct((BATCH, DEPTH, N_DIR, S5_LANES), F32)] * 2
        out_specs += [st, st]
        if st_buf is not None:
            for j, buf in enumerate(st_buf):
                aliases[len(args)] = 1 + j
                args.append(buf)
                in_specs.append(_any_spec())
    res = pl.pallas_call(
        functools.partial(_s5_kernel, l=l, n_seq=n_seq, n_seg=n_seg, seq_len=seq_len, col_major=col_major,
                          has_h0=h0 is not None, n_alias=len(aliases), emit_state=emit_state),
        grid=(nb,),
        in_specs=in_specs,
        out_specs=out_specs,
        out_shape=out_shape,
        input_output_aliases=aliases,
        scratch_shapes=[pltpu.VMEM((S5_DIM // LANES, n_rows, LANES), F32),
                        pltpu.VMEM((n_rows, 2 * S5_SLAB), F32),
                        pltpu.VMEM((n_rows, 2 * S5_SLAB), F32),
                        pltpu.VMEM((n_rows, S5_DIM), F32),
                        pltpu.VMEM((S5_DIM // LANES, n_rows, LANES), F32),
                        pltpu.VMEM((N_DIR, 2, S5_SEG, S5_SLAB), F32),
                        pltpu.VMEM((N_DIR, 2, S5_SEG, S5_LANES), F32)],
        compiler_params=_params(),
        name=f"s5_mixer_L{seq_len}",
    )(*args)
    return (res[0], res[1], res[2]) if emit_state else (res[0], None, None)


def _outffn_kernel(xa_ref, xb_ref, ys_ref, yg_ref, y5_ref, g1_ref, sh_ref, sc_ref, g2_ref, ng_ref,
                   fg_ref, wo_ref, w1_ref, w2_ref, *rest, l, final):
    n_out = 2 if final else 1
    o_refs = rest[:n_out]
    wob, w1b, w2b = rest[n_out:]
    i = pl.program_id(0)

    @pl.when(i < FFN_PREP)
    def _():
        wob[i] = wo_ref[...].astype(BF16)
        w1b[i] = w1_ref[...].astype(BF16)
        w2b[i] = w2_ref[...].astype(BF16)

    @pl.when(i >= FFN_PREP)
    def _():
        t = i - FFN_PREP
        n_ctx = N_CTX // FFN_TILE
        r = _mod_row(t, FFN_TILE)
        row = lambda ref: ref[pl.ds(r, 1), :]
        x = jnp.where(t < n_ctx, xa_ref[...], xb_ref[...])
        mixed = jnp.concatenate([ys_ref[...].astype(BF16), yg_ref[...].astype(BF16),
                                 y5_ref[...].astype(BF16)], axis=1)
        w_out = wob[...].reshape(D_MODEL, D_MODEL)
        x1 = x + row(g1_ref) * jnp.dot(mixed, w_out, preferred_element_type=F32)
        h = ((_rms(x1) * ng_ref[l:l + 1, :]) * (1.0 + row(sc_ref)) + row(sh_ref)).astype(BF16)
        acc = jnp.zeros((FFN_TILE, D_MODEL), F32)
        for j in range(FFN_PREP):
            a = jnp.maximum(jnp.dot(h, w1b[j], preferred_element_type=F32), 0.0)
            acc = acc + jnp.dot((a * a).astype(BF16), w2b[j], preferred_element_type=F32)
        x2 = x1 + row(g2_ref) * acc
        if not final:
            o_refs[0][...] = x2
        else:
            y = _rms(x2) * fg_ref[...]

            @pl.when(t < n_ctx)
            def _():
                o_refs[0][...] = y

            @pl.when(t >= n_ctx)
            def _():
                o_refs[1][...] = y


def _out_ffn(xa, xb, b_is_stream, y_ssd, y_sgu, y_s5, l, mods, norm_g, final_g, w_out, w1, w2, *, final):
    tile_of = lambda i: jnp.maximum(i - FFN_PREP, 0)
    tok = lambda w: pl.BlockSpec((FFN_TILE, w), lambda i: (tile_of(i), 0))
    chunk_of = lambda i: jnp.minimum(i, FFN_PREP - 1)
    sa, sb = _split_tok_specs(FFN_TILE, D_MODEL, b_is_stream, tile_of)
    if final:
        out_specs = list(_split_tok_specs(FFN_TILE, D_MODEL, False, tile_of))
        out_shape = [jax.ShapeDtypeStruct((N_CTX, D_MODEL), F32), jax.ShapeDtypeStruct((N_LAT, D_MODEL), F32)]
    else:
        out_specs = [tok(D_MODEL)]
        out_shape = [jax.ShapeDtypeStruct((N_TOK, D_MODEL), F32)]
    ko, kf = D_MODEL // FFN_PREP, D_FF // FFN_PREP
    return pl.pallas_call(
        functools.partial(_outffn_kernel, l=l, final=final),
        grid=(FFN_PREP + N_TOK // FFN_TILE,),
        in_specs=[sa, sb, tok(SSD_INNER), tok(SGU_DIM), tok(S5_DIM),
                  _mod_spec(l, 2), _mod_spec(l, 3), _mod_spec(l, 4), _mod_spec(l, 5),
                  _rows_spec(D_MODEL), pl.BlockSpec((1, D_MODEL), lambda i: (0, 0)),
                  pl.BlockSpec((None, ko, D_MODEL), lambda i: (l, chunk_of(i), 0)),
                  pl.BlockSpec((None, D_MODEL, kf), lambda i: (l, 0, chunk_of(i))),
                  pl.BlockSpec((None, kf, D_MODEL), lambda i: (l, chunk_of(i), 0))],
        out_specs=out_specs,
        out_shape=out_shape,
        scratch_shapes=[pltpu.VMEM((FFN_PREP, ko, D_MODEL), BF16),
                        pltpu.VMEM((FFN_PREP, D_MODEL, kf), BF16),
                        pltpu.VMEM((FFN_PREP, kf, D_MODEL), BF16)],
        compiler_params=_params(),
        name="out_ffn_final" if final else "out_ffn",
    )(xa, xb, y_ssd, y_sgu, y_s5, mods, mods, mods, mods, norm_g, final_g, w_out, w1, w2)


def kernel(x_prompt, x_sample, state_ssd, state_s5_re, state_s5_im, c, c_ctx, ada_w, ada_b, norm1_g,
           norm2_g, w_in, ssd_conv_w, ssd_conv_b, ssd_dt_bias, ssd_a_log, ssd_d, ssd_norm_g,
           sgu_norm_g, sgu_w, sgu_b, s5_lambda_re, s5_lambda_im, s5_log_dt, s5_b_re, s5_b_im,
           s5_c_re, s5_c_im, s5_d, s5_glu_w, s5_glu_b, w_out, ffn_w1, ffn_w2, final_norm_g):
    cvec = jnp.concatenate([c_ctx[None, :], c,
                            jnp.zeros((MOD_ROWS - 1 - DEC_BATCH, D_MODEL), F32)], axis=0)
    mods = _adaln_mods(cvec.T, ada_w, ada_b)
    bblk, cblk, pw = _s5_prep(s5_lambda_re, s5_lambda_im, s5_log_dt, s5_b_re, s5_b_im,
                              s5_c_re, s5_c_im)

    w_in_t = w_in.transpose(0, 2, 1)
    ssd_prm = dict(conv_w=ssd_conv_w, conv_b=ssd_conv_b,
                   dt_bias=ssd_dt_bias.reshape(DEPTH, N_DIR * SSD_HEADS),
                   a_log=ssd_a_log.reshape(DEPTH, N_DIR * SSD_HEADS),
                   d_vec=jnp.repeat(ssd_d, SSD_HEAD_DIM, axis=-1), norm_g=ssd_norm_g)
    s5_prm = dict(d_vec=s5_d, glu_w=s5_glu_w, glu_b=s5_glu_b)
    sgu_w_pair = sgu_w.reshape(DEPTH, SGU_HEADS // 2, 2, CHUNK, CHUNK).transpose(0, 1, 3, 2, 4)
    sgu_w_pair = sgu_w_pair.reshape(DEPTH, SGU_HEADS // 2, CHUNK, 2 * CHUNK).astype(BF16)
    sgu_b_full = jnp.repeat(sgu_b.transpose(0, 2, 1), SGU_DIM // SGU_HEADS, axis=2)
    norm1 = norm1_g
    norm2 = norm2_g
    sgu_g = sgu_norm_g
    final_g = final_norm_g.reshape(1, D_MODEL)

    lat_ssd = state_ssd.reshape(DEC_BATCH, DEPTH, N_DIR, SSD_HP, SSD_STATE)
    lat_s5 = (state_s5_re.reshape(DEC_BATCH, DEPTH, N_DIR, S5_LANES),
              state_s5_im.reshape(DEC_BATCH, DEPTH, N_DIR, S5_LANES))
    lat_blk = N_CTX // DEC_SEQ

    xa = x_prompt.reshape(N_CTX, D_MODEL)
    xb = x_sample.reshape(N_LAT, D_MODEL)
    b_is_stream = False
    st_ssd = None
    st_s5 = None
    for l in range(DEPTH):
        z, xbc, y_sgu, s5u, dtr = _in_proj(xa, xb, b_is_stream, l, mods, norm1, w_in_t, sgu_g,
                                           sgu_w_pair, sgu_b_full)
        y_s5, fr, fi = _s5_mixer(s5u, bblk, cblk, pw, s5_prm, l, seq_len=SEQ, n_seq=S5_SEG,
                                 nb=BATCH // S5_SEG, blk0=0, col_major=False, st_buf=st_s5,
                                 emit_state=True)
        st_s5 = (fr, fi)
        y_s5, _, _ = _s5_mixer(s5u, bblk, cblk, pw, s5_prm, l, seq_len=DEC_SEQ, n_seq=1, nb=DEC_BATCH,
                               blk0=lat_blk, col_major=True, h0=lat_s5, y_buf=y_s5, emit_state=False)
        y_ssd, st_ssd = _ssd_mixer(z, xbc, dtr, ssd_prm, l, L=SEQ, n_seq=SSD_CTX_SEQS,
                                   nb=BATCH // SSD_CTX_SEQS, blk0=0, st_buf=st_ssd, after=y_s5,
                                   emit_state=True)
        y_ssd, _ = _ssd_mixer(z, xbc, dtr, ssd_prm, l, L=DEC_SEQ, n_seq=1, nb=DEC_BATCH, blk0=lat_blk,
                              h0=lat_ssd, y_buf=y_ssd, emit_state=False)
        final = l == DEPTH - 1
        out = _out_ffn(xa, xb, b_is_stream, y_ssd, y_sgu, y_s5, l, mods, norm2, final_g, w_out, ffn_w1,
                       ffn_w2, final=final)
        if not final:
            xa = xb = out[0]
            b_is_stream = True

    y_prompt = out[0].reshape(BATCH, SEQ, D_MODEL)
    y_sample = out[1].reshape(DEC_BATCH, DEC_SEQ, D_MODEL)
    new_state_ssd = st_ssd.reshape(BATCH, DEPTH, N_DIR, SSD_HEADS, SSD_HEAD_DIM, SSD_STATE)
    s5_shape = (BATCH, DEPTH, N_DIR, S5_GROUPS, S5_STATE)
    return (y_prompt, y_sample, new_state_ssd, st_s5[0].reshape(s5_shape), st_s5[1].reshape(s5_shape))
```

```python
import functools
import math

import jax
import jax.numpy as jnp
from jax import lax
from jax.experimental import pallas as pl
from jax.experimental.pallas import tpu as pltpu

F32 = jnp.float32
BF16 = jnp.bfloat16

D_MODEL = 1024
BATCH = 16
SEQ = 256
DEPTH = 2
DEC_BATCH = 2
DEC_SEQ = 1024
GRID_W = 64
CHUNK = 128
N_DIR = 2
EPS = 1e-6
SSD_INNER = 512
SSD_HEAD_DIM = 64
SSD_HEADS = 8
SSD_GROUPS = 2
SSD_STATE = 128
SSD_CONV = 5
SSD_CONV_DIM = SSD_INNER + 2 * SSD_GROUPS * SSD_STATE
SGU_DIM = 256
SGU_HEADS = 4
S5_DIM = 256
S5_GROUP_CH = 16
S5_GROUPS = 16
S5_STATE = 64
D_FF = 4 * D_MODEL
OFF_XBC = SSD_INNER
OFF_DT = OFF_XBC + SSD_CONV_DIM
OFF_SGU = OFF_DT + N_DIR * SSD_HEADS
OFF_S5 = OFF_SGU + 2 * SGU_DIM
IN_DIM = OFF_S5 + S5_DIM

N_CTX = BATCH * SEQ
N_LAT = DEC_BATCH * DEC_SEQ
N_TOK = N_CTX + N_LAT
LANES = 128
SUBLANES = 8
MOD_ROWS = SUBLANES
SSD_HP = SSD_HEADS * SSD_HEAD_DIM
S5_LANES = S5_GROUPS * S5_STATE
S5_SEG = SUBLANES
S5_PW_ROWS = DEC_SEQ // S5_SEG
S5_SLAB = 256
S5_NSLAB = S5_LANES // S5_SLAB
S5_UNROLL = 16
PAIR = 2 * SSD_HEAD_DIM
N_PAIR = SSD_HEADS // 2
TOK_TILE = 512
N_CTX_TILES = N_CTX // TOK_TILE
FFN_TILE = 512
FFN_PREP = 8
ROW_SLAB = 256
CONV_LANES = 256
SSD_CTX_SEQS = 2
SSD_UNROLL = 8
VMEM_LIMIT = 56 * 1024 * 1024

_NT = (((1,), (1,)), ((), ()))


def _params(n_axes=1):
    return pltpu.CompilerParams(dimension_semantics=("arbitrary",) * n_axes,
                                vmem_limit_bytes=VMEM_LIMIT)


def _layer_spec(l, shape, **kw):
    return pl.BlockSpec((None,) + tuple(shape), lambda *_: (l,) + (0,) * len(shape), **kw)


def _mod_spec(l, k):
    return pl.BlockSpec((None, MOD_ROWS, D_MODEL), lambda *_: (l, 0, k))


def _rows_spec(n):
    return pl.BlockSpec((DEPTH, n), lambda *_: (0, 0))


def _any_spec():
    return pl.BlockSpec(memory_space=pl.ANY)


def _mod_row(i, tm):
    n_ctx = N_CTX // tm
    return jnp.where(i < n_ctx, 0, 1 + (i - n_ctx) // (DEC_SEQ // tm))


def _split_tok_specs(tm, width, b_has_ctx_rows, tile_of=lambda i: i):
    n_ctx = N_CTX // tm
    b_off = n_ctx if b_has_ctx_rows else 0
    a = pl.BlockSpec((tm, width), lambda i: (jnp.minimum(tile_of(i), n_ctx - 1), 0))
    b = pl.BlockSpec((tm, width), lambda i: (jnp.maximum(tile_of(i), n_ctx) - n_ctx + b_off, 0))
    return a, b


def _silu(x):
    return x * jax.nn.sigmoid(x)


def _gelu_tanh(x):
    c = math.sqrt(2.0 / math.pi)
    return 0.5 * x * (1.0 + jnp.tanh(c * (x + 0.044715 * (x * x * x))))


def _rms(x):
    return x * lax.rsqrt(jnp.mean(x * x, axis=-1, keepdims=True) + EPS)


def _bdot(a, b):
    return jnp.dot(a.astype(BF16), b.astype(BF16), preferred_element_type=F32)


def _split3(x):
    hi = x.astype(BF16)
    r = x - hi.astype(F32)
    mid = r.astype(BF16)
    lo = (r - mid.astype(F32)).astype(BF16)
    return jnp.concatenate([hi, mid, lo], axis=1)


def _mod_kernel(c_ref, w_ref, b_ref, o_ref):
    bias = b_ref[pl.ds(pl.program_id(0), 1), :]
    o_ref[...] = _bdot(_silu(c_ref[...]), w_ref[...]) + bias


def _adaln_mods(cvec, ada_w, ada_b):
    n_blk = 6
    return pl.pallas_call(
        _mod_kernel,
        grid=(DEPTH, n_blk),
        in_specs=[pl.BlockSpec((MOD_ROWS, D_MODEL), lambda l, j: (0, 0)),
                  pl.BlockSpec((None, D_MODEL, D_MODEL), lambda l, j: (l, 0, j)),
                  pl.BlockSpec((DEPTH, D_MODEL), lambda l, j: (0, j))],
        out_specs=pl.BlockSpec((None, MOD_ROWS, D_MODEL), lambda l, j: (l, 0, j)),
        out_shape=jax.ShapeDtypeStruct((DEPTH, MOD_ROWS, 6 * D_MODEL), F32),
        compiler_params=_params(2),
        name="adaln_mod",
    )(cvec, ada_w, ada_b)


_C_Z = 0
_C_XBC = _C_Z + SSD_INNER
_C_SGU = _C_XBC + SSD_CONV_DIM
_C_S5 = _C_SGU + 2 * SGU_DIM
_C_DT = _C_S5 + S5_DIM
_C_END = _C_DT + LANES


def _inproj_kernel(xa_ref, xb_ref, sh_ref, sc_ref, g_ref, win_ref, sg_ref, sw_ref, sb_ref,
                   z_ref, xbc_ref, ysgu_ref, s5_ref, dt_ref, w_ref, *, l):
    i = pl.program_id(0)

    @pl.when(i == 0)
    def _():
        moves = ((0, OFF_DT, _C_Z), (OFF_SGU, IN_DIM, _C_SGU), (OFF_DT, OFF_SGU, _C_DT))
        for src0, src1, dst0 in moves:
            for r0 in range(src0, src1, ROW_SLAB):
                n = min(ROW_SLAB, src1 - r0)
                w_ref[dst0 + r0 - src0:dst0 + r0 - src0 + n, :] = win_ref[r0:r0 + n, :].astype(BF16)
        n_pad = LANES - N_DIR * SSD_HEADS
        w_ref[_C_END - n_pad:_C_END, :] = jnp.zeros((n_pad, D_MODEL), BF16)

    r = _mod_row(i, TOK_TILE)
    x = jnp.where(i < N_CTX_TILES, xa_ref[...], xb_ref[...])
    shift = sh_ref[pl.ds(r, 1), :]
    scale = sc_ref[pl.ds(r, 1), :]
    h = (_rms(x) * g_ref[l:l + 1, :]) * (1.0 + scale) + shift
    hb = h.astype(BF16)
    proj = lambda c0, c1: lax.dot_general(hb, w_ref[c0:c1, :], _NT, preferred_element_type=F32)
    uv = _gelu_tanh(proj(_C_SGU, _C_S5))
    z_ref[...] = proj(_C_Z, _C_XBC)
    xbc_ref[...] = proj(_C_XBC, _C_SGU)
    s5_ref[...] = proj(_C_S5, _C_DT)
    dt_ref[...] = proj(_C_DT, _C_END)

    u = uv[:, :SGU_DIM]
    v = _rms(uv[:, SGU_DIM:]) * sg_ref[l:l + 1, :]
    lo_lane = lax.broadcasted_iota(jnp.int32, (CHUNK, LANES), 1) < (LANES // 2)
    for c in range(TOK_TILE // CHUNK):
        rows = slice(c * CHUNK, (c + 1) * CHUNK)
        mix = []
        for pr in range(SGU_HEADS // 2):
            vp = v[rows, pr * LANES:(pr + 1) * LANES]
            rhs = jnp.concatenate([jnp.where(lo_lane, vp, 0.0).astype(BF16),
                                   jnp.where(lo_lane, 0.0, vp).astype(BF16)], axis=0)
            mix.append(jnp.dot(sw_ref[pr], rhs, preferred_element_type=F32))
        ysgu_ref[rows, :] = u[rows, :] * (jnp.concatenate(mix, axis=1) + sb_ref[...])


def _in_proj(xa, xb, b_is_stream, l, mods, norm_g, w_in, sgu_g, sgu_w_pair, sgu_b_full):
    tok = lambda w: pl.BlockSpec((TOK_TILE, w), lambda i: (i, 0))
    widths = (SSD_INNER, SSD_CONV_DIM, SGU_DIM, S5_DIM, LANES)
    sa, sb = _split_tok_specs(TOK_TILE, D_MODEL, b_is_stream)
    return pl.pallas_call(
        functools.partial(_inproj_kernel, l=l),
        grid=(N_TOK // TOK_TILE,),
        in_specs=[sa, sb, _mod_spec(l, 0), _mod_spec(l, 1), _rows_spec(D_MODEL),
                  _layer_spec(l, (IN_DIM, D_MODEL), pipeline_mode=pl.Buffered(1)),
                  _rows_spec(SGU_DIM), _layer_spec(l, (SGU_HEADS // 2, CHUNK, 2 * CHUNK)),
                  _layer_spec(l, (CHUNK, SGU_DIM))],
        out_specs=[tok(w) for w in widths],
        out_shape=[jax.ShapeDtypeStruct((N_TOK, w), F32) for w in widths],
        scratch_shapes=[pltpu.VMEM((_C_END, D_MODEL), BF16)],
        compiler_params=_params(),
        name="in_proj",
    )(xa, xb, mods, mods, norm_g, w_in, sgu_g, sgu_w_pair, sgu_b_full)


def _ssd_kernel(*refs, l, L, n_seq, has_h0, n_alias, emit_state):
    z_ref, xbc_ref, dt_ref, cw_ref, cb_ref, dtb_ref, alog_ref, dvec_ref, ng_ref = refs[:9]
    lrow = slice(l, l + 1)
    lane_fill = jnp.zeros((1, LANES - N_DIR * SSD_HEADS), F32)
    k = 9
    h0_ref = refs[k] if has_h0 else None
    k += int(has_h0) + n_alias
    y_ref = refs[k]
    hout_ref = refs[k + 1] if emit_state else None
    xpad, xc, acol, atr, cbs, bmt, dtsp, yacc, hst = refs[k + 1 + int(emit_state):]
    Q = CHUNK
    nc = L // Q
    halo = SUBLANES
    pad = (SSD_CONV - 1) // 2

    pitch = L + halo
    for q in range(n_seq + 1):
        xpad[q * pitch:q * pitch + halo, :] = jnp.zeros((halo, SSD_CONV_DIM), F32)
    for g in range(n_seq * nc):
        o = halo + (g // nc) * pitch + (g % nc) * Q
        xpad[o:o + Q, :] = xbc_ref[g * Q:(g + 1) * Q, :]
    win = Q + 2 * halo
    for c in range(n_seq * nc):
        for lb in range(0, SSD_CONV_DIM, CONV_LANES):
            ln = slice(lb, lb + CONV_LANES)
            o = (c // nc) * pitch + (c % nc) * Q
            xa = xpad[o:o + win, ln]
            acc = cb_ref[lrow, ln] + cw_ref[pad:pad + 1, ln] * xa[halo:halo + Q, :]
            for t in range(SSD_CONV):
                if t != pad:
                    rolled = pltpu.roll(xa, (pad - t) % win, 0)
                    acc = acc + cw_ref[t:t + 1, ln] * rolled[halo:halo + Q, :]
            xc[c * Q:(c + 1) * Q, ln] = _silu(acc)

    raw = dt_ref[...] + jnp.concatenate([dtb_ref[lrow, :], lane_fill], axis=1)
    dt = jnp.maximum(raw, 0.0) + jnp.log(1.0 + jnp.exp(-jnp.abs(raw)))
    dtsp[...] = dt
    a_neg = -jnp.exp(jnp.concatenate([alog_ref[lrow, :], lane_fill], axis=1))
    row = lax.broadcasted_iota(jnp.int32, (Q, Q), 0)
    col = lax.broadcasted_iota(jnp.int32, (Q, Q), 1)
    lower = row >= col
    upper = col >= row
    tri3 = jnp.concatenate([lower.astype(BF16)] * 3, axis=1)
    fwd_lane = lax.broadcasted_iota(jnp.int32, (Q, LANES), 1) < SSD_HEADS
    for c in range(n_seq * nc):
        dta = dtsp[c * Q:(c + 1) * Q, :] * a_neg
        hi = dta.astype(BF16)
        rem = dta - hi.astype(F32)
        mid = rem.astype(BF16)
        lo = (rem - mid.astype(F32)).astype(BF16)
        pre = jnp.dot(tri3, jnp.concatenate([hi, mid, lo], axis=0), preferred_element_type=F32)
        suf = pre[Q - 1:Q, :] - pre + dta
        a = jnp.where(fwd_lane, pre, suf)
        acol[c * Q:(c + 1) * Q, :] = a
        atr[c] = a.T[0:N_DIR * SSD_HEADS, :]
        for g in range(SSD_GROUPS):
            b0 = SSD_INNER + g * SSD_STATE
            c0 = SSD_INNER + (SSD_GROUPS + g) * SSD_STATE
            bm = xc[c * Q:(c + 1) * Q, b0:b0 + SSD_STATE]
            cbs[c, g] = lax.dot_general(xc[c * Q:(c + 1) * Q, c0:c0 + SSD_STATE].astype(BF16),
                                        bm.astype(BF16), _NT, preferred_element_type=F32)
            bmt[c, g] = bm.T.astype(BF16)

    sel_row = lax.broadcasted_iota(jnp.int32, (3 * LANES, SSD_INNER), 0) & (LANES - 1)
    sel_head = lax.broadcasted_iota(jnp.int32, (3 * LANES, SSD_INNER), 1) >> (SSD_HEAD_DIM.bit_length() - 1)
    sel = [(sel_row == sel_head + d * SSD_HEADS).astype(BF16) for d in range(N_DIR)]
    mask2 = [jnp.concatenate([m, m], axis=1) for m in (lower, upper)]

    for q in range(n_seq):
        for d in range(N_DIR):
            for p in range(N_PAIR):
                if has_h0:
                    hst[q, d, p] = h0_ref[q, d, p * PAIR:(p + 1) * PAIR, :].T
                else:
                    hst[q, d, p] = jnp.zeros((SSD_STATE, PAIR), F32)

    lo_lane = lax.broadcasted_iota(jnp.int32, (Q, PAIR), 1) < SSD_HEAD_DIM

    def chunk(c, d, q):
        r0 = pl.multiple_of(c * Q, Q)
        rows = pl.ds(r0, Q)
        a = acol[rows, :]
        a_t = atr[c]
        dtp = jnp.dot(_split3(dtsp[rows, :]), sel[d], preferred_element_type=F32)
        a_end = a[Q - 1:Q, :] if d == 0 else a[0:1, :]
        dec = jnp.exp(a_end)
        for g in range(SSD_GROUPS):
            c0 = SSD_INNER + (SSD_GROUPS + g) * SSD_STATE
            cmb = xc[rows, c0:c0 + SSD_STATE].astype(BF16)
            cb = cbs[c, g]
            cb2 = jnp.concatenate([cb, cb], axis=1)
            for pr in range(2):
                p = g * 2 + pr
                j0 = d * SSD_HEADS + 2 * p
                j1 = j0 + 1
                xs = xc[rows, p * PAIR:(p + 1) * PAIR]
                ab0 = jnp.broadcast_to(a[:, j0:j0 + 1], (Q, Q))
                ab1 = jnp.broadcast_to(a[:, j1:j1 + 1], (Q, Q))
                seg = (jnp.concatenate([ab0, ab1], axis=1)
                       - jnp.concatenate([a_t[j0:j0 + 1, :], a_t[j1:j1 + 1, :]], axis=1))
                m = (cb2 * jnp.exp(jnp.where(mask2[d], seg, -jnp.inf))).astype(BF16)
                xdt = xs * dtp[:, p * PAIR:(p + 1) * PAIR]
                rhs = jnp.concatenate([jnp.where(lo_lane, xdt, 0.0).astype(BF16),
                                       jnp.where(lo_lane, 0.0, xdt).astype(BF16)], axis=0)
                y_diag = jnp.dot(m, rhs, preferred_element_type=F32)
                hp = hst[q, d, p]
                y_off = jnp.dot(cmb, hp.astype(BF16), preferred_element_type=F32)
                a_pair = jnp.where(lo_lane, ab0, ab1)
                y = y_diag + jnp.exp(a_pair) * y_off
                a_end_pair = jnp.where(lo_lane[0:1, :], a_end[:, j0:j0 + 1], a_end[:, j1:j1 + 1])
                xw = (xdt * jnp.exp(a_end_pair - a_pair)).astype(BF16)
                s_new = jnp.dot(bmt[c, g], xw, preferred_element_type=F32)
                decp = jnp.where(lo_lane[0:1, :], dec[:, j0:j0 + 1], dec[:, j1:j1 + 1])
                hst[q, d, p] = decp * hp + s_new
                if d == 0:
                    y = y + dvec_ref[lrow, p * PAIR:(p + 1) * PAIR] * xs
                yacc[d, rows, p * PAIR:(p + 1) * PAIR] = y

    def step(t, carry):
        for q in range(n_seq):
            chunk(q * nc + t, 0, q)
            chunk(q * nc + nc - 1 - t, 1, q)
        return carry

    lax.fori_loop(0, nc, step, 0, unroll=min(nc, SSD_UNROLL))
    for c in range(n_seq * nc):
        rows = slice(c * Q, (c + 1) * Q)
        y = (yacc[0, rows, :] + yacc[1, rows, :]) * _silu(z_ref[rows, :])
        y_ref[rows, :] = _rms(y) * ng_ref[lrow, :]
    if emit_state:
        for q in range(n_seq):
            for d in range(N_DIR):
                for p in range(N_PAIR):
                    hout_ref[q, d, p * PAIR:(p + 1) * PAIR, :] = hst[q, d, p].T


def _ssd_mixer(z, xbc, dtr, prm, l, *, L, n_seq, nb, blk0, h0=None, y_buf=None, st_buf=None, after=None,
               emit_state):
    R = n_seq * L
    seq = lambda w: pl.BlockSpec((R, w), lambda b: (blk0 + b, 0))
    st = pl.BlockSpec((n_seq, None, N_DIR, SSD_HP, SSD_STATE), lambda b: (b, l, 0, 0, 0))
    args = [z, xbc, dtr, prm["conv_w"], prm["conv_b"], prm["dt_bias"], prm["a_log"], prm["d_vec"],
            prm["norm_g"]]
    in_specs = [seq(SSD_INNER), seq(SSD_CONV_DIM), seq(LANES),
                _layer_spec(l, (SSD_CONV, SSD_CONV_DIM)), _rows_spec(SSD_CONV_DIM),
                _rows_spec(N_DIR * SSD_HEADS), _rows_spec(N_DIR * SSD_HEADS),
                _rows_spec(SSD_INNER), _rows_spec(SSD_INNER)]
    if h0 is not None:
        args.append(h0)
        in_specs.append(st)
    out_shape = [jax.ShapeDtypeStruct((N_TOK, SSD_INNER), F32)]
    out_specs = [seq(SSD_INNER)]
    aliases = {}
    if y_buf is not None:
        aliases[len(args)] = 0
        args.append(y_buf)
        in_specs.append(_any_spec())
    if emit_state:
        out_shape.append(jax.ShapeDtypeStruct((BATCH, DEPTH, N_DIR, SSD_HP, SSD_STATE), F32))
        out_specs.append(st)
        if st_buf is not None:
            aliases[len(args)] = 1
            args.append(st_buf)
            in_specs.append(_any_spec())
    if after is not None:
        args.append(after)
        in_specs.append(_any_spec())
    res = pl.pallas_call(
        functools.partial(_ssd_kernel, l=l, L=L, n_seq=n_seq, has_h0=h0 is not None,
                          n_alias=len(aliases) + int(after is not None), emit_state=emit_state),
        grid=(nb,),
        in_specs=in_specs,
        out_specs=out_specs,
        out_shape=out_shape,
        input_output_aliases=aliases,
        scratch_shapes=[pltpu.VMEM((n_seq * (L + SUBLANES) + SUBLANES, SSD_CONV_DIM), F32),
                        pltpu.VMEM((R, SSD_CONV_DIM), F32),
                        pltpu.VMEM((R, LANES), F32),
                        pltpu.VMEM((R // CHUNK, N_DIR * SSD_HEADS, CHUNK), F32),
                        pltpu.VMEM((R // CHUNK, SSD_GROUPS, CHUNK, CHUNK), F32),
                        pltpu.VMEM((R // CHUNK, SSD_GROUPS, SSD_STATE, CHUNK), BF16),
                        pltpu.VMEM((R, LANES), F32),
                        pltpu.VMEM((N_DIR, R, SSD_INNER), F32),
                        pltpu.VMEM((n_seq, N_DIR, N_PAIR, PAIR, SSD_STATE), F32)],
        compiler_params=_params(),
        name=f"ssd_mixer_L{L}",
    )(*args)
    return res if emit_state else (res[0], None)


_LOG2_CH = S5_GROUP_CH.bit_length() - 1
_LOG2_ST = S5_STATE.bit_length() - 1


def _cmul(ar, ai, br, bi):
    return ar * br - ai * bi, ar * bi + ai * br


def _s5prep_kernel(lre_ref, lim_ref, ldt_ref, btr_ref, bti_ref, ctr_ref, cti_ref,
                   bblk_ref, cblk_ref, pw_ref):
    brow = lax.broadcasted_iota(jnp.int32, (S5_DIM, S5_LANES), 0) >> _LOG2_CH
    bcol = lax.broadcasted_iota(jnp.int32, (S5_DIM, S5_LANES), 1) >> _LOG2_ST
    bmask = brow == bcol
    p1 = lax.broadcasted_iota(jnp.int32, (S5_SEG, S5_LANES), 0) + 1
    for d in range(N_DIR):
        lre = lre_ref[d]
        lim = lim_ref[d]
        step = jnp.exp(ldt_ref[d])
        mag = jnp.exp(lre * step)
        lbr = mag * jnp.cos(lim * step)
        lbi = mag * jnp.sin(lim * step)
        den = lre * lre + lim * lim
        nr = lbr - 1.0
        cr = (nr * lre + lbi * lim) / den
        ci = (lbi * lre - nr * lim) / den
        br, bi = _cmul(cr, ci, btr_ref[...], bti_ref[...])
        br = jnp.where(bmask, br, 0.0).astype(BF16)
        bi = jnp.where(bmask, bi, 0.0).astype(BF16)
        for sl in range(S5_NSLAB):
            bblk_ref[d, sl, :, 0:S5_SLAB] = br[:, sl * S5_SLAB:(sl + 1) * S5_SLAB]
            bblk_ref[d, sl, :, S5_SLAB:2 * S5_SLAB] = bi[:, sl * S5_SLAB:(sl + 1) * S5_SLAB]
        rr = jnp.ones((S5_SEG, S5_LANES), F32)
        ri = jnp.zeros((S5_SEG, S5_LANES), F32)
        sr, si = lbr, lbi
        for k in range(S5_SEG.bit_length()):
            bit = ((p1 >> k) & 1) == 1
            tr, ti = _cmul(rr, ri, sr, si)
            rr = jnp.where(bit, tr, rr)
            ri = jnp.where(bit, ti, ri)
            sr, si = _cmul(sr, si, sr, si)
        pw_ref[d, 0, 0:S5_SEG, :] = rr
        pw_ref[d, 1, 0:S5_SEG, :] = ri
        n = S5_SEG
        while n < S5_PW_ROWS:
            tr, ti = _cmul(pw_ref[d, 0, 0:n, :], pw_ref[d, 1, 0:n, :],
                           pw_ref[d, 0, n - 1:n, :], pw_ref[d, 1, n - 1:n, :])
            pw_ref[d, 0, n:2 * n, :] = tr
            pw_ref[d, 1, n:2 * n, :] = ti
            n *= 2
    crow = lax.broadcasted_iota(jnp.int32, (S5_LANES, S5_DIM), 0) >> _LOG2_ST
    ccol = lax.broadcasted_iota(jnp.int32, (S5_LANES, S5_DIM), 1) >> _LOG2_CH
    cmask = crow == ccol
    cr = jnp.where(cmask, ctr_ref[...], 0.0).astype(BF16)
    ci = jnp.where(cmask, -cti_ref[...], 0.0).astype(BF16)
    for sl in range(S5_NSLAB):
        cblk_ref[sl, 0:S5_SLAB, :] = cr[sl * S5_SLAB:(sl + 1) * S5_SLAB, :]
        cblk_ref[sl, S5_SLAB:2 * S5_SLAB, :] = ci[sl * S5_SLAB:(sl + 1) * S5_SLAB, :]


def _s5_prep(lam_re, lam_im, log_dt, b_re, b_im, c_re, c_im):
    row = lambda t: t.reshape(DEPTH, N_DIR, 1, S5_LANES)
    ldt = jnp.repeat(log_dt, S5_STATE, axis=-1)
    bt = lambda t: jnp.tile(t.transpose(0, 3, 1, 2).reshape(DEPTH, S5_GROUP_CH, S5_LANES),
                            (1, S5_GROUPS, 1))
    ct = lambda t: jnp.tile(t.transpose(0, 1, 3, 2).reshape(DEPTH, S5_LANES, S5_GROUP_CH),
                            (1, 1, S5_GROUPS))
    vec = pl.BlockSpec((None, N_DIR, 1, S5_LANES), lambda l: (l, 0, 0, 0))
    bsp = pl.BlockSpec((None, S5_DIM, S5_LANES), lambda l: (l, 0, 0))
    csp = pl.BlockSpec((None, S5_LANES, S5_DIM), lambda l: (l, 0, 0))
    return pl.pallas_call(
        _s5prep_kernel,
        grid=(DEPTH,),
        in_specs=[vec, vec, vec, bsp, bsp, csp, csp],
        out_specs=[pl.BlockSpec((None, N_DIR, S5_NSLAB, S5_DIM, 2 * S5_SLAB), lambda l: (l, 0, 0, 0, 0)),
                   pl.BlockSpec((None, S5_NSLAB, 2 * S5_SLAB, S5_DIM), lambda l: (l, 0, 0, 0)),
                   pl.BlockSpec((None, N_DIR, 2, S5_PW_ROWS, S5_LANES), lambda l: (l, 0, 0, 0, 0))],
        out_shape=[jax.ShapeDtypeStruct((DEPTH, N_DIR, S5_NSLAB, S5_DIM, 2 * S5_SLAB), BF16),
                   jax.ShapeDtypeStruct((DEPTH, S5_NSLAB, 2 * S5_SLAB, S5_DIM), BF16),
                   jax.ShapeDtypeStruct((DEPTH, N_DIR, 2, S5_PW_ROWS, S5_LANES), F32)],
        compiler_params=_params(),
        name="s5_prep",
    )(row(lam_re), row(lam_im), row(ldt), bt(b_re), bt(b_im), ct(c_re), ct(c_im))


def _s5_moves(n_seq, n_seg, seq_len, col_major):
    steps = seq_len // n_seg
    if not col_major:
        return [(q * seq_len + s * steps, steps, q * n_seg + s, S5_SEG)
                for q in range(n_seq) for s in range(n_seg)]
    assert n_seq == 1 and n_seg == S5_SEG
    grid_rows = seq_len // GRID_W
    wseg = GRID_W // n_seg
    return [(r * GRID_W + s * wseg, wseg, r * S5_SEG + s, grid_rows * S5_SEG)
            for r in range(grid_rows) for s in range(n_seg)]


def _s5_kernel(*refs, l, n_seq, n_seg, seq_len, col_major, has_h0, n_alias, emit_state):
    u_ref, bblk_ref, cblk_ref, pw_ref, dvec_ref, gw_ref, gb_ref = refs[:7]
    k = 7
    h0r_ref, h0i_ref = (refs[k], refs[k + 1]) if has_h0 else (None, None)
    k += 2 * int(has_h0) + n_alias
    y_ref = refs[k]
    fr_ref, fi_ref = (refs[k + 1], refs[k + 2]) if emit_state else (None, None)
    up, buf_f, buf_b, yacc, yp, cin, fin = refs[k + 1 + 2 * int(emit_state):]
    assert n_seq * n_seg == S5_SEG and not (has_h0 and n_seg == 1) and not (emit_state and n_seg > 1)
    n_rows = n_seq * seq_len
    steps = seq_len // n_seg
    n_planes = S5_DIM // LANES
    W = S5_SLAB
    moves = _s5_moves(n_seq, n_seg, seq_len, col_major)

    for src, n, dst, stride in moves:
        for t in range(n_planes):
            up[t, pl.ds(dst, n, stride=stride), :] = u_ref[src:src + n, t * LANES:(t + 1) * LANES]

    def u_rows(rs):
        return jnp.concatenate([up[t, rs, :] for t in range(n_planes)], axis=1)

    zero = jnp.zeros((S5_SEG, W), F32)
    zrow = jnp.zeros((1, W), F32)
    for sl in range(S5_NSLAB):
        ln = slice(sl * W, (sl + 1) * W)
        for r0 in range(0, n_rows, ROW_SLAB):
            rs = slice(r0, r0 + ROW_SLAB)
            ub = u_rows(rs).astype(BF16)
            buf_f[rs, :] = jnp.dot(ub, bblk_ref[0, sl], preferred_element_type=F32)
            buf_b[rs, :] = jnp.dot(ub, bblk_ref[1, sl], preferred_element_type=F32)
        lam = [[jnp.broadcast_to(pw_ref[d, c, 0:1, ln], (S5_SEG, W)) for c in range(2)]
               for d in range(N_DIR)]

        def body(t, carry):
            fr, fi, br, bi = carry
            rf = pl.ds(pl.multiple_of(t * S5_SEG, S5_SEG), S5_SEG)
            rb = pl.ds(pl.multiple_of((steps - 1 - t) * S5_SEG, S5_SEG), S5_SEG)
            nfr = lam[0][0] * fr - lam[0][1] * fi + buf_f[rf, 0:W]
            nfi = lam[0][0] * fi + lam[0][1] * fr + buf_f[rf, W:2 * W]
            nbr = lam[1][0] * br - lam[1][1] * bi + buf_b[rb, 0:W]
            nbi = lam[1][0] * bi + lam[1][1] * br + buf_b[rb, W:2 * W]
            buf_f[rf, 0:W] = nfr
            buf_f[rf, W:2 * W] = nfi
            buf_b[rb, 0:W] = nbr
            buf_b[rb, W:2 * W] = nbi
            return nfr, nfi, nbr, nbi

        ends = lax.fori_loop(0, steps, body, (zero, zero, zero, zero), unroll=S5_UNROLL)

        if n_seg == 1:
            if emit_state:
                for d in range(N_DIR):
                    fin[d, 0, :, ln] = ends[2 * d]
                    fin[d, 1, :, ln] = ends[2 * d + 1]
        else:
            for d in range(N_DIR):
                er, ei = ends[2 * d], ends[2 * d + 1]
                lpr = pw_ref[d, 0, steps - 1:steps, ln]
                lpi = pw_ref[d, 1, steps - 1:steps, ln]
                pr = h0r_ref[d:d + 1, ln] if has_h0 else zrow
                pi = h0i_ref[d:d + 1, ln] if has_h0 else zrow
                order = range(n_seg) if d == 0 else range(n_seg - 1, -1, -1)
                for s in order:
                    cin[d, 0, s:s + 1, :] = pr
                    cin[d, 1, s:s + 1, :] = pi
                    tr, ti = _cmul(lpr, lpi, pr, pi)
                    pr = er[s:s + 1, :] + tr
                    pi = ei[s:s + 1, :] + ti
            cfr = cin[0, 0]
            cfi = cin[0, 1]
            cbr = cin[1, 0]
            cbi = cin[1, 1]

            def fix(p, carry):
                rows = pl.ds(pl.multiple_of(p * S5_SEG, S5_SEG), S5_SEG)
                q = steps - 1 - p
                tfr, tfi = _cmul(pw_ref[0, 0, pl.ds(p, 1), ln], pw_ref[0, 1, pl.ds(p, 1), ln], cfr, cfi)
                tbr, tbi = _cmul(pw_ref[1, 0, pl.ds(q, 1), ln], pw_ref[1, 1, pl.ds(q, 1), ln], cbr, cbi)
                buf_f[rows, 0:W] = buf_f[rows, 0:W] + buf_b[rows, 0:W] + tfr + tbr
                buf_f[rows, W:2 * W] = buf_f[rows, W:2 * W] + buf_b[rows, W:2 * W] + tfi + tbi
                return carry

            lax.fori_loop(0, steps, fix, 0, unroll=S5_UNROLL)

        for r0 in range(0, n_rows, ROW_SLAB):
            rs = slice(r0, r0 + ROW_SLAB)
            hs = buf_f[rs, :] if n_seg > 1 else buf_f[rs, :] + buf_b[rs, :]
            part = jnp.dot(hs.astype(BF16), cblk_ref[sl], preferred_element_type=F32)
            if sl == 0:
                yacc[rs, :] = part
            else:
                yacc[rs, :] = yacc[rs, :] + part

    gwb = gw_ref[...].astype(BF16)
    for r0 in range(0, n_rows, ROW_SLAB):
        rs = slice(r0, r0 + ROW_SLAB)
        y = _gelu_tanh(yacc[rs, :] + dvec_ref[l:l + 1, :] * u_rows(rs))
        gate = jnp.dot(y.astype(BF16), gwb, preferred_element_type=F32) + gb_ref[l:l + 1, :]
        y = y * jax.nn.sigmoid(gate)
        for t in range(n_planes):
            yp[t, rs, :] = y[:, t * LANES:(t + 1) * LANES]
    for src, n, dst, stride in moves:
        for t in range(n_planes):
            y_ref[src:src + n, t * LANES:(t + 1) * LANES] = yp[t, pl.ds(dst, n, stride=stride), :]
    if emit_state:
        for q in range(n_seq):
            for d in range(N_DIR):
                fr_ref[q, d:d + 1, :] = fin[d, 0, q:q + 1, :]
                fi_ref[q, d:d + 1, :] = fin[d, 1, q:q + 1, :]


def _s5_mixer(u, bblk, cblk, pw, prm, l, *, seq_len, n_seq, nb, blk0, col_major, h0=None, y_buf=None,
              st_buf=None, emit_state):
    n_rows = n_seq * seq_len
    n_seg = S5_SEG // n_seq
    seq = pl.BlockSpec((n_rows, S5_DIM), lambda b: (blk0 + b, 0))
    st = pl.BlockSpec((n_seq, None, N_DIR, S5_LANES), lambda b: (b, l, 0, 0))
    once = dict(pipeline_mode=pl.Buffered(1))
    args = [u, bblk, cblk, pw, prm["d_vec"], prm["glu_w"], prm["glu_b"]]
    in_specs = [seq, _layer_spec(l, (N_DIR, S5_NSLAB, S5_DIM, 2 * S5_SLAB), **once),
                _layer_spec(l, (S5_NSLAB, 2 * S5_SLAB, S5_DIM), **once),
                _layer_spec(l, (N_DIR, 2, S5_PW_ROWS if n_seg > 1 else S5_SEG, S5_LANES), **once),
                _rows_spec(S5_DIM), _layer_spec(l, (S5_DIM, S5_DIM)), _rows_spec(S5_DIM)]
    if h0 is not None:
        h0_spec = pl.BlockSpec((None, None, N_DIR, S5_LANES), lambda b: (b, l, 0, 0))
        args += list(h0)
        in_specs += [h0_spec, h0_spec]
    out_shape = [jax.ShapeDtypeStruct((N_TOK, S5_DIM), F32)]
    out_specs = [seq]
    aliases = {}
    if y_buf is not None:
        aliases[len(args)] = 0
        args.append(y_buf)
        in_specs.append(_any_spec())
    if emit_state:
        out_shape += [jax.ShapeDtypeStruct((BATCH, DEPTH, N_DIR, S5_LANES), F32)] * 2
        out_specs += [st, st]
        if st_buf is not None:
            for j, buf in enumerate(st_buf):
                aliases[len(args)] = 1 + j
                args.append(buf)
                in_specs.append(_any_spec())
    res = pl.pallas_call(
        functools.partial(_s5_kernel, l=l, n_seq=n_seq, n_seg=n_seg, seq_len=seq_len, col_major=col_major,
                          has_h0=h0 is not None, n_alias=len(aliases), emit_state=emit_state),
        grid=(nb,),
        in_specs=in_specs,
        out_specs=out_specs,
        out_shape=out_shape,
        input_output_aliases=aliases,
        scratch_shapes=[pltpu.VMEM((S5_DIM // LANES, n_rows, LANES), F32),
                        pltpu.VMEM((n_rows, 2 * S5_SLAB), F32),
                        pltpu.VMEM((n_rows, 2 * S5_SLAB), F32),
                        pltpu.VMEM((n_rows, S5_DIM), F32),
                        pltpu.VMEM((S5_DIM // LANES, n_rows, LANES), F32),
                        pltpu.VMEM((N_DIR, 2, S5_SEG, S5_SLAB), F32),
                        pltpu.VMEM((N_DIR, 2, S5_SEG, S5_LANES), F32)],
        compiler_params=_params(),
        name=f"s5_mixer_L{seq_len}",
    )(*args)
    return (res[0], res[1], res[2]) if emit_state else (res[0], None, None)


def _outffn_kernel(xa_ref, xb_ref, ys_ref, yg_ref, y5_ref, g1_ref, sh_ref, sc_ref, g2_ref, ng_ref,
                   fg_ref, wo_ref, w1_ref, w2_ref, *rest, l, final):
    n_out = 2 if final else 1
    o_refs = rest[:n_out]
    wob, w1b, w2b = rest[n_out:]
    i = pl.program_id(0)

    @pl.when(i < FFN_PREP)
    def _():
        wob[i] = wo_ref[...].astype(BF16)
        w1b[i] = w1_ref[...].astype(BF16)
        w2b[i] = w2_ref[...].astype(BF16)

    @pl.when(i >= FFN_PREP)
    def _():
        t = i - FFN_PREP
        n_ctx = N_CTX // FFN_TILE
        r = _mod_row(t, FFN_TILE)
        row = lambda ref: ref[pl.ds(r, 1), :]
        x = jnp.where(t < n_ctx, xa_ref[...], xb_ref[...])
        mixed = jnp.concatenate([ys_ref[...].astype(BF16), yg_ref[...].astype(BF16),
                                 y5_ref[...].astype(BF16)], axis=1)
        w_out = wob[...].reshape(D_MODEL, D_MODEL)
        x1 = x + row(g1_ref) * jnp.dot(mixed, w_out, preferred_element_type=F32)
        h = ((_rms(x1) * ng_ref[l:l + 1, :]) * (1.0 + row(sc_ref)) + row(sh_ref)).astype(BF16)
        acc = jnp.zeros((FFN_TILE, D_MODEL), F32)
        for j in range(FFN_PREP):
            a = jnp.maximum(jnp.dot(h, w1b[j], preferred_element_type=F32), 0.0)
            acc = acc + jnp.dot((a * a).astype(BF16), w2b[j], preferred_element_type=F32)
        x2 = x1 + row(g2_ref) * acc
        if not final:
            o_refs[0][...] = x2
        else:
            y = _rms(x2) * fg_ref[...]

            @pl.when(t < n_ctx)
            def _():
                o_refs[0][...] = y

            @pl.when(t >= n_ctx)
            def _():
                o_refs[1][...] = y


def _out_ffn(xa, xb, b_is_stream, y_ssd, y_sgu, y_s5, l, mods, norm_g, final_g, w_out, w1, w2, *, final):
    tile_of = lambda i: jnp.maximum(i - FFN_PREP, 0)
    tok = lambda w: pl.BlockSpec((FFN_TILE, w), lambda i: (tile_of(i), 0))
    chunk_of = lambda i: jnp.minimum(i, FFN_PREP - 1)
    sa, sb = _split_tok_specs(FFN_TILE, D_MODEL, b_is_stream, tile_of)
    if final:
        out_specs = list(_split_tok_specs(FFN_TILE, D_MODEL, False, tile_of))
        out_shape = [jax.ShapeDtypeStruct((N_CTX, D_MODEL), F32), jax.ShapeDtypeStruct((N_LAT, D_MODEL), F32)]
    else:
        out_specs = [tok(D_MODEL)]
        out_shape = [jax.ShapeDtypeStruct((N_TOK, D_MODEL), F32)]
    ko, kf = D_MODEL // FFN_PREP, D_FF // FFN_PREP
    return pl.pallas_call(
        functools.partial(_outffn_kernel, l=l, final=final),
        grid=(FFN_PREP + N_TOK // FFN_TILE,),
        in_specs=[sa, sb, tok(SSD_INNER), tok(SGU_DIM), tok(S5_DIM),
                  _mod_spec(l, 2), _mod_spec(l, 3), _mod_spec(l, 4), _mod_spec(l, 5),
                  _rows_spec(D_MODEL), pl.BlockSpec((1, D_MODEL), lambda i: (0, 0)),
                  pl.BlockSpec((None, ko, D_MODEL), lambda i: (l, chunk_of(i), 0)),
                  pl.BlockSpec((None, D_MODEL, kf), lambda i: (l, 0, chunk_of(i))),
                  pl.BlockSpec((None, kf, D_MODEL), lambda i: (l, chunk_of(i), 0))],
        out_specs=out_specs,
        out_shape=out_shape,
        scratch_shapes=[pltpu.VMEM((FFN_PREP, ko, D_MODEL), BF16),
                        pltpu.VMEM((FFN_PREP, D_MODEL, kf), BF16),
                        pltpu.VMEM((FFN_PREP, kf, D_MODEL), BF16)],
        compiler_params=_params(),
        name="out_ffn_final" if final else "out_ffn",
    )(xa, xb, y_ssd, y_sgu, y_s5, mods, mods, mods, mods, norm_g, final_g, w_out, w1, w2)


def kernel(x_prompt, x_sample, state_ssd, state_s5_re, state_s5_im, c, c_ctx, ada_w, ada_b, norm1_g,
           norm2_g, w_in, ssd_conv_w, ssd_conv_b, ssd_dt_bias, ssd_a_log, ssd_d, ssd_norm_g,
           sgu_norm_g, sgu_w, sgu_b, s5_lambda_re, s5_lambda_im, s5_log_dt, s5_b_re, s5_b_im,
           s5_c_re, s5_c_im, s5_d, s5_glu_w, s5_glu_b, w_out, ffn_w1, ffn_w2, final_norm_g):
    cvec = jnp.concatenate([c_ctx[None, :], c,
                            jnp.zeros((MOD_ROWS - 1 - DEC_BATCH, D_MODEL), F32)], axis=0)
    mods = _adaln_mods(cvec, ada_w, ada_b)
    bblk, cblk, pw = _s5_prep(s5_lambda_re, s5_lambda_im, s5_log_dt, s5_b_re, s5_b_im,
                              s5_c_re, s5_c_im)

    w_in_t = w_in.transpose(0, 2, 1)
    ssd_prm = dict(conv_w=ssd_conv_w, conv_b=ssd_conv_b,
                   dt_bias=ssd_dt_bias.reshape(DEPTH, N_DIR * SSD_HEADS),
                   a_log=ssd_a_log.reshape(DEPTH, N_DIR * SSD_HEADS),
                   d_vec=jnp.repeat(ssd_d, SSD_HEAD_DIM, axis=-1), norm_g=ssd_norm_g)
    s5_prm = dict(d_vec=s5_d, glu_w=s5_glu_w, glu_b=s5_glu_b)
    sgu_w_pair = sgu_w.reshape(DEPTH, SGU_HEADS // 2, 2, CHUNK, CHUNK).transpose(0, 1, 3, 2, 4)
    sgu_w_pair = sgu_w_pair.reshape(DEPTH, SGU_HEADS // 2, CHUNK, 2 * CHUNK).astype(BF16)
    sgu_b_full = jnp.repeat(sgu_b.transpose(0, 2, 1), SGU_DIM // SGU_HEADS, axis=2)
    norm1 = norm1_g
    norm2 = norm2_g
    sgu_g = sgu_norm_g
    final_g = final_norm_g.reshape(1, D_MODEL)

    lat_ssd = state_ssd.reshape(DEC_BATCH, DEPTH, N_DIR, SSD_HP, SSD_STATE)
    lat_s5 = (state_s5_re.reshape(DEC_BATCH, DEPTH, N_DIR, S5_LANES),
              state_s5_im.reshape(DEC_BATCH, DEPTH, N_DIR, S5_LANES))
    lat_blk = N_CTX // DEC_SEQ

    xa = x_prompt.reshape(N_CTX, D_MODEL)
    xb = x_sample.reshape(N_LAT, D_MODEL)
    b_is_stream = False
    st_ssd = None
    st_s5 = None
    for l in range(DEPTH):
        z, xbc, y_sgu, s5u, dtr = _in_proj(xa, xb, b_is_stream, l, mods, norm1, w_in_t, sgu_g,
                                           sgu_w_pair, sgu_b_full)
        y_s5, fr, fi = _s5_mixer(s5u, bblk, cblk, pw, s5_prm, l, seq_len=SEQ, n_seq=S5_SEG,
                                 nb=BATCH // S5_SEG, blk0=0, col_major=False, st_buf=st_s5,
                                 emit_state=True)
        st_s5 = (fr, fi)
        y_s5, _, _ = _s5_mixer(s5u, bblk, cblk, pw, s5_prm, l, seq_len=DEC_SEQ, n_seq=1, nb=DEC_BATCH,
                               blk0=lat_blk, col_major=True, h0=lat_s5, y_buf=y_s5, emit_state=False)
        y_ssd, st_ssd = _ssd_mixer(z, xbc, dtr, ssd_prm, l, L=SEQ, n_seq=SSD_CTX_SEQS,
                                   nb=BATCH // SSD_CTX_SEQS, blk0=0, st_buf=st_ssd, after=y_s5,
                                   emit_state=True)
        y_ssd, _ = _ssd_mixer(z, xbc, dtr, ssd_prm, l, L=DEC_SEQ, n_seq=1, nb=DEC_BATCH, blk0=lat_blk,
                              h0=lat_ssd, y_buf=y_ssd, emit_state=False)
        final = l == DEPTH - 1
        out = _out_ffn(xa, xb, b_is_stream, y_ssd, y_sgu, y_s5, l, mods, norm2, final_g, w_out, ffn_w1,
                       ffn_w2, final=final)
        if not final:
            xa = xb = out[0]
            b_is_stream = True

    y_prompt = out[0].reshape(BATCH, SEQ, D_MODEL)
    y_sample = out[1].reshape(DEC_BATCH, DEC_SEQ, D_MODEL)
    new_state_ssd = st_ssd.reshape(BATCH, DEPTH, N_DIR, SSD_HEADS, SSD_HEAD_DIM, SSD_STATE)
    s5_shape = (BATCH, DEPTH, N_DIR, S5_GROUPS, S5_STATE)
    return (y_prompt, y_sample, new_state_ssd, st_s5[0].reshape(s5_shape), st_s5[1].reshape(s5_shape))
```

```python
import functools
import math

import jax
import jax.numpy as jnp
from jax import lax
from jax.experimental import pallas as pl
from jax.experimental.pallas import tpu as pltpu

F32 = jnp.float32
BF16 = jnp.bfloat16

D_MODEL = 1024
BATCH = 16
SEQ = 256
DEPTH = 2
DEC_BATCH = 2
DEC_SEQ = 1024
GRID_W = 64
CHUNK = 128
N_DIR = 2
EPS = 1e-6
SSD_INNER = 512
SSD_HEAD_DIM = 64
SSD_HEADS = 8
SSD_GROUPS = 2
SSD_STATE = 128
SSD_CONV = 5
SSD_CONV_DIM = SSD_INNER + 2 * SSD_GROUPS * SSD_STATE
SGU_DIM = 256
SGU_HEADS = 4
S5_DIM = 256
S5_GROUP_CH = 16
S5_GROUPS = 16
S5_STATE = 64
D_FF = 4 * D_MODEL
OFF_XBC = SSD_INNER
OFF_DT = OFF_XBC + SSD_CONV_DIM
OFF_SGU = OFF_DT + N_DIR * SSD_HEADS
OFF_S5 = OFF_SGU + 2 * SGU_DIM
IN_DIM = OFF_S5 + S5_DIM

N_CTX = BATCH * SEQ
N_LAT = DEC_BATCH * DEC_SEQ
N_TOK = N_CTX + N_LAT
LANES = 128
SUBLANES = 8
MOD_ROWS = SUBLANES
SSD_HP = SSD_HEADS * SSD_HEAD_DIM
S5_LANES = S5_GROUPS * S5_STATE
S5_SEG = SUBLANES
S5_PW_ROWS = DEC_SEQ // S5_SEG
S5_SLAB = 256
S5_NSLAB = S5_LANES // S5_SLAB
S5_UNROLL = 16
PAIR = 2 * SSD_HEAD_DIM
N_PAIR = SSD_HEADS // 2
TOK_TILE = 512
N_CTX_TILES = N_CTX // TOK_TILE
FFN_TILE = 512
FFN_PREP = 8
ROW_SLAB = 256
CONV_LANES = 256
SSD_CTX_SEQS = 2
SSD_UNROLL = 8
VMEM_LIMIT = 56 * 1024 * 1024

_NT = (((1,), (1,)), ((), ()))


def _params(n_axes=1):
    return pltpu.CompilerParams(dimension_semantics=("arbitrary",) * n_axes,
                                vmem_limit_bytes=VMEM_LIMIT)


def _layer_spec(l, shape, **kw):
    return pl.BlockSpec((None,) + tuple(shape), lambda *_: (l,) + (0,) * len(shape), **kw)


def _mod_spec(l, k):
    return pl.BlockSpec((None, MOD_ROWS, D_MODEL), lambda *_: (l, 0, k))


def _rows_spec(n):
    return pl.BlockSpec((DEPTH, n), lambda *_: (0, 0))


def _any_spec():
    return pl.BlockSpec(memory_space=pl.ANY)


def _mod_row(i, tm):
    n_ctx = N_CTX // tm
    return jnp.where(i < n_ctx, 0, 1 + (i - n_ctx) // (DEC_SEQ // tm))


def _split_tok_specs(tm, width, b_has_ctx_rows, tile_of=lambda i: i):
    n_ctx = N_CTX // tm
    b_off = n_ctx if b_has_ctx_rows else 0
    a = pl.BlockSpec((tm, width), lambda i: (jnp.minimum(tile_of(i), n_ctx - 1), 0))
    b = pl.BlockSpec((tm, width), lambda i: (jnp.maximum(tile_of(i), n_ctx) - n_ctx + b_off, 0))
    return a, b


def _silu(x):
    return x * jax.nn.sigmoid(x)


def _gelu_tanh(x):
    c = math.sqrt(2.0 / math.pi)
    return 0.5 * x * (1.0 + jnp.tanh(c * (x + 0.044715 * (x * x * x))))


def _rms(x):
    return x * lax.rsqrt(jnp.mean(x * x, axis=-1, keepdims=True) + EPS)


def _bdot(a, b):
    return jnp.dot(a.astype(BF16), b.astype(BF16), preferred_element_type=F32)


def _split3(x):
    hi = x.astype(BF16)
    r = x - hi.astype(F32)
    mid = r.astype(BF16)
    lo = (r - mid.astype(F32)).astype(BF16)
    return jnp.concatenate([hi, mid, lo], axis=1)


def _mod_kernel(c_ref, w_ref, b_ref, o_ref):
    bias = b_ref[pl.ds(pl.program_id(0), 1), :]
    o_ref[...] = _bdot(_silu(c_ref[...]), w_ref[...]) + bias


def _adaln_mods(cvec, ada_w, ada_b):
    n_blk = 6
    return pl.pallas_call(
        _mod_kernel,
        grid=(DEPTH, n_blk),
        in_specs=[pl.BlockSpec((MOD_ROWS, D_MODEL), lambda l, j: (0, 0)),
                  pl.BlockSpec((None, D_MODEL, D_MODEL), lambda l, j: (l, 0, j)),
                  pl.BlockSpec((DEPTH, D_MODEL), lambda l, j: (0, j))],
        out_specs=pl.BlockSpec((None, MOD_ROWS, D_MODEL), lambda l, j: (l, 0, j)),
        out_shape=jax.ShapeDtypeStruct((DEPTH, MOD_ROWS, 6 * D_MODEL), F32),
        compiler_params=_params(2),
        name="adaln_mod",
    )(cvec, ada_w, ada_b)


_C_Z = 0
_C_XBC = _C_Z + SSD_INNER
_C_SGU = _C_XBC + SSD_CONV_DIM
_C_S5 = _C_SGU + 2 * SGU_DIM
_C_DT = _C_S5 + S5_DIM
_C_END = _C_DT + LANES


def _inproj_kernel(xa_ref, xb_ref, sh_ref, sc_ref, g_ref, win_ref, sg_ref, sw_ref, sb_ref,
                   z_ref, xbc_ref, ysgu_ref, s5_ref, dt_ref, w_ref, *, l):
    i = pl.program_id(0)

    @pl.when(i == 0)
    def _():
        moves = ((0, OFF_DT, _C_Z), (OFF_SGU, IN_DIM, _C_SGU), (OFF_DT, OFF_SGU, _C_DT))
        for src0, src1, dst0 in moves:
            for r0 in range(src0, src1, ROW_SLAB):
                n = min(ROW_SLAB, src1 - r0)
                w_ref[dst0 + r0 - src0:dst0 + r0 - src0 + n, :] = win_ref[r0:r0 + n, :].astype(BF16)
        n_pad = LANES - N_DIR * SSD_HEADS
        w_ref[_C_END - n_pad:_C_END, :] = jnp.zeros((n_pad, D_MODEL), BF16)

    r = _mod_row(i, TOK_TILE)
    x = jnp.where(i < N_CTX_TILES, xa_ref[...], xb_ref[...])
    shift = sh_ref[pl.ds(r, 1), :]
    scale = sc_ref[pl.ds(r, 1), :]
    h = (_rms(x) * g_ref[l:l + 1, :]) * (1.0 + scale) + shift
    hb = h.astype(BF16)
    proj = lambda c0, c1: lax.dot_general(hb, w_ref[c0:c1, :], _NT, preferred_element_type=F32)
    uv = _gelu_tanh(proj(_C_SGU, _C_S5))
    z_ref[...] = proj(_C_Z, _C_XBC)
    xbc_ref[...] = proj(_C_XBC, _C_SGU)
    s5_ref[...] = proj(_C_S5, _C_DT)
    dt_ref[...] = proj(_C_DT, _C_END)

    u = uv[:, :SGU_DIM]
    v = _rms(uv[:, SGU_DIM:]) * sg_ref[l:l + 1, :]
    lo_lane = lax.broadcasted_iota(jnp.int32, (CHUNK, LANES), 1) < (LANES // 2)
    for c in range(TOK_TILE // CHUNK):
        rows = slice(c * CHUNK, (c + 1) * CHUNK)
        mix = []
        for pr in range(SGU_HEADS // 2):
            vp = v[rows, pr * LANES:(pr + 1) * LANES]
            rhs = jnp.concatenate([jnp.where(lo_lane, vp, 0.0).astype(BF16),
                                   jnp.where(lo_lane, 0.0, vp).astype(BF16)], axis=0)
            mix.append(jnp.dot(sw_ref[pr], rhs, preferred_element_type=F32))
        ysgu_ref[rows, :] = u[rows, :] * (jnp.concatenate(mix, axis=1) + sb_ref[...])


def _in_proj(xa, xb, b_is_stream, l, mods, norm_g, w_in, sgu_g, sgu_w_pair, sgu_b_full):
    tok = lambda w: pl.BlockSpec((TOK_TILE, w), lambda i: (i, 0))
    widths = (SSD_INNER, SSD_CONV_DIM, SGU_DIM, S5_DIM, LANES)
    sa, sb = _split_tok_specs(TOK_TILE, D_MODEL, b_is_stream)
    return pl.pallas_call(
        functools.partial(_inproj_kernel, l=l),
        grid=(N_TOK // TOK_TILE,),
        in_specs=[sa, sb, _mod_spec(l, 0), _mod_spec(l, 1), _rows_spec(D_MODEL),
                  _layer_spec(l, (IN_DIM, D_MODEL), pipeline_mode=pl.Buffered(1)),
                  _rows_spec(SGU_DIM), _layer_spec(l, (SGU_HEADS // 2, CHUNK, 2 * CHUNK)),
                  _layer_spec(l, (CHUNK, SGU_DIM))],
        out_specs=[tok(w) for w in widths],
        out_shape=[jax.ShapeDtypeStruct((N_TOK, w), F32) for w in widths],
        scratch_shapes=[pltpu.VMEM((_C_END, D_MODEL), BF16)],
        compiler_params=_params(),
        name="in_proj",
    )(xa, xb, mods, mods, norm_g, w_in, sgu_g, sgu_w_pair, sgu_b_full)


def _ssd_kernel(*refs, l, L, n_seq, has_h0, n_alias, emit_state):
    z_ref, xbc_ref, dt_ref, cw_ref, cb_ref, dtb_ref, alog_ref, dvec_ref, ng_ref = refs[:9]
    lrow = slice(l, l + 1)
    lane_fill = jnp.zeros((1, LANES - N_DIR * SSD_HEADS), F32)
    k = 9
    h0_ref = refs[k] if has_h0 else None
    k += int(has_h0) + n_alias
    y_ref = refs[k]
    hout_ref = refs[k + 1] if emit_state else None
    xpad, xc, acol, atr, cbs, bmt, dtsp, yacc, hst = refs[k + 1 + int(emit_state):]
    Q = CHUNK
    nc = L // Q
    halo = SUBLANES
    pad = (SSD_CONV - 1) // 2

    pitch = L + halo
    for q in range(n_seq + 1):
        xpad[q * pitch:q * pitch + halo, :] = jnp.zeros((halo, SSD_CONV_DIM), F32)
    for g in range(n_seq * nc):
        o = halo + (g // nc) * pitch + (g % nc) * Q
        xpad[o:o + Q, :] = xbc_ref[g * Q:(g + 1) * Q, :]
    win = Q + 2 * halo
    for c in range(n_seq * nc):
        for lb in range(0, SSD_CONV_DIM, CONV_LANES):
            ln = slice(lb, lb + CONV_LANES)
            o = (c // nc) * pitch + (c % nc) * Q
            xa = xpad[o:o + win, ln]
            acc = cb_ref[lrow, ln] + cw_ref[pad:pad + 1, ln] * xa[halo:halo + Q, :]
            for t in range(SSD_CONV):
                if t != pad:
                    rolled = pltpu.roll(xa, (pad - t) % win, 0)
                    acc = acc + cw_ref[t:t + 1, ln] * rolled[halo:halo + Q, :]
            xc[c * Q:(c + 1) * Q, ln] = _silu(acc)

    raw = dt_ref[...] + jnp.concatenate([dtb_ref[lrow, :], lane_fill], axis=1)
    dt = jnp.maximum(raw, 0.0) + jnp.log(1.0 + jnp.exp(-jnp.abs(raw)))
    dtsp[...] = dt
    a_neg = -jnp.exp(jnp.concatenate([alog_ref[lrow, :], lane_fill], axis=1))
    row = lax.broadcasted_iota(jnp.int32, (Q, Q), 0)
    col = lax.broadcasted_iota(jnp.int32, (Q, Q), 1)
    lower = row >= col
    upper = col >= row
    tri3 = jnp.concatenate([lower.astype(BF16)] * 3, axis=1)
    fwd_lane = lax.broadcasted_iota(jnp.int32, (Q, LANES), 1) < SSD_HEADS
    for c in range(n_seq * nc):
        dta = dtsp[c * Q:(c + 1) * Q, :] * a_neg
        hi = dta.astype(BF16)
        rem = dta - hi.astype(F32)
        mid = rem.astype(BF16)
        lo = (rem - mid.astype(F32)).astype(BF16)
        pre = jnp.dot(tri3, jnp.concatenate([hi, mid, lo], axis=0), preferred_element_type=F32)
        suf = pre[Q - 1:Q, :] - pre + dta
        a = jnp.where(fwd_lane, pre, suf)
        acol[c * Q:(c + 1) * Q, :] = a
        atr[c] = a.T[0:N_DIR * SSD_HEADS, :]
        for g in range(SSD_GROUPS):
            b0 = SSD_INNER + g * SSD_STATE
            c0 = SSD_INNER + (SSD_GROUPS + g) * SSD_STATE
            bm = xc[c * Q:(c + 1) * Q, b0:b0 + SSD_STATE]
            cbs[c, g] = lax.dot_general(xc[c * Q:(c + 1) * Q, c0:c0 + SSD_STATE].astype(BF16),
                                        bm.astype(BF16), _NT, preferred_element_type=F32)
            bmt[c, g] = bm.T.astype(BF16)

    sel_row = lax.broadcasted_iota(jnp.int32, (3 * LANES, SSD_INNER), 0) & (LANES - 1)
    sel_head = lax.broadcasted_iota(jnp.int32, (3 * LANES, SSD_INNER), 1) >> (SSD_HEAD_DIM.bit_length() - 1)
    sel = [(sel_row == sel_head + d * SSD_HEADS).astype(BF16) for d in range(N_DIR)]
    mask2 = [jnp.concatenate([m, m], axis=1) for m in (lower, upper)]

    for q in range(n_seq):
        for d in range(N_DIR):
            for p in range(N_PAIR):
                if has_h0:
                    hst[q, d, p] = h0_ref[q, d, p * PAIR:(p + 1) * PAIR, :].T
                else:
                    hst[q, d, p] = jnp.zeros((SSD_STATE, PAIR), F32)

    lo_lane = lax.broadcasted_iota(jnp.int32, (Q, PAIR), 1) < SSD_HEAD_DIM

    def chunk(c, d, q):
        r0 = pl.multiple_of(c * Q, Q)
        rows = pl.ds(r0, Q)
        a = acol[rows, :]
        a_t = atr[c]
        dtp = jnp.dot(_split3(dtsp[rows, :]), sel[d], preferred_element_type=F32)
        a_end = a[Q - 1:Q, :] if d == 0 else a[0:1, :]
        dec = jnp.exp(a_end)
        for g in range(SSD_GROUPS):
            c0 = SSD_INNER + (SSD_GROUPS + g) * SSD_STATE
            cmb = xc[rows, c0:c0 + SSD_STATE].astype(BF16)
            cb = cbs[c, g]
            cb2 = jnp.concatenate([cb, cb], axis=1)
            for pr in range(2):
                p = g * 2 + pr
                j0 = d * SSD_HEADS + 2 * p
                j1 = j0 + 1
                xs = xc[rows, p * PAIR:(p + 1) * PAIR]
                ab0 = jnp.broadcast_to(a[:, j0:j0 + 1], (Q, Q))
                ab1 = jnp.broadcast_to(a[:, j1:j1 + 1], (Q, Q))
                seg = (jnp.concatenate([ab0, ab1], axis=1)
                       - jnp.concatenate([a_t[j0:j0 + 1, :], a_t[j1:j1 + 1, :]], axis=1))
                m = (cb2 * jnp.exp(jnp.where(mask2[d], seg, -jnp.inf))).astype(BF16)
                xdt = xs * dtp[:, p * PAIR:(p + 1) * PAIR]
                rhs = jnp.concatenate([jnp.where(lo_lane, xdt, 0.0).astype(BF16),
                                       jnp.where(lo_lane, 0.0, xdt).astype(BF16)], axis=0)
                y_diag = jnp.dot(m, rhs, preferred_element_type=F32)
                hp = hst[q, d, p]
                y_off = jnp.dot(cmb, hp.astype(BF16), preferred_element_type=F32)
                a_pair = jnp.where(lo_lane, ab0, ab1)
                y = y_diag + jnp.exp(a_pair) * y_off
                a_end_pair = jnp.where(lo_lane[0:1, :], a_end[:, j0:j0 + 1], a_end[:, j1:j1 + 1])
                xw = (xdt * jnp.exp(a_end_pair - a_pair)).astype(BF16)
                s_new = jnp.dot(bmt[c, g], xw, preferred_element_type=F32)
                decp = jnp.where(lo_lane[0:1, :], dec[:, j0:j0 + 1], dec[:, j1:j1 + 1])
                hst[q, d, p] = decp * hp + s_new
                if d == 0:
                    y = y + dvec_ref[lrow, p * PAIR:(p + 1) * PAIR] * xs
                yacc[d, rows, p * PAIR:(p + 1) * PAIR] = y

    def step(t, carry):
        for q in range(n_seq):
            chunk(q * nc + t, 0, q)
            chunk(q * nc + nc - 1 - t, 1, q)
        return carry

    lax.fori_loop(0, nc, step, 0, unroll=min(nc, SSD_UNROLL))
    for c in range(n_seq * nc):
        rows = slice(c * Q, (c + 1) * Q)
        y = (yacc[0, rows, :] + yacc[1, rows, :]) * _silu(z_ref[rows, :])
        y_ref[rows, :] = _rms(y) * ng_ref[lrow, :]
    if emit_state:
        for q in range(n_seq):
            for d in range(N_DIR):
                for p in range(N_PAIR):
                    hout_ref[q, d, p * PAIR:(p + 1) * PAIR, :] = hst[q, d, p].T


def _ssd_mixer(z, xbc, dtr, prm, l, *, L, n_seq, nb, blk0, h0=None, y_buf=None, st_buf=None, after=None,
               emit_state):
    R = n_seq * L
    seq = lambda w: pl.BlockSpec((R, w), lambda b: (blk0 + b, 0))
    st = pl.BlockSpec((n_seq, None, N_DIR, SSD_HP, SSD_STATE), lambda b: (b, l, 0, 0, 0))
    args = [z, xbc, dtr, prm["conv_w"], prm["conv_b"], prm["dt_bias"], prm["a_log"], prm["d_vec"],
            prm["norm_g"]]
    in_specs = [seq(SSD_INNER), seq(SSD_CONV_DIM), seq(LANES),
                _layer_spec(l, (SSD_CONV, SSD_CONV_DIM)), _rows_spec(SSD_CONV_DIM),
                _rows_spec(N_DIR * SSD_HEADS), _rows_spec(N_DIR * SSD_HEADS),
                _rows_spec(SSD_INNER), _rows_spec(SSD_INNER)]
    if h0 is not None:
        args.append(h0)
        in_specs.append(st)
    out_shape = [jax.ShapeDtypeStruct((N_TOK, SSD_INNER), F32)]
    out_specs = [seq(SSD_INNER)]
    aliases = {}
    if y_buf is not None:
        aliases[len(args)] = 0
        args.append(y_buf)
        in_specs.append(_any_spec())
    if emit_state:
        out_shape.append(jax.ShapeDtypeStruct((BATCH, DEPTH, N_DIR, SSD_HP, SSD_STATE), F32))
        out_specs.append(st)
        if st_buf is not None:
            aliases[len(args)] = 1
            args.append(st_buf)
            in_specs.append(_any_spec())
    if after is not None:
        args.append(after)
        in_specs.append(_any_spec())
    res = pl.pallas_call(
        functools.partial(_ssd_kernel, l=l, L=L, n_seq=n_seq, has_h0=h0 is not None,
                          n_alias=len(aliases) + int(after is not None), emit_state=emit_state),
        grid=(nb,),
        in_specs=in_specs,
        out_specs=out_specs,
        out_shape=out_shape,
        input_output_aliases=aliases,
        scratch_shapes=[pltpu.VMEM((n_seq * (L + SUBLANES) + SUBLANES, SSD_CONV_DIM), F32),
                        pltpu.VMEM((R, SSD_CONV_DIM), F32),
                        pltpu.VMEM((R, LANES), F32),
                        pltpu.VMEM((R // CHUNK, N_DIR * SSD_HEADS, CHUNK), F32),
                        pltpu.VMEM((R // CHUNK, SSD_GROUPS, CHUNK, CHUNK), F32),
                        pltpu.VMEM((R // CHUNK, SSD_GROUPS, SSD_STATE, CHUNK), BF16),
                        pltpu.VMEM((R, LANES), F32),
                        pltpu.VMEM((N_DIR, R, SSD_INNER), F32),
                        pltpu.VMEM((n_seq, N_DIR, N_PAIR, PAIR, SSD_STATE), F32)],
        compiler_params=_params(),
        name=f"ssd_mixer_L{L}",
    )(*args)
    return res if emit_state else (res[0], None)


_LOG2_CH = S5_GROUP_CH.bit_length() - 1
_LOG2_ST = S5_STATE.bit_length() - 1


def _cmul(ar, ai, br, bi):
    return ar * br - ai * bi, ar * bi + ai * br


def _s5prep_kernel(lre_ref, lim_ref, ldt_ref, btr_ref, bti_ref, ctr_ref, cti_ref,
                   bblk_ref, cblk_ref, pw_ref):
    brow = lax.broadcasted_iota(jnp.int32, (S5_DIM, S5_LANES), 0) >> _LOG2_CH
    bcol = lax.broadcasted_iota(jnp.int32, (S5_DIM, S5_LANES), 1) >> _LOG2_ST
    bmask = brow == bcol
    p1 = lax.broadcasted_iota(jnp.int32, (S5_SEG, S5_LANES), 0) + 1
    for d in range(N_DIR):
        lre = lre_ref[d]
        lim = lim_ref[d]
        step = jnp.exp(ldt_ref[d])
        mag = jnp.exp(lre * step)
        lbr = mag * jnp.cos(lim * step)
        lbi = mag * jnp.sin(lim * step)
        den = lre * lre + lim * lim
        nr = lbr - 1.0
        cr = (nr * lre + lbi * lim) / den
        ci = (lbi * lre - nr * lim) / den
        br, bi = _cmul(cr, ci, btr_ref[...], bti_ref[...])
        br = jnp.where(bmask, br, 0.0).astype(BF16)
        bi = jnp.where(bmask, bi, 0.0).astype(BF16)
        for sl in range(S5_NSLAB):
            bblk_ref[d, sl, :, 0:S5_SLAB] = br[:, sl * S5_SLAB:(sl + 1) * S5_SLAB]
            bblk_ref[d, sl, :, S5_SLAB:2 * S5_SLAB] = bi[:, sl * S5_SLAB:(sl + 1) * S5_SLAB]
        rr = jnp.ones((S5_SEG, S5_LANES), F32)
        ri = jnp.zeros((S5_SEG, S5_LANES), F32)
        sr, si = lbr, lbi
        for k in range(S5_SEG.bit_length()):
            bit = ((p1 >> k) & 1) == 1
            tr, ti = _cmul(rr, ri, sr, si)
            rr = jnp.where(bit, tr, rr)
            ri = jnp.where(bit, ti, ri)
            sr, si = _cmul(sr, si, sr, si)
        pw_ref[d, 0, 0:S5_SEG, :] = rr
        pw_ref[d, 1, 0:S5_SEG, :] = ri
        n = S5_SEG
        while n < S5_PW_ROWS:
            tr, ti = _cmul(pw_ref[d, 0, 0:n, :], pw_ref[d, 1, 0:n, :],
                           pw_ref[d, 0, n - 1:n, :], pw_ref[d, 1, n - 1:n, :])
            pw_ref[d, 0, n:2 * n, :] = tr
            pw_ref[d, 1, n:2 * n, :] = ti
            n *= 2
    crow = lax.broadcasted_iota(jnp.int32, (S5_LANES, S5_DIM), 0) >> _LOG2_ST
    ccol = lax.broadcasted_iota(jnp.int32, (S5_LANES, S5_DIM), 1) >> _LOG2_CH
    cmask = crow == ccol
    cr = jnp.where(cmask, ctr_ref[...], 0.0).astype(BF16)
    ci = jnp.where(cmask, -cti_ref[...], 0.0).astype(BF16)
    for sl in range(S5_NSLAB):
        cblk_ref[sl, 0:S5_SLAB, :] = cr[sl * S5_SLAB:(sl + 1) * S5_SLAB, :]
        cblk_ref[sl, S5_SLAB:2 * S5_SLAB, :] = ci[sl * S5_SLAB:(sl + 1) * S5_SLAB, :]


def _s5_prep(lam_re, lam_im, log_dt, b_re, b_im, c_re, c_im):
    row = lambda t: t.reshape(DEPTH, N_DIR, 1, S5_LANES)
    ldt = jnp.repeat(log_dt, S5_STATE, axis=-1)
    bt = lambda t: jnp.tile(t.transpose(0, 3, 1, 2).reshape(DEPTH, S5_GROUP_CH, S5_LANES),
                            (1, S5_GROUPS, 1))
    ct = lambda t: jnp.tile(t.transpose(0, 1, 3, 2).reshape(DEPTH, S5_LANES, S5_GROUP_CH),
                            (1, 1, S5_GROUPS))
    vec = pl.BlockSpec((None, N_DIR, 1, S5_LANES), lambda l: (l, 0, 0, 0))
    bsp = pl.BlockSpec((None, S5_DIM, S5_LANES), lambda l: (l, 0, 0))
    csp = pl.BlockSpec((None, S5_LANES, S5_DIM), lambda l: (l, 0, 0))
    return pl.pallas_call(
        _s5prep_kernel,
        grid=(DEPTH,),
        in_specs=[vec, vec, vec, bsp, bsp, csp, csp],
        out_specs=[pl.BlockSpec((None, N_DIR, S5_NSLAB, S5_DIM, 2 * S5_SLAB), lambda l: (l, 0, 0, 0, 0)),
                   pl.BlockSpec((None, S5_NSLAB, 2 * S5_SLAB, S5_DIM), lambda l: (l, 0, 0, 0)),
                   pl.BlockSpec((None, N_DIR, 2, S5_PW_ROWS, S5_LANES), lambda l: (l, 0, 0, 0, 0))],
        out_shape=[jax.ShapeDtypeStruct((DEPTH, N_DIR, S5_NSLAB, S5_DIM, 2 * S5_SLAB), BF16),
                   jax.ShapeDtypeStruct((DEPTH, S5_NSLAB, 2 * S5_SLAB, S5_DIM), BF16),
                   jax.ShapeDtypeStruct((DEPTH, N_DIR, 2, S5_PW_ROWS, S5_LANES), F32)],
        compiler_params=_params(),
        name="s5_prep",
    )(row(lam_re), row(lam_im), row(ldt), bt(b_re), bt(b_im), ct(c_re), ct(c_im))


def _s5_moves(n_seq, n_seg, seq_len, col_major):
    steps = seq_len // n_seg
    if not col_major:
        return [(q * seq_len + s * steps, steps, q * n_seg + s, S5_SEG)
                for q in range(n_seq) for s in range(n_seg)]
    assert n_seq == 1 and n_seg == S5_SEG
    grid_rows = seq_len // GRID_W
    wseg = GRID_W // n_seg
    return [(r * GRID_W + s * wseg, wseg, r * S5_SEG + s, grid_rows * S5_SEG)
            for r in range(grid_rows) for s in range(n_seg)]


def _s5_kernel(*refs, l, n_seq, n_seg, seq_len, col_major, has_h0, n_alias, emit_state):
    u_ref, bblk_ref, cblk_ref, pw_ref, dvec_ref, gw_ref, gb_ref = refs[:7]
    k = 7
    h0r_ref, h0i_ref = (refs[k], refs[k + 1]) if has_h0 else (None, None)
    k += 2 * int(has_h0) + n_alias
    y_ref = refs[k]
    fr_ref, fi_ref = (refs[k + 1], refs[k + 2]) if emit_state else (None, None)
    up, buf_f, buf_b, yacc, yp, cin, fin = refs[k + 1 + 2 * int(emit_state):]
    assert n_seq * n_seg == S5_SEG and not (has_h0 and n_seg == 1) and not (emit_state and n_seg > 1)
    n_rows = n_seq * seq_len
    steps = seq_len // n_seg
    n_planes = S5_DIM // LANES
    W = S5_SLAB
    moves = _s5_moves(n_seq, n_seg, seq_len, col_major)

    for src, n, dst, stride in moves:
        for t in range(n_planes):
            up[t, pl.ds(dst, n, stride=stride), :] = u_ref[src:src + n, t * LANES:(t + 1) * LANES]

    def u_rows(rs):
        return jnp.concatenate([up[t, rs, :] for t in range(n_planes)], axis=1)

    zero = jnp.zeros((S5_SEG, W), F32)
    zrow = jnp.zeros((1, W), F32)
    for sl in range(S5_NSLAB):
        ln = slice(sl * W, (sl + 1) * W)
        for r0 in range(0, n_rows, ROW_SLAB):
            rs = slice(r0, r0 + ROW_SLAB)
            ub = u_rows(rs).astype(BF16)
            buf_f[rs, :] = jnp.dot(ub, bblk_ref[0, sl], preferred_element_type=F32)
            buf_b[rs, :] = jnp.dot(ub, bblk_ref[1, sl], preferred_element_type=F32)
        lam = [[jnp.broadcast_to(pw_ref[d, c, 0:1, ln], (S5_SEG, W)) for c in range(2)]
               for d in range(N_DIR)]

        def body(t, carry):
            fr, fi, br, bi = carry
            rf = pl.ds(pl.multiple_of(t * S5_SEG, S5_SEG), S5_SEG)
            rb = pl.ds(pl.multiple_of((steps - 1 - t) * S5_SEG, S5_SEG), S5_SEG)
            nfr = lam[0][0] * fr - lam[0][1] * fi + buf_f[rf, 0:W]
            nfi = lam[0][0] * fi + lam[0][1] * fr + buf_f[rf, W:2 * W]
            nbr = lam[1][0] * br - lam[1][1] * bi + buf_b[rb, 0:W]
            nbi = lam[1][0] * bi + lam[1][1] * br + buf_b[rb, W:2 * W]
            buf_f[rf, 0:W] = nfr
            buf_f[rf, W:2 * W] = nfi
            buf_b[rb, 0:W] = nbr
            buf_b[rb, W:2 * W] = nbi
            return nfr, nfi, nbr, nbi

        ends = lax.fori_loop(0, steps, body, (zero, zero, zero, zero), unroll=S5_UNROLL)

        if n_seg == 1:
            if emit_state:
                for d in range(N_DIR):
                    fin[d, 0, :, ln] = ends[2 * d]
                    fin[d, 1, :, ln] = ends[2 * d + 1]
        else:
            for d in range(N_DIR):
                er, ei = ends[2 * d], ends[2 * d + 1]
                lpr = pw_ref[d, 0, steps - 1:steps, ln]
                lpi = pw_ref[d, 1, steps - 1:steps, ln]
                pr = h0r_ref[d:d + 1, ln] if has_h0 else zrow
                pi = h0i_ref[d:d + 1, ln] if has_h0 else zrow
                order = range(n_seg) if d == 0 else range(n_seg - 1, -1, -1)
                for s in order:
                    cin[d, 0, s:s + 1, :] = pr
                    cin[d, 1, s:s + 1, :] = pi
                    tr, ti = _cmul(lpr, lpi, pr, pi)
                    pr = er[s:s + 1, :] + tr
                    pi = ei[s:s + 1, :] + ti
            cfr = cin[0, 0]
            cfi = cin[0, 1]
            cbr = cin[1, 0]
            cbi = cin[1, 1]

            def fix(p, carry):
                rows = pl.ds(pl.multiple_of(p * S5_SEG, S5_SEG), S5_SEG)
                q = steps - 1 - p
                tfr, tfi = _cmul(pw_ref[0, 0, pl.ds(p, 1), ln], pw_ref[0, 1, pl.ds(p, 1), ln], cfr, cfi)
                tbr, tbi = _cmul(pw_ref[1, 0, pl.ds(q, 1), ln], pw_ref[1, 1, pl.ds(q, 1), ln], cbr, cbi)
                buf_f[rows, 0:W] = buf_f[rows, 0:W] + buf_b[rows, 0:W] + tfr + tbr
                buf_f[rows, W:2 * W] = buf_f[rows, W:2 * W] + buf_b[rows, W:2 * W] + tfi + tbi
                return carry

            lax.fori_loop(0, steps, fix, 0, unroll=S5_UNROLL)

        for r0 in range(0, n_rows, ROW_SLAB):
            rs = slice(r0, r0 + ROW_SLAB)
            hs = buf_f[rs, :] if n_seg > 1 else buf_f[rs, :] + buf_b[rs, :]
            part = jnp.dot(hs.astype(BF16), cblk_ref[sl], preferred_element_type=F32)
            if sl == 0:
                yacc[rs, :] = part
            else:
                yacc[rs, :] = yacc[rs, :] + part

    gwb = gw_ref[...].astype(BF16)
    for r0 in range(0, n_rows, ROW_SLAB):
        rs = slice(r0, r0 + ROW_SLAB)
        y = _gelu_tanh(yacc[rs, :] + dvec_ref[l:l + 1, :] * u_rows(rs))
        gate = jnp.dot(y.astype(BF16), gwb, preferred_element_type=F32) + gb_ref[l:l + 1, :]
        y = y * jax.nn.sigmoid(gate)
        for t in range(n_planes):
            yp[t, rs, :] = y[:, t * LANES:(t + 1) * LANES]
    for src, n, dst, stride in moves:
        for t in range(n_planes):
            y_ref[src:src + n, t * LANES:(t + 1) * LANES] = yp[t, pl.ds(dst, n, stride=stride), :]
    if emit_state:
        for q in range(n_seq):
            for d in range(N_DIR):
                fr_ref[q, d:d + 1, :] = fin[d, 0, q:q + 1, :]
                fi_ref[q, d:d + 1, :] = fin[d, 1, q:q + 1, :]


def _s5_mixer(u, bblk, cblk, pw, prm, l, *, seq_len, n_seq, nb, blk0, col_major, h0=None, y_buf=None,
              st_buf=None, emit_state):
    n_rows = n_seq * seq_len
    n_seg = S5_SEG // n_seq
    seq = pl.BlockSpec((n_rows, S5_DIM), lambda b: (blk0 + b, 0))
    st = pl.BlockSpec((n_seq, None, N_DIR, S5_LANES), lambda b: (b, l, 0, 0))
    once = dict(pipeline_mode=pl.Buffered(1))
    args = [u, bblk, cblk, pw, prm["d_vec"], prm["glu_w"], prm["glu_b"]]
    in_specs = [seq, _layer_spec(l, (N_DIR, S5_NSLAB, S5_DIM, 2 * S5_SLAB), **once),
                _layer_spec(l, (S5_NSLAB, 2 * S5_SLAB, S5_DIM), **once),
                _layer_spec(l, (N_DIR, 2, S5_PW_ROWS if n_seg > 1 else S5_SEG, S5_LANES), **once),
                _rows_spec(S5_DIM), _layer_spec(l, (S5_DIM, S5_DIM)), _rows_spec(S5_DIM)]
    if h0 is not None:
        h0_spec = pl.BlockSpec((None, None, N_DIR, S5_LANES), lambda b: (b, l, 0, 0))
        args += list(h0)
        in_specs += [h0_spec, h0_spec]
    out_shape = [jax.ShapeDtypeStruct((N_TOK, S5_DIM), F32)]
    out_specs = [seq]
    aliases = {}
    if y_buf is not None:
        aliases[len(args)] = 0
        args.append(y_buf)
        in_specs.append(_any_spec())
    if emit_state:
        out_shape += [jax.ShapeDtypeStruct((BATCH, DEPTH, N_DIR, S5_LANES), F32)] * 2
        out_specs += [st, st]
        if st_buf is not None:
            for j, buf in enumerate(st_buf):
                aliases[len(args)] = 1 + j
                args.append(buf)
                in_specs.append(_any_spec())
    res = pl.pallas_call(
        functools.partial(_s5_kernel, l=l, n_seq=n_seq, n_seg=n_seg, seq_len=seq_len, col_major=col_major,
                          has_h0=h0 is not None, n_alias=len(aliases), emit_state=emit_state),
        grid=(nb,),
        in_specs=in_specs,
        out_specs=out_specs,
        out_shape=out_shape,
        input_output_aliases=aliases,
        scratch_shapes=[pltpu.VMEM((S5_DIM // LANES, n_rows, LANES), F32),
                        pltpu.VMEM((n_rows, 2 * S5_SLAB), F32),
                        pltpu.VMEM((n_rows, 2 * S5_SLAB), F32),
                        pltpu.VMEM((n_rows, S5_DIM), F32),
                        pltpu.VMEM((S5_DIM // LANES, n_rows, LANES), F32),
                        pltpu.VMEM((N_DIR, 2, S5_SEG, S5_SLAB), F32),
                        pltpu.VMEM((N_DIR, 2, S5_SEG, S5_LANES), F32)],
        compiler_params=_params(),
        name=f"s5_mixer_L{seq_len}",
    )(*args)
    return (res[0], res[1], res[2]) if emit_state else (res[0], None, None)


def _outffn_kernel(xa_ref, xb_ref, ys_ref, yg_ref, y5_ref, g1_ref, sh_ref, sc_ref, g2_ref, ng_ref,
                   fg_ref, wo_hbm, w1_hbm, w2_hbm, *rest, l, final):
    n_out = 2 if final else 1
    o_refs = rest[:n_out]
    wob, w1b, w2b, st1, st2, sem = rest[n_out:]
    t = pl.program_id(0)
    n_ctx = N_CTX // FFN_TILE
    kf = D_FF // FFN_PREP
    half = D_MODEL // 2
    assert half == kf, "w_out halves are staged in the w2 chunk slots"

    def w1_copy(j):
        return pltpu.make_async_copy(w1_hbm.at[l, :, pl.ds(j * kf, kf)], st1.at[j % 2], sem.at[0, j % 2])

    def w2_copy(j):
        return pltpu.make_async_copy(w2_hbm.at[l, pl.ds(j * kf, kf), :], st2.at[j % 2], sem.at[1, j % 2])

    def wo_copy(s):
        return pltpu.make_async_copy(wo_hbm.at[l, pl.ds(s * half, half), :], st2.at[s], sem.at[1, s])

    def tile(first):
        if first:
            for s in range(2):
                wo_copy(s).start()
                w1_copy(s).start()
            for s in range(2):
                wo_copy(s).wait()
                wob[s * half:(s + 1) * half, :] = st2[s].astype(BF16)
            for s in range(2):
                w2_copy(s).start()
        r = _mod_row(t, FFN_TILE)
        row = lambda ref: ref[pl.ds(r, 1), :]
        x = jnp.where(t < n_ctx, xa_ref[...], xb_ref[...])
        mixed = jnp.concatenate([ys_ref[...].astype(BF16), yg_ref[...].astype(BF16),
                                 y5_ref[...].astype(BF16)], axis=1)
        x1 = x + row(g1_ref) * jnp.dot(mixed, wob[...], preferred_element_type=F32)
        h = ((_rms(x1) * ng_ref[l:l + 1, :]) * (1.0 + row(sc_ref)) + row(sh_ref)).astype(BF16)
        acc = jnp.zeros((FFN_TILE, D_MODEL), F32)
        for j in range(FFN_PREP):
            if first:
                w1_copy(j).wait()
                w1b[j] = st1[j % 2].astype(BF16)
                w2_copy(j).wait()
                w2b[j] = st2[j % 2].astype(BF16)
                if j + 2 < FFN_PREP:
                    w1_copy(j + 2).start()
                    w2_copy(j + 2).start()
            a = jnp.maximum(jnp.dot(h, w1b[j], preferred_element_type=F32), 0.0)
            acc = acc + jnp.dot((a * a).astype(BF16), w2b[j], preferred_element_type=F32)
        x2 = x1 + row(g2_ref) * acc
        if not final:
            o_refs[0][...] = x2
        else:
            y = _rms(x2) * fg_ref[...]

            @pl.when(t < n_ctx)
            def _():
                o_refs[0][...] = y

            @pl.when(t >= n_ctx)
            def _():
                o_refs[1][...] = y

    @pl.when(t == 0)
    def _():
        tile(True)

    @pl.when(t > 0)
    def _():
        tile(False)


def _out_ffn(xa, xb, b_is_stream, y_ssd, y_sgu, y_s5, l, mods, norm_g, final_g, w_out, w1, w2, *, final):
    tok = lambda w: pl.BlockSpec((FFN_TILE, w), lambda i: (i, 0))
    sa, sb = _split_tok_specs(FFN_TILE, D_MODEL, b_is_stream)
    if final:
        out_specs = list(_split_tok_specs(FFN_TILE, D_MODEL, False))
        out_shape = [jax.ShapeDtypeStruct((N_CTX, D_MODEL), F32), jax.ShapeDtypeStruct((N_LAT, D_MODEL), F32)]
    else:
        out_specs = [tok(D_MODEL)]
        out_shape = [jax.ShapeDtypeStruct((N_TOK, D_MODEL), F32)]
    kf = D_FF // FFN_PREP
    return pl.pallas_call(
        functools.partial(_outffn_kernel, l=l, final=final),
        grid=(N_TOK // FFN_TILE,),
        in_specs=[sa, sb, tok(SSD_INNER), tok(SGU_DIM), tok(S5_DIM),
                  _mod_spec(l, 2), _mod_spec(l, 3), _mod_spec(l, 4), _mod_spec(l, 5),
                  _rows_spec(D_MODEL), pl.BlockSpec((1, D_MODEL), lambda i: (0, 0)),
                  _any_spec(), _any_spec(), _any_spec()],
        out_specs=out_specs,
        out_shape=out_shape,
        scratch_shapes=[pltpu.VMEM((D_MODEL, D_MODEL), BF16),
                        pltpu.VMEM((FFN_PREP, D_MODEL, kf), BF16),
                        pltpu.VMEM((FFN_PREP, kf, D_MODEL), BF16),
                        pltpu.VMEM((2, D_MODEL, kf), F32),
                        pltpu.VMEM((2, kf, D_MODEL), F32),
                        pltpu.SemaphoreType.DMA((2, 2))],
        compiler_params=_params(),
        name="out_ffn_final" if final else "out_ffn",
    )(xa, xb, y_ssd, y_sgu, y_s5, mods, mods, mods, mods, norm_g, final_g, w_out, w1, w2)


def kernel(x_prompt, x_sample, state_ssd, state_s5_re, state_s5_im, c, c_ctx, ada_w, ada_b, norm1_g,
           norm2_g, w_in, ssd_conv_w, ssd_conv_b, ssd_dt_bias, ssd_a_log, ssd_d, ssd_norm_g,
           sgu_norm_g, sgu_w, sgu_b, s5_lambda_re, s5_lambda_im, s5_log_dt, s5_b_re, s5_b_im,
           s5_c_re, s5_c_im, s5_d, s5_glu_w, s5_glu_b, w_out, ffn_w1, ffn_w2, final_norm_g):
    cvec = jnp.concatenate([c_ctx[None, :], c,
                            jnp.zeros((MOD_ROWS - 1 - DEC_BATCH, D_MODEL), F32)], axis=0)
    mods = _adaln_mods(cvec, ada_w, ada_b)
    bblk, cblk, pw = _s5_prep(s5_lambda_re, s5_lambda_im, s5_log_dt, s5_b_re, s5_b_im,
                              s5_c_re, s5_c_im)

    w_in_t = w_in.transpose(0, 2, 1)
    ssd_prm = dict(conv_w=ssd_conv_w, conv_b=ssd_conv_b,
                   dt_bias=ssd_dt_bias.reshape(DEPTH, N_DIR * SSD_HEADS),
                   a_log=ssd_a_log.reshape(DEPTH, N_DIR * SSD_HEADS),
                   d_vec=jnp.repeat(ssd_d, SSD_HEAD_DIM, axis=-1), norm_g=ssd_norm_g)
    s5_prm = dict(d_vec=s5_d, glu_w=s5_glu_w, glu_b=s5_glu_b)
    sgu_w_pair = sgu_w.reshape(DEPTH, SGU_HEADS // 2, 2, CHUNK, CHUNK).transpose(0, 1, 3, 2, 4)
    sgu_w_pair = sgu_w_pair.reshape(DEPTH, SGU_HEADS // 2, CHUNK, 2 * CHUNK).astype(BF16)
    sgu_b_full = jnp.repeat(sgu_b.transpose(0, 2, 1), SGU_DIM // SGU_HEADS, axis=2)
    norm1 = norm1_g
    norm2 = norm2_g
    sgu_g = sgu_norm_g
    final_g = final_norm_g.reshape(1, D_MODEL)

    lat_ssd = state_ssd.reshape(DEC_BATCH, DEPTH, N_DIR, SSD_HP, SSD_STATE)
    lat_s5 = (state_s5_re.reshape(DEC_BATCH, DEPTH, N_DIR, S5_LANES),
              state_s5_im.reshape(DEC_BATCH, DEPTH, N_DIR, S5_LANES))
    lat_blk = N_CTX // DEC_SEQ

    xa = x_prompt.reshape(N_CTX, D_MODEL)
    xb = x_sample.reshape(N_LAT, D_MODEL)
    b_is_stream = False
    st_ssd = None
    st_s5 = None
    for l in range(DEPTH):
        z, xbc, y_sgu, s5u, dtr = _in_proj(xa, xb, b_is_stream, l, mods, norm1, w_in_t, sgu_g,
                                           sgu_w_pair, sgu_b_full)
        y_s5, fr, fi = _s5_mixer(s5u, bblk, cblk, pw, s5_prm, l, seq_len=SEQ, n_seq=S5_SEG,
                                 nb=BATCH // S5_SEG, blk0=0, col_major=False, st_buf=st_s5,
                                 emit_state=True)
        st_s5 = (fr, fi)
        y_s5, _, _ = _s5_mixer(s5u, bblk, cblk, pw, s5_prm, l, seq_len=DEC_SEQ, n_seq=1, nb=DEC_BATCH,
                               blk0=lat_blk, col_major=True, h0=lat_s5, y_buf=y_s5, emit_state=False)
        y_ssd, st_ssd = _ssd_mixer(z, xbc, dtr, ssd_prm, l, L=SEQ, n_seq=SSD_CTX_SEQS,
                                   nb=BATCH // SSD_CTX_SEQS, blk0=0, st_buf=st_ssd, after=y_s5,
                                   emit_state=True)
        y_ssd, _ = _ssd_mixer(z, xbc, dtr, ssd_prm, l, L=DEC_SEQ, n_seq=1, nb=DEC_BATCH, blk0=lat_blk,
                              h0=lat_ssd, y_buf=y_ssd, emit_state=False)
        final = l == DEPTH - 1
        out = _out_ffn(xa, xb, b_is_stream, y_ssd, y_sgu, y_s5, l, mods, norm2, final_g, w_out, ffn_w1,
                       ffn_w2, final=final)
        if not final:
            xa = xb = out[0]
            b_is_stream = True

    y_prompt = out[0].reshape(BATCH, SEQ, D_MODEL)
    y_sample = out[1].reshape(DEC_BATCH, DEC_SEQ, D_MODEL)
    new_state_ssd = st_ssd.reshape(BATCH, DEPTH, N_DIR, SSD_HEADS, SSD_HEAD_DIM, SSD_STATE)
    s5_shape = (BATCH, DEPTH, N_DIR, S5_GROUPS, S5_STATE)
    return (y_prompt, y_sample, new_state_ssd, st_s5[0].reshape(s5_shape), st_s5[1].reshape(s5_shape))
```

```python
import functools
import math

import jax
import jax.numpy as jnp
from jax import lax
from jax.experimental import pallas as pl
from jax.experimental.pallas import tpu as pltpu

F32 = jnp.float32
BF16 = jnp.bfloat16

D_MODEL = 1024
BATCH = 16
SEQ = 256
DEPTH = 2
DEC_BATCH = 2
DEC_SEQ = 1024
GRID_W = 64
CHUNK = 128
N_DIR = 2
EPS = 1e-6
SSD_INNER = 512
SSD_HEAD_DIM = 64
SSD_HEADS = 8
SSD_GROUPS = 2
SSD_STATE = 128
SSD_CONV = 5
SSD_CONV_DIM = SSD_INNER + 2 * SSD_GROUPS * SSD_STATE
SGU_DIM = 256
SGU_HEADS = 4
S5_DIM = 256
S5_GROUP_CH = 16
S5_GROUPS = 16
S5_STATE = 64
D_FF = 4 * D_MODEL
OFF_XBC = SSD_INNER
OFF_DT = OFF_XBC + SSD_CONV_DIM
OFF_SGU = OFF_DT + N_DIR * SSD_HEADS
OFF_S5 = OFF_SGU + 2 * SGU_DIM
IN_DIM = OFF_S5 + S5_DIM

N_CTX = BATCH * SEQ
N_LAT = DEC_BATCH * DEC_SEQ
N_TOK = N_CTX + N_LAT
LANES = 128
SUBLANES = 8
MOD_ROWS = SUBLANES
SSD_HP = SSD_HEADS * SSD_HEAD_DIM
S5_LANES = S5_GROUPS * S5_STATE
S5_SEG = SUBLANES
S5_PW_ROWS = DEC_SEQ // S5_SEG
S5_SLAB = 256
S5_NSLAB = S5_LANES // S5_SLAB
S5_UNROLL = 16
PAIR = 2 * SSD_HEAD_DIM
N_PAIR = SSD_HEADS // 2
TOK_TILE = 512
N_CTX_TILES = N_CTX // TOK_TILE
FFN_TILE = 512
FFN_PREP = 8
ROW_SLAB = 256
CONV_LANES = 256
SSD_CTX_SEQS = 2
SSD_UNROLL = 8
VMEM_LIMIT = 56 * 1024 * 1024

_NT = (((1,), (1,)), ((), ()))


def _params(n_axes=1):
    return pltpu.CompilerParams(dimension_semantics=("arbitrary",) * n_axes,
                                vmem_limit_bytes=VMEM_LIMIT)


def _layer_spec(l, shape, **kw):
    return pl.BlockSpec((None,) + tuple(shape), lambda *_: (l,) + (0,) * len(shape), **kw)


def _mod_spec(l, k):
    return pl.BlockSpec((None, MOD_ROWS, D_MODEL), lambda *_: (l, 0, k))


def _rows_spec(n):
    return pl.BlockSpec((DEPTH, n), lambda *_: (0, 0))


def _any_spec():
    return pl.BlockSpec(memory_space=pl.ANY)


def _mod_row(i, tm):
    n_ctx = N_CTX // tm
    return jnp.where(i < n_ctx, 0, 1 + (i - n_ctx) // (DEC_SEQ // tm))


def _split_tok_specs(tm, width, b_has_ctx_rows, tile_of=lambda i: i):
    n_ctx = N_CTX // tm
    b_off = n_ctx if b_has_ctx_rows else 0
    a = pl.BlockSpec((tm, width), lambda i: (jnp.minimum(tile_of(i), n_ctx - 1), 0))
    b = pl.BlockSpec((tm, width), lambda i: (jnp.maximum(tile_of(i), n_ctx) - n_ctx + b_off, 0))
    return a, b


def _silu(x):
    return x * jax.nn.sigmoid(x)


def _gelu_tanh(x):
    c = math.sqrt(2.0 / math.pi)
    return 0.5 * x * (1.0 + jnp.tanh(c * (x + 0.044715 * (x * x * x))))


def _rms(x):
    return x * lax.rsqrt(jnp.mean(x * x, axis=-1, keepdims=True) + EPS)


def _bdot(a, b):
    return jnp.dot(a.astype(BF16), b.astype(BF16), preferred_element_type=F32)


def _split3(x):
    hi = x.astype(BF16)
    r = x - hi.astype(F32)
    mid = r.astype(BF16)
    lo = (r - mid.astype(F32)).astype(BF16)
    return jnp.concatenate([hi, mid, lo], axis=1)


def _mod_kernel(c_ref, w_ref, b_ref, o_ref):
    bias = b_ref[pl.ds(pl.program_id(0), 1), :]
    o_ref[...] = _bdot(_silu(c_ref[...]), w_ref[...]) + bias


def _adaln_mods(cvec, ada_w, ada_b):
    n_blk = 6
    return pl.pallas_call(
        _mod_kernel,
        grid=(DEPTH, n_blk),
        in_specs=[pl.BlockSpec((MOD_ROWS, D_MODEL), lambda l, j: (0, 0)),
                  pl.BlockSpec((None, D_MODEL, D_MODEL), lambda l, j: (l, 0, j)),
                  pl.BlockSpec((DEPTH, D_MODEL), lambda l, j: (0, j))],
        out_specs=pl.BlockSpec((None, MOD_ROWS, D_MODEL), lambda l, j: (l, 0, j)),
        out_shape=jax.ShapeDtypeStruct((DEPTH, MOD_ROWS, 6 * D_MODEL), F32),
        compiler_params=_params(2),
        name="adaln_mod",
    )(cvec, ada_w, ada_b)


_C_Z = 0
_C_XBC = _C_Z + SSD_INNER
_C_SGU = _C_XBC + SSD_CONV_DIM
_C_S5 = _C_SGU + 2 * SGU_DIM
_C_DT = _C_S5 + S5_DIM
_C_END = _C_DT + LANES


def _inproj_kernel(xa_ref, xb_ref, sh_ref, sc_ref, g_ref, win_ref, sg_ref, sw_ref, sb_ref,
                   z_ref, xbc_ref, ysgu_ref, s5_ref, dt_ref, w_ref, *, l):
    i = pl.program_id(0)

    @pl.when(i == 0)
    def _():
        moves = ((0, OFF_DT, _C_Z), (OFF_SGU, IN_DIM, _C_SGU), (OFF_DT, OFF_SGU, _C_DT))
        for src0, src1, dst0 in moves:
            for r0 in range(src0, src1, ROW_SLAB):
                n = min(ROW_SLAB, src1 - r0)
                w_ref[dst0 + r0 - src0:dst0 + r0 - src0 + n, :] = win_ref[r0:r0 + n, :].astype(BF16)
        n_pad = LANES - N_DIR * SSD_HEADS
        w_ref[_C_END - n_pad:_C_END, :] = jnp.zeros((n_pad, D_MODEL), BF16)

    r = _mod_row(i, TOK_TILE)
    x = jnp.where(i < N_CTX_TILES, xa_ref[...], xb_ref[...])
    shift = sh_ref[pl.ds(r, 1), :]
    scale = sc_ref[pl.ds(r, 1), :]
    h = (_rms(x) * g_ref[l:l + 1, :]) * (1.0 + scale) + shift
    hb = h.astype(BF16)
    proj = lambda c0, c1: lax.dot_general(hb, w_ref[c0:c1, :], _NT, preferred_element_type=F32)
    uv = _gelu_tanh(proj(_C_SGU, _C_S5))
    z_ref[...] = proj(_C_Z, _C_XBC)
    xbc_ref[...] = proj(_C_XBC, _C_SGU)
    s5_ref[...] = proj(_C_S5, _C_DT)
    dt_ref[...] = proj(_C_DT, _C_END)

    u = uv[:, :SGU_DIM]
    v = _rms(uv[:, SGU_DIM:]) * sg_ref[l:l + 1, :]
    lo_lane = lax.broadcasted_iota(jnp.int32, (CHUNK, LANES), 1) < (LANES // 2)
    for c in range(TOK_TILE // CHUNK):
        rows = slice(c * CHUNK, (c + 1) * CHUNK)
        mix = []
        for pr in range(SGU_HEADS // 2):
            vp = v[rows, pr * LANES:(pr + 1) * LANES]
            rhs = jnp.concatenate([jnp.where(lo_lane, vp, 0.0).astype(BF16),
                                   jnp.where(lo_lane, 0.0, vp).astype(BF16)], axis=0)
            mix.append(jnp.dot(sw_ref[pr], rhs, preferred_element_type=F32))
        ysgu_ref[rows, :] = u[rows, :] * (jnp.concatenate(mix, axis=1) + sb_ref[...])


def _in_proj(xa, xb, b_is_stream, l, mods, norm_g, w_in, sgu_g, sgu_w_pair, sgu_b_full):
    tok = lambda w: pl.BlockSpec((TOK_TILE, w), lambda i: (i, 0))
    widths = (SSD_INNER, SSD_CONV_DIM, SGU_DIM, S5_DIM, LANES)
    sa, sb = _split_tok_specs(TOK_TILE, D_MODEL, b_is_stream)
    return pl.pallas_call(
        functools.partial(_inproj_kernel, l=l),
        grid=(N_TOK // TOK_TILE,),
        in_specs=[sa, sb, _mod_spec(l, 0), _mod_spec(l, 1), _rows_spec(D_MODEL),
                  _layer_spec(l, (IN_DIM, D_MODEL), pipeline_mode=pl.Buffered(1)),
                  _rows_spec(SGU_DIM), _layer_spec(l, (SGU_HEADS // 2, CHUNK, 2 * CHUNK)),
                  _layer_spec(l, (CHUNK, SGU_DIM))],
        out_specs=[tok(w) for w in widths],
        out_shape=[jax.ShapeDtypeStruct((N_TOK, w), F32) for w in widths],
        scratch_shapes=[pltpu.VMEM((_C_END, D_MODEL), BF16)],
        compiler_params=_params(),
        name="in_proj",
    )(xa, xb, mods, mods, norm_g, w_in, sgu_g, sgu_w_pair, sgu_b_full)


def _ssd_kernel(*refs, l, L, n_seq, has_h0, n_alias, emit_state):
    z_ref, xbc_ref, dt_ref, cw_ref, cb_ref, dtb_ref, alog_ref, dvec_ref, ng_ref = refs[:9]
    lrow = slice(l, l + 1)
    lane_fill = jnp.zeros((1, LANES - N_DIR * SSD_HEADS), F32)
    k = 9
    h0_ref = refs[k] if has_h0 else None
    k += int(has_h0) + n_alias
    y_ref = refs[k]
    hout_ref = refs[k + 1] if emit_state else None
    xpad, xc, acol, atr, cbs, bmt, dtsp, yacc, hst = refs[k + 1 + int(emit_state):]
    Q = CHUNK
    nc = L // Q
    halo = SUBLANES
    pad = (SSD_CONV - 1) // 2

    pitch = L + halo
    for q in range(n_seq + 1):
        xpad[q * pitch:q * pitch + halo, :] = jnp.zeros((halo, SSD_CONV_DIM), F32)
    for g in range(n_seq * nc):
        o = halo + (g // nc) * pitch + (g % nc) * Q
        xpad[o:o + Q, :] = xbc_ref[g * Q:(g + 1) * Q, :]
    win = Q + 2 * halo
    for c in range(n_seq * nc):
        for lb in range(0, SSD_CONV_DIM, CONV_LANES):
            ln = slice(lb, lb + CONV_LANES)
            o = (c // nc) * pitch + (c % nc) * Q
            xa = xpad[o:o + win, ln]
            acc = cb_ref[lrow, ln] + cw_ref[pad:pad + 1, ln] * xa[halo:halo + Q, :]
            for t in range(SSD_CONV):
                if t != pad:
                    rolled = pltpu.roll(xa, (pad - t) % win, 0)
                    acc = acc + cw_ref[t:t + 1, ln] * rolled[halo:halo + Q, :]
            xc[c * Q:(c + 1) * Q, ln] = _silu(acc)

    raw = dt_ref[...] + jnp.concatenate([dtb_ref[lrow, :], lane_fill], axis=1)
    dt = jnp.maximum(raw, 0.0) + jnp.log(1.0 + jnp.exp(-jnp.abs(raw)))
    dtsp[...] = dt
    a_neg = -jnp.exp(jnp.concatenate([alog_ref[lrow, :], lane_fill], axis=1))
    row = lax.broadcasted_iota(jnp.int32, (Q, Q), 0)
    col = lax.broadcasted_iota(jnp.int32, (Q, Q), 1)
    lower = row >= col
    upper = col >= row
    tri3 = jnp.concatenate([lower.astype(BF16)] * 3, axis=1)
    fwd_lane = lax.broadcasted_iota(jnp.int32, (Q, LANES), 1) < SSD_HEADS
    for c in range(n_seq * nc):
        dta = dtsp[c * Q:(c + 1) * Q, :] * a_neg
        hi = dta.astype(BF16)
        rem = dta - hi.astype(F32)
        mid = rem.astype(BF16)
        lo = (rem - mid.astype(F32)).astype(BF16)
        pre = jnp.dot(tri3, jnp.concatenate([hi, mid, lo], axis=0), preferred_element_type=F32)
        suf = pre[Q - 1:Q, :] - pre + dta
        a = jnp.where(fwd_lane, pre, suf)
        acol[c * Q:(c + 1) * Q, :] = a
        atr[c] = a.T[0:N_DIR * SSD_HEADS, :]
        for g in range(SSD_GROUPS):
            b0 = SSD_INNER + g * SSD_STATE
            c0 = SSD_INNER + (SSD_GROUPS + g) * SSD_STATE
            bm = xc[c * Q:(c + 1) * Q, b0:b0 + SSD_STATE]
            cbs[c, g] = lax.dot_general(xc[c * Q:(c + 1) * Q, c0:c0 + SSD_STATE].astype(BF16),
                                        bm.astype(BF16), _NT, preferred_element_type=F32)
            bmt[c, g] = bm.T.astype(BF16)

    sel_row = lax.broadcasted_iota(jnp.int32, (3 * LANES, SSD_INNER), 0) & (LANES - 1)
    sel_head = lax.broadcasted_iota(jnp.int32, (3 * LANES, SSD_INNER), 1) >> (SSD_HEAD_DIM.bit_length() - 1)
    sel = [(sel_row == sel_head + d * SSD_HEADS).astype(BF16) for d in range(N_DIR)]
    mask2 = [jnp.concatenate([m, m], axis=1) for m in (lower, upper)]

    for q in range(n_seq):
        for d in range(N_DIR):
            for p in range(N_PAIR):
                if has_h0:
                    hst[q, d, p] = h0_ref[q, d, p * PAIR:(p + 1) * PAIR, :].T
                else:
                    hst[q, d, p] = jnp.zeros((SSD_STATE, PAIR), F32)

    lo_lane = lax.broadcasted_iota(jnp.int32, (Q, PAIR), 1) < SSD_HEAD_DIM

    def chunk(c, d, q):
        r0 = pl.multiple_of(c * Q, Q)
        rows = pl.ds(r0, Q)
        a = acol[rows, :]
        a_t = atr[c]
        dtp = jnp.dot(_split3(dtsp[rows, :]), sel[d], preferred_element_type=F32)
        a_end = a[Q - 1:Q, :] if d == 0 else a[0:1, :]
        dec = jnp.exp(a_end)
        for g in range(SSD_GROUPS):
            c0 = SSD_INNER + (SSD_GROUPS + g) * SSD_STATE
            cmb = xc[rows, c0:c0 + SSD_STATE].astype(BF16)
            cb = cbs[c, g]
            cb2 = jnp.concatenate([cb, cb], axis=1)
            for pr in range(2):
                p = g * 2 + pr
                j0 = d * SSD_HEADS + 2 * p
                j1 = j0 + 1
                xs = xc[rows, p * PAIR:(p + 1) * PAIR]
                ab0 = jnp.broadcast_to(a[:, j0:j0 + 1], (Q, Q))
                ab1 = jnp.broadcast_to(a[:, j1:j1 + 1], (Q, Q))
                seg = (jnp.concatenate([ab0, ab1], axis=1)
                       - jnp.concatenate([a_t[j0:j0 + 1, :], a_t[j1:j1 + 1, :]], axis=1))
                m = (cb2 * jnp.exp(jnp.where(mask2[d], seg, -jnp.inf))).astype(BF16)
                xdt = xs * dtp[:, p * PAIR:(p + 1) * PAIR]
                rhs = jnp.concatenate([jnp.where(lo_lane, xdt, 0.0).astype(BF16),
                                       jnp.where(lo_lane, 0.0, xdt).astype(BF16)], axis=0)
                y_diag = jnp.dot(m, rhs, preferred_element_type=F32)
                hp = hst[q, d, p]
                y_off = jnp.dot(cmb, hp.astype(BF16), preferred_element_type=F32)
                a_pair = jnp.where(lo_lane, ab0, ab1)
                y = y_diag + jnp.exp(a_pair) * y_off
                a_end_pair = jnp.where(lo_lane[0:1, :], a_end[:, j0:j0 + 1], a_end[:, j1:j1 + 1])
                xw = (xdt * jnp.exp(a_end_pair - a_pair)).astype(BF16)
                s_new = jnp.dot(bmt[c, g], xw, preferred_element_type=F32)
                decp = jnp.where(lo_lane[0:1, :], dec[:, j0:j0 + 1], dec[:, j1:j1 + 1])
                hst[q, d, p] = decp * hp + s_new
                if d == 0:
                    y = y + dvec_ref[lrow, p * PAIR:(p + 1) * PAIR] * xs
                yacc[d, rows, p * PAIR:(p + 1) * PAIR] = y

    def step(t, carry):
        for q in range(n_seq):
            chunk(q * nc + t, 0, q)
            chunk(q * nc + nc - 1 - t, 1, q)
        return carry

    lax.fori_loop(0, nc, step, 0, unroll=min(nc, SSD_UNROLL))
    for c in range(n_seq * nc):
        rows = slice(c * Q, (c + 1) * Q)
        y = (yacc[0, rows, :] + yacc[1, rows, :]) * _silu(z_ref[rows, :])
        y_ref[rows, :] = _rms(y) * ng_ref[lrow, :]
    if emit_state:
        for q in range(n_seq):
            for d in range(N_DIR):
                for p in range(N_PAIR):
                    hout_ref[q, d, p * PAIR:(p + 1) * PAIR, :] = hst[q, d, p].T


def _ssd_mixer(z, xbc, dtr, prm, l, *, L, n_seq, nb, blk0, h0=None, y_buf=None, st_buf=None, after=None,
               emit_state):
    R = n_seq * L
    seq = lambda w: pl.BlockSpec((R, w), lambda b: (blk0 + b, 0))
    st = pl.BlockSpec((n_seq, None, N_DIR, SSD_HP, SSD_STATE), lambda b: (b, l, 0, 0, 0))
    args = [z, xbc, dtr, prm["conv_w"], prm["conv_b"], prm["dt_bias"], prm["a_log"], prm["d_vec"],
            prm["norm_g"]]
    in_specs = [seq(SSD_INNER), seq(SSD_CONV_DIM), seq(LANES),
                _layer_spec(l, (SSD_CONV, SSD_CONV_DIM)), _rows_spec(SSD_CONV_DIM),
                _rows_spec(N_DIR * SSD_HEADS), _rows_spec(N_DIR * SSD_HEADS),
                _rows_spec(SSD_INNER), _rows_spec(SSD_INNER)]
    if h0 is not None:
        args.append(h0)
        in_specs.append(st)
    out_shape = [jax.ShapeDtypeStruct((N_TOK, SSD_INNER), F32)]
    out_specs = [seq(SSD_INNER)]
    aliases = {}
    if y_buf is not None:
        aliases[len(args)] = 0
        args.append(y_buf)
        in_specs.append(_any_spec())
    if emit_state:
        out_shape.append(jax.ShapeDtypeStruct((BATCH, DEPTH, N_DIR, SSD_HP, SSD_STATE), F32))
        out_specs.append(st)
        if st_buf is not None:
            aliases[len(args)] = 1
            args.append(st_buf)
            in_specs.append(_any_spec())
    if after is not None:
        args.append(after)
        in_specs.append(_any_spec())
    res = pl.pallas_call(
        functools.partial(_ssd_kernel, l=l, L=L, n_seq=n_seq, has_h0=h0 is not None,
                          n_alias=len(aliases) + int(after is not None), emit_state=emit_state),
        grid=(nb,),
        in_specs=in_specs,
        out_specs=out_specs,
        out_shape=out_shape,
        input_output_aliases=aliases,
        scratch_shapes=[pltpu.VMEM((n_seq * (L + SUBLANES) + SUBLANES, SSD_CONV_DIM), F32),
                        pltpu.VMEM((R, SSD_CONV_DIM), F32),
                        pltpu.VMEM((R, LANES), F32),
                        pltpu.VMEM((R // CHUNK, N_DIR * SSD_HEADS, CHUNK), F32),
                        pltpu.VMEM((R // CHUNK, SSD_GROUPS, CHUNK, CHUNK), F32),
                        pltpu.VMEM((R // CHUNK, SSD_GROUPS, SSD_STATE, CHUNK), BF16),
                        pltpu.VMEM((R, LANES), F32),
                        pltpu.VMEM((N_DIR, R, SSD_INNER), F32),
                        pltpu.VMEM((n_seq, N_DIR, N_PAIR, PAIR, SSD_STATE), F32)],
        compiler_params=_params(),
        name=f"ssd_mixer_L{L}",
    )(*args)
    return res if emit_state else (res[0], None)


_LOG2_CH = S5_GROUP_CH.bit_length() - 1
_LOG2_ST = S5_STATE.bit_length() - 1


def _cmul(ar, ai, br, bi):
    return ar * br - ai * bi, ar * bi + ai * br


def _s5prep_kernel(lre_ref, lim_ref, ldt_ref, btr_ref, bti_ref, ctr_ref, cti_ref,
                   bblk_ref, cblk_ref, pw_ref):
    brow = lax.broadcasted_iota(jnp.int32, (S5_DIM, S5_LANES), 0) >> _LOG2_CH
    bcol = lax.broadcasted_iota(jnp.int32, (S5_DIM, S5_LANES), 1) >> _LOG2_ST
    bmask = brow == bcol
    p1 = lax.broadcasted_iota(jnp.int32, (S5_SEG, S5_LANES), 0) + 1
    for d in range(N_DIR):
        lre = lre_ref[d]
        lim = lim_ref[d]
        step = jnp.exp(ldt_ref[d])
        mag = jnp.exp(lre * step)
        lbr = mag * jnp.cos(lim * step)
        lbi = mag * jnp.sin(lim * step)
        den = lre * lre + lim * lim
        nr = lbr - 1.0
        cr = (nr * lre + lbi * lim) / den
        ci = (lbi * lre - nr * lim) / den
        br, bi = _cmul(cr, ci, btr_ref[...], bti_ref[...])
        br = jnp.where(bmask, br, 0.0).astype(BF16)
        bi = jnp.where(bmask, bi, 0.0).astype(BF16)
        for sl in range(S5_NSLAB):
            bblk_ref[d, sl, :, 0:S5_SLAB] = br[:, sl * S5_SLAB:(sl + 1) * S5_SLAB]
            bblk_ref[d, sl, :, S5_SLAB:2 * S5_SLAB] = bi[:, sl * S5_SLAB:(sl + 1) * S5_SLAB]
        rr = jnp.ones((S5_SEG, S5_LANES), F32)
        ri = jnp.zeros((S5_SEG, S5_LANES), F32)
        sr, si = lbr, lbi
        for k in range(S5_SEG.bit_length()):
            bit = ((p1 >> k) & 1) == 1
            tr, ti = _cmul(rr, ri, sr, si)
            rr = jnp.where(bit, tr, rr)
            ri = jnp.where(bit, ti, ri)
            sr, si = _cmul(sr, si, sr, si)
        pw_ref[d, 0, 0:S5_SEG, :] = rr
        pw_ref[d, 1, 0:S5_SEG, :] = ri
        n = S5_SEG
        while n < S5_PW_ROWS:
            tr, ti = _cmul(pw_ref[d, 0, 0:n, :], pw_ref[d, 1, 0:n, :],
                           pw_ref[d, 0, n - 1:n, :], pw_ref[d, 1, n - 1:n, :])
            pw_ref[d, 0, n:2 * n, :] = tr
            pw_ref[d, 1, n:2 * n, :] = ti
            n *= 2
    crow = lax.broadcasted_iota(jnp.int32, (S5_LANES, S5_DIM), 0) >> _LOG2_ST
    ccol = lax.broadcasted_iota(jnp.int32, (S5_LANES, S5_DIM), 1) >> _LOG2_CH
    cmask = crow == ccol
    cr = jnp.where(cmask, ctr_ref[...], 0.0).astype(BF16)
    ci = jnp.where(cmask, -cti_ref[...], 0.0).astype(BF16)
    for sl in range(S5_NSLAB):
        cblk_ref[sl, 0:S5_SLAB, :] = cr[sl * S5_SLAB:(sl + 1) * S5_SLAB, :]
        cblk_ref[sl, S5_SLAB:2 * S5_SLAB, :] = ci[sl * S5_SLAB:(sl + 1) * S5_SLAB, :]


def _s5_prep(lam_re, lam_im, log_dt, b_re, b_im, c_re, c_im):
    row = lambda t: t.reshape(DEPTH, N_DIR, 1, S5_LANES)
    ldt = jnp.repeat(log_dt, S5_STATE, axis=-1)
    bt = lambda t: jnp.tile(t.transpose(0, 3, 1, 2).reshape(DEPTH, S5_GROUP_CH, S5_LANES),
                            (1, S5_GROUPS, 1))
    ct = lambda t: jnp.tile(t.transpose(0, 1, 3, 2).reshape(DEPTH, S5_LANES, S5_GROUP_CH),
                            (1, 1, S5_GROUPS))
    vec = pl.BlockSpec((None, N_DIR, 1, S5_LANES), lambda l: (l, 0, 0, 0))
    bsp = pl.BlockSpec((None, S5_DIM, S5_LANES), lambda l: (l, 0, 0))
    csp = pl.BlockSpec((None, S5_LANES, S5_DIM), lambda l: (l, 0, 0))
    return pl.pallas_call(
        _s5prep_kernel,
        grid=(DEPTH,),
        in_specs=[vec, vec, vec, bsp, bsp, csp, csp],
        out_specs=[pl.BlockSpec((None, N_DIR, S5_NSLAB, S5_DIM, 2 * S5_SLAB), lambda l: (l, 0, 0, 0, 0)),
                   pl.BlockSpec((None, S5_NSLAB, 2 * S5_SLAB, S5_DIM), lambda l: (l, 0, 0, 0)),
                   pl.BlockSpec((None, N_DIR, 2, S5_PW_ROWS, S5_LANES), lambda l: (l, 0, 0, 0, 0))],
        out_shape=[jax.ShapeDtypeStruct((DEPTH, N_DIR, S5_NSLAB, S5_DIM, 2 * S5_SLAB), BF16),
                   jax.ShapeDtypeStruct((DEPTH, S5_NSLAB, 2 * S5_SLAB, S5_DIM), BF16),
                   jax.ShapeDtypeStruct((DEPTH, N_DIR, 2, S5_PW_ROWS, S5_LANES), F32)],
        compiler_params=_params(),
        name="s5_prep",
    )(row(lam_re), row(lam_im), row(ldt), bt(b_re), bt(b_im), ct(c_re), ct(c_im))


def _s5_moves(n_seq, n_seg, seq_len, col_major):
    steps = seq_len // n_seg
    if not col_major:
        return [(q * seq_len + s * steps, steps, q * n_seg + s, S5_SEG)
                for q in range(n_seq) for s in range(n_seg)]
    assert n_seq == 1 and n_seg == S5_SEG
    grid_rows = seq_len // GRID_W
    wseg = GRID_W // n_seg
    return [(r * GRID_W + s * wseg, wseg, r * S5_SEG + s, grid_rows * S5_SEG)
            for r in range(grid_rows) for s in range(n_seg)]


def _s5_kernel(*refs, l, n_seq, n_seg, seq_len, col_major, has_h0, n_alias, emit_state):
    u_ref, bblk_ref, cblk_ref, pw_ref, dvec_ref, gw_ref, gb_ref = refs[:7]
    k = 7
    h0r_ref, h0i_ref = (refs[k], refs[k + 1]) if has_h0 else (None, None)
    k += 2 * int(has_h0) + n_alias
    y_ref = refs[k]
    fr_ref, fi_ref = (refs[k + 1], refs[k + 2]) if emit_state else (None, None)
    up, buf_f, buf_b, yacc, yp, cin, fin = refs[k + 1 + 2 * int(emit_state):]
    assert n_seq * n_seg == S5_SEG and not (has_h0 and n_seg == 1) and not (emit_state and n_seg > 1)
    n_rows = n_seq * seq_len
    steps = seq_len // n_seg
    n_planes = S5_DIM // LANES
    W = S5_SLAB
    moves = _s5_moves(n_seq, n_seg, seq_len, col_major)

    for src, n, dst, stride in moves:
        for t in range(n_planes):
            up[t, pl.ds(dst, n, stride=stride), :] = u_ref[src:src + n, t * LANES:(t + 1) * LANES]

    def u_rows(rs):
        return jnp.concatenate([up[t, rs, :] for t in range(n_planes)], axis=1)

    zero = jnp.zeros((S5_SEG, W), F32)
    zrow = jnp.zeros((1, W), F32)
    for sl in range(S5_NSLAB):
        ln = slice(sl * W, (sl + 1) * W)
        for r0 in range(0, n_rows, ROW_SLAB):
            rs = slice(r0, r0 + ROW_SLAB)
            ub = u_rows(rs).astype(BF16)
            buf_f[rs, :] = jnp.dot(ub, bblk_ref[0, sl], preferred_element_type=F32)
            buf_b[rs, :] = jnp.dot(ub, bblk_ref[1, sl], preferred_element_type=F32)
        lam = [[jnp.broadcast_to(pw_ref[d, c, 0:1, ln], (S5_SEG, W)) for c in range(2)]
               for d in range(N_DIR)]

        def body(t, carry):
            fr, fi, br, bi = carry
            rf = pl.ds(pl.multiple_of(t * S5_SEG, S5_SEG), S5_SEG)
            rb = pl.ds(pl.multiple_of((steps - 1 - t) * S5_SEG, S5_SEG), S5_SEG)
            nfr = lam[0][0] * fr - lam[0][1] * fi + buf_f[rf, 0:W]
            nfi = lam[0][0] * fi + lam[0][1] * fr + buf_f[rf, W:2 * W]
            nbr = lam[1][0] * br - lam[1][1] * bi + buf_b[rb, 0:W]
            nbi = lam[1][0] * bi + lam[1][1] * br + buf_b[rb, W:2 * W]
            buf_f[rf, 0:W] = nfr
            buf_f[rf, W:2 * W] = nfi
            buf_b[rb, 0:W] = nbr
            buf_b[rb, W:2 * W] = nbi
            return nfr, nfi, nbr, nbi

        ends = lax.fori_loop(0, steps, body, (zero, zero, zero, zero), unroll=S5_UNROLL)

        if n_seg == 1:
            if emit_state:
                for d in range(N_DIR):
                    fin[d, 0, :, ln] = ends[2 * d]
                    fin[d, 1, :, ln] = ends[2 * d + 1]
        else:
            for d in range(N_DIR):
                er, ei = ends[2 * d], ends[2 * d + 1]
                lpr = pw_ref[d, 0, steps - 1:steps, ln]
                lpi = pw_ref[d, 1, steps - 1:steps, ln]
                pr = h0r_ref[d:d + 1, ln] if has_h0 else zrow
                pi = h0i_ref[d:d + 1, ln] if has_h0 else zrow
                order = range(n_seg) if d == 0 else range(n_seg - 1, -1, -1)
                for s in order:
                    cin[d, 0, s:s + 1, :] = pr
                    cin[d, 1, s:s + 1, :] = pi
                    tr, ti = _cmul(lpr, lpi, pr, pi)
                    pr = er[s:s + 1, :] + tr
                    pi = ei[s:s + 1, :] + ti
            cfr = cin[0, 0]
            cfi = cin[0, 1]
            cbr = cin[1, 0]
            cbi = cin[1, 1]

            def fix(p, carry):
                rows = pl.ds(pl.multiple_of(p * S5_SEG, S5_SEG), S5_SEG)
                q = steps - 1 - p
                tfr, tfi = _cmul(pw_ref[0, 0, pl.ds(p, 1), ln], pw_ref[0, 1, pl.ds(p, 1), ln], cfr, cfi)
                tbr, tbi = _cmul(pw_ref[1, 0, pl.ds(q, 1), ln], pw_ref[1, 1, pl.ds(q, 1), ln], cbr, cbi)
                buf_f[rows, 0:W] = buf_f[rows, 0:W] + buf_b[rows, 0:W] + tfr + tbr
                buf_f[rows, W:2 * W] = buf_f[rows, W:2 * W] + buf_b[rows, W:2 * W] + tfi + tbi
                return carry

            lax.fori_loop(0, steps, fix, 0, unroll=S5_UNROLL)

        for r0 in range(0, n_rows, ROW_SLAB):
            rs = slice(r0, r0 + ROW_SLAB)
            hs = buf_f[rs, :] if n_seg > 1 else buf_f[rs, :] + buf_b[rs, :]
            part = jnp.dot(hs.astype(BF16), cblk_ref[sl], preferred_element_type=F32)
            if sl == 0:
                yacc[rs, :] = part
            else:
                yacc[rs, :] = yacc[rs, :] + part

    gwb = gw_ref[...].astype(BF16)
    for r0 in range(0, n_rows, ROW_SLAB):
        rs = slice(r0, r0 + ROW_SLAB)
        y = _gelu_tanh(yacc[rs, :] + dvec_ref[l:l + 1, :] * u_rows(rs))
        gate = jnp.dot(y.astype(BF16), gwb, preferred_element_type=F32) + gb_ref[l:l + 1, :]
        y = y * jax.nn.sigmoid(gate)
        for t in range(n_planes):
            yp[t, rs, :] = y[:, t * LANES:(t + 1) * LANES]
    for src, n, dst, stride in moves:
        for t in range(n_planes):
            y_ref[src:src + n, t * LANES:(t + 1) * LANES] = yp[t, pl.ds(dst, n, stride=stride), :]
    if emit_state:
        for q in range(n_seq):
            for d in range(N_DIR):
                fr_ref[q, d:d + 1, :] = fin[d, 0, q:q + 1, :]
                fi_ref[q, d:d + 1, :] = fin[d, 1, q:q + 1, :]


def _s5_mixer(u, bblk, cblk, pw, prm, l, *, seq_len, n_seq, nb, blk0, col_major, h0=None, y_buf=None,
              st_buf=None, emit_state):
    n_rows = n_seq * seq_len
    n_seg = S5_SEG // n_seq
    seq = pl.BlockSpec((n_rows, S5_DIM), lambda b: (blk0 + b, 0))
    st = pl.BlockSpec((n_seq, None, N_DIR, S5_LANES), lambda b: (b, l, 0, 0))
    once = dict(pipeline_mode=pl.Buffered(1))
    args = [u, bblk, cblk, pw, prm["d_vec"], prm["glu_w"], prm["glu_b"]]
    in_specs = [seq, _layer_spec(l, (N_DIR, S5_NSLAB, S5_DIM, 2 * S5_SLAB), **once),
                _layer_spec(l, (S5_NSLAB, 2 * S5_SLAB, S5_DIM), **once),
                _layer_spec(l, (N_DIR, 2, S5_PW_ROWS if n_seg > 1 else S5_SEG, S5_LANES), **once),
                _rows_spec(S5_DIM), _layer_spec(l, (S5_DIM, S5_DIM)), _rows_spec(S5_DIM)]
    if h0 is not None:
        h0_spec = pl.BlockSpec((None, None, N_DIR, S5_LANES), lambda b: (b, l, 0, 0))
        args += list(h0)
        in_specs += [h0_spec, h0_spec]
    out_shape = [jax.ShapeDtypeStruct((N_TOK, S5_DIM), F32)]
    out_specs = [seq]
    aliases = {}
    if y_buf is not None:
        aliases[len(args)] = 0
        args.append(y_buf)
        in_specs.append(_any_spec())
    if emit_state:
        out_shape += [jax.ShapeDtypeStruct((BATCH, DEPTH, N_DIR, S5_LANES), F32)] * 2
        out_specs += [st, st]
        if st_buf is not None:
            for j, buf in enumerate(st_buf):
                aliases[len(args)] = 1 + j
                args.append(buf)
                in_specs.append(_any_spec())
    res = pl.pallas_call(
        functools.partial(_s5_kernel, l=l, n_seq=n_seq, n_seg=n_seg, seq_len=seq_len, col_major=col_major,
                          has_h0=h0 is not None, n_alias=len(aliases), emit_state=emit_state),
        grid=(nb,),
        in_specs=in_specs,
        out_specs=out_specs,
        out_shape=out_shape,
        input_output_aliases=aliases,
        scratch_shapes=[pltpu.VMEM((S5_DIM // LANES, n_rows, LANES), F32),
                        pltpu.VMEM((n_rows, 2 * S5_SLAB), F32),
                        pltpu.VMEM((n_rows, 2 * S5_SLAB), F32),
                        pltpu.VMEM((n_rows, S5_DIM), F32),
                        pltpu.VMEM((S5_DIM // LANES, n_rows, LANES), F32),
                        pltpu.VMEM((N_DIR, 2, S5_SEG, S5_SLAB), F32),
                        pltpu.VMEM((N_DIR, 2, S5_SEG, S5_LANES), F32)],
        compiler_params=_params(),
        name=f"s5_mixer_L{seq_len}",
    )(*args)
    return (res[0], res[1], res[2]) if emit_state else (res[0], None, None)


def _outffn_kernel(xa_ref, xb_ref, ys_ref, yg_ref, y5_ref, g1_ref, sh_ref, sc_ref, g2_ref, ng_ref,
                   fg_ref, wo_hbm, w1_hbm, w2_hbm, *rest, l, final):
    n_out = 2 if final else 1
    o_refs = rest[:n_out]
    wob, w1b, w2b, st1, st2, sem = rest[n_out:]
    t = pl.program_id(0)
    n_ctx = N_CTX // FFN_TILE
    kf = D_FF // FFN_PREP
    half = D_MODEL // 2
    assert half == kf, "w_out halves are staged in the w2 chunk slots"

    def w1_copy(j):
        return pltpu.make_async_copy(w1_hbm.at[l, :, pl.ds(j * kf, kf)], st1.at[j % 2], sem.at[0, j % 2])

    def w2_copy(j):
        return pltpu.make_async_copy(w2_hbm.at[l, pl.ds(j * kf, kf), :], st2.at[j % 2], sem.at[1, j % 2])

    def wo_copy(s):
        return pltpu.make_async_copy(wo_hbm.at[l, pl.ds(s * half, half), :], st2.at[s], sem.at[1, s])

    def tile(first):
        if first:
            for s in range(2):
                wo_copy(s).start(priority=1)
                w1_copy(s).start()
            for s in range(2):
                wo_copy(s).wait()
                wob[s * half:(s + 1) * half, :] = st2[s].astype(BF16)
            for s in range(2):
                w2_copy(s).start(priority=1)
        r = _mod_row(t, FFN_TILE)
        row = lambda ref: ref[pl.ds(r, 1), :]
        x = jnp.where(t < n_ctx, xa_ref[...], xb_ref[...])
        mixed = jnp.concatenate([ys_ref[...].astype(BF16), yg_ref[...].astype(BF16),
                                 y5_ref[...].astype(BF16)], axis=1)
        x1 = x + row(g1_ref) * jnp.dot(mixed, wob[...], preferred_element_type=F32)
        h = ((_rms(x1) * ng_ref[l:l + 1, :]) * (1.0 + row(sc_ref)) + row(sh_ref)).astype(BF16)
        acc = jnp.zeros((FFN_TILE, D_MODEL), F32)
        for j in range(FFN_PREP):
            if first:
                w1_copy(j).wait()
                w1b[j] = st1[j % 2].astype(BF16)
                w2_copy(j).wait()
                w2b[j] = st2[j % 2].astype(BF16)
                if j + 2 < FFN_PREP:
                    w1_copy(j + 2).start()
                    w2_copy(j + 2).start(priority=1)
            a = jnp.maximum(jnp.dot(h, w1b[j], preferred_element_type=F32), 0.0)
            acc = acc + jnp.dot((a * a).astype(BF16), w2b[j], preferred_element_type=F32)
        x2 = x1 + row(g2_ref) * acc
        if not final:
            o_refs[0][...] = x2
        else:
            y = _rms(x2) * fg_ref[...]

            @pl.when(t < n_ctx)
            def _():
                o_refs[0][...] = y

            @pl.when(t >= n_ctx)
            def _():
                o_refs[1][...] = y

    @pl.when(t == 0)
    def _():
        tile(True)

    @pl.when(t > 0)
    def _():
        tile(False)


def _out_ffn(xa, xb, b_is_stream, y_ssd, y_sgu, y_s5, l, mods, norm_g, final_g, w_out, w1, w2, *, final):
    tok = lambda w: pl.BlockSpec((FFN_TILE, w), lambda i: (i, 0))
    sa, sb = _split_tok_specs(FFN_TILE, D_MODEL, b_is_stream)
    if final:
        out_specs = list(_split_tok_specs(FFN_TILE, D_MODEL, False))
        out_shape = [jax.ShapeDtypeStruct((N_CTX, D_MODEL), F32), jax.ShapeDtypeStruct((N_LAT, D_MODEL), F32)]
    else:
        out_specs = [tok(D_MODEL)]
        out_shape = [jax.ShapeDtypeStruct((N_TOK, D_MODEL), F32)]
    kf = D_FF // FFN_PREP
    return pl.pallas_call(
        functools.partial(_outffn_kernel, l=l, final=final),
        grid=(N_TOK // FFN_TILE,),
        in_specs=[sa, sb, tok(SSD_INNER), tok(SGU_DIM), tok(S5_DIM),
                  _mod_spec(l, 2), _mod_spec(l, 3), _mod_spec(l, 4), _mod_spec(l, 5),
                  _rows_spec(D_MODEL), pl.BlockSpec((1, D_MODEL), lambda i: (0, 0)),
                  _any_spec(), _any_spec(), _any_spec()],
        out_specs=out_specs,
        out_shape=out_shape,
        scratch_shapes=[pltpu.VMEM((D_MODEL, D_MODEL), BF16),
                        pltpu.VMEM((FFN_PREP, D_MODEL, kf), BF16),
                        pltpu.VMEM((FFN_PREP, kf, D_MODEL), BF16),
                        pltpu.VMEM((2, D_MODEL, kf), F32),
                        pltpu.VMEM((2, kf, D_MODEL), F32),
                        pltpu.SemaphoreType.DMA((2, 2))],
        compiler_params=_params(),
        name="out_ffn_final" if final else "out_ffn",
    )(xa, xb, y_ssd, y_sgu, y_s5, mods, mods, mods, mods, norm_g, final_g, w_out, w1, w2)


def kernel(x_prompt, x_sample, state_ssd, state_s5_re, state_s5_im, c, c_ctx, ada_w, ada_b, norm1_g,
           norm2_g, w_in, ssd_conv_w, ssd_conv_b, ssd_dt_bias, ssd_a_log, ssd_d, ssd_norm_g,
           sgu_norm_g, sgu_w, sgu_b, s5_lambda_re, s5_lambda_im, s5_log_dt, s5_b_re, s5_b_im,
           s5_c_re, s5_c_im, s5_d, s5_glu_w, s5_glu_b, w_out, ffn_w1, ffn_w2, final_norm_g):
    cvec = jnp.concatenate([c_ctx[None, :], c,
                            jnp.zeros((MOD_ROWS - 1 - DEC_BATCH, D_MODEL), F32)], axis=0)
    mods = _adaln_mods(cvec, ada_w, ada_b)
    bblk, cblk, pw = _s5_prep(s5_lambda_re, s5_lambda_im, s5_log_dt, s5_b_re, s5_b_im,
                              s5_c_re, s5_c_im)

    w_in_t = w_in.transpose(0, 2, 1)
    ssd_prm = dict(conv_w=ssd_conv_w, conv_b=ssd_conv_b,
                   dt_bias=ssd_dt_bias.reshape(DEPTH, N_DIR * SSD_HEADS),
                   a_log=ssd_a_log.reshape(DEPTH, N_DIR * SSD_HEADS),
                   d_vec=jnp.repeat(ssd_d, SSD_HEAD_DIM, axis=-1), norm_g=ssd_norm_g)
    s5_prm = dict(d_vec=s5_d, glu_w=s5_glu_w, glu_b=s5_glu_b)
    sgu_w_pair = sgu_w.reshape(DEPTH, SGU_HEADS // 2, 2, CHUNK, CHUNK).transpose(0, 1, 3, 2, 4)
    sgu_w_pair = sgu_w_pair.reshape(DEPTH, SGU_HEADS // 2, CHUNK, 2 * CHUNK).astype(BF16)
    sgu_b_full = jnp.repeat(sgu_b.transpose(0, 2, 1), SGU_DIM // SGU_HEADS, axis=2)
    norm1 = norm1_g
    norm2 = norm2_g
    sgu_g = sgu_norm_g
    final_g = final_norm_g.reshape(1, D_MODEL)

    lat_ssd = state_ssd.reshape(DEC_BATCH, DEPTH, N_DIR, SSD_HP, SSD_STATE)
    lat_s5 = (state_s5_re.reshape(DEC_BATCH, DEPTH, N_DIR, S5_LANES),
              state_s5_im.reshape(DEC_BATCH, DEPTH, N_DIR, S5_LANES))
    lat_blk = N_CTX // DEC_SEQ

    xa = x_prompt.reshape(N_CTX, D_MODEL)
    xb = x_sample.reshape(N_LAT, D_MODEL)
    b_is_stream = False
    st_ssd = None
    st_s5 = None
    for l in range(DEPTH):
        z, xbc, y_sgu, s5u, dtr = _in_proj(xa, xb, b_is_stream, l, mods, norm1, w_in_t, sgu_g,
                                           sgu_w_pair, sgu_b_full)
        y_s5, fr, fi = _s5_mixer(s5u, bblk, cblk, pw, s5_prm, l, seq_len=SEQ, n_seq=S5_SEG,
                                 nb=BATCH // S5_SEG, blk0=0, col_major=False, st_buf=st_s5,
                                 emit_state=True)
        st_s5 = (fr, fi)
        y_s5, _, _ = _s5_mixer(s5u, bblk, cblk, pw, s5_prm, l, seq_len=DEC_SEQ, n_seq=1, nb=DEC_BATCH,
                               blk0=lat_blk, col_major=True, h0=lat_s5, y_buf=y_s5, emit_state=False)
        y_ssd, st_ssd = _ssd_mixer(z, xbc, dtr, ssd_prm, l, L=SEQ, n_seq=SSD_CTX_SEQS,
                                   nb=BATCH // SSD_CTX_SEQS, blk0=0, st_buf=st_ssd, after=y_s5,
                                   emit_state=True)
        y_ssd, _ = _ssd_mixer(z, xbc, dtr, ssd_prm, l, L=DEC_SEQ, n_seq=1, nb=DEC_BATCH, blk0=lat_blk,
                              h0=lat_ssd, y_buf=y_ssd, emit_state=False)
        final = l == DEPTH - 1
        out = _out_ffn(xa, xb, b_is_stream, y_ssd, y_sgu, y_s5, l, mods, norm2, final_g, w_out, ffn_w1,
                       ffn_w2, final=final)
        if not final:
            xa = xb = out[0]
            b_is_stream = True

    y_prompt = out[0].reshape(BATCH, SEQ, D_MODEL)
    y_sample = out[1].reshape(DEC_BATCH, DEC_SEQ, D_MODEL)
    new_state_ssd = st_ssd.reshape(BATCH, DEPTH, N_DIR, SSD_HEADS, SSD_HEAD_DIM, SSD_STATE)
    s5_shape = (BATCH, DEPTH, N_DIR, S5_GROUPS, S5_STATE)
    return (y_prompt, y_sample, new_state_ssd, st_s5[0].reshape(s5_shape), st_s5[1].reshape(s5_shape))
```

```python
import functools
import math

import jax
import jax.numpy as jnp
from jax import lax
from jax.experimental import pallas as pl
from jax.experimental.pallas import tpu as pltpu

F32 = jnp.float32
BF16 = jnp.bfloat16

D_MODEL = 1024
BATCH = 16
SEQ = 256
DEPTH = 2
DEC_BATCH = 2
DEC_SEQ = 1024
GRID_W = 64
CHUNK = 128
N_DIR = 2
EPS = 1e-6
SSD_INNER = 512
SSD_HEAD_DIM = 64
SSD_HEADS = 8
SSD_GROUPS = 2
SSD_STATE = 128
SSD_CONV = 5
SSD_CONV_DIM = SSD_INNER + 2 * SSD_GROUPS * SSD_STATE
SGU_DIM = 256
SGU_HEADS = 4
S5_DIM = 256
S5_GROUP_CH = 16
S5_GROUPS = 16
S5_STATE = 64
D_FF = 4 * D_MODEL
OFF_XBC = SSD_INNER
OFF_DT = OFF_XBC + SSD_CONV_DIM
OFF_SGU = OFF_DT + N_DIR * SSD_HEADS
OFF_S5 = OFF_SGU + 2 * SGU_DIM
IN_DIM = OFF_S5 + S5_DIM

N_CTX = BATCH * SEQ
N_LAT = DEC_BATCH * DEC_SEQ
N_TOK = N_CTX + N_LAT
LANES = 128
SUBLANES = 8
MOD_ROWS = SUBLANES
SSD_HP = SSD_HEADS * SSD_HEAD_DIM
S5_LANES = S5_GROUPS * S5_STATE
S5_SEG = SUBLANES
S5_PW_ROWS = DEC_SEQ // S5_SEG
S5_SLAB = 256
S5_NSLAB = S5_LANES // S5_SLAB
S5_UNROLL = 16
PAIR = 2 * SSD_HEAD_DIM
N_PAIR = SSD_HEADS // 2
TOK_TILE = 512
N_CTX_TILES = N_CTX // TOK_TILE
FFN_TILE = 512
FFN_PREP = 8
FFN_SLOTS = 3
FFN_VMEM_LIMIT = 61 * 1024 * 1024
ROW_SLAB = 256
CONV_LANES = 256
SSD_CTX_SEQS = 2
SSD_UNROLL = 8
VMEM_LIMIT = 56 * 1024 * 1024

_NT = (((1,), (1,)), ((), ()))


def _params(n_axes=1):
    return pltpu.CompilerParams(dimension_semantics=("arbitrary",) * n_axes,
                                vmem_limit_bytes=VMEM_LIMIT)


def _layer_spec(l, shape, **kw):
    return pl.BlockSpec((None,) + tuple(shape), lambda *_: (l,) + (0,) * len(shape), **kw)


def _mod_spec(l, k):
    return pl.BlockSpec((None, MOD_ROWS, D_MODEL), lambda *_: (l, 0, k))


def _rows_spec(n):
    return pl.BlockSpec((DEPTH, n), lambda *_: (0, 0))


def _any_spec():
    return pl.BlockSpec(memory_space=pl.ANY)


def _mod_row(i, tm):
    n_ctx = N_CTX // tm
    return jnp.where(i < n_ctx, 0, 1 + (i - n_ctx) // (DEC_SEQ // tm))


def _split_tok_specs(tm, width, b_has_ctx_rows, tile_of=lambda i: i):
    n_ctx = N_CTX // tm
    b_off = n_ctx if b_has_ctx_rows else 0
    a = pl.BlockSpec((tm, width), lambda i: (jnp.minimum(tile_of(i), n_ctx - 1), 0))
    b = pl.BlockSpec((tm, width), lambda i: (jnp.maximum(tile_of(i), n_ctx) - n_ctx + b_off, 0))
    return a, b


def _silu(x):
    return x * jax.nn.sigmoid(x)


def _gelu_tanh(x):
    c = math.sqrt(2.0 / math.pi)
    return 0.5 * x * (1.0 + jnp.tanh(c * (x + 0.044715 * (x * x * x))))


def _rms(x):
    return x * lax.rsqrt(jnp.mean(x * x, axis=-1, keepdims=True) + EPS)


def _bdot(a, b):
    return jnp.dot(a.astype(BF16), b.astype(BF16), preferred_element_type=F32)


def _split3(x):
    hi = x.astype(BF16)
    r = x - hi.astype(F32)
    mid = r.astype(BF16)
    lo = (r - mid.astype(F32)).astype(BF16)
    return jnp.concatenate([hi, mid, lo], axis=1)


def _mod_kernel(c_ref, w_ref, b_ref, o_ref):
    bias = b_ref[pl.ds(pl.program_id(0), 1), :]
    o_ref[...] = _bdot(_silu(c_ref[...]), w_ref[...]) + bias


def _adaln_mods(cvec, ada_w, ada_b):
    n_blk = 6
    return pl.pallas_call(
        _mod_kernel,
        grid=(DEPTH, n_blk),
        in_specs=[pl.BlockSpec((MOD_ROWS, D_MODEL), lambda l, j: (0, 0)),
                  pl.BlockSpec((None, D_MODEL, D_MODEL), lambda l, j: (l, 0, j)),
                  pl.BlockSpec((DEPTH, D_MODEL), lambda l, j: (0, j))],
        out_specs=pl.BlockSpec((None, MOD_ROWS, D_MODEL), lambda l, j: (l, 0, j)),
        out_shape=jax.ShapeDtypeStruct((DEPTH, MOD_ROWS, 6 * D_MODEL), F32),
        compiler_params=_params(2),
        name="adaln_mod",
    )(cvec, ada_w, ada_b)


_C_Z = 0
_C_XBC = _C_Z + SSD_INNER
_C_SGU = _C_XBC + SSD_CONV_DIM
_C_S5 = _C_SGU + 2 * SGU_DIM
_C_DT = _C_S5 + S5_DIM
_C_END = _C_DT + LANES


def _inproj_kernel(xa_ref, xb_ref, sh_ref, sc_ref, g_ref, win_ref, sg_ref, sw_ref, sb_ref,
                   z_ref, xbc_ref, ysgu_ref, s5_ref, dt_ref, w_ref, *, l):
    i = pl.program_id(0)

    @pl.when(i == 0)
    def _():
        moves = ((0, OFF_DT, _C_Z), (OFF_SGU, IN_DIM, _C_SGU), (OFF_DT, OFF_SGU, _C_DT))
        for src0, src1, dst0 in moves:
            for r0 in range(src0, src1, ROW_SLAB):
                n = min(ROW_SLAB, src1 - r0)
                w_ref[dst0 + r0 - src0:dst0 + r0 - src0 + n, :] = win_ref[r0:r0 + n, :].astype(BF16)
        n_pad = LANES - N_DIR * SSD_HEADS
        w_ref[_C_END - n_pad:_C_END, :] = jnp.zeros((n_pad, D_MODEL), BF16)

    r = _mod_row(i, TOK_TILE)
    x = jnp.where(i < N_CTX_TILES, xa_ref[...], xb_ref[...])
    shift = sh_ref[pl.ds(r, 1), :]
    scale = sc_ref[pl.ds(r, 1), :]
    h = (_rms(x) * g_ref[l:l + 1, :]) * (1.0 + scale) + shift
    hb = h.astype(BF16)
    proj = lambda c0, c1: lax.dot_general(hb, w_ref[c0:c1, :], _NT, preferred_element_type=F32)
    uv = _gelu_tanh(proj(_C_SGU, _C_S5))
    z_ref[...] = proj(_C_Z, _C_XBC)
    xbc_ref[...] = proj(_C_XBC, _C_SGU)
    s5_ref[...] = proj(_C_S5, _C_DT)
    dt_ref[...] = proj(_C_DT, _C_END)

    u = uv[:, :SGU_DIM]
    v = _rms(uv[:, SGU_DIM:]) * sg_ref[l:l + 1, :]
    lo_lane = lax.broadcasted_iota(jnp.int32, (CHUNK, LANES), 1) < (LANES // 2)
    for c in range(TOK_TILE // CHUNK):
        rows = slice(c * CHUNK, (c + 1) * CHUNK)
        mix = []
        for pr in range(SGU_HEADS // 2):
            vp = v[rows, pr * LANES:(pr + 1) * LANES]
            rhs = jnp.concatenate([jnp.where(lo_lane, vp, 0.0).astype(BF16),
                                   jnp.where(lo_lane, 0.0, vp).astype(BF16)], axis=0)
            mix.append(jnp.dot(sw_ref[pr], rhs, preferred_element_type=F32))
        ysgu_ref[rows, :] = u[rows, :] * (jnp.concatenate(mix, axis=1) + sb_ref[...])


def _in_proj(xa, xb, b_is_stream, l, mods, norm_g, w_in, sgu_g, sgu_w_pair, sgu_b_full):
    tok = lambda w: pl.BlockSpec((TOK_TILE, w), lambda i: (i, 0))
    widths = (SSD_INNER, SSD_CONV_DIM, SGU_DIM, S5_DIM, LANES)
    sa, sb = _split_tok_specs(TOK_TILE, D_MODEL, b_is_stream)
    return pl.pallas_call(
        functools.partial(_inproj_kernel, l=l),
        grid=(N_TOK // TOK_TILE,),
        in_specs=[sa, sb, _mod_spec(l, 0), _mod_spec(l, 1), _rows_spec(D_MODEL),
                  _layer_spec(l, (IN_DIM, D_MODEL), pipeline_mode=pl.Buffered(1)),
                  _rows_spec(SGU_DIM), _layer_spec(l, (SGU_HEADS // 2, CHUNK, 2 * CHUNK)),
                  _layer_spec(l, (CHUNK, SGU_DIM))],
        out_specs=[tok(w) for w in widths],
        out_shape=[jax.ShapeDtypeStruct((N_TOK, w), F32) for w in widths],
        scratch_shapes=[pltpu.VMEM((_C_END, D_MODEL), BF16)],
        compiler_params=_params(),
        name="in_proj",
    )(xa, xb, mods, mods, norm_g, w_in, sgu_g, sgu_w_pair, sgu_b_full)


def _ssd_kernel(*refs, l, L, n_seq, has_h0, n_alias, emit_state):
    z_ref, xbc_ref, dt_ref, cw_ref, cb_ref, dtb_ref, alog_ref, dvec_ref, ng_ref = refs[:9]
    lrow = slice(l, l + 1)
    lane_fill = jnp.zeros((1, LANES - N_DIR * SSD_HEADS), F32)
    k = 9
    h0_ref = refs[k] if has_h0 else None
    k += int(has_h0) + n_alias
    y_ref = refs[k]
    hout_ref = refs[k + 1] if emit_state else None
    xpad, xc, acol, atr, cbs, bmt, dtsp, yacc, hst = refs[k + 1 + int(emit_state):]
    Q = CHUNK
    nc = L // Q
    halo = SUBLANES
    pad = (SSD_CONV - 1) // 2

    pitch = L + halo
    for q in range(n_seq + 1):
        xpad[q * pitch:q * pitch + halo, :] = jnp.zeros((halo, SSD_CONV_DIM), F32)
    for g in range(n_seq * nc):
        o = halo + (g // nc) * pitch + (g % nc) * Q
        xpad[o:o + Q, :] = xbc_ref[g * Q:(g + 1) * Q, :]
    win = Q + 2 * halo
    for c in range(n_seq * nc):
        for lb in range(0, SSD_CONV_DIM, CONV_LANES):
            ln = slice(lb, lb + CONV_LANES)
            o = (c // nc) * pitch + (c % nc) * Q
            xa = xpad[o:o + win, ln]
            acc = cb_ref[lrow, ln] + cw_ref[pad:pad + 1, ln] * xa[halo:halo + Q, :]
            for t in range(SSD_CONV):
                if t != pad:
                    rolled = pltpu.roll(xa, (pad - t) % win, 0)
                    acc = acc + cw_ref[t:t + 1, ln] * rolled[halo:halo + Q, :]
            xc[c * Q:(c + 1) * Q, ln] = _silu(acc)

    raw = dt_ref[...] + jnp.concatenate([dtb_ref[lrow, :], lane_fill], axis=1)
    dt = jnp.maximum(raw, 0.0) + jnp.log(1.0 + jnp.exp(-jnp.abs(raw)))
    dtsp[...] = dt
    a_neg = -jnp.exp(jnp.concatenate([alog_ref[lrow, :], lane_fill], axis=1))
    row = lax.broadcasted_iota(jnp.int32, (Q, Q), 0)
    col = lax.broadcasted_iota(jnp.int32, (Q, Q), 1)
    lower = row >= col
    upper = col >= row
    tri3 = jnp.concatenate([lower.astype(BF16)] * 3, axis=1)
    fwd_lane = lax.broadcasted_iota(jnp.int32, (Q, LANES), 1) < SSD_HEADS
    for c in range(n_seq * nc):
        dta = dtsp[c * Q:(c + 1) * Q, :] * a_neg
        hi = dta.astype(BF16)
        rem = dta - hi.astype(F32)
        mid = rem.astype(BF16)
        lo = (rem - mid.astype(F32)).astype(BF16)
        pre = jnp.dot(tri3, jnp.concatenate([hi, mid, lo], axis=0), preferred_element_type=F32)
        suf = pre[Q - 1:Q, :] - pre + dta
        a = jnp.where(fwd_lane, pre, suf)
        acol[c * Q:(c + 1) * Q, :] = a
        atr[c] = a.T[0:N_DIR * SSD_HEADS, :]
        for g in range(SSD_GROUPS):
            b0 = SSD_INNER + g * SSD_STATE
            c0 = SSD_INNER + (SSD_GROUPS + g) * SSD_STATE
            bm = xc[c * Q:(c + 1) * Q, b0:b0 + SSD_STATE]
            cbs[c, g] = lax.dot_general(xc[c * Q:(c + 1) * Q, c0:c0 + SSD_STATE].astype(BF16),
                                        bm.astype(BF16), _NT, preferred_element_type=F32)
            bmt[c, g] = bm.T.astype(BF16)

    sel_row = lax.broadcasted_iota(jnp.int32, (3 * LANES, SSD_INNER), 0) & (LANES - 1)
    sel_head = lax.broadcasted_iota(jnp.int32, (3 * LANES, SSD_INNER), 1) >> (SSD_HEAD_DIM.bit_length() - 1)
    sel = [(sel_row == sel_head + d * SSD_HEADS).astype(BF16) for d in range(N_DIR)]
    mask2 = [jnp.concatenate([m, m], axis=1) for m in (lower, upper)]

    for q in range(n_seq):
        for d in range(N_DIR):
            for p in range(N_PAIR):
                if has_h0:
                    hst[q, d, p] = h0_ref[q, d, p * PAIR:(p + 1) * PAIR, :].T
                else:
                    hst[q, d, p] = jnp.zeros((SSD_STATE, PAIR), F32)

    lo_lane = lax.broadcasted_iota(jnp.int32, (Q, PAIR), 1) < SSD_HEAD_DIM

    def chunk(c, d, q):
        r0 = pl.multiple_of(c * Q, Q)
        rows = pl.ds(r0, Q)
        a = acol[rows, :]
        a_t = atr[c]
        dtp = jnp.dot(_split3(dtsp[rows, :]), sel[d], preferred_element_type=F32)
        a_end = a[Q - 1:Q, :] if d == 0 else a[0:1, :]
        dec = jnp.exp(a_end)
        for g in range(SSD_GROUPS):
            c0 = SSD_INNER + (SSD_GROUPS + g) * SSD_STATE
            cmb = xc[rows, c0:c0 + SSD_STATE].astype(BF16)
            cb = cbs[c, g]
            cb2 = jnp.concatenate([cb, cb], axis=1)
            for pr in range(2):
                p = g * 2 + pr
                j0 = d * SSD_HEADS + 2 * p
                j1 = j0 + 1
                xs = xc[rows, p * PAIR:(p + 1) * PAIR]
                ab0 = jnp.broadcast_to(a[:, j0:j0 + 1], (Q, Q))
                ab1 = jnp.broadcast_to(a[:, j1:j1 + 1], (Q, Q))
                seg = (jnp.concatenate([ab0, ab1], axis=1)
                       - jnp.concatenate([a_t[j0:j0 + 1, :], a_t[j1:j1 + 1, :]], axis=1))
                m = (cb2 * jnp.exp(jnp.where(mask2[d], seg, -jnp.inf))).astype(BF16)
                xdt = xs * dtp[:, p * PAIR:(p + 1) * PAIR]
                rhs = jnp.concatenate([jnp.where(lo_lane, xdt, 0.0).astype(BF16),
                                       jnp.where(lo_lane, 0.0, xdt).astype(BF16)], axis=0)
                y_diag = jnp.dot(m, rhs, preferred_element_type=F32)
                hp = hst[q, d, p]
                y_off = jnp.dot(cmb, hp.astype(BF16), preferred_element_type=F32)
                a_pair = jnp.where(lo_lane, ab0, ab1)
                y = y_diag + jnp.exp(a_pair) * y_off
                a_end_pair = jnp.where(lo_lane[0:1, :], a_end[:, j0:j0 + 1], a_end[:, j1:j1 + 1])
                xw = (xdt * jnp.exp(a_end_pair - a_pair)).astype(BF16)
                s_new = jnp.dot(bmt[c, g], xw, preferred_element_type=F32)
                decp = jnp.where(lo_lane[0:1, :], dec[:, j0:j0 + 1], dec[:, j1:j1 + 1])
                hst[q, d, p] = decp * hp + s_new
                if d == 0:
                    y = y + dvec_ref[lrow, p * PAIR:(p + 1) * PAIR] * xs
                yacc[d, rows, p * PAIR:(p + 1) * PAIR] = y

    def step(t, carry):
        for q in range(n_seq):
            chunk(q * nc + t, 0, q)
            chunk(q * nc + nc - 1 - t, 1, q)
        return carry

    lax.fori_loop(0, nc, step, 0, unroll=min(nc, SSD_UNROLL))
    for c in range(n_seq * nc):
        rows = slice(c * Q, (c + 1) * Q)
        y = (yacc[0, rows, :] + yacc[1, rows, :]) * _silu(z_ref[rows, :])
        y_ref[rows, :] = _rms(y) * ng_ref[lrow, :]
    if emit_state:
        for q in range(n_seq):
            for d in range(N_DIR):
                for p in range(N_PAIR):
                    hout_ref[q, d, p * PAIR:(p + 1) * PAIR, :] = hst[q, d, p].T


def _ssd_mixer(z, xbc, dtr, prm, l, *, L, n_seq, nb, blk0, h0=None, y_buf=None, st_buf=None, after=None,
               emit_state):
    R = n_seq * L
    seq = lambda w: pl.BlockSpec((R, w), lambda b: (blk0 + b, 0))
    st = pl.BlockSpec((n_seq, None, N_DIR, SSD_HP, SSD_STATE), lambda b: (b, l, 0, 0, 0))
    args = [z, xbc, dtr, prm["conv_w"], prm["conv_b"], prm["dt_bias"], prm["a_log"], prm["d_vec"],
            prm["norm_g"]]
    in_specs = [seq(SSD_INNER), seq(SSD_CONV_DIM), seq(LANES),
                _layer_spec(l, (SSD_CONV, SSD_CONV_DIM)), _rows_spec(SSD_CONV_DIM),
                _rows_spec(N_DIR * SSD_HEADS), _rows_spec(N_DIR * SSD_HEADS),
                _rows_spec(SSD_INNER), _rows_spec(SSD_INNER)]
    if h0 is not None:
        args.append(h0)
        in_specs.append(st)
    out_shape = [jax.ShapeDtypeStruct((N_TOK, SSD_INNER), F32)]
    out_specs = [seq(SSD_INNER)]
    aliases = {}
    if y_buf is not None:
        aliases[len(args)] = 0
        args.append(y_buf)
        in_specs.append(_any_spec())
    if emit_state:
        out_shape.append(jax.ShapeDtypeStruct((BATCH, DEPTH, N_DIR, SSD_HP, SSD_STATE), F32))
        out_specs.append(st)
        if st_buf is not None:
            aliases[len(args)] = 1
            args.append(st_buf)
            in_specs.append(_any_spec())
    if after is not None:
        args.append(after)
        in_specs.append(_any_spec())
    res = pl.pallas_call(
        functools.partial(_ssd_kernel, l=l, L=L, n_seq=n_seq, has_h0=h0 is not None,
                          n_alias=len(aliases) + int(after is not None), emit_state=emit_state),
        grid=(nb,),
        in_specs=in_specs,
        out_specs=out_specs,
        out_shape=out_shape,
        input_output_aliases=aliases,
        scratch_shapes=[pltpu.VMEM((n_seq * (L + SUBLANES) + SUBLANES, SSD_CONV_DIM), F32),
                        pltpu.VMEM((R, SSD_CONV_DIM), F32),
                        pltpu.VMEM((R, LANES), F32),
                        pltpu.VMEM((R // CHUNK, N_DIR * SSD_HEADS, CHUNK), F32),
                        pltpu.VMEM((R // CHUNK, SSD_GROUPS, CHUNK, CHUNK), F32),
                        pltpu.VMEM((R // CHUNK, SSD_GROUPS, SSD_STATE, CHUNK), BF16),
                        pltpu.VMEM((R, LANES), F32),
                        pltpu.VMEM((N_DIR, R, SSD_INNER), F32),
                        pltpu.VMEM((n_seq, N_DIR, N_PAIR, PAIR, SSD_STATE), F32)],
        compiler_params=_params(),
        name=f"ssd_mixer_L{L}",
    )(*args)
    return res if emit_state else (res[0], None)


_LOG2_CH = S5_GROUP_CH.bit_length() - 1
_LOG2_ST = S5_STATE.bit_length() - 1


def _cmul(ar, ai, br, bi):
    return ar * br - ai * bi, ar * bi + ai * br


def _s5prep_kernel(lre_ref, lim_ref, ldt_ref, btr_ref, bti_ref, ctr_ref, cti_ref,
                   bblk_ref, cblk_ref, pw_ref):
    brow = lax.broadcasted_iota(jnp.int32, (S5_DIM, S5_LANES), 0) >> _LOG2_CH
    bcol = lax.broadcasted_iota(jnp.int32, (S5_DIM, S5_LANES), 1) >> _LOG2_ST
    bmask = brow == bcol
    p1 = lax.broadcasted_iota(jnp.int32, (S5_SEG, S5_LANES), 0) + 1
    for d in range(N_DIR):
        lre = lre_ref[d]
        lim = lim_ref[d]
        step = jnp.exp(ldt_ref[d])
        mag = jnp.exp(lre * step)
        lbr = mag * jnp.cos(lim * step)
        lbi = mag * jnp.sin(lim * step)
        den = lre * lre + lim * lim
        nr = lbr - 1.0
        cr = (nr * lre + lbi * lim) / den
        ci = (lbi * lre - nr * lim) / den
        br, bi = _cmul(cr, ci, btr_ref[...], bti_ref[...])
        br = jnp.where(bmask, br, 0.0).astype(BF16)
        bi = jnp.where(bmask, bi, 0.0).astype(BF16)
        for sl in range(S5_NSLAB):
            bblk_ref[d, sl, :, 0:S5_SLAB] = br[:, sl * S5_SLAB:(sl + 1) * S5_SLAB]
            bblk_ref[d, sl, :, S5_SLAB:2 * S5_SLAB] = bi[:, sl * S5_SLAB:(sl + 1) * S5_SLAB]
        rr = jnp.ones((S5_SEG, S5_LANES), F32)
        ri = jnp.zeros((S5_SEG, S5_LANES), F32)
        sr, si = lbr, lbi
        for k in range(S5_SEG.bit_length()):
            bit = ((p1 >> k) & 1) == 1
            tr, ti = _cmul(rr, ri, sr, si)
            rr = jnp.where(bit, tr, rr)
            ri = jnp.where(bit, ti, ri)
            sr, si = _cmul(sr, si, sr, si)
        pw_ref[d, 0, 0:S5_SEG, :] = rr
        pw_ref[d, 1, 0:S5_SEG, :] = ri
        n = S5_SEG
        while n < S5_PW_ROWS:
            tr, ti = _cmul(pw_ref[d, 0, 0:n, :], pw_ref[d, 1, 0:n, :],
                           pw_ref[d, 0, n - 1:n, :], pw_ref[d, 1, n - 1:n, :])
            pw_ref[d, 0, n:2 * n, :] = tr
            pw_ref[d, 1, n:2 * n, :] = ti
            n *= 2
    crow = lax.broadcasted_iota(jnp.int32, (S5_LANES, S5_DIM), 0) >> _LOG2_ST
    ccol = lax.broadcasted_iota(jnp.int32, (S5_LANES, S5_DIM), 1) >> _LOG2_CH
    cmask = crow == ccol
    cr = jnp.where(cmask, ctr_ref[...], 0.0).astype(BF16)
    ci = jnp.where(cmask, -cti_ref[...], 0.0).astype(BF16)
    for sl in range(S5_NSLAB):
        cblk_ref[sl, 0:S5_SLAB, :] = cr[sl * S5_SLAB:(sl + 1) * S5_SLAB, :]
        cblk_ref[sl, S5_SLAB:2 * S5_SLAB, :] = ci[sl * S5_SLAB:(sl + 1) * S5_SLAB, :]


def _s5_prep(lam_re, lam_im, log_dt, b_re, b_im, c_re, c_im):
    row = lambda t: t.reshape(DEPTH, N_DIR, 1, S5_LANES)
    ldt = jnp.repeat(log_dt, S5_STATE, axis=-1)
    bt = lambda t: jnp.tile(t.transpose(0, 3, 1, 2).reshape(DEPTH, S5_GROUP_CH, S5_LANES),
                            (1, S5_GROUPS, 1))
    ct = lambda t: jnp.tile(t.transpose(0, 1, 3, 2).reshape(DEPTH, S5_LANES, S5_GROUP_CH),
                            (1, 1, S5_GROUPS))
    vec = pl.BlockSpec((None, N_DIR, 1, S5_LANES), lambda l: (l, 0, 0, 0))
    bsp = pl.BlockSpec((None, S5_DIM, S5_LANES), lambda l: (l, 0, 0))
    csp = pl.BlockSpec((None, S5_LANES, S5_DIM), lambda l: (l, 0, 0))
    return pl.pallas_call(
        _s5prep_kernel,
        grid=(DEPTH,),
        in_specs=[vec, vec, vec, bsp, bsp, csp, csp],
        out_specs=[pl.BlockSpec((None, N_DIR, S5_NSLAB, S5_DIM, 2 * S5_SLAB), lambda l: (l, 0, 0, 0, 0)),
                   pl.BlockSpec((None, S5_NSLAB, 2 * S5_SLAB, S5_DIM), lambda l: (l, 0, 0, 0)),
                   pl.BlockSpec((None, N_DIR, 2, S5_PW_ROWS, S5_LANES), lambda l: (l, 0, 0, 0, 0))],
        out_shape=[jax.ShapeDtypeStruct((DEPTH, N_DIR, S5_NSLAB, S5_DIM, 2 * S5_SLAB), BF16),
                   jax.ShapeDtypeStruct((DEPTH, S5_NSLAB, 2 * S5_SLAB, S5_DIM), BF16),
                   jax.ShapeDtypeStruct((DEPTH, N_DIR, 2, S5_PW_ROWS, S5_LANES), F32)],
        compiler_params=_params(),
        name="s5_prep",
    )(row(lam_re), row(lam_im), row(ldt), bt(b_re), bt(b_im), ct(c_re), ct(c_im))


def _s5_moves(n_seq, n_seg, seq_len, col_major):
    steps = seq_len // n_seg
    if not col_major:
        return [(q * seq_len + s * steps, steps, q * n_seg + s, S5_SEG)
                for q in range(n_seq) for s in range(n_seg)]
    assert n_seq == 1 and n_seg == S5_SEG
    grid_rows = seq_len // GRID_W
    wseg = GRID_W // n_seg
    return [(r * GRID_W + s * wseg, wseg, r * S5_SEG + s, grid_rows * S5_SEG)
            for r in range(grid_rows) for s in range(n_seg)]


def _s5_kernel(*refs, l, n_seq, n_seg, seq_len, col_major, has_h0, n_alias, emit_state):
    u_ref, bblk_ref, cblk_ref, pw_ref, dvec_ref, gw_ref, gb_ref = refs[:7]
    k = 7
    h0r_ref, h0i_ref = (refs[k], refs[k + 1]) if has_h0 else (None, None)
    k += 2 * int(has_h0) + n_alias
    y_ref = refs[k]
    fr_ref, fi_ref = (refs[k + 1], refs[k + 2]) if emit_state else (None, None)
    up, buf_f, buf_b, yacc, yp, cin, fin = refs[k + 1 + 2 * int(emit_state):]
    assert n_seq * n_seg == S5_SEG and not (has_h0 and n_seg == 1) and not (emit_state and n_seg > 1)
    n_rows = n_seq * seq_len
    steps = seq_len // n_seg
    n_planes = S5_DIM // LANES
    W = S5_SLAB
    moves = _s5_moves(n_seq, n_seg, seq_len, col_major)

    for src, n, dst, stride in moves:
        for t in range(n_planes):
            up[t, pl.ds(dst, n, stride=stride), :] = u_ref[src:src + n, t * LANES:(t + 1) * LANES]

    def u_rows(rs):
        return jnp.concatenate([up[t, rs, :] for t in range(n_planes)], axis=1)

    zero = jnp.zeros((S5_SEG, W), F32)
    zrow = jnp.zeros((1, W), F32)
    for sl in range(S5_NSLAB):
        ln = slice(sl * W, (sl + 1) * W)
        for r0 in range(0, n_rows, ROW_SLAB):
            rs = slice(r0, r0 + ROW_SLAB)
            ub = u_rows(rs).astype(BF16)
            buf_f[rs, :] = jnp.dot(ub, bblk_ref[0, sl], preferred_element_type=F32)
            buf_b[rs, :] = jnp.dot(ub, bblk_ref[1, sl], preferred_element_type=F32)
        lam = [[jnp.broadcast_to(pw_ref[d, c, 0:1, ln], (S5_SEG, W)) for c in range(2)]
               for d in range(N_DIR)]

        def body(t, carry):
            fr, fi, br, bi = carry
            rf = pl.ds(pl.multiple_of(t * S5_SEG, S5_SEG), S5_SEG)
            rb = pl.ds(pl.multiple_of((steps - 1 - t) * S5_SEG, S5_SEG), S5_SEG)
            nfr = lam[0][0] * fr - lam[0][1] * fi + buf_f[rf, 0:W]
            nfi = lam[0][0] * fi + lam[0][1] * fr + buf_f[rf, W:2 * W]
            nbr = lam[1][0] * br - lam[1][1] * bi + buf_b[rb, 0:W]
            nbi = lam[1][0] * bi + lam[1][1] * br + buf_b[rb, W:2 * W]
            buf_f[rf, 0:W] = nfr
            buf_f[rf, W:2 * W] = nfi
            buf_b[rb, 0:W] = nbr
            buf_b[rb, W:2 * W] = nbi
            return nfr, nfi, nbr, nbi

        ends = lax.fori_loop(0, steps, body, (zero, zero, zero, zero), unroll=S5_UNROLL)

        if n_seg == 1:
            if emit_state:
                for d in range(N_DIR):
                    fin[d, 0, :, ln] = ends[2 * d]
                    fin[d, 1, :, ln] = ends[2 * d + 1]
        else:
            for d in range(N_DIR):
                er, ei = ends[2 * d], ends[2 * d + 1]
                lpr = pw_ref[d, 0, steps - 1:steps, ln]
                lpi = pw_ref[d, 1, steps - 1:steps, ln]
                pr = h0r_ref[d:d + 1, ln] if has_h0 else zrow
                pi = h0i_ref[d:d + 1, ln] if has_h0 else zrow
                order = range(n_seg) if d == 0 else range(n_seg - 1, -1, -1)
                for s in order:
                    cin[d, 0, s:s + 1, :] = pr
                    cin[d, 1, s:s + 1, :] = pi
                    tr, ti = _cmul(lpr, lpi, pr, pi)
                    pr = er[s:s + 1, :] + tr
                    pi = ei[s:s + 1, :] + ti
            cfr = cin[0, 0]
            cfi = cin[0, 1]
            cbr = cin[1, 0]
            cbi = cin[1, 1]

            def fix(p, carry):
                rows = pl.ds(pl.multiple_of(p * S5_SEG, S5_SEG), S5_SEG)
                q = steps - 1 - p
                tfr, tfi = _cmul(pw_ref[0, 0, pl.ds(p, 1), ln], pw_ref[0, 1, pl.ds(p, 1), ln], cfr, cfi)
                tbr, tbi = _cmul(pw_ref[1, 0, pl.ds(q, 1), ln], pw_ref[1, 1, pl.ds(q, 1), ln], cbr, cbi)
                buf_f[rows, 0:W] = buf_f[rows, 0:W] + buf_b[rows, 0:W] + tfr + tbr
                buf_f[rows, W:2 * W] = buf_f[rows, W:2 * W] + buf_b[rows, W:2 * W] + tfi + tbi
                return carry

            lax.fori_loop(0, steps, fix, 0, unroll=S5_UNROLL)

        for r0 in range(0, n_rows, ROW_SLAB):
            rs = slice(r0, r0 + ROW_SLAB)
            hs = buf_f[rs, :] if n_seg > 1 else buf_f[rs, :] + buf_b[rs, :]
            part = jnp.dot(hs.astype(BF16), cblk_ref[sl], preferred_element_type=F32)
            if sl == 0:
                yacc[rs, :] = part
            else:
                yacc[rs, :] = yacc[rs, :] + part

    gwb = gw_ref[...].astype(BF16)
    for r0 in range(0, n_rows, ROW_SLAB):
        rs = slice(r0, r0 + ROW_SLAB)
        y = _gelu_tanh(yacc[rs, :] + dvec_ref[l:l + 1, :] * u_rows(rs))
        gate = jnp.dot(y.astype(BF16), gwb, preferred_element_type=F32) + gb_ref[l:l + 1, :]
        y = y * jax.nn.sigmoid(gate)
        for t in range(n_planes):
            yp[t, rs, :] = y[:, t * LANES:(t + 1) * LANES]
    for src, n, dst, stride in moves:
        for t in range(n_planes):
            y_ref[src:src + n, t * LANES:(t + 1) * LANES] = yp[t, pl.ds(dst, n, stride=stride), :]
    if emit_state:
        for q in range(n_seq):
            for d in range(N_DIR):
                fr_ref[q, d:d + 1, :] = fin[d, 0, q:q + 1, :]
                fi_ref[q, d:d + 1, :] = fin[d, 1, q:q + 1, :]


def _s5_mixer(u, bblk, cblk, pw, prm, l, *, seq_len, n_seq, nb, blk0, col_major, h0=None, y_buf=None,
              st_buf=None, emit_state):
    n_rows = n_seq * seq_len
    n_seg = S5_SEG // n_seq
    seq = pl.BlockSpec((n_rows, S5_DIM), lambda b: (blk0 + b, 0))
    st = pl.BlockSpec((n_seq, None, N_DIR, S5_LANES), lambda b: (b, l, 0, 0))
    once = dict(pipeline_mode=pl.Buffered(1))
    args = [u, bblk, cblk, pw, prm["d_vec"], prm["glu_w"], prm["glu_b"]]
    in_specs = [seq, _layer_spec(l, (N_DIR, S5_NSLAB, S5_DIM, 2 * S5_SLAB), **once),
                _layer_spec(l, (S5_NSLAB, 2 * S5_SLAB, S5_DIM), **once),
                _layer_spec(l, (N_DIR, 2, S5_PW_ROWS if n_seg > 1 else S5_SEG, S5_LANES), **once),
                _rows_spec(S5_DIM), _layer_spec(l, (S5_DIM, S5_DIM)), _rows_spec(S5_DIM)]
    if h0 is not None:
        h0_spec = pl.BlockSpec((None, None, N_DIR, S5_LANES), lambda b: (b, l, 0, 0))
        args += list(h0)
        in_specs += [h0_spec, h0_spec]
    out_shape = [jax.ShapeDtypeStruct((N_TOK, S5_DIM), F32)]
    out_specs = [seq]
    aliases = {}
    if y_buf is not None:
        aliases[len(args)] = 0
        args.append(y_buf)
        in_specs.append(_any_spec())
    if emit_state:
        out_shape += [jax.ShapeDtypeStruct((BATCH, DEPTH, N_DIR, S5_LANES), F32)] * 2
        out_specs += [st, st]
        if st_buf is not None:
            for j, buf in enumerate(st_buf):
                aliases[len(args)] = 1 + j
                args.append(buf)
                in_specs.append(_any_spec())
    res = pl.pallas_call(
        functools.partial(_s5_kernel, l=l, n_seq=n_seq, n_seg=n_seg, seq_len=seq_len, col_major=col_major,
                          has_h0=h0 is not None, n_alias=len(aliases), emit_state=emit_state),
        grid=(nb,),
        in_specs=in_specs,
        out_specs=out_specs,
        out_shape=out_shape,
        input_output_aliases=aliases,
        scratch_shapes=[pltpu.VMEM((S5_DIM // LANES, n_rows, LANES), F32),
                        pltpu.VMEM((n_rows, 2 * S5_SLAB), F32),
                        pltpu.VMEM((n_rows, 2 * S5_SLAB), F32),
                        pltpu.VMEM((n_rows, S5_DIM), F32),
                        pltpu.VMEM((S5_DIM // LANES, n_rows, LANES), F32),
                        pltpu.VMEM((N_DIR, 2, S5_SEG, S5_SLAB), F32),
                        pltpu.VMEM((N_DIR, 2, S5_SEG, S5_LANES), F32)],
        compiler_params=_params(),
        name=f"s5_mixer_L{seq_len}",
    )(*args)
    return (res[0], res[1], res[2]) if emit_state else (res[0], None, None)


def _outffn_kernel(xa_ref, xb_ref, ys_ref, yg_ref, y5_ref, g1_ref, sh_ref, sc_ref, g2_ref, ng_ref,
                   fg_ref, wo_hbm, w1_hbm, w2_hbm, *rest, l, final):
    n_out = 2 if final else 1
    o_refs = rest[:n_out]
    wob, w1b, w2b, st1, st2, sem = rest[n_out:]
    t = pl.program_id(0)
    n_ctx = N_CTX // FFN_TILE
    kf = D_FF // FFN_PREP
    half = D_MODEL // 2
    ns = FFN_SLOTS
    assert half == kf and ns >= 2, "w_out halves are staged in the first two w2 chunk slots"

    def w1_copy(j):
        return pltpu.make_async_copy(w1_hbm.at[l, :, pl.ds(j * kf, kf)], st1.at[j % ns], sem.at[0, j % ns])

    def w2_copy(j):
        return pltpu.make_async_copy(w2_hbm.at[l, pl.ds(j * kf, kf), :], st2.at[j % ns], sem.at[1, j % ns])

    def wo_copy(s):
        return pltpu.make_async_copy(wo_hbm.at[l, pl.ds(s * half, half), :], st2.at[s], sem.at[1, s])

    def tile(first):
        if first:
            for s in range(2):
                wo_copy(s).start()
            for s in range(ns):
                w1_copy(s).start()
            for s in range(2):
                wo_copy(s).wait()
                wob[s * half:(s + 1) * half, :] = st2[s].astype(BF16)
            for s in range(ns):
                w2_copy(s).start()
        r = _mod_row(t, FFN_TILE)
        row = lambda ref: ref[pl.ds(r, 1), :]
        x = jnp.where(t < n_ctx, xa_ref[...], xb_ref[...])
        mixed = jnp.concatenate([ys_ref[...].astype(BF16), yg_ref[...].astype(BF16),
                                 y5_ref[...].astype(BF16)], axis=1)
        x1 = x + row(g1_ref) * jnp.dot(mixed, wob[...], preferred_element_type=F32)
        h = ((_rms(x1) * ng_ref[l:l + 1, :]) * (1.0 + row(sc_ref)) + row(sh_ref)).astype(BF16)
        acc = jnp.zeros((FFN_TILE, D_MODEL), F32)
        for j in range(FFN_PREP):
            if first:
                w1_copy(j).wait()
                w1b[j] = st1[j % ns].astype(BF16)
                w2_copy(j).wait()
                w2b[j] = st2[j % ns].astype(BF16)
                if j + ns < FFN_PREP:
                    w1_copy(j + ns).start()
                    w2_copy(j + ns).start()
            a = jnp.maximum(jnp.dot(h, w1b[j], preferred_element_type=F32), 0.0)
            acc = acc + jnp.dot((a * a).astype(BF16), w2b[j], preferred_element_type=F32)
        x2 = x1 + row(g2_ref) * acc
        if not final:
            o_refs[0][...] = x2
        else:
            y = _rms(x2) * fg_ref[...]

            @pl.when(t < n_ctx)
            def _():
                o_refs[0][...] = y

            @pl.when(t >= n_ctx)
            def _():
                o_refs[1][...] = y

    @pl.when(t == 0)
    def _():
        tile(True)

    @pl.when(t > 0)
    def _():
        tile(False)


def _out_ffn(xa, xb, b_is_stream, y_ssd, y_sgu, y_s5, l, mods, norm_g, final_g, w_out, w1, w2, *, final):
    tok = lambda w: pl.BlockSpec((FFN_TILE, w), lambda i: (i, 0))
    sa, sb = _split_tok_specs(FFN_TILE, D_MODEL, b_is_stream)
    if final:
        out_specs = list(_split_tok_specs(FFN_TILE, D_MODEL, False))
        out_shape = [jax.ShapeDtypeStruct((N_CTX, D_MODEL), F32), jax.ShapeDtypeStruct((N_LAT, D_MODEL), F32)]
    else:
        out_specs = [tok(D_MODEL)]
        out_shape = [jax.ShapeDtypeStruct((N_TOK, D_MODEL), F32)]
    kf = D_FF // FFN_PREP
    return pl.pallas_call(
        functools.partial(_outffn_kernel, l=l, final=final),
        grid=(N_TOK // FFN_TILE,),
        in_specs=[sa, sb, tok(SSD_INNER), tok(SGU_DIM), tok(S5_DIM),
                  _mod_spec(l, 2), _mod_spec(l, 3), _mod_spec(l, 4), _mod_spec(l, 5),
                  _rows_spec(D_MODEL), pl.BlockSpec((1, D_MODEL), lambda i: (0, 0)),
                  _any_spec(), _any_spec(), _any_spec()],
        out_specs=out_specs,
        out_shape=out_shape,
        scratch_shapes=[pltpu.VMEM((D_MODEL, D_MODEL), BF16),
                        pltpu.VMEM((FFN_PREP, D_MODEL, kf), BF16),
                        pltpu.VMEM((FFN_PREP, kf, D_MODEL), BF16),
                        pltpu.VMEM((FFN_SLOTS, D_MODEL, kf), F32),
                        pltpu.VMEM((FFN_SLOTS, kf, D_MODEL), F32),
                        pltpu.SemaphoreType.DMA((2, FFN_SLOTS))],
        compiler_params=pltpu.CompilerParams(dimension_semantics=("arbitrary",),
                                             vmem_limit_bytes=FFN_VMEM_LIMIT),
        name="out_ffn_final" if final else "out_ffn",
    )(xa, xb, y_ssd, y_sgu, y_s5, mods, mods, mods, mods, norm_g, final_g, w_out, w1, w2)


def kernel(x_prompt, x_sample, state_ssd, state_s5_re, state_s5_im, c, c_ctx, ada_w, ada_b, norm1_g,
           norm2_g, w_in, ssd_conv_w, ssd_conv_b, ssd_dt_bias, ssd_a_log, ssd_d, ssd_norm_g,
           sgu_norm_g, sgu_w, sgu_b, s5_lambda_re, s5_lambda_im, s5_log_dt, s5_b_re, s5_b_im,
           s5_c_re, s5_c_im, s5_d, s5_glu_w, s5_glu_b, w_out, ffn_w1, ffn_w2, final_norm_g):
    cvec = jnp.concatenate([c_ctx[None, :], c,
                            jnp.zeros((MOD_ROWS - 1 - DEC_BATCH, D_MODEL), F32)], axis=0)
    mods = _adaln_mods(cvec, ada_w, ada_b)
    bblk, cblk, pw = _s5_prep(s5_lambda_re, s5_lambda_im, s5_log_dt, s5_b_re, s5_b_im,
                              s5_c_re, s5_c_im)

    w_in_t = w_in.transpose(0, 2, 1)
    ssd_prm = dict(conv_w=ssd_conv_w, conv_b=ssd_conv_b,
                   dt_bias=ssd_dt_bias.reshape(DEPTH, N_DIR * SSD_HEADS),
                   a_log=ssd_a_log.reshape(DEPTH, N_DIR * SSD_HEADS),
                   d_vec=jnp.repeat(ssd_d, SSD_HEAD_DIM, axis=-1), norm_g=ssd_norm_g)
    s5_prm = dict(d_vec=s5_d, glu_w=s5_glu_w, glu_b=s5_glu_b)
    sgu_w_pair = sgu_w.reshape(DEPTH, SGU_HEADS // 2, 2, CHUNK, CHUNK).transpose(0, 1, 3, 2, 4)
    sgu_w_pair = sgu_w_pair.reshape(DEPTH, SGU_HEADS // 2, CHUNK, 2 * CHUNK).astype(BF16)
    sgu_b_full = jnp.repeat(sgu_b.transpose(0, 2, 1), SGU_DIM // SGU_HEADS, axis=2)
    norm1 = norm1_g
    norm2 = norm2_g
    sgu_g = sgu_norm_g
    final_g = final_norm_g.reshape(1, D_MODEL)

    lat_ssd = state_ssd.reshape(DEC_BATCH, DEPTH, N_DIR, SSD_HP, SSD_STATE)
    lat_s5 = (state_s5_re.reshape(DEC_BATCH, DEPTH, N_DIR, S5_LANES),
              state_s5_im.reshape(DEC_BATCH, DEPTH, N_DIR, S5_LANES))
    lat_blk = N_CTX // DEC_SEQ

    xa = x_prompt.reshape(N_CTX, D_MODEL)
    xb = x_sample.reshape(N_LAT, D_MODEL)
    b_is_stream = False
    st_ssd = None
    st_s5 = None
    for l in range(DEPTH):
        z, xbc, y_sgu, s5u, dtr = _in_proj(xa, xb, b_is_stream, l, mods, norm1, w_in_t, sgu_g,
                                           sgu_w_pair, sgu_b_full)
        y_s5, fr, fi = _s5_mixer(s5u, bblk, cblk, pw, s5_prm, l, seq_len=SEQ, n_seq=S5_SEG,
                                 nb=BATCH // S5_SEG, blk0=0, col_major=False, st_buf=st_s5,
                                 emit_state=True)
        st_s5 = (fr, fi)
        y_s5, _, _ = _s5_mixer(s5u, bblk, cblk, pw, s5_prm, l, seq_len=DEC_SEQ, n_seq=1, nb=DEC_BATCH,
                               blk0=lat_blk, col_major=True, h0=lat_s5, y_buf=y_s5, emit_state=False)
        y_ssd, st_ssd = _ssd_mixer(z, xbc, dtr, ssd_prm, l, L=SEQ, n_seq=SSD_CTX_SEQS,
                                   nb=BATCH // SSD_CTX_SEQS, blk0=0, st_buf=st_ssd, after=y_s5,
                                   emit_state=True)
        y_ssd, _ = _ssd_mixer(z, xbc, dtr, ssd_prm, l, L=DEC_SEQ, n_seq=1, nb=DEC_BATCH, blk0=lat_blk,
                              h0=lat_ssd, y_buf=y_ssd, emit_state=False)
        final = l == DEPTH - 1
        out = _out_ffn(xa, xb, b_is_stream, y_ssd, y_sgu, y_s5, l, mods, norm2, final_g, w_out, ffn_w1,
                       ffn_w2, final=final)
        if not final:
            xa = xb = out[0]
            b_is_stream = True

    y_prompt = out[0].reshape(BATCH, SEQ, D_MODEL)
    y_sample = out[1].reshape(DEC_BATCH, DEC_SEQ, D_MODEL)
    new_state_ssd = st_ssd.reshape(BATCH, DEPTH, N_DIR, SSD_HEADS, SSD_HEAD_DIM, SSD_STATE)
    s5_shape = (BATCH, DEPTH, N_DIR, S5_GROUPS, S5_STATE)
    return (y_prompt, y_sample, new_state_ssd, st_s5[0].reshape(s5_shape), st_s5[1].reshape(s5_shape))
```
